```python
import math
import jax
import jax.numpy as jnp
from jax import lax
import numpy as np

D_MODEL = 2048
BATCH = 4
SEQ = 2048
DEPTH = 2
DEC_BATCH = 128
DEC_SEQ = 4
PAST_LEN = 16384
PAGE_SIZE = 128

MIX_WIDTH = D_MODEL
GROUP_WIDTH = MIX_WIDTH // 4
CONV_W = 4
EPS = 1e-6
PLE_DIM = 256
FFN_HIDDEN = ((8 * D_MODEL + 3 * 256 - 1) // (3 * 256)) * 256

S5_CH = GROUP_WIDTH
S5_GROUP_CH = 16
S5_GROUPS = S5_CH // S5_GROUP_CH
S5_STATE = 64

GDN_HEADS = 4
GDN_DK = GROUP_WIDTH // GDN_HEADS
GDN_DV = GROUP_WIDTH // GDN_HEADS
GDN_CHUNK = 64
GDN_CONV_DIM = 2 * GDN_HEADS * GDN_DK + GDN_HEADS * GDN_DV

SSD_INNER = GROUP_WIDTH
SSD_HEADDIM = 64
SSD_HEADS = SSD_INNER // SSD_HEADDIM
SSD_NGROUPS = 2
SSD_STATE = 128
SSD_CHUNK = 128
SSD_CONV_DIM = SSD_INNER + 2 * SSD_NGROUPS * SSD_STATE

RET_HEADS = 4
RET_DK = GROUP_WIDTH // RET_HEADS
RET_DV = GROUP_WIDTH // RET_HEADS
RET_CHUNK = 128
ROPE_BASE = 10000.0

IN_SIZES = (
    S5_CH,
    GDN_HEADS * GDN_DK, GDN_HEADS * GDN_DK, GDN_HEADS * GDN_DV, GDN_HEADS * GDN_DV, GDN_HEADS, GDN_HEADS,
    SSD_INNER, SSD_CONV_DIM, SSD_HEADS,
    RET_HEADS * RET_DK, RET_HEADS * RET_DK, RET_HEADS * RET_DV, RET_HEADS * RET_DV,
)
IN_TOTAL = sum(IN_SIZES)

kernel_name = 'hybrid_s5_gdn_ssd_retention_step'


def split_last(t, sizes):
    return jnp.split(t, [int(s) for s in np.cumsum(sizes)[:-1]], axis=-1)


def rms_norm(x, w):
    xf = x.astype(jnp.float32)
    y = xf * lax.rsqrt(jnp.mean(xf * xf, axis=-1, keepdims=True) + EPS)
    return y.astype(x.dtype) * w


def l2_normalize(x):
    xf = x.astype(jnp.float32)
    return xf * lax.rsqrt(jnp.sum(xf * xf, axis=-1, keepdims=True) + EPS)


def causal_conv(x, buf, w):
    L = x.shape[1]
    xp = jnp.concatenate([buf.astype(x.dtype), x], axis=1)
    y = xp[:, 0:L] * w[0]
    for j in range(1, CONV_W):
        y = y + xp[:, j:j + L] * w[j]
    return y, xp[:, L:]


def rotary(x, positions):
    half = x.shape[-1] // 2
    inv_freq = ROPE_BASE ** (-jnp.arange(half, dtype=jnp.float32) / half)
    ang = positions.astype(jnp.float32)[:, None] * inv_freq[None, :]
    cos = jnp.cos(ang)[None, :, None, :]
    sin = jnp.sin(ang)[None, :, None, :]
    xf = x.astype(jnp.float32)
    x1, x2 = xf[..., :half], xf[..., half:]
    return jnp.concatenate([x1 * cos - x2 * sin, x1 * sin + x2 * cos], axis=-1)


def to_chunks(t, chunk):
    L = t.shape[1]
    pad = (-L) % chunk
    t = jnp.pad(t, [(0, 0), (0, pad)] + [(0, 0)] * (t.ndim - 2))
    n = t.shape[1] // chunk
    t = t.reshape((t.shape[0], n, chunk) + t.shape[2:])
    return jnp.moveaxis(t, 2, 3)


def from_chunks(t, L):
    t = jnp.moveaxis(t, 3, 2)
    t = t.reshape((t.shape[0], t.shape[1] * t.shape[2]) + t.shape[3:])
    return t[:, :L]


def intra_decay(G):
    C = G.shape[-1]
    causal = jnp.tril(jnp.ones((C, C), dtype=bool))
    diff = G[..., :, None] - G[..., None, :]
    return jnp.where(causal, jnp.exp(jnp.where(causal, diff, 0.0)), 0.0)


def decay_linear_attention(q, k, v, log_a, s0, chunk):
    L = q.shape[1]
    C = min(chunk, L)
    qc, kc, vc = (to_chunks(t.astype(jnp.float32), C) for t in (q, k, v))
    G = jnp.cumsum(to_chunks(log_a.astype(jnp.float32), C), axis=-1)
    G_last = G[..., -1]
    scores = jnp.einsum('bnhid,bnhjd->bnhij', qc, kc) * intra_decay(G)
    intra = jnp.einsum('bnhij,bnhjv->bnhiv', scores, vc)
    chunk_states = jnp.einsum('bnhcd,bnhcv->bnhdv', kc * jnp.exp(G_last[..., None] - G)[..., None], vc)

    def step(S, inp):
        cs, gl = inp
        return S * gl[..., None, None] + cs, S

    s_final, s_prev = lax.scan(step, s0.astype(jnp.float32),
                               (jnp.moveaxis(chunk_states, 1, 0), jnp.moveaxis(jnp.exp(G_last), 1, 0)))
    s_prev = jnp.moveaxis(s_prev, 0, 1)
    inter = jnp.einsum('bnhcd,bnhdv->bnhcv', qc * jnp.exp(G)[..., None], s_prev)
    return from_chunks(intra + inter, L), s_final


def gated_delta_rule(q, k, v, beta, g, s0, chunk):
    L = q.shape[1]
    C = min(chunk, L)
    qc, kc, vc = (to_chunks(t.astype(jnp.float32), C) for t in (q, k, v))
    bc = to_chunks(beta.astype(jnp.float32), C)
    G = jnp.cumsum(to_chunks(g.astype(jnp.float32), C), axis=-1)
    decay = intra_decay(G)
    strict = jnp.tril(jnp.ones((C, C), dtype=bool), -1)
    kk = jnp.einsum('bnhid,bnhjd->bnhij', kc, kc)
    a_mat = jnp.where(strict, bc[..., :, None] * kk * decay, 0.0) + jnp.eye(C, dtype=jnp.float32)
    u = lax.linalg.triangular_solve(a_mat, vc * bc[..., None], left_side=True, lower=True, unit_diagonal=True)
    w = lax.linalg.triangular_solve(a_mat, kc * (bc * jnp.exp(G))[..., None], left_side=True, lower=True,
                                    unit_diagonal=True)
    scores = jnp.einsum('bnhid,bnhjd->bnhij', qc, kc) * decay
    q_dec = qc * jnp.exp(G)[..., None]
    k_dec = kc * jnp.exp(G[..., -1:] - G)[..., None]
    g_last = jnp.exp(G[..., -1])

    def step(S, inp):
        u_c, w_c, s_c, qd_c, kd_c, gl_c = inp
        v_new = u_c - jnp.einsum('bhcd,bhdv->bhcv', w_c, S)
        o = jnp.einsum('bhcd,bhdv->bhcv', qd_c, S) + jnp.einsum('bhij,bhjv->bhiv', s_c, v_new)
        S = S * gl_c[..., None, None] + jnp.einsum('bhcd,bhcv->bhdv', kd_c, v_new)
        return S, o

    xs = tuple(jnp.moveaxis(t, 1, 0) for t in (u, w, scores, q_dec, k_dec, g_last))
    s_final, o = lax.scan(step, s0.astype(jnp.float32), xs)
    return from_chunks(jnp.moveaxis(o, 0, 1), L), s_final


def complex_affine_combine(e1, e2):
    a1r, a1i, b1r, b1i = e1
    a2r, a2i, b2r, b2i = e2
    return (a2r * a1r - a2i * a1i, a2r * a1i + a2i * a1r,
            a2r * b1r - a2i * b1i + b2r, a2r * b1i + a2i * b1r + b2i)


def s5_mixer(u, h_re0, h_im0, lw):
    f32 = jnp.float32
    bsz, L, _ = u.shape
    a_re = lw['s5_a_re'].astype(f32)
    a_im = lw['s5_a_im'].astype(f32)
    step = jnp.exp(lw['s5_log_step'].astype(f32))[:, None]
    mag = jnp.exp(a_re * step)
    lam_re = mag * jnp.cos(a_im * step)
    lam_im = mag * jnp.sin(a_im * step)
    den = a_re * a_re + a_im * a_im
    coef_re = ((lam_re - 1.0) * a_re + lam_im * a_im) / den
    coef_im = (lam_im * a_re - (lam_re - 1.0) * a_im) / den
    b_re = lw['s5_b_re'].astype(f32)
    b_im = lw['s5_b_im'].astype(f32)
    bb_re = coef_re[..., None] * b_re - coef_im[..., None] * b_im
    bb_im = coef_re[..., None] * b_im + coef_im[..., None] * b_re
    ug = u.astype(f32).reshape(bsz, L, S5_GROUPS, S5_GROUP_CH)
    drive_re = jnp.einsum('blgc,gnc->blgn', ug, bb_re)
    drive_im = jnp.einsum('blgc,gnc->blgn', ug, bb_im)
    h0_re = h_re0.astype(f32)
    h0_im = h_im0.astype(f32)
    drive_re = drive_re.at[:, 0].add(lam_re * h0_re - lam_im * h0_im)
    drive_im = drive_im.at[:, 0].add(lam_re * h0_im + lam_im * h0_re)
    lam_re_b = jnp.broadcast_to(lam_re, drive_re.shape)
    lam_im_b = jnp.broadcast_to(lam_im, drive_im.shape)
    _, _, hs_re, hs_im = lax.associative_scan(complex_affine_combine, (lam_re_b, lam_im_b, drive_re, drive_im), axis=1)
    c_re = lw['s5_c_re'].astype(f32)
    c_im = lw['s5_c_im'].astype(f32)
    y = jnp.einsum('blgn,gcn->blgc', hs_re, c_re) - jnp.einsum('blgn,gcn->blgc', hs_im, c_im)
    y = y.reshape(bsz, L, S5_CH) + lw['s5_d'].astype(f32) * u.astype(f32)
    y = jax.nn.gelu(y)
    out = y * jax.nn.sigmoid(y @ lw['s5_w_glu'].astype(f32) + lw['s5_b_glu'].astype(f32))
    return out.astype(u.dtype), hs_re[:, -1], hs_im[:, -1]


def gdn_mixer(q, k, v, z, b_logit, a_logit, buf0, s0, lw):
    dt_out = z.dtype
    bsz, L, _ = q.shape
    qkv, buf = causal_conv(jnp.concatenate([q, k, v], axis=-1), buf0, lw['gdn_conv_w'])
    qkv = jax.nn.silu(qkv)
    qh, kh, vh = split_last(qkv, (GDN_HEADS * GDN_DK, GDN_HEADS * GDN_DK, GDN_HEADS * GDN_DV))
    qh = l2_normalize(qh.reshape(bsz, L, GDN_HEADS, GDN_DK)) * (GDN_DK ** -0.5)
    kh = l2_normalize(kh.reshape(bsz, L, GDN_HEADS, GDN_DK))
    vh = vh.reshape(bsz, L, GDN_HEADS, GDN_DV)
    beta = jax.nn.sigmoid(b_logit.astype(jnp.float32))
    g = -jnp.exp(lw['gdn_a_log'].astype(jnp.float32)) * jax.nn.softplus(
        a_logit.astype(jnp.float32) + lw['gdn_dt_bias'].astype(jnp.float32))
    o, s = gated_delta_rule(qh, kh, vh, beta, g, s0, GDN_CHUNK)
    o = o * lax.rsqrt(jnp.mean(o * o, axis=-1, keepdims=True) + EPS) * lw['gdn_norm_w'].astype(jnp.float32)
    o = o * jax.nn.silu(z.astype(jnp.float32).reshape(bsz, L, GDN_HEADS, GDN_DV))
    return o.reshape(bsz, L, GDN_HEADS * GDN_DV).astype(dt_out), buf, s


def ssd_mixer(z, xbc, dt_raw, buf0, s0, lw):
    f32 = jnp.float32
    bsz, L, _ = z.shape
    xbc, buf = causal_conv(xbc, buf0, lw['ssd_conv_w'])
    xbc = jax.nn.silu(xbc + lw['ssd_conv_b'])
    xs, bm, cm = split_last(xbc, (SSD_INNER, SSD_NGROUPS * SSD_STATE, SSD_NGROUPS * SSD_STATE))
    rep = SSD_HEADS // SSD_NGROUPS
    xs = xs.astype(f32).reshape(bsz, L, SSD_HEADS, SSD_HEADDIM)
    bm = jnp.repeat(bm.astype(f32).reshape(bsz, L, SSD_NGROUPS, SSD_STATE), rep, axis=2)
    cm = jnp.repeat(cm.astype(f32).reshape(bsz, L, SSD_NGROUPS, SSD_STATE), rep, axis=2)
    dt = jax.nn.softplus(dt_raw.astype(f32) + lw['ssd_dt_bias'].astype(f32))
    a = -jnp.exp(lw['ssd_a_log'].astype(f32))
    y, s = decay_linear_attention(cm, bm * dt[..., None], xs, dt * a, s0, SSD_CHUNK)
    y = y + xs * lw['ssd_d'].astype(f32)[:, None]
    y = y.reshape(bsz, L, SSD_INNER) * jax.nn.silu(z.astype(f32))
    y = y.reshape(bsz, L, SSD_NGROUPS, SSD_INNER // SSD_NGROUPS)
    y = y * lax.rsqrt(jnp.mean(y * y, axis=-1, keepdims=True) + EPS)
    y = y.reshape(bsz, L, SSD_INNER) * lw['ssd_norm_w'].astype(f32)
    return y.astype(z.dtype), buf, s


def retention_mixer(q, k, v, gate, s0, positions, lw):
    f32 = jnp.float32
    bsz, L, _ = q.shape
    qh = rotary(q.reshape(bsz, L, RET_HEADS, RET_DK), positions)
    kh = rotary(k.reshape(bsz, L, RET_HEADS, RET_DK), positions) * (RET_DK ** -0.5)
    vh = v.reshape(bsz, L, RET_HEADS, RET_DV)
    log_gamma = jnp.log(1.0 - 2.0 ** (-5.0 - jnp.arange(RET_HEADS, dtype=f32)))
    log_a = jnp.broadcast_to(log_gamma, (bsz, L, RET_HEADS))
    o, s = decay_linear_attention(qh, kh, vh, log_a, s0, RET_CHUNK)
    mu = jnp.mean(o, axis=-1, keepdims=True)
    var = jnp.mean((o - mu) ** 2, axis=-1, keepdims=True)
    o = ((o - mu) * lax.rsqrt(var + EPS)).reshape(bsz, L, RET_HEADS * RET_DV)
    o = o * lw['ret_ln_w'].astype(f32) + lw['ret_ln_b'].astype(f32)
    out = jax.nn.silu(gate.astype(f32)) * o
    return out.astype(q.dtype), s


def trunk_layer(h, p_emb, st, lw, positions):
    s5_re0, s5_im0, gdn_s0, gdn_buf0, ssd_s0, ssd_buf0, ret_s0 = st
    xn = rms_norm(h, lw['norm_mix'])
    (s5_u, g_q, g_k, g_v, g_z, g_b, g_a, c_z, c_xbc, c_dt, r_q, r_k, r_v, r_g) = split_last(
        xn @ lw['w_in'], IN_SIZES)
    out_a, s5_re, s5_im = s5_mixer(s5_u, s5_re0, s5_im0, lw)
    out_b, gdn_buf, gdn_s = gdn_mixer(g_q, g_k, g_v, g_z, g_b, g_a, gdn_buf0, gdn_s0, lw)
    out_c, ssd_buf, ssd_s = ssd_mixer(c_z, c_xbc, c_dt, ssd_buf0, ssd_s0, lw)
    out_d, ret_s = retention_mixer(r_q, r_k, r_v, r_g, ret_s0, positions, lw)
    mixed = jnp.concatenate([out_a, out_b, out_c, out_d], axis=-1)
    h = h + mixed @ lw['w_out']
    hn = rms_norm(h, lw['norm_ffn'])
    gate, up = split_last(hn @ lw['w_ffn_in'], (FFN_HIDDEN, FFN_HIDDEN))
    h = h + (jax.nn.silu(gate) * up) @ lw['w_ffn_out']
    ple_gate = jax.nn.sigmoid(rms_norm(h, lw['norm_ple']) @ lw['w_ple_gate'])
    h = h + ple_gate * (p_emb @ lw['w_ple_proj'])
    return h, (s5_re, s5_im, gdn_s, gdn_buf, ssd_s, ssd_buf, ret_s)


def run_trunk(x, p, states, layers, norm_final, positions):
    h = x
    per_layer = []
    for i in range(DEPTH):
        h, new_st = trunk_layer(h, p[i], [s[i] for s in states], layers[i], positions)
        per_layer.append(new_st)
    y = rms_norm(h, norm_final)
    stacked = [jnp.stack([st[j] for st in per_layer]) for j in range(len(states))]
    return y, stacked


def setup_inputs(seed: int = 0) -> dict:
    key = jax.random.key(seed)
    ks = iter(jax.random.split(key, 96))
    f32 = jnp.float32

    def nrm(shape, scale):
        return jax.random.normal(next(ks), shape, f32) * scale

    def unif(shape, lo, hi):
        return jax.random.uniform(next(ks), shape, f32, lo, hi)

    def gain(shape):
        return 1.0 + nrm(shape, 0.02)

    def dt_bias(shape):
        dt = jnp.exp(unif(shape, math.log(1e-3), math.log(1e-1)))
        return dt + jnp.log(-jnp.expm1(-dt))

    n_idx = jnp.arange(S5_STATE, dtype=f32)
    return {
        'x_prompt': nrm((BATCH, SEQ, D_MODEL), 1.0),
        'x_sample': nrm((DEC_BATCH, DEC_SEQ, D_MODEL), 1.0),
        'p_prompt': nrm((DEPTH, BATCH, SEQ, PLE_DIM), 1.0),
        'p_sample': nrm((DEPTH, DEC_BATCH, DEC_SEQ, PLE_DIM), 1.0),
        'state_s5_re': nrm((DEPTH, DEC_BATCH, S5_GROUPS, S5_STATE), 0.3),
        'state_s5_im': nrm((DEPTH, DEC_BATCH, S5_GROUPS, S5_STATE), 0.3),
        'state_gdn': nrm((DEPTH, DEC_BATCH, GDN_HEADS, GDN_DK, GDN_DV), 0.1),
        'state_gdn_conv': nrm((DEPTH, DEC_BATCH, CONV_W - 1, GDN_CONV_DIM), 1.0),
        'state_ssd': nrm((DEPTH, DEC_BATCH, SSD_HEADS, SSD_STATE, SSD_HEADDIM), 0.1),
        'state_ssd_conv': nrm((DEPTH, DEC_BATCH, CONV_W - 1, SSD_CONV_DIM), 1.0),
        'state_ret': nrm((DEPTH, DEC_BATCH, RET_HEADS, RET_DK, RET_DV), 0.3),
        'norm_mix': gain((DEPTH, D_MODEL)),
        'w_in': nrm((DEPTH, D_MODEL, IN_TOTAL), D_MODEL ** -0.5),
        's5_a_re': -0.5 + nrm((DEPTH, S5_GROUPS, S5_STATE), 0.01),
        's5_a_im': math.pi * n_idx + nrm((DEPTH, S5_GROUPS, S5_STATE), 0.01),
        's5_b_re': nrm((DEPTH, S5_GROUPS, S5_STATE, S5_GROUP_CH), (2 * S5_GROUP_CH) ** -0.5),
        's5_b_im': nrm((DEPTH, S5_GROUPS, S5_STATE, S5_GROUP_CH), (2 * S5_GROUP_CH) ** -0.5),
        's5_c_re': nrm((DEPTH, S5_GROUPS, S5_GROUP_CH, S5_STATE), S5_STATE ** -0.5),
        's5_c_im': nrm((DEPTH, S5_GROUPS, S5_GROUP_CH, S5_STATE), S5_STATE ** -0.5),
        's5_d': nrm((DEPTH, S5_CH), 0.5),
        's5_log_step': unif((DEPTH, S5_GROUPS), math.log(1e-3), math.log(1e-1)),
        's5_w_glu': nrm((DEPTH, S5_CH, S5_CH), S5_CH ** -0.5),
        's5_b_glu': nrm((DEPTH, S5_CH), 0.02),
        'gdn_conv_w': nrm((DEPTH, CONV_W, GDN_CONV_DIM), CONV_W ** -0.5),
        'gdn_a_log': jnp.log(unif((DEPTH, GDN_HEADS), 1.0, 16.0)),
        'gdn_dt_bias': dt_bias((DEPTH, GDN_HEADS)),
        'gdn_norm_w': gain((DEPTH, GDN_DV)),
        'ssd_conv_w': nrm((DEPTH, CONV_W, SSD_CONV_DIM), CONV_W ** -0.5),
        'ssd_conv_b': nrm((DEPTH, SSD_CONV_DIM), 0.02),
        'ssd_dt_bias': dt_bias((DEPTH, SSD_HEADS)),
        'ssd_a_log': jnp.log(unif((DEPTH, SSD_HEADS), 1.0, 16.0)),
        'ssd_d': gain((DEPTH, SSD_HEADS)),
        'ssd_norm_w': gain((DEPTH, SSD_INNER)),
        'ret_ln_w': gain((DEPTH, RET_HEADS * RET_DV)),
        'ret_ln_b': nrm((DEPTH, RET_HEADS * RET_DV), 0.02),
        'w_out': nrm((DEPTH, MIX_WIDTH, D_MODEL), MIX_WIDTH ** -0.5),
        'norm_ffn': gain((DEPTH, D_MODEL)),
        'w_ffn_in': nrm((DEPTH, D_MODEL, 2 * FFN_HIDDEN), D_MODEL ** -0.5),
        'w_ffn_out': nrm((DEPTH, FFN_HIDDEN, D_MODEL), FFN_HIDDEN ** -0.5),
        'norm_ple': gain((DEPTH, D_MODEL)),
        'w_ple_gate': nrm((DEPTH, D_MODEL, D_MODEL), D_MODEL ** -0.5),
        'w_ple_proj': nrm((DEPTH, PLE_DIM, D_MODEL), PLE_DIM ** -0.5),
        'norm_final': gain((D_MODEL,)),
    }


def reference(x_prompt, x_sample, p_prompt, p_sample, state_s5_re, state_s5_im, state_gdn, state_gdn_conv,
              state_ssd, state_ssd_conv, state_ret, norm_mix, w_in, s5_a_re, s5_a_im, s5_b_re, s5_b_im,
              s5_c_re, s5_c_im, s5_d, s5_log_step, s5_w_glu, s5_b_glu, gdn_conv_w, gdn_a_log, gdn_dt_bias,
              gdn_norm_w, ssd_conv_w, ssd_conv_b, ssd_dt_bias, ssd_a_log, ssd_d, ssd_norm_w, ret_ln_w, ret_ln_b,
              w_out, norm_ffn, w_ffn_in, w_ffn_out, norm_ple, w_ple_gate, w_ple_proj, norm_final):
    layers = [dict(
        norm_mix=norm_mix[i], w_in=w_in[i],
        s5_a_re=s5_a_re[i], s5_a_im=s5_a_im[i], s5_b_re=s5_b_re[i], s5_b_im=s5_b_im[i],
        s5_c_re=s5_c_re[i], s5_c_im=s5_c_im[i], s5_d=s5_d[i], s5_log_step=s5_log_step[i],
        s5_w_glu=s5_w_glu[i], s5_b_glu=s5_b_glu[i],
        gdn_conv_w=gdn_conv_w[i], gdn_a_log=gdn_a_log[i], gdn_dt_bias=gdn_dt_bias[i], gdn_norm_w=gdn_norm_w[i],
        ssd_conv_w=ssd_conv_w[i], ssd_conv_b=ssd_conv_b[i], ssd_dt_bias=ssd_dt_bias[i], ssd_a_log=ssd_a_log[i],
        ssd_d=ssd_d[i], ssd_norm_w=ssd_norm_w[i],
        ret_ln_w=ret_ln_w[i], ret_ln_b=ret_ln_b[i],
        w_out=w_out[i], norm_ffn=norm_ffn[i], w_ffn_in=w_ffn_in[i], w_ffn_out=w_ffn_out[i],
        norm_ple=norm_ple[i], w_ple_gate=w_ple_gate[i], w_ple_proj=w_ple_proj[i],
    ) for i in range(DEPTH)]

    bp = x_prompt.shape[0]

    def zeros(shape):
        return jnp.zeros((DEPTH, bp) + shape, jnp.float32)

    prompt_init = [zeros((S5_GROUPS, S5_STATE)), zeros((S5_GROUPS, S5_STATE)),
                   zeros((GDN_HEADS, GDN_DK, GDN_DV)), zeros((CONV_W - 1, GDN_CONV_DIM)),
                   zeros((SSD_HEADS, SSD_STATE, SSD_HEADDIM)), zeros((CONV_W - 1, SSD_CONV_DIM)),
                   zeros((RET_HEADS, RET_DK, RET_DV))]
    pos_prompt = jnp.arange(x_prompt.shape[1], dtype=jnp.int32)
    y_prompt, prompt_states = run_trunk(x_prompt, p_prompt, prompt_init, layers, norm_final, pos_prompt)
    (prompt_s5_re, prompt_s5_im, prompt_gdn, prompt_gdn_conv, prompt_ssd, prompt_ssd_conv,
     prompt_ret) = prompt_states

    sample_init = [state_s5_re, state_s5_im, state_gdn, state_gdn_conv, state_ssd, state_ssd_conv, state_ret]
    pos_sample = PAST_LEN + jnp.arange(x_sample.shape[1], dtype=jnp.int32)
    y_sample, sample_states = run_trunk(x_sample, p_sample, sample_init, layers, norm_final, pos_sample)
    (sample_s5_re, sample_s5_im, sample_gdn, sample_gdn_conv, sample_ssd, sample_ssd_conv,
     sample_ret) = sample_states

    return (y_prompt, y_sample,
            prompt_s5_re, prompt_s5_im, prompt_gdn, prompt_gdn_conv, prompt_ssd, prompt_ssd_conv, prompt_ret,
            sample_s5_re, sample_s5_im, sample_gdn, sample_gdn_conv, sample_ssd, sample_ssd_conv, sample_ret)
```

```python
import functools
import math

import jax
import jax.numpy as jnp
import numpy as np
from jax import lax
from jax.experimental import pallas as pl
from jax.experimental.pallas import tpu as pltpu

F32 = jnp.float32
BF16 = jnp.bfloat16

D_MODEL = 2048
DEPTH = 2
GROUP_WIDTH = D_MODEL // 4
CONV_W = 4
EPS = 1e-6
PLE_DIM = 256
FFN_HIDDEN = ((8 * D_MODEL + 3 * 256 - 1) // (3 * 256)) * 256

S5_CH = GROUP_WIDTH
S5_GROUP_CH = 16
S5_GROUPS = S5_CH // S5_GROUP_CH
S5_STATE = 64

GDN_HEADS = 4
GDN_DK = GROUP_WIDTH // GDN_HEADS
GDN_DV = GROUP_WIDTH // GDN_HEADS
GDN_CHUNK = 64
GDN_CONV_DIM = 2 * GDN_HEADS * GDN_DK + GDN_HEADS * GDN_DV

SSD_INNER = GROUP_WIDTH
SSD_HEADDIM = 64
SSD_HEADS = SSD_INNER // SSD_HEADDIM
SSD_NGROUPS = 2
SSD_STATE = 128
SSD_CHUNK = 128
SSD_CONV_DIM = SSD_INNER + 2 * SSD_NGROUPS * SSD_STATE

RET_HEADS = 4
RET_DK = GROUP_WIDTH // RET_HEADS
RET_DV = GROUP_WIDTH // RET_HEADS
RET_CHUNK = 128
ROPE_BASE = 10000.0
PAST_LEN = 16384

IN_SIZES = (
    S5_CH,
    GDN_HEADS * GDN_DK, GDN_HEADS * GDN_DK, GDN_HEADS * GDN_DV, GDN_HEADS * GDN_DV, GDN_HEADS, GDN_HEADS,
    SSD_INNER, SSD_CONV_DIM, SSD_HEADS,
    RET_HEADS * RET_DK, RET_HEADS * RET_DK, RET_HEADS * RET_DV, RET_HEADS * RET_DV,
)
IN_OFFS = tuple(int(v) for v in np.cumsum((0,) + IN_SIZES))

_WIDE = (0, 1, 2, 3, 4, 7, 8, 10, 11, 12, 13)
_NARROW = (5, 6, 9)
PK_OFF = {}
_o = 0
for _s in _WIDE:
    PK_OFF[_s] = _o
    _o += IN_SIZES[_s]
PK_NARROW = _o
for _s in _NARROW:
    PK_OFF[_s] = _o
    _o += IN_SIZES[_s]
PK_USED = _o
PK_TOTAL = 6400

V7X_VMEM_LIMIT = 58 * 1024 * 1024


def _pack_w_in(w_in):
    cols = [w_in[:, IN_OFFS[s]:IN_OFFS[s + 1]] for s in _WIDE + _NARROW]
    cols.append(jnp.zeros((w_in.shape[0], PK_TOTAL - PK_USED), w_in.dtype))
    return jnp.concatenate(cols, axis=1).astype(BF16)


def _rms_rows(x, nw):
    ms = jnp.mean(x * x, axis=-1, keepdims=True)
    return x * lax.rsqrt(ms + EPS) * nw


def _rms_mm_kernel(x_ref, nw_ref, w_ref, o_ref, xn_ref):
    @pl.when(pl.program_id(1) == 0)
    def _():
        xn_ref[...] = _rms_rows(x_ref[...], nw_ref[...]).astype(BF16)

    o_ref[...] = jnp.dot(xn_ref[...], w_ref[...], preferred_element_type=F32)


def rms_matmul(x, nw, w, *, tm, tn):
    m, k = x.shape
    n = w.shape[1]
    return pl.pallas_call(
        _rms_mm_kernel,
        grid=(m // tm, n // tn),
        in_specs=[
            pl.BlockSpec((tm, k), lambda i, j: (i, 0)),
            pl.BlockSpec((1, k), lambda i, j: (0, 0)),
            pl.BlockSpec((k, tn), lambda i, j: (0, j)),
        ],
        out_specs=pl.BlockSpec((tm, tn), lambda i, j: (i, j)),
        out_shape=jax.ShapeDtypeStruct((m, n), F32),
        scratch_shapes=[pltpu.VMEM((tm, k), BF16)],
        compiler_params=pltpu.CompilerParams(
            dimension_semantics=("arbitrary", "arbitrary"), vmem_limit_bytes=V7X_VMEM_LIMIT),
        name="rms_matmul",
    )(x, nw.reshape(1, k), w)


def _mm_res_kernel(a_ref, w_ref, h_ref, o_ref):
    o_ref[...] = h_ref[...] + jnp.dot(a_ref[...], w_ref[...], preferred_element_type=F32)


def matmul_residual(a, w, h, *, tm):
    m, k = a.shape
    n = w.shape[1]
    return pl.pallas_call(
        _mm_res_kernel,
        grid=(m // tm,),
        in_specs=[
            pl.BlockSpec((tm, k), lambda i: (i, 0)),
            pl.BlockSpec((k, n), lambda i: (0, 0)),
            pl.BlockSpec((tm, n), lambda i: (i, 0)),
        ],
        out_specs=pl.BlockSpec((tm, n), lambda i: (i, 0)),
        out_shape=jax.ShapeDtypeStruct((m, n), F32),
        compiler_params=pltpu.CompilerParams(
            dimension_semantics=("arbitrary",), vmem_limit_bytes=V7X_VMEM_LIMIT),
        name="matmul_residual",
    )(a, w, h)


def _silu(x):
    return x * jax.nn.sigmoid(x)


def _ffn_kernel(h_ref, nw_ref, wg_ref, wu_ref, wo_ref, o_ref, xn_ref):
    @pl.when(pl.program_id(1) == 0)
    def _():
        h = h_ref[...]
        xn_ref[...] = _rms_rows(h, nw_ref[...]).astype(BF16)
        o_ref[...] = h

    xn = xn_ref[...]
    gate = jnp.dot(xn, wg_ref[...], preferred_element_type=F32)
    up = jnp.dot(xn, wu_ref[...], preferred_element_type=F32)
    act = (_silu(gate) * up).astype(BF16)
    o_ref[...] += jnp.dot(act, wo_ref[...], preferred_element_type=F32)


def ffn_residual(h, nw, w_in, w_out, *, tm, th):
    m, k = h.shape
    hidden = w_out.shape[0]
    nj = hidden // th
    return pl.pallas_call(
        _ffn_kernel,
        grid=(m // tm, nj),
        in_specs=[
            pl.BlockSpec((tm, k), lambda i, j: (i, 0)),
            pl.BlockSpec((1, k), lambda i, j: (0, 0)),
            pl.BlockSpec((k, th), lambda i, j: (0, j)),
            pl.BlockSpec((k, th), lambda i, j: (0, j + nj)),
            pl.BlockSpec((th, k), lambda i, j: (j, 0)),
        ],
        out_specs=pl.BlockSpec((tm, k), lambda i, j: (i, 0)),
        out_shape=jax.ShapeDtypeStruct((m, k), F32),
        scratch_shapes=[pltpu.VMEM((tm, k), BF16)],
        compiler_params=pltpu.CompilerParams(
            dimension_semantics=("arbitrary", "arbitrary"), vmem_limit_bytes=V7X_VMEM_LIMIT),
        name="ffn_residual",
    )(h, nw.reshape(1, k), w_in, w_in, w_out)


def _ple_kernel(h_ref, nw_ref, wg_ref, p_ref, wp_ref, nf_ref, o_ref, *y_ref):
    h = h_ref[...]
    xn = _rms_rows(h, nw_ref[...]).astype(BF16)
    gate = jax.nn.sigmoid(jnp.dot(xn, wg_ref[...], preferred_element_type=F32))
    proj = jnp.dot(p_ref[...].astype(BF16), wp_ref[...], preferred_element_type=F32)
    out = h + gate * proj
    o_ref[...] = out
    if y_ref:
        y_ref[0][...] = _rms_rows(out, nf_ref[...])


def ple_residual(h, nw, wg, p, wp, nf, *, tm, final):
    m, k = h.shape
    pd = p.shape[1]
    row = pl.BlockSpec((tm, k), lambda i: (i, 0))
    vec = pl.BlockSpec((1, k), lambda i: (0, 0))
    out_shape = [jax.ShapeDtypeStruct((m, k), F32)] * (2 if final else 1)
    return pl.pallas_call(
        _ple_kernel,
        grid=(m // tm,),
        in_specs=[
            row, vec,
            pl.BlockSpec((k, k), lambda i: (0, 0)),
            pl.BlockSpec((tm, pd), lambda i: (i, 0)),
            pl.BlockSpec((pd, k), lambda i: (0, 0)),
            vec,
        ],
        out_specs=[row] * len(out_shape),
        out_shape=out_shape,
        compiler_params=pltpu.CompilerParams(
            dimension_semantics=("arbitrary",), vmem_limit_bytes=V7X_VMEM_LIMIT),
        name="ple_residual",
    )(h, nw.reshape(1, k), wg, p, wp, nf.reshape(1, k))


def split_last(t, sizes):
    return jnp.split(t, [int(s) for s in np.cumsum(sizes)[:-1]], axis=-1)


def l2_normalize(x):
    xf = x.astype(jnp.float32)
    return xf * lax.rsqrt(jnp.sum(xf * xf, axis=-1, keepdims=True) + EPS)


def causal_conv(x, buf, w):
    L = x.shape[1]
    xp = jnp.concatenate([buf.astype(x.dtype), x], axis=1)
    y = xp[:, 0:L] * w[0]
    for j in range(1, CONV_W):
        y = y + xp[:, j:j + L] * w[j]
    return y, xp[:, L:]


def rotary(x, positions):
    half = x.shape[-1] // 2
    inv_freq = ROPE_BASE ** (-jnp.arange(half, dtype=jnp.float32) / half)
    ang = positions.astype(jnp.float32)[:, None] * inv_freq[None, :]
    cos = jnp.cos(ang)[None, :, None, :]
    sin = jnp.sin(ang)[None, :, None, :]
    xf = x.astype(jnp.float32)
    x1, x2 = xf[..., :half], xf[..., half:]
    return jnp.concatenate([x1 * cos - x2 * sin, x1 * sin + x2 * cos], axis=-1)


def to_chunks(t, chunk):
    L = t.shape[1]
    pad = (-L) % chunk
    t = jnp.pad(t, [(0, 0), (0, pad)] + [(0, 0)] * (t.ndim - 2))
    n = t.shape[1] // chunk
    t = t.reshape((t.shape[0], n, chunk) + t.shape[2:])
    return jnp.moveaxis(t, 2, 3)


def from_chunks(t, L):
    t = jnp.moveaxis(t, 3, 2)
    t = t.reshape((t.shape[0], t.shape[1] * t.shape[2]) + t.shape[3:])
    return t[:, :L]


def intra_decay(G):
    C = G.shape[-1]
    causal = jnp.tril(jnp.ones((C, C), dtype=bool))
    diff = G[..., :, None] - G[..., None, :]
    return jnp.where(causal, jnp.exp(jnp.where(causal, diff, 0.0)), 0.0)


def decay_linear_attention(q, k, v, log_a, s0, chunk):
    L = q.shape[1]
    C = min(chunk, L)
    qc, kc, vc = (to_chunks(t.astype(jnp.float32), C) for t in (q, k, v))
    G = jnp.cumsum(to_chunks(log_a.astype(jnp.float32), C), axis=-1)
    G_last = G[..., -1]
    scores = jnp.einsum('bnhid,bnhjd->bnhij', qc, kc) * intra_decay(G)
    intra = jnp.einsum('bnhij,bnhjv->bnhiv', scores, vc)
    chunk_states = jnp.einsum('bnhcd,bnhcv->bnhdv', kc * jnp.exp(G_last[..., None] - G)[..., None], vc)

    def step(S, inp):
        cs, gl = inp
        return S * gl[..., None, None] + cs, S

    s_final, s_prev = lax.scan(step, s0.astype(jnp.float32),
                               (jnp.moveaxis(chunk_states, 1, 0), jnp.moveaxis(jnp.exp(G_last), 1, 0)))
    s_prev = jnp.moveaxis(s_prev, 0, 1)
    inter = jnp.einsum('bnhcd,bnhdv->bnhcv', qc * jnp.exp(G)[..., None], s_prev)
    return from_chunks(intra + inter, L), s_final


def gated_delta_rule(q, k, v, beta, g, s0, chunk):
    L = q.shape[1]
    C = min(chunk, L)
    qc, kc, vc = (to_chunks(t.astype(jnp.float32), C) for t in (q, k, v))
    bc = to_chunks(beta.astype(jnp.float32), C)
    G = jnp.cumsum(to_chunks(g.astype(jnp.float32), C), axis=-1)
    decay = intra_decay(G)
    strict = jnp.tril(jnp.ones((C, C), dtype=bool), -1)
    kk = jnp.einsum('bnhid,bnhjd->bnhij', kc, kc)
    a_mat = jnp.where(strict, bc[..., :, None] * kk * decay, 0.0) + jnp.eye(C, dtype=jnp.float32)
    u = lax.linalg.triangular_solve(a_mat, vc * bc[..., None], left_side=True, lower=True, unit_diagonal=True)
    w = lax.linalg.triangular_solve(a_mat, kc * (bc * jnp.exp(G))[..., None], left_side=True, lower=True,
                                    unit_diagonal=True)
    scores = jnp.einsum('bnhid,bnhjd->bnhij', qc, kc) * decay
    q_dec = qc * jnp.exp(G)[..., None]
    k_dec = kc * jnp.exp(G[..., -1:] - G)[..., None]
    g_last = jnp.exp(G[..., -1])

    def step(S, inp):
        u_c, w_c, s_c, qd_c, kd_c, gl_c = inp
        v_new = u_c - jnp.einsum('bhcd,bhdv->bhcv', w_c, S)
        o = jnp.einsum('bhcd,bhdv->bhcv', qd_c, S) + jnp.einsum('bhij,bhjv->bhiv', s_c, v_new)
        S = S * gl_c[..., None, None] + jnp.einsum('bhcd,bhcv->bhdv', kd_c, v_new)
        return S, o

    xs = tuple(jnp.moveaxis(t, 1, 0) for t in (u, w, scores, q_dec, k_dec, g_last))
    s_final, o = lax.scan(step, s0.astype(jnp.float32), xs)
    return from_chunks(jnp.moveaxis(o, 0, 1), L), s_final


def complex_affine_combine(e1, e2):
    a1r, a1i, b1r, b1i = e1
    a2r, a2i, b2r, b2i = e2
    return (a2r * a1r - a2i * a1i, a2r * a1i + a2i * a1r,
            a2r * b1r - a2i * b1i + b2r, a2r * b1i + a2i * b1r + b2i)


def s5_mixer(u, h_re0, h_im0, lw):
    f32 = jnp.float32
    bsz, L, _ = u.shape
    a_re = lw['s5_a_re'].astype(f32)
    a_im = lw['s5_a_im'].astype(f32)
    step = jnp.exp(lw['s5_log_step'].astype(f32))[:, None]
    mag = jnp.exp(a_re * step)
    lam_re = mag * jnp.cos(a_im * step)
    lam_im = mag * jnp.sin(a_im * step)
    den = a_re * a_re + a_im * a_im
    coef_re = ((lam_re - 1.0) * a_re + lam_im * a_im) / den
    coef_im = (lam_im * a_re - (lam_re - 1.0) * a_im) / den
    b_re = lw['s5_b_re'].astype(f32)
    b_im = lw['s5_b_im'].astype(f32)
    bb_re = coef_re[..., None] * b_re - coef_im[..., None] * b_im
    bb_im = coef_re[..., None] * b_im + coef_im[..., None] * b_re
    ug = u.astype(f32).reshape(bsz, L, S5_GROUPS, S5_GROUP_CH)
    drive_re = jnp.einsum('blgc,gnc->blgn', ug, bb_re)
    drive_im = jnp.einsum('blgc,gnc->blgn', ug, bb_im)
    h0_re = h_re0.astype(f32)
    h0_im = h_im0.astype(f32)
    drive_re = drive_re.at[:, 0].add(lam_re * h0_re - lam_im * h0_im)
    drive_im = drive_im.at[:, 0].add(lam_re * h0_im + lam_im * h0_re)
    lam_re_b = jnp.broadcast_to(lam_re, drive_re.shape)
    lam_im_b = jnp.broadcast_to(lam_im, drive_im.shape)
    _, _, hs_re, hs_im = lax.associative_scan(complex_affine_combine, (lam_re_b, lam_im_b, drive_re, drive_im), axis=1)
    c_re = lw['s5_c_re'].astype(f32)
    c_im = lw['s5_c_im'].astype(f32)
    y = jnp.einsum('blgn,gcn->blgc', hs_re, c_re) - jnp.einsum('blgn,gcn->blgc', hs_im, c_im)
    y = y.reshape(bsz, L, S5_CH) + lw['s5_d'].astype(f32) * u.astype(f32)
    y = jax.nn.gelu(y)
    out = y * jax.nn.sigmoid(y @ lw['s5_w_glu'].astype(f32) + lw['s5_b_glu'].astype(f32))
    return out.astype(u.dtype), hs_re[:, -1], hs_im[:, -1]


def gdn_mixer(q, k, v, z, b_logit, a_logit, buf0, s0, lw):
    dt_out = z.dtype
    bsz, L, _ = q.shape
    qkv, buf = causal_conv(jnp.concatenate([q, k, v], axis=-1), buf0, lw['gdn_conv_w'])
    qkv = jax.nn.silu(qkv)
    qh, kh, vh = split_last(qkv, (GDN_HEADS * GDN_DK, GDN_HEADS * GDN_DK, GDN_HEADS * GDN_DV))
    qh = l2_normalize(qh.reshape(bsz, L, GDN_HEADS, GDN_DK)) * (GDN_DK ** -0.5)
    kh = l2_normalize(kh.reshape(bsz, L, GDN_HEADS, GDN_DK))
    vh = vh.reshape(bsz, L, GDN_HEADS, GDN_DV)
    beta = jax.nn.sigmoid(b_logit.astype(jnp.float32))
    g = -jnp.exp(lw['gdn_a_log'].astype(jnp.float32)) * jax.nn.softplus(
        a_logit.astype(jnp.float32) + lw['gdn_dt_bias'].astype(jnp.float32))
    o, s = gated_delta_rule(qh, kh, vh, beta, g, s0, GDN_CHUNK)
    o = o * lax.rsqrt(jnp.mean(o * o, axis=-1, keepdims=True) + EPS) * lw['gdn_norm_w'].astype(jnp.float32)
    o = o * jax.nn.silu(z.astype(jnp.float32).reshape(bsz, L, GDN_HEADS, GDN_DV))
    return o.reshape(bsz, L, GDN_HEADS * GDN_DV).astype(dt_out), buf, s


def ssd_mixer(z, xbc, dt_raw, buf0, s0, lw):
    f32 = jnp.float32
    bsz, L, _ = z.shape
    xbc, buf = causal_conv(xbc, buf0, lw['ssd_conv_w'])
    xbc = jax.nn.silu(xbc + lw['ssd_conv_b'])
    xs, bm, cm = split_last(xbc, (SSD_INNER, SSD_NGROUPS * SSD_STATE, SSD_NGROUPS * SSD_STATE))
    rep = SSD_HEADS // SSD_NGROUPS
    xs = xs.astype(f32).reshape(bsz, L, SSD_HEADS, SSD_HEADDIM)
    bm = jnp.repeat(bm.astype(f32).reshape(bsz, L, SSD_NGROUPS, SSD_STATE), rep, axis=2)
    cm = jnp.repeat(cm.astype(f32).reshape(bsz, L, SSD_NGROUPS, SSD_STATE), rep, axis=2)
    dt = jax.nn.softplus(dt_raw.astype(f32) + lw['ssd_dt_bias'].astype(f32))
    a = -jnp.exp(lw['ssd_a_log'].astype(f32))
    y, s = decay_linear_attention(cm, bm * dt[..., None], xs, dt * a, s0, SSD_CHUNK)
    y = y + xs * lw['ssd_d'].astype(f32)[:, None]
    y = y.reshape(bsz, L, SSD_INNER) * jax.nn.silu(z.astype(f32))
    y = y.reshape(bsz, L, SSD_NGROUPS, SSD_INNER // SSD_NGROUPS)
    y = y * lax.rsqrt(jnp.mean(y * y, axis=-1, keepdims=True) + EPS)
    y = y.reshape(bsz, L, SSD_INNER) * lw['ssd_norm_w'].astype(f32)
    return y.astype(z.dtype), buf, s


def retention_mixer(q, k, v, gate, s0, positions, lw):
    f32 = jnp.float32
    bsz, L, _ = q.shape
    qh = rotary(q.reshape(bsz, L, RET_HEADS, RET_DK), positions)
    kh = rotary(k.reshape(bsz, L, RET_HEADS, RET_DK), positions) * (RET_DK ** -0.5)
    vh = v.reshape(bsz, L, RET_HEADS, RET_DV)
    log_gamma = jnp.log(1.0 - 2.0 ** (-5.0 - jnp.arange(RET_HEADS, dtype=f32)))
    log_a = jnp.broadcast_to(log_gamma, (bsz, L, RET_HEADS))
    o, s = decay_linear_attention(qh, kh, vh, log_a, s0, RET_CHUNK)
    mu = jnp.mean(o, axis=-1, keepdims=True)
    var = jnp.mean((o - mu) ** 2, axis=-1, keepdims=True)
    o = ((o - mu) * lax.rsqrt(var + EPS)).reshape(bsz, L, RET_HEADS * RET_DV)
    o = o * lw['ret_ln_w'].astype(f32) + lw['ret_ln_b'].astype(f32)
    out = jax.nn.silu(gate.astype(f32)) * o
    return out.astype(q.dtype), s


def _mixers(y, st, lw, positions):
    def col(s):
        return y[..., PK_OFF[s]:PK_OFF[s] + IN_SIZES[s]]

    s5_re0, s5_im0, gdn_s0, gdn_buf0, ssd_s0, ssd_buf0, ret_s0 = st
    out_a, s5_re, s5_im = s5_mixer(col(0), s5_re0, s5_im0, lw)
    out_b, gdn_buf, gdn_s = gdn_mixer(col(1), col(2), col(3), col(4), col(5), col(6), gdn_buf0, gdn_s0, lw)
    out_c, ssd_buf, ssd_s = ssd_mixer(col(7), col(8), col(9), ssd_buf0, ssd_s0, lw)
    out_d, ret_s = retention_mixer(col(10), col(11), col(12), col(13), ret_s0, positions, lw)
    mixed = jnp.concatenate([out_a, out_b, out_c, out_d], axis=-1)
    return mixed, (s5_re, s5_im, gdn_s, gdn_buf, ssd_s, ssd_buf, ret_s)


TM = 1088


def kernel(x_prompt, x_sample, p_prompt, p_sample, state_s5_re, state_s5_im, state_gdn, state_gdn_conv, state_ssd, state_ssd_conv, state_ret, norm_mix, w_in, s5_a_re, s5_a_im, s5_b_re, s5_b_im, s5_c_re, s5_c_im, s5_d, s5_log_step, s5_w_glu, s5_b_glu, gdn_conv_w, gdn_a_log, gdn_dt_bias, gdn_norm_w, ssd_conv_w, ssd_conv_b, ssd_dt_bias, ssd_a_log, ssd_d, ssd_norm_w, ret_ln_w, ret_ln_b, w_out, norm_ffn, w_ffn_in, w_ffn_out, norm_ple, w_ple_gate, w_ple_proj, norm_final):
    bp, lp, d = x_prompt.shape
    bs, ls, _ = x_sample.shape
    np_tok = bp * lp
    ns_tok = bs * ls

    mixer_w = dict(
        s5_a_re=s5_a_re, s5_a_im=s5_a_im, s5_b_re=s5_b_re, s5_b_im=s5_b_im, s5_c_re=s5_c_re, s5_c_im=s5_c_im,
        s5_d=s5_d, s5_log_step=s5_log_step, s5_w_glu=s5_w_glu, s5_b_glu=s5_b_glu,
        gdn_conv_w=gdn_conv_w, gdn_a_log=gdn_a_log, gdn_dt_bias=gdn_dt_bias, gdn_norm_w=gdn_norm_w,
        ssd_conv_w=ssd_conv_w, ssd_conv_b=ssd_conv_b, ssd_dt_bias=ssd_dt_bias, ssd_a_log=ssd_a_log,
        ssd_d=ssd_d, ssd_norm_w=ssd_norm_w, ret_ln_w=ret_ln_w, ret_ln_b=ret_ln_b)

    def zeros(shape):
        return jnp.zeros((bp,) + shape, F32)

    prompt_init = [zeros((S5_GROUPS, S5_STATE)), zeros((S5_GROUPS, S5_STATE)),
                   zeros((GDN_HEADS, GDN_DK, GDN_DV)), zeros((CONV_W - 1, GDN_CONV_DIM)),
                   zeros((SSD_HEADS, SSD_STATE, SSD_HEADDIM)), zeros((CONV_W - 1, SSD_CONV_DIM)),
                   zeros((RET_HEADS, RET_DK, RET_DV))]
    sample_states = [state_s5_re, state_s5_im, state_gdn, state_gdn_conv, state_ssd, state_ssd_conv, state_ret]
    pos_prompt = jnp.arange(lp, dtype=jnp.int32)
    pos_sample = PAST_LEN + jnp.arange(ls, dtype=jnp.int32)

    h = jnp.concatenate([x_prompt.reshape(np_tok, d), x_sample.reshape(ns_tok, d)], axis=0)
    new_p, new_s = [], []
    y_final = None
    for i in range(DEPTH):
        lw = {k: v[i] for k, v in mixer_w.items()}
        y = rms_matmul(h, norm_mix[i], _pack_w_in(w_in[i]), tm=TM, tn=1280)
        mixed_p, st_p = _mixers(y[:np_tok].reshape(bp, lp, PK_TOTAL), prompt_init, lw, pos_prompt)
        mixed_s, st_s = _mixers(y[np_tok:].reshape(bs, ls, PK_TOTAL), [s[i] for s in sample_states], lw, pos_sample)
        new_p.append(st_p)
        new_s.append(st_s)
        mixed = jnp.concatenate([mixed_p.reshape(np_tok, d), mixed_s.reshape(ns_tok, d)], axis=0).astype(BF16)
        h = matmul_residual(mixed, w_out[i].astype(BF16), h, tm=TM)
        h = ffn_residual(h, norm_ffn[i], w_ffn_in[i].astype(BF16), w_ffn_out[i].astype(BF16), tm=TM // 2, th=512)
        p = jnp.concatenate([p_prompt[i].reshape(np_tok, PLE_DIM), p_sample[i].reshape(ns_tok, PLE_DIM)], axis=0)
        final = i == DEPTH - 1
        outs = ple_residual(h, norm_ple[i], w_ple_gate[i].astype(BF16), p, w_ple_proj[i].astype(BF16),
                            norm_final, tm=TM // 2, final=final)
        h = outs[0]
        if final:
            y_final = outs[1]

    y_prompt = y_final[:np_tok].reshape(bp, lp, d)
    y_sample = y_final[np_tok:].reshape(bs, ls, d)
    stack_p = [jnp.stack([st[j] for st in new_p]) for j in range(7)]
    stack_s = [jnp.stack([st[j] for st in new_s]) for j in range(7)]
    return (y_prompt, y_sample, *stack_p, *stack_s)
```

```python
import functools
import math

import jax
import jax.numpy as jnp
import numpy as np
from jax import lax
from jax.experimental import pallas as pl
from jax.experimental.pallas import tpu as pltpu

F32 = jnp.float32
BF16 = jnp.bfloat16

D_MODEL = 2048
DEPTH = 2
GROUP_WIDTH = D_MODEL // 4
CONV_W = 4
EPS = 1e-6
PLE_DIM = 256
FFN_HIDDEN = ((8 * D_MODEL + 3 * 256 - 1) // (3 * 256)) * 256

S5_CH = GROUP_WIDTH
S5_GROUP_CH = 16
S5_GROUPS = S5_CH // S5_GROUP_CH
S5_STATE = 64

GDN_HEADS = 4
GDN_DK = GROUP_WIDTH // GDN_HEADS
GDN_DV = GROUP_WIDTH // GDN_HEADS
GDN_CHUNK = 64
GDN_CONV_DIM = 2 * GDN_HEADS * GDN_DK + GDN_HEADS * GDN_DV

SSD_INNER = GROUP_WIDTH
SSD_HEADDIM = 64
SSD_HEADS = SSD_INNER // SSD_HEADDIM
SSD_NGROUPS = 2
SSD_STATE = 128
SSD_CHUNK = 128
SSD_CONV_DIM = SSD_INNER + 2 * SSD_NGROUPS * SSD_STATE

RET_HEADS = 4
RET_DK = GROUP_WIDTH // RET_HEADS
RET_DV = GROUP_WIDTH // RET_HEADS
RET_CHUNK = 128
ROPE_BASE = 10000.0
PAST_LEN = 16384

IN_SIZES = (
    S5_CH,
    GDN_HEADS * GDN_DK, GDN_HEADS * GDN_DK, GDN_HEADS * GDN_DV, GDN_HEADS * GDN_DV, GDN_HEADS, GDN_HEADS,
    SSD_INNER, SSD_CONV_DIM, SSD_HEADS,
    RET_HEADS * RET_DK, RET_HEADS * RET_DK, RET_HEADS * RET_DV, RET_HEADS * RET_DV,
)
IN_OFFS = tuple(int(v) for v in np.cumsum((0,) + IN_SIZES))

_WIDE = (0, 1, 2, 3, 4, 7, 8, 10, 11, 12, 13)
_NARROW = (5, 6, 9)
PK_OFF = {}
_o = 0
for _s in _WIDE:
    PK_OFF[_s] = _o
    _o += IN_SIZES[_s]
PK_NARROW = _o
for _s in _NARROW:
    PK_OFF[_s] = _o
    _o += IN_SIZES[_s]
PK_USED = _o
PK_TOTAL = 6400

V7X_VMEM_LIMIT = 58 * 1024 * 1024


def _pack_w_in(w_in):
    cols = [w_in[:, IN_OFFS[s]:IN_OFFS[s + 1]] for s in _WIDE + _NARROW]
    cols.append(jnp.zeros((w_in.shape[0], PK_TOTAL - PK_USED), w_in.dtype))
    return jnp.concatenate(cols, axis=1).astype(BF16)


def _rms_rows(x, nw):
    ms = jnp.mean(x * x, axis=-1, keepdims=True)
    return x * lax.rsqrt(ms + EPS) * nw


def _rms_mm_kernel(x_ref, nw_ref, w_ref, o_ref, xn_ref):
    @pl.when(pl.program_id(1) == 0)
    def _():
        xn_ref[...] = _rms_rows(x_ref[...], nw_ref[...]).astype(BF16)

    o_ref[...] = jnp.dot(xn_ref[...], w_ref[...], preferred_element_type=F32)


def rms_matmul(x, nw, w, *, tm, tn):
    m, k = x.shape
    n = w.shape[1]
    return pl.pallas_call(
        _rms_mm_kernel,
        grid=(m // tm, n // tn),
        in_specs=[
            pl.BlockSpec((tm, k), lambda i, j: (i, 0)),
            pl.BlockSpec((1, k), lambda i, j: (0, 0)),
            pl.BlockSpec((k, tn), lambda i, j: (0, j)),
        ],
        out_specs=pl.BlockSpec((tm, tn), lambda i, j: (i, j)),
        out_shape=jax.ShapeDtypeStruct((m, n), F32),
        scratch_shapes=[pltpu.VMEM((tm, k), BF16)],
        compiler_params=pltpu.CompilerParams(
            dimension_semantics=("arbitrary", "arbitrary"), vmem_limit_bytes=V7X_VMEM_LIMIT),
        name="rms_matmul",
    )(x, nw.reshape(1, k), w)


def _mm_res_kernel(a_ref, w_ref, h_ref, o_ref):
    o_ref[...] = h_ref[...] + jnp.dot(a_ref[...], w_ref[...], preferred_element_type=F32)


def matmul_residual(a, w, h, *, tm):
    m, k = a.shape
    n = w.shape[1]
    return pl.pallas_call(
        _mm_res_kernel,
        grid=(m // tm,),
        in_specs=[
            pl.BlockSpec((tm, k), lambda i: (i, 0)),
            pl.BlockSpec((k, n), lambda i: (0, 0)),
            pl.BlockSpec((tm, n), lambda i: (i, 0)),
        ],
        out_specs=pl.BlockSpec((tm, n), lambda i: (i, 0)),
        out_shape=jax.ShapeDtypeStruct((m, n), F32),
        compiler_params=pltpu.CompilerParams(
            dimension_semantics=("arbitrary",), vmem_limit_bytes=V7X_VMEM_LIMIT),
        name="matmul_residual",
    )(a, w, h)


def _silu(x):
    return x * jax.nn.sigmoid(x)


def _ffn_kernel(h_ref, nw_ref, wg_ref, wu_ref, wo_ref, o_ref, xn_ref):
    @pl.when(pl.program_id(1) == 0)
    def _():
        h = h_ref[...]
        xn_ref[...] = _rms_rows(h, nw_ref[...]).astype(BF16)
        o_ref[...] = h

    xn = xn_ref[...]
    gate = jnp.dot(xn, wg_ref[...], preferred_element_type=F32)
    up = jnp.dot(xn, wu_ref[...], preferred_element_type=F32)
    act = (_silu(gate) * up).astype(BF16)
    o_ref[...] += jnp.dot(act, wo_ref[...], preferred_element_type=F32)


def ffn_residual(h, nw, w_in, w_out, *, tm, th):
    m, k = h.shape
    hidden = w_out.shape[0]
    nj = hidden // th
    return pl.pallas_call(
        _ffn_kernel,
        grid=(m // tm, nj),
        in_specs=[
            pl.BlockSpec((tm, k), lambda i, j: (i, 0)),
            pl.BlockSpec((1, k), lambda i, j: (0, 0)),
            pl.BlockSpec((k, th), lambda i, j: (0, j)),
            pl.BlockSpec((k, th), lambda i, j: (0, j + nj)),
            pl.BlockSpec((th, k), lambda i, j: (j, 0)),
        ],
        out_specs=pl.BlockSpec((tm, k), lambda i, j: (i, 0)),
        out_shape=jax.ShapeDtypeStruct((m, k), F32),
        scratch_shapes=[pltpu.VMEM((tm, k), BF16)],
        compiler_params=pltpu.CompilerParams(
            dimension_semantics=("arbitrary", "arbitrary"), vmem_limit_bytes=V7X_VMEM_LIMIT),
        name="ffn_residual",
    )(h, nw.reshape(1, k), w_in, w_in, w_out)


def _ple_kernel(h_ref, nw_ref, wg_ref, p_ref, wp_ref, nf_ref, o_ref, *y_ref):
    h = h_ref[...]
    xn = _rms_rows(h, nw_ref[...]).astype(BF16)
    gate = jax.nn.sigmoid(jnp.dot(xn, wg_ref[...], preferred_element_type=F32))
    proj = jnp.dot(p_ref[...].astype(BF16), wp_ref[...], preferred_element_type=F32)
    out = h + gate * proj
    o_ref[...] = out
    if y_ref:
        y_ref[0][...] = _rms_rows(out, nf_ref[...])


def ple_residual(h, nw, wg, p, wp, nf, *, tm, final):
    m, k = h.shape
    pd = p.shape[1]
    row = pl.BlockSpec((tm, k), lambda i: (i, 0))
    vec = pl.BlockSpec((1, k), lambda i: (0, 0))
    out_shape = [jax.ShapeDtypeStruct((m, k), F32)] * (2 if final else 1)
    return pl.pallas_call(
        _ple_kernel,
        grid=(m // tm,),
        in_specs=[
            row, vec,
            pl.BlockSpec((k, k), lambda i: (0, 0)),
            pl.BlockSpec((tm, pd), lambda i: (i, 0)),
            pl.BlockSpec((pd, k), lambda i: (0, 0)),
            vec,
        ],
        out_specs=[row] * len(out_shape),
        out_shape=out_shape,
        compiler_params=pltpu.CompilerParams(
            dimension_semantics=("arbitrary",), vmem_limit_bytes=V7X_VMEM_LIMIT),
        name="ple_residual",
    )(h, nw.reshape(1, k), wg, p, wp, nf.reshape(1, k))


S5_HALF_CH = S5_CH // 2
S5_HALF_ST = (S5_GROUPS // 2) * S5_STATE
S5_LANES = 4 * S5_HALF_ST
S5_SLAB = 512


def _s5_tables(lw):
    a_re = lw['s5_a_re'].astype(F32)
    a_im = lw['s5_a_im'].astype(F32)
    step = jnp.exp(lw['s5_log_step'].astype(F32))[:, None]
    mag = jnp.exp(a_re * step)
    lam_re = mag * jnp.cos(a_im * step)
    lam_im = mag * jnp.sin(a_im * step)
    den = a_re * a_re + a_im * a_im
    coef_re = ((lam_re - 1.0) * a_re + lam_im * a_im) / den
    coef_im = (lam_im * a_re - (lam_re - 1.0) * a_im) / den
    b_re = lw['s5_b_re'].astype(F32)
    b_im = lw['s5_b_im'].astype(F32)
    bb_re = coef_re[..., None] * b_re - coef_im[..., None] * b_im
    bb_im = coef_re[..., None] * b_im + coef_im[..., None] * b_re
    gh = S5_GROUPS // 2
    eye = jnp.eye(gh, dtype=F32)

    def in_blockdiag(b):
        return jnp.einsum('gnc,gh->gchn', b, eye).reshape(gh * S5_GROUP_CH, gh * S5_STATE)

    def out_blockdiag(c):
        return jnp.einsum('gcn,gh->gnhc', c, eye).reshape(gh * S5_STATE, gh * S5_GROUP_CH)

    c_re = lw['s5_c_re'].astype(F32)
    c_im = lw['s5_c_im'].astype(F32)
    bb = jnp.stack([jnp.concatenate([in_blockdiag(bb_re[h * gh:(h + 1) * gh]),
                                     in_blockdiag(bb_im[h * gh:(h + 1) * gh])], axis=1) for h in range(2)])
    cm = jnp.stack([jnp.concatenate([out_blockdiag(c_re[h * gh:(h + 1) * gh]),
                                     -out_blockdiag(c_im[h * gh:(h + 1) * gh])], axis=0) for h in range(2)])
    lam = jnp.stack([lam_re.reshape(-1), lam_im.reshape(-1)])
    lam2 = jnp.stack([lam[0] * lam[0] - lam[1] * lam[1], 2.0 * lam[0] * lam[1]])
    return dict(bb=bb.astype(BF16), cm=cm.astype(BF16), lam=lam, lam2=lam2,
                d=lw['s5_d'].astype(F32).reshape(1, S5_CH), wglu=lw['s5_w_glu'].astype(BF16),
                bglu=lw['s5_b_glu'].astype(F32).reshape(1, S5_CH))


def _s5_drive(u, bb_ref, sc_ref):
    ub = u.astype(BF16)
    for hf in range(2):
        sc_ref[:, hf * 2 * S5_HALF_ST:(hf + 1) * 2 * S5_HALF_ST] = jnp.dot(
            ub[:, hf * S5_HALF_CH:(hf + 1) * S5_HALF_CH], bb_ref[hf], preferred_element_type=F32)


def _s5_readout(sc_ref, u, cm_ref, d_ref, wglu_ref, bglu_ref):
    ys = [jnp.dot(sc_ref[:, hf * 2 * S5_HALF_ST:(hf + 1) * 2 * S5_HALF_ST].astype(BF16), cm_ref[hf],
                  preferred_element_type=F32) for hf in range(2)]
    y = jnp.concatenate(ys, axis=1) + d_ref[...] * u
    y = jax.nn.gelu(y)
    z = jnp.dot(y.astype(BF16), wglu_ref[...], preferred_element_type=F32) + bglu_ref[...]
    return y * jax.nn.sigmoid(z)


def _s5_slabs():
    for hf in range(2):
        for sl in range(S5_HALF_ST // S5_SLAB):
            re0 = hf * 2 * S5_HALF_ST + sl * S5_SLAB
            yield re0, re0 + S5_HALF_ST, hf * S5_HALF_ST + sl * S5_SLAB


def _s5_prompt_kernel(u_ref, bb_ref, m_ref, cm_ref, d_ref, wglu_ref, bglu_ref, o_ref, st_ref, sc_ref, carry_ref):
    @pl.when(pl.program_id(0) == 0)
    def _():
        carry_ref[...] = jnp.zeros_like(carry_ref)

    u = u_ref[...]
    _s5_drive(u, bb_ref, sc_ref)
    first_step = lax.broadcasted_iota(jnp.int32, (8, S5_SLAB), 0) < 4
    n_pairs = u_ref.shape[0] // 8
    for re0, im0, l0 in _s5_slabs():
        mr = m_ref[0, :, l0:l0 + S5_SLAB]
        mi = m_ref[1, :, l0:l0 + S5_SLAB]
        nr = m_ref[2, :, l0:l0 + S5_SLAB]
        ni = m_ref[3, :, l0:l0 + S5_SLAB]

        def body(k, carry, re0=re0, im0=im0, mr=mr, mi=mi, nr=nr, ni=ni):
            hr, hi = carry
            base = pl.multiple_of(k * 8, 8)
            xr = sc_ref[pl.ds(base, 8), re0:re0 + S5_SLAB]
            xi = sc_ref[pl.ds(base, 8), im0:im0 + S5_SLAB]
            xr_s = pltpu.roll(xr, 4, 0)
            xi_s = pltpu.roll(xi, 4, 0)
            outr = (mr * hr - mi * hi) + xr + (nr * xr_s - ni * xi_s)
            outi = (mr * hi + mi * hr) + xi + (nr * xi_s + ni * xr_s)
            sc_ref[pl.ds(base, 8), re0:re0 + S5_SLAB] = outr
            sc_ref[pl.ds(base, 8), im0:im0 + S5_SLAB] = outi
            return (jnp.where(first_step, pltpu.roll(outr, 4, 0), outr),
                    jnp.where(first_step, pltpu.roll(outi, 4, 0), outi))

        hr, hi = lax.fori_loop(0, n_pairs, body,
                               (carry_ref[:, re0:re0 + S5_SLAB], carry_ref[:, im0:im0 + S5_SLAB]))
        carry_ref[:, re0:re0 + S5_SLAB] = hr
        carry_ref[:, im0:im0 + S5_SLAB] = hi

    o_ref[...] = _s5_readout(sc_ref, u, cm_ref, d_ref, wglu_ref, bglu_ref).astype(o_ref.dtype)
    st_ref[...] = carry_ref[...]


def _const_spec(shape):
    return pl.BlockSpec(shape, lambda c: (0,) * len(shape))


def s5_prompt(u_tm, tb, *, rows):
    n = u_tm.shape[0]
    zero = jnp.zeros_like(tb['lam'])
    m = jnp.stack([jnp.concatenate([jnp.broadcast_to(a[k][None], (4, a.shape[1])),
                                    jnp.broadcast_to(b[k][None], (4, b.shape[1]))], axis=0)
                   for a, b, k in ((tb['lam'], tb['lam2'], 0), (tb['lam'], tb['lam2'], 1),
                                   (zero, tb['lam'], 0), (zero, tb['lam'], 1))])
    return pl.pallas_call(
        _s5_prompt_kernel,
        grid=(n // rows,),
        in_specs=[
            pl.BlockSpec((rows, S5_CH), lambda c: (c, 0)),
            _const_spec(tb['bb'].shape), _const_spec(m.shape), _const_spec(tb['cm'].shape),
            _const_spec((1, S5_CH)), _const_spec((S5_CH, S5_CH)), _const_spec((1, S5_CH)),
        ],
        out_specs=[pl.BlockSpec((rows, S5_CH), lambda c: (c, 0)), _const_spec((8, S5_LANES))],
        out_shape=[jax.ShapeDtypeStruct((n, S5_CH), BF16), jax.ShapeDtypeStruct((8, S5_LANES), F32)],
        scratch_shapes=[pltpu.VMEM((rows, S5_LANES), F32), pltpu.VMEM((8, S5_LANES), F32)],
        compiler_params=pltpu.CompilerParams(
            dimension_semantics=("arbitrary",), vmem_limit_bytes=V7X_VMEM_LIMIT),
        name="s5_prompt",
    )(u_tm, tb['bb'], m, tb['cm'], tb['d'], tb['wglu'], tb['bglu'])


def _s5_sample_kernel(u_ref, h0_ref, bb_ref, lam_ref, cm_ref, d_ref, wglu_ref, bglu_ref, o_ref, st_ref, sc_ref):
    u = u_ref[...]
    _s5_drive(u, bb_ref, sc_ref)
    n_seq = h0_ref.shape[0]
    n_steps = u_ref.shape[0] // n_seq
    for re0, im0, l0 in _s5_slabs():
        lr = lam_ref[0:1, l0:l0 + S5_SLAB]
        li = lam_ref[1:2, l0:l0 + S5_SLAB]

        def body(rb, _, re0=re0, im0=im0, lr=lr, li=li):
            r0 = pl.multiple_of(rb * 8, 8)
            hr = h0_ref[pl.ds(r0, 8), re0:re0 + S5_SLAB]
            hi = h0_ref[pl.ds(r0, 8), im0:im0 + S5_SLAB]
            for t in range(n_steps):
                rows = pl.ds(t * n_seq + r0, 8)
                nr = (lr * hr - li * hi) + sc_ref[rows, re0:re0 + S5_SLAB]
                ni = (lr * hi + li * hr) + sc_ref[rows, im0:im0 + S5_SLAB]
                sc_ref[rows, re0:re0 + S5_SLAB] = nr
                sc_ref[rows, im0:im0 + S5_SLAB] = ni
                hr, hi = nr, ni
            st_ref[pl.ds(r0, 8), re0:re0 + S5_SLAB] = hr
            st_ref[pl.ds(r0, 8), im0:im0 + S5_SLAB] = hi
            return 0

        lax.fori_loop(0, n_seq // 8, body, 0)

    o_ref[...] = _s5_readout(sc_ref, u, cm_ref, d_ref, wglu_ref, bglu_ref).astype(o_ref.dtype)


def s5_sample(u_tm, h0, tb):
    n = u_tm.shape[0]
    nb = h0.shape[0]
    return pl.pallas_call(
        _s5_sample_kernel,
        grid=(1,),
        in_specs=[_const_spec((n, S5_CH)), _const_spec((nb, S5_LANES)), _const_spec(tb['bb'].shape),
                  _const_spec(tb['lam'].shape), _const_spec(tb['cm'].shape), _const_spec((1, S5_CH)),
                  _const_spec((S5_CH, S5_CH)), _const_spec((1, S5_CH))],
        out_specs=[_const_spec((n, S5_CH)), _const_spec((nb, S5_LANES))],
        out_shape=[jax.ShapeDtypeStruct((n, S5_CH), BF16), jax.ShapeDtypeStruct((nb, S5_LANES), F32)],
        scratch_shapes=[pltpu.VMEM((n, S5_LANES), F32)],
        compiler_params=pltpu.CompilerParams(
            dimension_semantics=("arbitrary",), vmem_limit_bytes=V7X_VMEM_LIMIT),
        name="s5_sample",
    )(u_tm, h0, tb['bb'], tb['lam'], tb['cm'], tb['d'], tb['wglu'], tb['bglu'])


def _s5_state_to_lanes(re, im):
    b = re.shape[0]
    return jnp.stack([re.reshape(b, 2, S5_HALF_ST), im.reshape(b, 2, S5_HALF_ST)], axis=2).reshape(b, S5_LANES)


def _s5_lanes_to_state(st):
    b = st.shape[0]
    st = st.reshape(b, 2, 2, S5_HALF_ST)
    return st[:, :, 0].reshape(b, S5_GROUPS, S5_STATE), st[:, :, 1].reshape(b, S5_GROUPS, S5_STATE)


def _ret_tables(positions, chunk):
    half = RET_DK // 2
    inv_freq = ROPE_BASE ** (-jnp.arange(half, dtype=F32) / half)
    ang = positions.astype(F32)[:, None] * inv_freq[None, :]
    cos = jnp.cos(ang)
    sin = jnp.sin(ang)
    cos2 = jnp.concatenate([cos, cos], axis=1)
    sin2 = jnp.concatenate([-sin, sin], axis=1)
    log_gamma = jnp.log(1.0 - 2.0 ** (-5.0 - jnp.arange(RET_HEADS, dtype=F32)))
    g = (jnp.arange(chunk, dtype=F32) + 1.0)[None, :] * log_gamma[:, None]
    diff = g[:, :, None] - g[:, None, :]
    causal = jnp.tril(jnp.ones((chunk, chunk), dtype=bool))
    dmat = jnp.where(causal, jnp.exp(jnp.where(causal, diff, 0.0)), 0.0)
    lanes = (RET_HEADS, chunk, RET_DK)
    qdec = jnp.broadcast_to(jnp.exp(g)[:, :, None], lanes)
    kdec = jnp.broadcast_to(jnp.exp(g[:, -1:] - g)[:, :, None], lanes)
    gall = jnp.broadcast_to(jnp.exp(g[:, -1])[:, None, None], (RET_HEADS, 1, RET_DV))
    return cos2, sin2, dmat, qdec, kdec, gall


def _ret_rotate(x, cos2, sin2):
    return x * cos2 + pltpu.roll(x, RET_DK // 2, 1) * sin2


def _group_layernorm_gate(o, gate, w, b):
    mu = jnp.mean(o, axis=-1, keepdims=True)
    xc = o - mu
    var = jnp.mean(xc * xc, axis=-1, keepdims=True)
    return _silu(gate) * (xc * lax.rsqrt(var + EPS) * w + b)


def _dot_nt(a, b):
    return lax.dot_general(a, b, (((1,), (1,)), ((), ())), preferred_element_type=F32)


def _dot_tn(a, b):
    return lax.dot_general(a, b, (((0,), (0,)), ((), ())), preferred_element_type=F32)


def _ret_prompt_kernel(q_ref, k_ref, v_ref, g_ref, cos_ref, sin_ref, dm_ref, qd_ref, kd_ref, ga_ref,
                       lnw_ref, lnb_ref, o_ref, s_ref, *, chunk):
    dm, qd, kd, ga = dm_ref[0], qd_ref[0], kd_ref[0], ga_ref[0]
    lnw, lnb = lnw_ref[0], lnb_ref[0]

    def body(c, s):
        r = pl.ds(pl.multiple_of(c * chunk, chunk), chunk)
        cos2, sin2 = cos_ref[r, :], sin_ref[r, :]
        q = _ret_rotate(q_ref[r, :], cos2, sin2)
        k = _ret_rotate(k_ref[r, :], cos2, sin2) * (RET_DK ** -0.5)
        vb = v_ref[r, :].astype(BF16)
        scores = _dot_nt(q.astype(BF16), k.astype(BF16)) * dm
        o = jnp.dot(scores.astype(BF16), vb, preferred_element_type=F32)
        o = o + jnp.dot((q * qd).astype(BF16), s.astype(BF16), preferred_element_type=F32)
        o_ref[r, :] = _group_layernorm_gate(o, g_ref[r, :], lnw, lnb).astype(o_ref.dtype)
        return s * ga + _dot_tn((k * kd).astype(BF16), vb)

    s_ref[0, 0] = lax.fori_loop(0, q_ref.shape[0] // chunk, body, jnp.zeros((RET_DK, RET_DV), F32))


def ret_prompt(y, n_seq, seq_len, lw):
    chunk = min(RET_CHUNK, seq_len)
    cos2, sin2, dmat, qdec, kdec, gall = _ret_tables(jnp.arange(seq_len, dtype=jnp.int32), chunk)
    col = lambda s: (lambda b, h: (b, PK_OFF[s] // RET_DK + h))
    head = lambda shape: pl.BlockSpec((1,) + shape, lambda b, h: (h, 0, 0))
    tok = lambda s: pl.BlockSpec((seq_len, RET_DK), col(s))
    tab = pl.BlockSpec((seq_len, RET_DK), lambda b, h: (0, 0))
    return pl.pallas_call(
        functools.partial(_ret_prompt_kernel, chunk=chunk),
        grid=(n_seq, RET_HEADS),
        in_specs=[tok(10), tok(11), tok(12), tok(13), tab, tab,
                  head((chunk, chunk)), head((chunk, RET_DK)), head((chunk, RET_DK)), head((1, RET_DV)),
                  head((1, RET_DV)), head((1, RET_DV))],
        out_specs=[pl.BlockSpec((seq_len, RET_DV), lambda b, h: (b, h)),
                   pl.BlockSpec((1, 1, RET_DK, RET_DV), lambda b, h: (b, h, 0, 0))],
        out_shape=[jax.ShapeDtypeStruct((n_seq * seq_len, RET_HEADS * RET_DV), BF16),
                   jax.ShapeDtypeStruct((n_seq, RET_HEADS, RET_DK, RET_DV), F32)],
        compiler_params=pltpu.CompilerParams(
            dimension_semantics=("arbitrary", "arbitrary"), vmem_limit_bytes=V7X_VMEM_LIMIT),
        name="ret_prompt",
    )(y, y, y, y, cos2, sin2, dmat, qdec, kdec, gall,
      lw['ret_ln_w'].astype(F32).reshape(RET_HEADS, 1, RET_DV), lw['ret_ln_b'].astype(F32).reshape(RET_HEADS, 1, RET_DV))


NARROW_B = 0
NARROW_A = GDN_HEADS
NARROW_DT = 2 * GDN_HEADS
TAIL = 8


def _shift_rows(x, tail, s):
    xr = pltpu.roll(x, s, 0)
    tr = pltpu.roll(tail, s, 0)
    row = lax.broadcasted_iota(jnp.int32, tail.shape, 0)
    return jnp.concatenate([jnp.where(row < s, tr, xr[0:TAIL]), xr[TAIL:]], axis=0)


def _causal_conv(x, tail, w_ref):
    y = x * w_ref[CONV_W - 1:CONV_W, :]
    for s in range(1, CONV_W):
        y = y + _shift_rows(x, tail, s) * w_ref[CONV_W - 1 - s:CONV_W - s, :]
    return y


def _softplus(x):
    return jnp.maximum(x, 0.0) + jnp.log1p(jnp.exp(-jnp.abs(x)))


def _lane_row(vals, lane0, width=128):
    return jnp.zeros((1, width), F32).at[0, lane0:lane0 + vals.shape[0]].set(vals.astype(F32))


SSD_PAIRS = SSD_HEADS // 2
SSD_BC = SSD_NGROUPS * SSD_STATE


def _ssd_prompt_kernel(z_ref, xbc_ref, nar_ref, cw_ref, cb_ref, dtb_ref, a_ref, dsk_ref, nw_ref, tri_ref,
                       o_ref, s_ref, tail_ref, *, chunk):
    s_ref[...] = jnp.zeros_like(s_ref)
    tail_ref[...] = jnp.zeros_like(tail_ref)
    causal = (lax.broadcasted_iota(jnp.int32, (chunk, chunk), 0)
              >= lax.broadcasted_iota(jnp.int32, (chunk, chunk), 1))
    lane = lax.broadcasted_iota(jnp.int32, (chunk, 2 * SSD_HEADDIM), 1)
    first_head = lane < SSD_HEADDIM
    rep = SSD_HEADS // SSD_NGROUPS

    def body(c, carry):
        r = pl.ds(pl.multiple_of(c * chunk, chunk), chunk)
        raw = xbc_ref[r, :]
        xbc = _silu(_causal_conv(raw, tail_ref[0], cw_ref) + cb_ref[...])
        tail_ref[0] = raw[chunk - TAIL:, :]
        xs = xbc[:, :SSD_INNER]
        dt = _softplus(nar_ref[r, :] + dtb_ref[...])
        g = jnp.dot(tri_ref[...], dt * a_ref[...], preferred_element_type=F32,
                    precision=lax.Precision.HIGHEST)
        e_in = jnp.exp(g)
        e_out = dt * jnp.exp(g[chunk - 1:chunk, :] - g)
        e_all = jnp.exp(g[chunk - 1:chunk, :])
        g_t = g.T
        dt_t = dt.T
        ys = []
        for p in range(SSD_PAIRS):
            grp = (2 * p) // rep
            bm = xbc[:, SSD_INNER + grp * SSD_STATE:SSD_INNER + (grp + 1) * SSD_STATE]
            cm = xbc[:, SSD_INNER + SSD_BC + grp * SSD_STATE:SSD_INNER + SSD_BC + (grp + 1) * SSD_STATE]
            cb = _dot_nt(cm.astype(BF16), bm.astype(BF16))
            xp = xs[:, p * 128:(p + 1) * 128]
            xpb = xp.astype(BF16)
            sp = s_ref[0, p]
            spb = sp.astype(BF16)
            outs, upds, gls = [], [], []
            for hh in range(2):
                ln = NARROW_DT + 2 * p + hh
                diff = g[:, ln:ln + 1] - g_t[ln:ln + 1, :]
                m = jnp.where(causal, cb * jnp.exp(jnp.where(causal, diff, 0.0)) * dt_t[ln:ln + 1, :], 0.0)
                o = jnp.dot(m.astype(BF16), xpb, preferred_element_type=F32)
                o = o + jnp.dot((cm * e_in[:, ln:ln + 1]).astype(BF16), spb, preferred_element_type=F32)
                outs.append(o)
                upds.append(_dot_tn((bm * e_out[:, ln:ln + 1]).astype(BF16), xpb))
                gls.append(e_all[:, ln:ln + 1])
            s_ref[0, p] = sp * jnp.where(first_head, gls[0], gls[1]) + jnp.where(first_head, upds[0], upds[1])
            ys.append(jnp.where(first_head, outs[0], outs[1]) + xp * dsk_ref[:, p * 128:(p + 1) * 128])
        y = jnp.concatenate(ys, axis=1) * _silu(z_ref[r, :])
        gw = SSD_INNER // SSD_NGROUPS
        yn = [y[:, i * gw:(i + 1) * gw] * lax.rsqrt(
            jnp.mean(y[:, i * gw:(i + 1) * gw] * y[:, i * gw:(i + 1) * gw], axis=-1, keepdims=True) + EPS)
            for i in range(SSD_NGROUPS)]
        o_ref[r, :] = (jnp.concatenate(yn, axis=1) * nw_ref[...]).astype(o_ref.dtype)
        return carry

    lax.fori_loop(0, z_ref.shape[0] // chunk, body, 0)


def _ssd_params(lw, chunk):
    return (lw['ssd_conv_w'].astype(F32), lw['ssd_conv_b'].astype(F32).reshape(1, SSD_CONV_DIM),
            _lane_row(lw['ssd_dt_bias'], NARROW_DT), _lane_row(-jnp.exp(lw['ssd_a_log'].astype(F32)), NARROW_DT),
            jnp.repeat(lw['ssd_d'].astype(F32), SSD_HEADDIM).reshape(1, SSD_INNER),
            lw['ssd_norm_w'].astype(F32).reshape(1, SSD_INNER),
            jnp.tril(jnp.ones((chunk, chunk), F32)))


def _ssd_state_from_pairs(s):
    b = s.shape[0]
    s = s.reshape(b, SSD_PAIRS, SSD_STATE, 2, SSD_HEADDIM)
    return jnp.swapaxes(s, 2, 3).reshape(b, SSD_HEADS, SSD_STATE, SSD_HEADDIM)


def ssd_prompt(y, n_seq, seq_len, lw):
    chunk = min(SSD_CHUNK, seq_len)
    params = _ssd_params(lw, chunk)
    tok = lambda s, w: pl.BlockSpec((seq_len, w), lambda b: (b, PK_OFF[s] // w))
    out, s, tail = pl.pallas_call(
        functools.partial(_ssd_prompt_kernel, chunk=chunk),
        grid=(n_seq,),
        in_specs=[tok(7, SSD_INNER), tok(8, SSD_CONV_DIM),
                  pl.BlockSpec((seq_len, 128), lambda b: (b, PK_NARROW // 128))]
                 + [_const_spec(p.shape) for p in params],
        out_specs=[pl.BlockSpec((seq_len, SSD_INNER), lambda b: (b, 0)),
                   pl.BlockSpec((1, SSD_PAIRS, SSD_STATE, 2 * SSD_HEADDIM), lambda b: (b, 0, 0, 0)),
                   pl.BlockSpec((1, TAIL, SSD_CONV_DIM), lambda b: (b, 0, 0))],
        out_shape=[jax.ShapeDtypeStruct((n_seq * seq_len, SSD_INNER), BF16),
                   jax.ShapeDtypeStruct((n_seq, SSD_PAIRS, SSD_STATE, 2 * SSD_HEADDIM), F32),
                   jax.ShapeDtypeStruct((n_seq, TAIL, SSD_CONV_DIM), F32)],
        compiler_params=pltpu.CompilerParams(
            dimension_semantics=("arbitrary",), vmem_limit_bytes=V7X_VMEM_LIMIT),
        name="ssd_prompt",
    )(y, y, y, *params)
    return out, tail[:, TAIL - (CONV_W - 1):], _ssd_state_from_pairs(s)


GDN_QKV = GDN_HEADS * GDN_DK
HIGHEST = lax.Precision.HIGHEST


def _split_bf16(x):
    hi = x.astype(BF16)
    return hi, (x - hi.astype(F32)).astype(BF16)


def _dot3(a, b):
    a_hi, a_lo = _split_bf16(a)
    b_hi, b_lo = _split_bf16(b)
    m = a.shape[0]
    p = jnp.dot(jnp.concatenate([a_hi, a_lo], axis=0), b_hi, preferred_element_type=F32)
    return p[:m] + p[m:] + jnp.dot(a_hi, b_lo, preferred_element_type=F32)


def _two_block_diag(x0, x1):
    z0 = jnp.zeros_like(x0)
    z1 = jnp.zeros_like(x1)
    return jnp.concatenate([jnp.concatenate([x0, z1], axis=1), jnp.concatenate([z0, x1], axis=1)], axis=0)


def _l2_rows(x):
    return x * lax.rsqrt(jnp.sum(x * x, axis=-1, keepdims=True) + EPS)


def _gdn_prompt_kernel(q_ref, k_ref, v_ref, z_ref, nar_ref, cw_ref, dtb_ref, a_ref, nw_ref, tri_ref, sel_ref,
                       o_ref, s_ref, tail_ref, *, chunk):
    s_ref[...] = jnp.zeros_like(s_ref)
    tail_ref[...] = jnp.zeros_like(tail_ref)
    cat = GDN_HEADS * chunk
    lane = lax.broadcasted_iota(jnp.int32, (chunk, cat), 1)
    row = lax.broadcasted_iota(jnp.int32, (chunk, cat), 0)
    col = jnp.bitwise_and(lane, chunk - 1)
    causal = row >= col
    strict = row > col
    eye_cat = jnp.where(row == col, 1.0, 0.0).astype(F32)
    head_mask = [(lane >= h * chunk) & (lane < (h + 1) * chunk) for h in range(GDN_HEADS)]
    nar_lane = lax.broadcasted_iota(jnp.int32, (chunk, 128), 1)
    ones_cc = jnp.ones((chunk, chunk), F32)

    def mm_cat(l_cat, r_cat):
        bd = jnp.concatenate([jnp.where(m, r_cat, 0.0) for m in head_mask], axis=0)
        return _dot3(l_cat, bd)

    def body(c, carry):
        r = pl.ds(pl.multiple_of(c * chunk, chunk), chunk)
        qkv = []
        for i, ref in enumerate((q_ref, k_ref, v_ref)):
            cols = slice(i * GDN_QKV, (i + 1) * GDN_QKV)
            raw = ref[r, :]
            qkv.append(_silu(_causal_conv(raw, tail_ref[0, :, cols], cw_ref.at[:, cols])))
            tail_ref[0, :, cols] = raw[chunk - TAIL:, :]
        q, k, v = qkv
        nar = nar_ref[r, :]
        beta = jax.nn.sigmoid(nar)
        g = jnp.dot(tri_ref[...], a_ref[...] * _softplus(nar + dtb_ref[...]),
                    preferred_element_type=F32, precision=HIGHEST)
        bg = jnp.dot(jnp.where(nar_lane < NARROW_A, beta, g), sel_ref[...],
                     preferred_element_type=F32, precision=HIGHEST)
        b_c, g_c = bg[:, :cat], bg[:, cat:]
        g_r = jnp.dot(ones_cc, g_c * eye_cat, preferred_element_type=F32, precision=HIGHEST)
        decay = jnp.where(causal, jnp.exp(jnp.where(causal, g_c - g_r, 0.0)), 0.0)
        e_in = jnp.exp(g_c)
        e_out = jnp.exp(g_c[chunk - 1:chunk, :] - g_c)
        e_all = jnp.exp(g_c[chunk - 1:chunk, :])

        qn = [_l2_rows(q[:, h * GDN_DK:(h + 1) * GDN_DK]) * (GDN_DK ** -0.5) for h in range(GDN_HEADS)]
        kn = [_l2_rows(k[:, h * GDN_DK:(h + 1) * GDN_DK]) for h in range(GDN_HEADS)]
        vh = [v[:, h * GDN_DV:(h + 1) * GDN_DV] for h in range(GDN_HEADS)]
        kk, qk = [], []
        for p in range(GDN_HEADS // 2):
            h0, h1 = 2 * p, 2 * p + 1
            rhs = _two_block_diag(kn[h0], kn[h1]).astype(BF16)
            kk.append(_dot_nt(jnp.concatenate([kn[h0], kn[h1]], axis=1).astype(BF16), rhs))
            qk.append(_dot_nt(jnp.concatenate([qn[h0], qn[h1]], axis=1).astype(BF16), rhs))
        kk = jnp.concatenate(kk, axis=1)
        scores = jnp.concatenate(qk, axis=1) * decay
        a_cat = jnp.where(strict, b_c * kk * decay, 0.0)

        t_cat = eye_cat - a_cat
        pw = mm_cat(a_cat, a_cat)
        n_sq = chunk.bit_length() - 2
        for i in range(n_sq):
            t_cat = t_cat + mm_cat(t_cat, pw)
            if i + 1 < n_sq:
                pw = mm_cat(pw, pw)

        for p in range(GDN_HEADS // 2):
            heads = (2 * p, 2 * p + 1)
            col1 = lambda x, h: x[:, h * chunk:h * chunk + 1]
            rhs = _two_block_diag(*[jnp.concatenate(
                [vh[h] * col1(b_c, h), kn[h] * (col1(b_c, h) * col1(e_in, h))], axis=1) for h in heads])
            uw = _dot3(t_cat[:, p * 2 * chunk:(p + 1) * 2 * chunk], rhs)
            v_new, q_s = [], []
            for i, h in enumerate(heads):
                u = uw[:, (2 * i) * GDN_DV:(2 * i + 1) * GDN_DV]
                w = uw[:, (2 * i + 1) * GDN_DV:(2 * i + 2) * GDN_DV]
                wq = jnp.concatenate([w, qn[h] * col1(e_in, h)], axis=0).astype(BF16)
                ws = jnp.dot(wq, s_ref[0, h].astype(BF16), preferred_element_type=F32)
                v_new.append(u - ws[:chunk])
                q_s.append(ws[chunk:])
            intra = jnp.dot(scores[:, p * 2 * chunk:(p + 1) * 2 * chunk].astype(BF16),
                            _two_block_diag(*v_new).astype(BF16), preferred_element_type=F32)
            for i, h in enumerate(heads):
                o = q_s[i] + intra[:, i * GDN_DV:(i + 1) * GDN_DV]
                s_ref[0, h] = s_ref[0, h] * col1(e_all, h) + _dot_tn(
                    (kn[h] * col1(e_out, h)).astype(BF16), v_new[i].astype(BF16))
                o = o * lax.rsqrt(jnp.mean(o * o, axis=-1, keepdims=True) + EPS) * nw_ref[...]
                hc = slice(h * GDN_DV, (h + 1) * GDN_DV)
                o_ref[r, hc] = (o * _silu(z_ref[r, hc])).astype(o_ref.dtype)
        return carry

    lax.fori_loop(0, q_ref.shape[0] // chunk, body, 0)


def _gdn_params(lw, chunk):
    cat = GDN_HEADS * chunk
    lanes = jnp.arange(cat) // chunk
    sel = jnp.zeros((128, 2 * cat), F32)
    sel = sel.at[NARROW_B + lanes, jnp.arange(cat)].set(1.0)
    sel = sel.at[NARROW_A + lanes, cat + jnp.arange(cat)].set(1.0)
    return (lw['gdn_conv_w'].astype(F32), _lane_row(lw['gdn_dt_bias'], NARROW_A),
            _lane_row(-jnp.exp(lw['gdn_a_log'].astype(F32)), NARROW_A),
            lw['gdn_norm_w'].astype(F32).reshape(1, GDN_DV), jnp.tril(jnp.ones((chunk, chunk), F32)), sel)


def gdn_prompt(y, n_seq, seq_len, lw):
    chunk = min(GDN_CHUNK, seq_len)
    params = _gdn_params(lw, chunk)
    tok = lambda s: pl.BlockSpec((seq_len, GDN_QKV), lambda b: (b, PK_OFF[s] // GDN_QKV))
    out, s, tail = pl.pallas_call(
        functools.partial(_gdn_prompt_kernel, chunk=chunk),
        grid=(n_seq,),
        in_specs=[tok(1), tok(2), tok(3), tok(4), pl.BlockSpec((seq_len, 128), lambda b: (b, PK_NARROW // 128))]
                 + [_const_spec(p.shape) for p in params],
        out_specs=[pl.BlockSpec((seq_len, GDN_QKV), lambda b: (b, 0)),
                   pl.BlockSpec((1, GDN_HEADS, GDN_DK, GDN_DV), lambda b: (b, 0, 0, 0)),
                   pl.BlockSpec((1, TAIL, GDN_CONV_DIM), lambda b: (b, 0, 0))],
        out_shape=[jax.ShapeDtypeStruct((n_seq * seq_len, GDN_QKV), BF16),
                   jax.ShapeDtypeStruct((n_seq, GDN_HEADS, GDN_DK, GDN_DV), F32),
                   jax.ShapeDtypeStruct((n_seq, TAIL, GDN_CONV_DIM), F32)],
        compiler_params=pltpu.CompilerParams(
            dimension_semantics=("arbitrary",), vmem_limit_bytes=V7X_VMEM_LIMIT),
        name="gdn_prompt",
    )(y, y, y, y, y, *params)
    return out, tail[:, TAIL - (CONV_W - 1):], s


def split_last(t, sizes):
    return jnp.split(t, [int(s) for s in np.cumsum(sizes)[:-1]], axis=-1)


def l2_normalize(x):
    xf = x.astype(jnp.float32)
    return xf * lax.rsqrt(jnp.sum(xf * xf, axis=-1, keepdims=True) + EPS)


def causal_conv(x, buf, w):
    L = x.shape[1]
    xp = jnp.concatenate([buf.astype(x.dtype), x], axis=1)
    y = xp[:, 0:L] * w[0]
    for j in range(1, CONV_W):
        y = y + xp[:, j:j + L] * w[j]
    return y, xp[:, L:]


def rotary(x, positions):
    half = x.shape[-1] // 2
    inv_freq = ROPE_BASE ** (-jnp.arange(half, dtype=jnp.float32) / half)
    ang = positions.astype(jnp.float32)[:, None] * inv_freq[None, :]
    cos = jnp.cos(ang)[None, :, None, :]
    sin = jnp.sin(ang)[None, :, None, :]
    xf = x.astype(jnp.float32)
    x1, x2 = xf[..., :half], xf[..., half:]
    return jnp.concatenate([x1 * cos - x2 * sin, x1 * sin + x2 * cos], axis=-1)


def to_chunks(t, chunk):
    L = t.shape[1]
    pad = (-L) % chunk
    t = jnp.pad(t, [(0, 0), (0, pad)] + [(0, 0)] * (t.ndim - 2))
    n = t.shape[1] // chunk
    t = t.reshape((t.shape[0], n, chunk) + t.shape[2:])
    return jnp.moveaxis(t, 2, 3)


def from_chunks(t, L):
    t = jnp.moveaxis(t, 3, 2)
    t = t.reshape((t.shape[0], t.shape[1] * t.shape[2]) + t.shape[3:])
    return t[:, :L]


def intra_decay(G):
    C = G.shape[-1]
    causal = jnp.tril(jnp.ones((C, C), dtype=bool))
    diff = G[..., :, None] - G[..., None, :]
    return jnp.where(causal, jnp.exp(jnp.where(causal, diff, 0.0)), 0.0)


def decay_linear_attention(q, k, v, log_a, s0, chunk):
    L = q.shape[1]
    C = min(chunk, L)
    qc, kc, vc = (to_chunks(t.astype(jnp.float32), C) for t in (q, k, v))
    G = jnp.cumsum(to_chunks(log_a.astype(jnp.float32), C), axis=-1)
    G_last = G[..., -1]
    scores = jnp.einsum('bnhid,bnhjd->bnhij', qc, kc) * intra_decay(G)
    intra = jnp.einsum('bnhij,bnhjv->bnhiv', scores, vc)
    chunk_states = jnp.einsum('bnhcd,bnhcv->bnhdv', kc * jnp.exp(G_last[..., None] - G)[..., None], vc)

    def step(S, inp):
        cs, gl = inp
        return S * gl[..., None, None] + cs, S

    s_final, s_prev = lax.scan(step, s0.astype(jnp.float32),
                               (jnp.moveaxis(chunk_states, 1, 0), jnp.moveaxis(jnp.exp(G_last), 1, 0)))
    s_prev = jnp.moveaxis(s_prev, 0, 1)
    inter = jnp.einsum('bnhcd,bnhdv->bnhcv', qc * jnp.exp(G)[..., None], s_prev)
    return from_chunks(intra + inter, L), s_final


def gated_delta_rule(q, k, v, beta, g, s0, chunk):
    L = q.shape[1]
    C = min(chunk, L)
    qc, kc, vc = (to_chunks(t.astype(jnp.float32), C) for t in (q, k, v))
    bc = to_chunks(beta.astype(jnp.float32), C)
    G = jnp.cumsum(to_chunks(g.astype(jnp.float32), C), axis=-1)
    decay = intra_decay(G)
    strict = jnp.tril(jnp.ones((C, C), dtype=bool), -1)
    kk = jnp.einsum('bnhid,bnhjd->bnhij', kc, kc)
    a_mat = jnp.where(strict, bc[..., :, None] * kk * decay, 0.0) + jnp.eye(C, dtype=jnp.float32)
    u = lax.linalg.triangular_solve(a_mat, vc * bc[..., None], left_side=True, lower=True, unit_diagonal=True)
    w = lax.linalg.triangular_solve(a_mat, kc * (bc * jnp.exp(G))[..., None], left_side=True, lower=True,
                                    unit_diagonal=True)
    scores = jnp.einsum('bnhid,bnhjd->bnhij', qc, kc) * decay
    q_dec = qc * jnp.exp(G)[..., None]
    k_dec = kc * jnp.exp(G[..., -1:] - G)[..., None]
    g_last = jnp.exp(G[..., -1])

    def step(S, inp):
        u_c, w_c, s_c, qd_c, kd_c, gl_c = inp
        v_new = u_c - jnp.einsum('bhcd,bhdv->bhcv', w_c, S)
        o = jnp.einsum('bhcd,bhdv->bhcv', qd_c, S) + jnp.einsum('bhij,bhjv->bhiv', s_c, v_new)
        S = S * gl_c[..., None, None] + jnp.einsum('bhcd,bhcv->bhdv', kd_c, v_new)
        return S, o

    xs = tuple(jnp.moveaxis(t, 1, 0) for t in (u, w, scores, q_dec, k_dec, g_last))
    s_final, o = lax.scan(step, s0.astype(jnp.float32), xs)
    return from_chunks(jnp.moveaxis(o, 0, 1), L), s_final


def complex_affine_combine(e1, e2):
    a1r, a1i, b1r, b1i = e1
    a2r, a2i, b2r, b2i = e2
    return (a2r * a1r - a2i * a1i, a2r * a1i + a2i * a1r,
            a2r * b1r - a2i * b1i + b2r, a2r * b1i + a2i * b1r + b2i)


def s5_mixer(u, h_re0, h_im0, lw):
    f32 = jnp.float32
    bsz, L, _ = u.shape
    a_re = lw['s5_a_re'].astype(f32)
    a_im = lw['s5_a_im'].astype(f32)
    step = jnp.exp(lw['s5_log_step'].astype(f32))[:, None]
    mag = jnp.exp(a_re * step)
    lam_re = mag * jnp.cos(a_im * step)
    lam_im = mag * jnp.sin(a_im * step)
    den = a_re * a_re + a_im * a_im
    coef_re = ((lam_re - 1.0) * a_re + lam_im * a_im) / den
    coef_im = (lam_im * a_re - (lam_re - 1.0) * a_im) / den
    b_re = lw['s5_b_re'].astype(f32)
    b_im = lw['s5_b_im'].astype(f32)
    bb_re = coef_re[..., None] * b_re - coef_im[..., None] * b_im
    bb_im = coef_re[..., None] * b_im + coef_im[..., None] * b_re
    ug = u.astype(f32).reshape(bsz, L, S5_GROUPS, S5_GROUP_CH)
    drive_re = jnp.einsum('blgc,gnc->blgn', ug, bb_re)
    drive_im = jnp.einsum('blgc,gnc->blgn', ug, bb_im)
    h0_re = h_re0.astype(f32)
    h0_im = h_im0.astype(f32)
    drive_re = drive_re.at[:, 0].add(lam_re * h0_re - lam_im * h0_im)
    drive_im = drive_im.at[:, 0].add(lam_re * h0_im + lam_im * h0_re)
    lam_re_b = jnp.broadcast_to(lam_re, drive_re.shape)
    lam_im_b = jnp.broadcast_to(lam_im, drive_im.shape)
    _, _, hs_re, hs_im = lax.associative_scan(complex_affine_combine, (lam_re_b, lam_im_b, drive_re, drive_im), axis=1)
    c_re = lw['s5_c_re'].astype(f32)
    c_im = lw['s5_c_im'].astype(f32)
    y = jnp.einsum('blgn,gcn->blgc', hs_re, c_re) - jnp.einsum('blgn,gcn->blgc', hs_im, c_im)
    y = y.reshape(bsz, L, S5_CH) + lw['s5_d'].astype(f32) * u.astype(f32)
    y = jax.nn.gelu(y)
    out = y * jax.nn.sigmoid(y @ lw['s5_w_glu'].astype(f32) + lw['s5_b_glu'].astype(f32))
    return out.astype(u.dtype), hs_re[:, -1], hs_im[:, -1]


def gdn_mixer(q, k, v, z, b_logit, a_logit, buf0, s0, lw):
    dt_out = z.dtype
    bsz, L, _ = q.shape
    qkv, buf = causal_conv(jnp.concatenate([q, k, v], axis=-1), buf0, lw['gdn_conv_w'])
    qkv = jax.nn.silu(qkv)
    qh, kh, vh = split_last(qkv, (GDN_HEADS * GDN_DK, GDN_HEADS * GDN_DK, GDN_HEADS * GDN_DV))
    qh = l2_normalize(qh.reshape(bsz, L, GDN_HEADS, GDN_DK)) * (GDN_DK ** -0.5)
    kh = l2_normalize(kh.reshape(bsz, L, GDN_HEADS, GDN_DK))
    vh = vh.reshape(bsz, L, GDN_HEADS, GDN_DV)
    beta = jax.nn.sigmoid(b_logit.astype(jnp.float32))
    g = -jnp.exp(lw['gdn_a_log'].astype(jnp.float32)) * jax.nn.softplus(
        a_logit.astype(jnp.float32) + lw['gdn_dt_bias'].astype(jnp.float32))
    o, s = gated_delta_rule(qh, kh, vh, beta, g, s0, GDN_CHUNK)
    o = o * lax.rsqrt(jnp.mean(o * o, axis=-1, keepdims=True) + EPS) * lw['gdn_norm_w'].astype(jnp.float32)
    o = o * jax.nn.silu(z.astype(jnp.float32).reshape(bsz, L, GDN_HEADS, GDN_DV))
    return o.reshape(bsz, L, GDN_HEADS * GDN_DV).astype(dt_out), buf, s


def ssd_mixer(z, xbc, dt_raw, buf0, s0, lw):
    f32 = jnp.float32
    bsz, L, _ = z.shape
    xbc, buf = causal_conv(xbc, buf0, lw['ssd_conv_w'])
    xbc = jax.nn.silu(xbc + lw['ssd_conv_b'])
    xs, bm, cm = split_last(xbc, (SSD_INNER, SSD_NGROUPS * SSD_STATE, SSD_NGROUPS * SSD_STATE))
    rep = SSD_HEADS // SSD_NGROUPS
    xs = xs.astype(f32).reshape(bsz, L, SSD_HEADS, SSD_HEADDIM)
    bm = jnp.repeat(bm.astype(f32).reshape(bsz, L, SSD_NGROUPS, SSD_STATE), rep, axis=2)
    cm = jnp.repeat(cm.astype(f32).reshape(bsz, L, SSD_NGROUPS, SSD_STATE), rep, axis=2)
    dt = jax.nn.softplus(dt_raw.astype(f32) + lw['ssd_dt_bias'].astype(f32))
    a = -jnp.exp(lw['ssd_a_log'].astype(f32))
    y, s = decay_linear_attention(cm, bm * dt[..., None], xs, dt * a, s0, SSD_CHUNK)
    y = y + xs * lw['ssd_d'].astype(f32)[:, None]
    y = y.reshape(bsz, L, SSD_INNER) * jax.nn.silu(z.astype(f32))
    y = y.reshape(bsz, L, SSD_NGROUPS, SSD_INNER // SSD_NGROUPS)
    y = y * lax.rsqrt(jnp.mean(y * y, axis=-1, keepdims=True) + EPS)
    y = y.reshape(bsz, L, SSD_INNER) * lw['ssd_norm_w'].astype(f32)
    return y.astype(z.dtype), buf, s


def retention_mixer(q, k, v, gate, s0, positions, lw):
    f32 = jnp.float32
    bsz, L, _ = q.shape
    qh = rotary(q.reshape(bsz, L, RET_HEADS, RET_DK), positions)
    kh = rotary(k.reshape(bsz, L, RET_HEADS, RET_DK), positions) * (RET_DK ** -0.5)
    vh = v.reshape(bsz, L, RET_HEADS, RET_DV)
    log_gamma = jnp.log(1.0 - 2.0 ** (-5.0 - jnp.arange(RET_HEADS, dtype=f32)))
    log_a = jnp.broadcast_to(log_gamma, (bsz, L, RET_HEADS))
    o, s = decay_linear_attention(qh, kh, vh, log_a, s0, RET_CHUNK)
    mu = jnp.mean(o, axis=-1, keepdims=True)
    var = jnp.mean((o - mu) ** 2, axis=-1, keepdims=True)
    o = ((o - mu) * lax.rsqrt(var + EPS)).reshape(bsz, L, RET_HEADS * RET_DV)
    o = o * lw['ret_ln_w'].astype(f32) + lw['ret_ln_b'].astype(f32)
    out = jax.nn.silu(gate.astype(f32)) * o
    return out.astype(q.dtype), s


def _mixers(y, st, lw, positions):
    def col(s):
        return y[..., PK_OFF[s]:PK_OFF[s] + IN_SIZES[s]]

    gdn_s0, gdn_buf0, ssd_s0, ssd_buf0, ret_s0 = st
    out_b, gdn_buf, gdn_s = gdn_mixer(col(1), col(2), col(3), col(4), col(5), col(6), gdn_buf0, gdn_s0, lw)
    out_c, ssd_buf, ssd_s = ssd_mixer(col(7), col(8), col(9), ssd_buf0, ssd_s0, lw)
    out_d, ret_s = retention_mixer(col(10), col(11), col(12), col(13), ret_s0, positions, lw)
    mixed = jnp.concatenate([out_b, out_c, out_d], axis=-1).astype(BF16)
    return mixed, (gdn_s, gdn_buf, ssd_s, ssd_buf, ret_s)


TM = 1088


def kernel(x_prompt, x_sample, p_prompt, p_sample, state_s5_re, state_s5_im, state_gdn, state_gdn_conv, state_ssd, state_ssd_conv, state_ret, norm_mix, w_in, s5_a_re, s5_a_im, s5_b_re, s5_b_im, s5_c_re, s5_c_im, s5_d, s5_log_step, s5_w_glu, s5_b_glu, gdn_conv_w, gdn_a_log, gdn_dt_bias, gdn_norm_w, ssd_conv_w, ssd_conv_b, ssd_dt_bias, ssd_a_log, ssd_d, ssd_norm_w, ret_ln_w, ret_ln_b, w_out, norm_ffn, w_ffn_in, w_ffn_out, norm_ple, w_ple_gate, w_ple_proj, norm_final):
    bp, lp, d = x_prompt.shape
    bs, ls, _ = x_sample.shape
    np_tok = bp * lp
    ns_tok = bs * ls

    mixer_w = dict(
        s5_a_re=s5_a_re, s5_a_im=s5_a_im, s5_b_re=s5_b_re, s5_b_im=s5_b_im, s5_c_re=s5_c_re, s5_c_im=s5_c_im,
        s5_d=s5_d, s5_log_step=s5_log_step, s5_w_glu=s5_w_glu, s5_b_glu=s5_b_glu,
        gdn_conv_w=gdn_conv_w, gdn_a_log=gdn_a_log, gdn_dt_bias=gdn_dt_bias, gdn_norm_w=gdn_norm_w,
        ssd_conv_w=ssd_conv_w, ssd_conv_b=ssd_conv_b, ssd_dt_bias=ssd_dt_bias, ssd_a_log=ssd_a_log,
        ssd_d=ssd_d, ssd_norm_w=ssd_norm_w, ret_ln_w=ret_ln_w, ret_ln_b=ret_ln_b)

    assert bp == 4, "the prompt S5 kernel packs two time steps of four sequences per vreg"

    def zeros(shape):
        return jnp.zeros((bp,) + shape, F32)

    prompt_init = [zeros((GDN_HEADS, GDN_DK, GDN_DV)), zeros((CONV_W - 1, GDN_CONV_DIM)),
                   zeros((SSD_HEADS, SSD_STATE, SSD_HEADDIM)), zeros((CONV_W - 1, SSD_CONV_DIM)),
                   zeros((RET_HEADS, RET_DK, RET_DV))]
    sample_states = [state_gdn, state_gdn_conv, state_ssd, state_ssd_conv, state_ret]
    pos_prompt = jnp.arange(lp, dtype=jnp.int32)
    pos_sample = PAST_LEN + jnp.arange(ls, dtype=jnp.int32)

    def tm_rows(t):
        return jnp.swapaxes(t, 0, 1).reshape(t.shape[0] * t.shape[1], t.shape[2])

    def bm_seqs(t, b):
        return jnp.swapaxes(t.reshape(t.shape[0] // b, b, t.shape[1]), 0, 1)

    h = jnp.concatenate([x_prompt.reshape(np_tok, d), tm_rows(x_sample)], axis=0)
    new_p, new_s = [], []
    y_final = None
    for i in range(DEPTH):
        lw = {k: v[i] for k, v in mixer_w.items()}
        y = rms_matmul(h, norm_mix[i], _pack_w_in(w_in[i]), tm=TM, tn=1280)
        yp = y[:np_tok].reshape(bp, lp, PK_TOTAL)
        ys = bm_seqs(y[np_tok:], bs)

        tb = _s5_tables(lw)
        a_p, st5_p = s5_prompt(tm_rows(yp[..., :S5_CH]), tb, rows=512)
        a_p = bm_seqs(a_p, bp).reshape(np_tok, S5_CH)
        a_s, st5_s = s5_sample(y[np_tok:, :S5_CH], _s5_state_to_lanes(state_s5_re[i], state_s5_im[i]), tb)

        b_p, gdn_buf_p, gdn_s_p = gdn_prompt(y, bp, lp, lw)
        c_p, ssd_buf_p, ssd_s_p = ssd_prompt(y, bp, lp, lw)
        d_p, ret_s_p = ret_prompt(y, bp, lp, lw)
        mixed_s, st_s = _mixers(ys, [s[i] for s in sample_states], lw, pos_sample)
        new_p.append(_s5_lanes_to_state(st5_p[:bp]) + (gdn_s_p, gdn_buf_p, ssd_s_p, ssd_buf_p, ret_s_p))
        new_s.append(_s5_lanes_to_state(st5_s) + st_s)
        mixed = jnp.concatenate([
            jnp.concatenate([a_p, b_p, c_p, d_p], axis=1),
            jnp.concatenate([a_s, tm_rows(mixed_s)], axis=1)], axis=0)
        h = matmul_residual(mixed, w_out[i].astype(BF16), h, tm=TM)
        h = ffn_residual(h, norm_ffn[i], w_ffn_in[i].astype(BF16), w_ffn_out[i].astype(BF16), tm=TM // 2, th=512)
        p = jnp.concatenate([p_prompt[i].reshape(np_tok, PLE_DIM), tm_rows(p_sample[i])], axis=0)
        final = i == DEPTH - 1
        outs = ple_residual(h, norm_ple[i], w_ple_gate[i].astype(BF16), p, w_ple_proj[i].astype(BF16),
                            norm_final, tm=TM // 2, final=final)
        h = outs[0]
        if final:
            y_final = outs[1]

    y_prompt = y_final[:np_tok].reshape(bp, lp, d)
    y_sample = bm_seqs(y_final[np_tok:], bs)
    stack_p = [jnp.stack([st[j] for st in new_p]) for j in range(7)]
    stack_s = [jnp.stack([st[j] for st in new_s]) for j in range(7)]
    return (y_prompt, y_sample, *stack_p, *stack_s)
```

```python
import functools
import math

import jax
import jax.numpy as jnp
import numpy as np
from jax import lax
from jax.experimental import pallas as pl
from jax.experimental.pallas import tpu as pltpu

F32 = jnp.float32
BF16 = jnp.bfloat16

D_MODEL = 2048
DEPTH = 2
GROUP_WIDTH = D_MODEL // 4
CONV_W = 4
EPS = 1e-6
PLE_DIM = 256
FFN_HIDDEN = ((8 * D_MODEL + 3 * 256 - 1) // (3 * 256)) * 256

S5_CH = GROUP_WIDTH
S5_GROUP_CH = 16
S5_GROUPS = S5_CH // S5_GROUP_CH
S5_STATE = 64

GDN_HEADS = 4
GDN_DK = GROUP_WIDTH // GDN_HEADS
GDN_DV = GROUP_WIDTH // GDN_HEADS
GDN_CHUNK = 64
GDN_CONV_DIM = 2 * GDN_HEADS * GDN_DK + GDN_HEADS * GDN_DV

SSD_INNER = GROUP_WIDTH
SSD_HEADDIM = 64
SSD_HEADS = SSD_INNER // SSD_HEADDIM
SSD_NGROUPS = 2
SSD_STATE = 128
SSD_CHUNK = 128
SSD_CONV_DIM = SSD_INNER + 2 * SSD_NGROUPS * SSD_STATE

RET_HEADS = 4
RET_DK = GROUP_WIDTH // RET_HEADS
RET_DV = GROUP_WIDTH // RET_HEADS
RET_CHUNK = 128
ROPE_BASE = 10000.0
PAST_LEN = 16384

IN_SIZES = (
    S5_CH,
    GDN_HEADS * GDN_DK, GDN_HEADS * GDN_DK, GDN_HEADS * GDN_DV, GDN_HEADS * GDN_DV, GDN_HEADS, GDN_HEADS,
    SSD_INNER, SSD_CONV_DIM, SSD_HEADS,
    RET_HEADS * RET_DK, RET_HEADS * RET_DK, RET_HEADS * RET_DV, RET_HEADS * RET_DV,
)
IN_OFFS = tuple(int(v) for v in np.cumsum((0,) + IN_SIZES))

_WIDE = (0, 1, 2, 3, 4, 7, 8, 10, 11, 12, 13)
_NARROW = (5, 6, 9)
PK_OFF = {}
_o = 0
for _s in _WIDE:
    PK_OFF[_s] = _o
    _o += IN_SIZES[_s]
PK_NARROW = _o
for _s in _NARROW:
    PK_OFF[_s] = _o
    _o += IN_SIZES[_s]
PK_USED = _o
PK_TOTAL = 6400

V7X_VMEM_LIMIT = 58 * 1024 * 1024


def _pack_w_in(w_in):
    cols = [w_in[:, IN_OFFS[s]:IN_OFFS[s + 1]] for s in _WIDE + _NARROW]
    cols.append(jnp.zeros((w_in.shape[0], PK_TOTAL - PK_USED), w_in.dtype))
    return jnp.concatenate(cols, axis=1).astype(BF16)


def _rms_rows(x, nw):
    ms = jnp.mean(x * x, axis=-1, keepdims=True)
    return x * lax.rsqrt(ms + EPS) * nw


def _rms_mm_kernel(x_ref, nw_ref, w_ref, o_ref, xn_ref):
    @pl.when(pl.program_id(1) == 0)
    def _():
        xn_ref[...] = _rms_rows(x_ref[...], nw_ref[...]).astype(BF16)

    o_ref[...] = jnp.dot(xn_ref[...], w_ref[...], preferred_element_type=F32)


def rms_matmul(x, nw, w, *, tm, tn):
    m, k = x.shape
    n = w.shape[1]
    return pl.pallas_call(
        _rms_mm_kernel,
        grid=(m // tm, n // tn),
        in_specs=[
            pl.BlockSpec((tm, k), lambda i, j: (i, 0)),
            pl.BlockSpec((1, k), lambda i, j: (0, 0)),
            pl.BlockSpec((k, tn), lambda i, j: (0, j)),
        ],
        out_specs=pl.BlockSpec((tm, tn), lambda i, j: (i, j)),
        out_shape=jax.ShapeDtypeStruct((m, n), F32),
        scratch_shapes=[pltpu.VMEM((tm, k), BF16)],
        compiler_params=pltpu.CompilerParams(
            dimension_semantics=("arbitrary", "arbitrary"), vmem_limit_bytes=V7X_VMEM_LIMIT),
        name="rms_matmul",
    )(x, nw.reshape(1, k), w)


def _mm_res_kernel(a_ref, w_ref, h_ref, o_ref):
    o_ref[...] = h_ref[...] + jnp.dot(a_ref[...], w_ref[...], preferred_element_type=F32)


def matmul_residual(a, w, h, *, tm):
    m, k = a.shape
    n = w.shape[1]
    return pl.pallas_call(
        _mm_res_kernel,
        grid=(m // tm,),
        in_specs=[
            pl.BlockSpec((tm, k), lambda i: (i, 0)),
            pl.BlockSpec((k, n), lambda i: (0, 0)),
            pl.BlockSpec((tm, n), lambda i: (i, 0)),
        ],
        out_specs=pl.BlockSpec((tm, n), lambda i: (i, 0)),
        out_shape=jax.ShapeDtypeStruct((m, n), F32),
        compiler_params=pltpu.CompilerParams(
            dimension_semantics=("arbitrary",), vmem_limit_bytes=V7X_VMEM_LIMIT),
        name="matmul_residual",
    )(a, w, h)


def _silu(x):
    return x * jax.nn.sigmoid(x)


def _ffn_kernel(h_ref, nw_ref, wg_ref, wu_ref, wo_ref, o_ref, xn_ref):
    @pl.when(pl.program_id(1) == 0)
    def _():
        h = h_ref[...]
        xn_ref[...] = _rms_rows(h, nw_ref[...]).astype(BF16)
        o_ref[...] = h

    xn = xn_ref[...]
    gate = jnp.dot(xn, wg_ref[...], preferred_element_type=F32)
    up = jnp.dot(xn, wu_ref[...], preferred_element_type=F32)
    act = (_silu(gate) * up).astype(BF16)
    o_ref[...] += jnp.dot(act, wo_ref[...], preferred_element_type=F32)


def ffn_residual(h, nw, w_in, w_out, *, tm, th):
    m, k = h.shape
    hidden = w_out.shape[0]
    nj = hidden // th
    return pl.pallas_call(
        _ffn_kernel,
        grid=(m // tm, nj),
        in_specs=[
            pl.BlockSpec((tm, k), lambda i, j: (i, 0)),
            pl.BlockSpec((1, k), lambda i, j: (0, 0)),
            pl.BlockSpec((k, th), lambda i, j: (0, j)),
            pl.BlockSpec((k, th), lambda i, j: (0, j + nj)),
            pl.BlockSpec((th, k), lambda i, j: (j, 0)),
        ],
        out_specs=pl.BlockSpec((tm, k), lambda i, j: (i, 0)),
        out_shape=jax.ShapeDtypeStruct((m, k), F32),
        scratch_shapes=[pltpu.VMEM((tm, k), BF16)],
        compiler_params=pltpu.CompilerParams(
            dimension_semantics=("arbitrary", "arbitrary"), vmem_limit_bytes=V7X_VMEM_LIMIT),
        name="ffn_residual",
    )(h, nw.reshape(1, k), w_in, w_in, w_out)


def _ple_kernel(h_ref, nw_ref, wg_ref, p_ref, wp_ref, nf_ref, o_ref, *y_ref):
    h = h_ref[...]
    xn = _rms_rows(h, nw_ref[...]).astype(BF16)
    gate = jax.nn.sigmoid(jnp.dot(xn, wg_ref[...], preferred_element_type=F32))
    proj = jnp.dot(p_ref[...].astype(BF16), wp_ref[...], preferred_element_type=F32)
    out = h + gate * proj
    o_ref[...] = out
    if y_ref:
        y_ref[0][...] = _rms_rows(out, nf_ref[...])


def ple_residual(h, nw, wg, p, wp, nf, *, tm, final):
    m, k = h.shape
    pd = p.shape[1]
    row = pl.BlockSpec((tm, k), lambda i: (i, 0))
    vec = pl.BlockSpec((1, k), lambda i: (0, 0))
    out_shape = [jax.ShapeDtypeStruct((m, k), F32)] * (2 if final else 1)
    return pl.pallas_call(
        _ple_kernel,
        grid=(m // tm,),
        in_specs=[
            row, vec,
            pl.BlockSpec((k, k), lambda i: (0, 0)),
            pl.BlockSpec((tm, pd), lambda i: (i, 0)),
            pl.BlockSpec((pd, k), lambda i: (0, 0)),
            vec,
        ],
        out_specs=[row] * len(out_shape),
        out_shape=out_shape,
        compiler_params=pltpu.CompilerParams(
            dimension_semantics=("arbitrary",), vmem_limit_bytes=V7X_VMEM_LIMIT),
        name="ple_residual",
    )(h, nw.reshape(1, k), wg, p, wp, nf.reshape(1, k))


S5_HALF_CH = S5_CH // 2
S5_HALF_ST = (S5_GROUPS // 2) * S5_STATE
S5_LANES = 4 * S5_HALF_ST
S5_SLAB = 512


def _s5_tables(lw):
    a_re = lw['s5_a_re'].astype(F32)
    a_im = lw['s5_a_im'].astype(F32)
    step = jnp.exp(lw['s5_log_step'].astype(F32))[:, None]
    mag = jnp.exp(a_re * step)
    lam_re = mag * jnp.cos(a_im * step)
    lam_im = mag * jnp.sin(a_im * step)
    den = a_re * a_re + a_im * a_im
    coef_re = ((lam_re - 1.0) * a_re + lam_im * a_im) / den
    coef_im = (lam_im * a_re - (lam_re - 1.0) * a_im) / den
    b_re = lw['s5_b_re'].astype(F32)
    b_im = lw['s5_b_im'].astype(F32)
    bb_re = coef_re[..., None] * b_re - coef_im[..., None] * b_im
    bb_im = coef_re[..., None] * b_im + coef_im[..., None] * b_re
    gh = S5_GROUPS // 2
    eye = jnp.eye(gh, dtype=F32)

    def in_blockdiag(b):
        return jnp.einsum('gnc,gh->gchn', b, eye).reshape(gh * S5_GROUP_CH, gh * S5_STATE)

    def out_blockdiag(c):
        return jnp.einsum('gcn,gh->gnhc', c, eye).reshape(gh * S5_STATE, gh * S5_GROUP_CH)

    c_re = lw['s5_c_re'].astype(F32)
    c_im = lw['s5_c_im'].astype(F32)
    bb = jnp.stack([jnp.concatenate([in_blockdiag(bb_re[h * gh:(h + 1) * gh]),
                                     in_blockdiag(bb_im[h * gh:(h + 1) * gh])], axis=1) for h in range(2)])
    cm = jnp.stack([jnp.concatenate([out_blockdiag(c_re[h * gh:(h + 1) * gh]),
                                     -out_blockdiag(c_im[h * gh:(h + 1) * gh])], axis=0) for h in range(2)])
    lam = jnp.stack([lam_re.reshape(-1), lam_im.reshape(-1)])
    lam2 = jnp.stack([lam[0] * lam[0] - lam[1] * lam[1], 2.0 * lam[0] * lam[1]])
    return dict(bb=bb.astype(BF16), cm=cm.astype(BF16), lam=lam, lam2=lam2,
                d=lw['s5_d'].astype(F32).reshape(1, S5_CH), wglu=lw['s5_w_glu'].astype(BF16),
                bglu=lw['s5_b_glu'].astype(F32).reshape(1, S5_CH))


def _s5_drive(u, bb_ref, sc_ref):
    ub = u.astype(BF16)
    for hf in range(2):
        sc_ref[:, hf * 2 * S5_HALF_ST:(hf + 1) * 2 * S5_HALF_ST] = jnp.dot(
            ub[:, hf * S5_HALF_CH:(hf + 1) * S5_HALF_CH], bb_ref[hf], preferred_element_type=F32)


def _s5_readout(sc_ref, u, cm_ref, d_ref, wglu_ref, bglu_ref):
    ys = [jnp.dot(sc_ref[:, hf * 2 * S5_HALF_ST:(hf + 1) * 2 * S5_HALF_ST].astype(BF16), cm_ref[hf],
                  preferred_element_type=F32) for hf in range(2)]
    y = jnp.concatenate(ys, axis=1) + d_ref[...] * u
    y = jax.nn.gelu(y)
    z = jnp.dot(y.astype(BF16), wglu_ref[...], preferred_element_type=F32) + bglu_ref[...]
    return y * jax.nn.sigmoid(z)


def _s5_slabs():
    for hf in range(2):
        for sl in range(S5_HALF_ST // S5_SLAB):
            re0 = hf * 2 * S5_HALF_ST + sl * S5_SLAB
            yield re0, re0 + S5_HALF_ST, hf * S5_HALF_ST + sl * S5_SLAB


def _s5_prompt_kernel(u_ref, bb_ref, m_ref, cm_ref, d_ref, wglu_ref, bglu_ref, o_ref, st_ref, sc_ref, carry_ref):
    @pl.when(pl.program_id(0) == 0)
    def _():
        carry_ref[...] = jnp.zeros_like(carry_ref)

    u = u_ref[...]
    _s5_drive(u, bb_ref, sc_ref)
    first_step = lax.broadcasted_iota(jnp.int32, (8, S5_SLAB), 0) < 4
    n_pairs = u_ref.shape[0] // 8
    for re0, im0, l0 in _s5_slabs():
        mr = m_ref[0, :, l0:l0 + S5_SLAB]
        mi = m_ref[1, :, l0:l0 + S5_SLAB]
        nr = m_ref[2, :, l0:l0 + S5_SLAB]
        ni = m_ref[3, :, l0:l0 + S5_SLAB]

        def body(k, carry, re0=re0, im0=im0, mr=mr, mi=mi, nr=nr, ni=ni):
            hr, hi = carry
            base = pl.multiple_of(k * 8, 8)
            xr = sc_ref[pl.ds(base, 8), re0:re0 + S5_SLAB]
            xi = sc_ref[pl.ds(base, 8), im0:im0 + S5_SLAB]
            xr_s = pltpu.roll(xr, 4, 0)
            xi_s = pltpu.roll(xi, 4, 0)
            outr = (mr * hr - mi * hi) + xr + (nr * xr_s - ni * xi_s)
            outi = (mr * hi + mi * hr) + xi + (nr * xi_s + ni * xr_s)
            sc_ref[pl.ds(base, 8), re0:re0 + S5_SLAB] = outr
            sc_ref[pl.ds(base, 8), im0:im0 + S5_SLAB] = outi
            return (jnp.where(first_step, pltpu.roll(outr, 4, 0), outr),
                    jnp.where(first_step, pltpu.roll(outi, 4, 0), outi))

        hr, hi = lax.fori_loop(0, n_pairs, body,
                               (carry_ref[:, re0:re0 + S5_SLAB], carry_ref[:, im0:im0 + S5_SLAB]))
        carry_ref[:, re0:re0 + S5_SLAB] = hr
        carry_ref[:, im0:im0 + S5_SLAB] = hi

    o_ref[...] = _s5_readout(sc_ref, u, cm_ref, d_ref, wglu_ref, bglu_ref).astype(o_ref.dtype)
    st_ref[...] = carry_ref[...]


def _const_spec(shape):
    return pl.BlockSpec(shape, lambda c: (0,) * len(shape))


def s5_prompt(u_tm, tb, *, rows):
    n = u_tm.shape[0]
    zero = jnp.zeros_like(tb['lam'])
    m = jnp.stack([jnp.concatenate([jnp.broadcast_to(a[k][None], (4, a.shape[1])),
                                    jnp.broadcast_to(b[k][None], (4, b.shape[1]))], axis=0)
                   for a, b, k in ((tb['lam'], tb['lam2'], 0), (tb['lam'], tb['lam2'], 1),
                                   (zero, tb['lam'], 0), (zero, tb['lam'], 1))])
    return pl.pallas_call(
        _s5_prompt_kernel,
        grid=(n // rows,),
        in_specs=[
            pl.BlockSpec((rows, S5_CH), lambda c: (c, 0)),
            _const_spec(tb['bb'].shape), _const_spec(m.shape), _const_spec(tb['cm'].shape),
            _const_spec((1, S5_CH)), _const_spec((S5_CH, S5_CH)), _const_spec((1, S5_CH)),
        ],
        out_specs=[pl.BlockSpec((rows, S5_CH), lambda c: (c, 0)), _const_spec((8, S5_LANES))],
        out_shape=[jax.ShapeDtypeStruct((n, S5_CH), BF16), jax.ShapeDtypeStruct((8, S5_LANES), F32)],
        scratch_shapes=[pltpu.VMEM((rows, S5_LANES), F32), pltpu.VMEM((8, S5_LANES), F32)],
        compiler_params=pltpu.CompilerParams(
            dimension_semantics=("arbitrary",), vmem_limit_bytes=V7X_VMEM_LIMIT),
        name="s5_prompt",
    )(u_tm, tb['bb'], m, tb['cm'], tb['d'], tb['wglu'], tb['bglu'])


def _s5_sample_kernel(u_ref, h0_ref, bb_ref, lam_ref, cm_ref, d_ref, wglu_ref, bglu_ref, o_ref, st_ref, sc_ref):
    u = u_ref[...]
    _s5_drive(u, bb_ref, sc_ref)
    n_seq = h0_ref.shape[0]
    n_steps = u_ref.shape[0] // n_seq
    for re0, im0, l0 in _s5_slabs():
        lr = lam_ref[0:1, l0:l0 + S5_SLAB]
        li = lam_ref[1:2, l0:l0 + S5_SLAB]

        def body(rb, _, re0=re0, im0=im0, lr=lr, li=li):
            r0 = pl.multiple_of(rb * 8, 8)
            hr = h0_ref[pl.ds(r0, 8), re0:re0 + S5_SLAB]
            hi = h0_ref[pl.ds(r0, 8), im0:im0 + S5_SLAB]
            for t in range(n_steps):
                rows = pl.ds(t * n_seq + r0, 8)
                nr = (lr * hr - li * hi) + sc_ref[rows, re0:re0 + S5_SLAB]
                ni = (lr * hi + li * hr) + sc_ref[rows, im0:im0 + S5_SLAB]
                sc_ref[rows, re0:re0 + S5_SLAB] = nr
                sc_ref[rows, im0:im0 + S5_SLAB] = ni
                hr, hi = nr, ni
            st_ref[pl.ds(r0, 8), re0:re0 + S5_SLAB] = hr
            st_ref[pl.ds(r0, 8), im0:im0 + S5_SLAB] = hi
            return 0

        lax.fori_loop(0, n_seq // 8, body, 0)

    o_ref[...] = _s5_readout(sc_ref, u, cm_ref, d_ref, wglu_ref, bglu_ref).astype(o_ref.dtype)


def s5_sample(u_tm, h0, tb):
    n = u_tm.shape[0]
    nb = h0.shape[0]
    return pl.pallas_call(
        _s5_sample_kernel,
        grid=(1,),
        in_specs=[_const_spec((n, S5_CH)), _const_spec((nb, S5_LANES)), _const_spec(tb['bb'].shape),
                  _const_spec(tb['lam'].shape), _const_spec(tb['cm'].shape), _const_spec((1, S5_CH)),
                  _const_spec((S5_CH, S5_CH)), _const_spec((1, S5_CH))],
        out_specs=[_const_spec((n, S5_CH)), _const_spec((nb, S5_LANES))],
        out_shape=[jax.ShapeDtypeStruct((n, S5_CH), BF16), jax.ShapeDtypeStruct((nb, S5_LANES), F32)],
        scratch_shapes=[pltpu.VMEM((n, S5_LANES), F32)],
        compiler_params=pltpu.CompilerParams(
            dimension_semantics=("arbitrary",), vmem_limit_bytes=V7X_VMEM_LIMIT),
        name="s5_sample",
    )(u_tm, h0, tb['bb'], tb['lam'], tb['cm'], tb['d'], tb['wglu'], tb['bglu'])


def _s5_state_to_lanes(re, im):
    b = re.shape[0]
    return jnp.stack([re.reshape(b, 2, S5_HALF_ST), im.reshape(b, 2, S5_HALF_ST)], axis=2).reshape(b, S5_LANES)


def _s5_lanes_to_state(st):
    b = st.shape[0]
    st = st.reshape(b, 2, 2, S5_HALF_ST)
    return st[:, :, 0].reshape(b, S5_GROUPS, S5_STATE), st[:, :, 1].reshape(b, S5_GROUPS, S5_STATE)


def _ret_tables(positions, chunk):
    half = RET_DK // 2
    inv_freq = ROPE_BASE ** (-jnp.arange(half, dtype=F32) / half)
    ang = positions.astype(F32)[:, None] * inv_freq[None, :]
    cos = jnp.cos(ang)
    sin = jnp.sin(ang)
    cos2 = jnp.concatenate([cos, cos], axis=1)
    sin2 = jnp.concatenate([-sin, sin], axis=1)
    log_gamma = jnp.log(1.0 - 2.0 ** (-5.0 - jnp.arange(RET_HEADS, dtype=F32)))
    g = (jnp.arange(chunk, dtype=F32) + 1.0)[None, :] * log_gamma[:, None]
    diff = g[:, :, None] - g[:, None, :]
    causal = jnp.tril(jnp.ones((chunk, chunk), dtype=bool))
    dmat = jnp.where(causal, jnp.exp(jnp.where(causal, diff, 0.0)), 0.0)
    lanes = (RET_HEADS, chunk, RET_DK)
    qdec = jnp.broadcast_to(jnp.exp(g)[:, :, None], lanes)
    kdec = jnp.broadcast_to(jnp.exp(g[:, -1:] - g)[:, :, None], lanes)
    gall = jnp.broadcast_to(jnp.exp(g[:, -1])[:, None, None], (RET_HEADS, 1, RET_DV))
    return cos2, sin2, dmat, qdec, kdec, gall


def _ret_rotate(x, cos2, sin2):
    return x * cos2 + pltpu.roll(x, RET_DK // 2, 1) * sin2


def _group_layernorm_gate(o, gate, w, b):
    mu = jnp.mean(o, axis=-1, keepdims=True)
    xc = o - mu
    var = jnp.mean(xc * xc, axis=-1, keepdims=True)
    return _silu(gate) * (xc * lax.rsqrt(var + EPS) * w + b)


def _dot_nt(a, b):
    return lax.dot_general(a, b, (((1,), (1,)), ((), ())), preferred_element_type=F32)


def _dot_tn(a, b):
    return lax.dot_general(a, b, (((0,), (0,)), ((), ())), preferred_element_type=F32)


def _ret_prompt_kernel(q_ref, k_ref, v_ref, g_ref, cos_ref, sin_ref, dm_ref, qd_ref, kd_ref, ga_ref,
                       lnw_ref, lnb_ref, o_ref, s_ref, *, chunk):
    dm, qd, kd, ga = dm_ref[0], qd_ref[0], kd_ref[0], ga_ref[0]
    lnw, lnb = lnw_ref[0], lnb_ref[0]

    def body(c, s):
        r = pl.ds(pl.multiple_of(c * chunk, chunk), chunk)
        cos2, sin2 = cos_ref[r, :], sin_ref[r, :]
        q = _ret_rotate(q_ref[r, :], cos2, sin2)
        k = _ret_rotate(k_ref[r, :], cos2, sin2) * (RET_DK ** -0.5)
        vb = v_ref[r, :].astype(BF16)
        scores = _dot_nt(q.astype(BF16), k.astype(BF16)) * dm
        o = jnp.dot(scores.astype(BF16), vb, preferred_element_type=F32)
        o = o + jnp.dot((q * qd).astype(BF16), s.astype(BF16), preferred_element_type=F32)
        o_ref[r, :] = _group_layernorm_gate(o, g_ref[r, :], lnw, lnb).astype(o_ref.dtype)
        return s * ga + _dot_tn((k * kd).astype(BF16), vb)

    s_ref[0, 0] = lax.fori_loop(0, q_ref.shape[0] // chunk, body, jnp.zeros((RET_DK, RET_DV), F32))


def ret_prompt(y, n_seq, seq_len, lw):
    chunk = min(RET_CHUNK, seq_len)
    cos2, sin2, dmat, qdec, kdec, gall = _ret_tables(jnp.arange(seq_len, dtype=jnp.int32), chunk)
    col = lambda s: (lambda b, h: (b, PK_OFF[s] // RET_DK + h))
    head = lambda shape: pl.BlockSpec((1,) + shape, lambda b, h: (h, 0, 0))
    tok = lambda s: pl.BlockSpec((seq_len, RET_DK), col(s))
    tab = pl.BlockSpec((seq_len, RET_DK), lambda b, h: (0, 0))
    return pl.pallas_call(
        functools.partial(_ret_prompt_kernel, chunk=chunk),
        grid=(n_seq, RET_HEADS),
        in_specs=[tok(10), tok(11), tok(12), tok(13), tab, tab,
                  head((chunk, chunk)), head((chunk, RET_DK)), head((chunk, RET_DK)), head((1, RET_DV)),
                  head((1, RET_DV)), head((1, RET_DV))],
        out_specs=[pl.BlockSpec((seq_len, RET_DV), lambda b, h: (b, h)),
                   pl.BlockSpec((1, 1, RET_DK, RET_DV), lambda b, h: (b, h, 0, 0))],
        out_shape=[jax.ShapeDtypeStruct((n_seq * seq_len, RET_HEADS * RET_DV), BF16),
                   jax.ShapeDtypeStruct((n_seq, RET_HEADS, RET_DK, RET_DV), F32)],
        compiler_params=pltpu.CompilerParams(
            dimension_semantics=("arbitrary", "arbitrary"), vmem_limit_bytes=V7X_VMEM_LIMIT),
        name="ret_prompt",
    )(y, y, y, y, cos2, sin2, dmat, qdec, kdec, gall,
      lw['ret_ln_w'].astype(F32).reshape(RET_HEADS, 1, RET_DV), lw['ret_ln_b'].astype(F32).reshape(RET_HEADS, 1, RET_DV))


SAMPLE_BB = 16
LHS_ROWS = 16


def _ret_sample_kernel(q_ref, k_ref, v_ref, g_ref, cos_ref, sin_ref, s0_ref, lnw_ref, lnb_ref, o_ref, s_ref,
                       qd_scr, kd_scr, v_scr, acc_scr, lq, lk, lv, *, decay):
    n_t, n_b = q_ref.shape[0], q_ref.shape[1]
    for h in range(RET_HEADS):
        hc = slice(h * RET_DK, (h + 1) * RET_DK)
        qs, ks, vs = [], [], []
        for t in range(n_t):
            cos2, sin2 = cos_ref[t:t + 1, :], sin_ref[t:t + 1, :]
            qs.append(_ret_rotate(q_ref[t, :, hc], cos2, sin2))
            ks.append(_ret_rotate(k_ref[t, :, hc], cos2, sin2) * (RET_DK ** -0.5))
            vs.append(v_ref[t, :, hc])
            qd_scr[h, t] = qs[t] * decay['q'][h][t]
            kd_scr[h, t] = ks[t] * decay['k'][h][t]
            v_scr[h, t] = vs[t]
        for i in range(n_t):
            acc = None
            for j in range(i + 1):
                term = (jnp.sum(qs[i] * ks[j], axis=-1, keepdims=True) * decay['m'][h][i][j]) * vs[j]
                acc = term if acc is None else acc + term
            acc_scr[h, i] = acc

    for tile in (lq, lk, lv):
        tile[...] = jnp.zeros_like(tile)

    def per_seq(b, carry):
        row = pl.ds(b, 1)
        for h in range(RET_HEADS):
            for t in range(n_t):
                lq[h, t:t + 1, :] = qd_scr[h, t, row, :]
                lk[h, t:t + 1, :] = kd_scr[h, t, row, :]
                lv[h, t:t + 1, :] = v_scr[h, t, row, :]
            s0 = s0_ref[b, h]
            inter = jnp.dot(lq[h].astype(BF16), s0.astype(BF16), preferred_element_type=F32)
            s_ref[b, h] = s0 * decay['all'][h] + _dot_tn(lk[h].astype(BF16), lv[h].astype(BF16))
            for t in range(n_t):
                acc_scr[h, t, row, :] = acc_scr[h, t, row, :] + inter[t:t + 1, :]
        return carry

    lax.fori_loop(0, n_b, per_seq, 0)

    for h in range(RET_HEADS):
        hc = slice(h * RET_DV, (h + 1) * RET_DV)
        for t in range(n_t):
            o_ref[t, :, hc] = _group_layernorm_gate(acc_scr[h, t], g_ref[t, :, hc], lnw_ref[h], lnb_ref[h]
                                                    ).astype(o_ref.dtype)


def ret_sample(ys, s0, first_pos, lw):
    n_t, n_b, _ = ys.shape
    cos2, sin2, _, _, _, _ = _ret_tables(first_pos + jnp.arange(n_t, dtype=jnp.int32), n_t)
    gamma = 1.0 - 2.0 ** (-5.0 - np.arange(RET_HEADS, dtype=np.float64))
    decay = dict(m=[[[float(g ** (i - j)) for j in range(n_t)] for i in range(n_t)] for g in gamma],
                 q=[[float(g ** (i + 1)) for i in range(n_t)] for g in gamma],
                 k=[[float(g ** (n_t - 1 - j)) for j in range(n_t)] for g in gamma],
                 all=[float(g ** n_t) for g in gamma])
    width = RET_HEADS * RET_DK
    tok = lambda s: pl.BlockSpec((n_t, SAMPLE_BB, width), lambda i: (0, i, PK_OFF[s] // width))
    state = pl.BlockSpec((SAMPLE_BB, RET_HEADS, RET_DK, RET_DV), lambda i: (i, 0, 0, 0))
    per_tok = pltpu.VMEM((RET_HEADS, n_t, SAMPLE_BB, RET_DK), F32)
    tile = pltpu.VMEM((RET_HEADS, LHS_ROWS, RET_DK), F32)
    return pl.pallas_call(
        functools.partial(_ret_sample_kernel, decay=decay),
        grid=(n_b // SAMPLE_BB,),
        in_specs=[tok(10), tok(11), tok(12), tok(13), _const_spec((n_t, RET_DK)), _const_spec((n_t, RET_DK)),
                  state, _const_spec((RET_HEADS, 1, RET_DV)), _const_spec((RET_HEADS, 1, RET_DV))],
        out_specs=[pl.BlockSpec((n_t, SAMPLE_BB, width), lambda i: (0, i, 0)), state],
        out_shape=[jax.ShapeDtypeStruct((n_t, n_b, width), BF16), jax.ShapeDtypeStruct(s0.shape, F32)],
        scratch_shapes=[per_tok, per_tok, per_tok, per_tok, tile, tile, tile],
        compiler_params=pltpu.CompilerParams(
            dimension_semantics=("arbitrary",), vmem_limit_bytes=V7X_VMEM_LIMIT),
        name="ret_sample",
    )(ys, ys, ys, ys, cos2, sin2, s0,
      lw['ret_ln_w'].astype(F32).reshape(RET_HEADS, 1, RET_DV), lw['ret_ln_b'].astype(F32).reshape(RET_HEADS, 1, RET_DV))


NARROW_B = 0
NARROW_A = GDN_HEADS
NARROW_DT = 2 * GDN_HEADS
TAIL = 8


def _shift_rows(x, tail, s):
    xr = pltpu.roll(x, s, 0)
    tr = pltpu.roll(tail, s, 0)
    row = lax.broadcasted_iota(jnp.int32, tail.shape, 0)
    return jnp.concatenate([jnp.where(row < s, tr, xr[0:TAIL]), xr[TAIL:]], axis=0)


def _causal_conv(x, tail, w_ref):
    y = x * w_ref[CONV_W - 1:CONV_W, :]
    for s in range(1, CONV_W):
        y = y + _shift_rows(x, tail, s) * w_ref[CONV_W - 1 - s:CONV_W - s, :]
    return y


def _softplus(x):
    return jnp.maximum(x, 0.0) + jnp.log1p(jnp.exp(-jnp.abs(x)))


def _lane_row(vals, lane0, width=128):
    return jnp.zeros((1, width), F32).at[0, lane0:lane0 + vals.shape[0]].set(vals.astype(F32))


SSD_PAIRS = SSD_HEADS // 2
SSD_BC = SSD_NGROUPS * SSD_STATE


def _ssd_prompt_kernel(z_ref, xbc_ref, nar_ref, cw_ref, cb_ref, dtb_ref, a_ref, dsk_ref, nw_ref, tri_ref,
                       o_ref, s_ref, tail_ref, *, chunk):
    s_ref[...] = jnp.zeros_like(s_ref)
    tail_ref[...] = jnp.zeros_like(tail_ref)
    causal = (lax.broadcasted_iota(jnp.int32, (chunk, chunk), 0)
              >= lax.broadcasted_iota(jnp.int32, (chunk, chunk), 1))
    lane = lax.broadcasted_iota(jnp.int32, (chunk, 2 * SSD_HEADDIM), 1)
    first_head = lane < SSD_HEADDIM
    rep = SSD_HEADS // SSD_NGROUPS

    def body(c, carry):
        r = pl.ds(pl.multiple_of(c * chunk, chunk), chunk)
        raw = xbc_ref[r, :]
        xbc = _silu(_causal_conv(raw, tail_ref[0], cw_ref) + cb_ref[...])
        tail_ref[0] = raw[chunk - TAIL:, :]
        xs = xbc[:, :SSD_INNER]
        dt = _softplus(nar_ref[r, :] + dtb_ref[...])
        g = jnp.dot(tri_ref[...], dt * a_ref[...], preferred_element_type=F32,
                    precision=lax.Precision.HIGHEST)
        e_in = jnp.exp(g)
        e_out = dt * jnp.exp(g[chunk - 1:chunk, :] - g)
        e_all = jnp.exp(g[chunk - 1:chunk, :])
        g_t = g.T
        dt_t = dt.T
        ys = []
        for p in range(SSD_PAIRS):
            grp = (2 * p) // rep
            bm = xbc[:, SSD_INNER + grp * SSD_STATE:SSD_INNER + (grp + 1) * SSD_STATE]
            cm = xbc[:, SSD_INNER + SSD_BC + grp * SSD_STATE:SSD_INNER + SSD_BC + (grp + 1) * SSD_STATE]
            cb = _dot_nt(cm.astype(BF16), bm.astype(BF16))
            xp = xs[:, p * 128:(p + 1) * 128]
            xpb = xp.astype(BF16)
            sp = s_ref[0, p]
            spb = sp.astype(BF16)
            outs, upds, gls = [], [], []
            for hh in range(2):
                ln = NARROW_DT + 2 * p + hh
                diff = g[:, ln:ln + 1] - g_t[ln:ln + 1, :]
                m = jnp.where(causal, cb * jnp.exp(jnp.where(causal, diff, 0.0)) * dt_t[ln:ln + 1, :], 0.0)
                o = jnp.dot(m.astype(BF16), xpb, preferred_element_type=F32)
                o = o + jnp.dot((cm * e_in[:, ln:ln + 1]).astype(BF16), spb, preferred_element_type=F32)
                outs.append(o)
                upds.append(_dot_tn((bm * e_out[:, ln:ln + 1]).astype(BF16), xpb))
                gls.append(e_all[:, ln:ln + 1])
            s_ref[0, p] = sp * jnp.where(first_head, gls[0], gls[1]) + jnp.where(first_head, upds[0], upds[1])
            ys.append(jnp.where(first_head, outs[0], outs[1]) + xp * dsk_ref[:, p * 128:(p + 1) * 128])
        y = jnp.concatenate(ys, axis=1) * _silu(z_ref[r, :])
        gw = SSD_INNER // SSD_NGROUPS
        yn = [y[:, i * gw:(i + 1) * gw] * lax.rsqrt(
            jnp.mean(y[:, i * gw:(i + 1) * gw] * y[:, i * gw:(i + 1) * gw], axis=-1, keepdims=True) + EPS)
            for i in range(SSD_NGROUPS)]
        o_ref[r, :] = (jnp.concatenate(yn, axis=1) * nw_ref[...]).astype(o_ref.dtype)
        return carry

    lax.fori_loop(0, z_ref.shape[0] // chunk, body, 0)


def _ssd_params(lw, chunk):
    return (lw['ssd_conv_w'].astype(F32), lw['ssd_conv_b'].astype(F32).reshape(1, SSD_CONV_DIM),
            _lane_row(lw['ssd_dt_bias'], NARROW_DT), _lane_row(-jnp.exp(lw['ssd_a_log'].astype(F32)), NARROW_DT),
            jnp.repeat(lw['ssd_d'].astype(F32), SSD_HEADDIM).reshape(1, SSD_INNER),
            lw['ssd_norm_w'].astype(F32).reshape(1, SSD_INNER),
            jnp.tril(jnp.ones((chunk, chunk), F32)))


def _ssd_state_from_pairs(s):
    b = s.shape[0]
    s = s.reshape(b, SSD_PAIRS, SSD_STATE, 2, SSD_HEADDIM)
    return jnp.swapaxes(s, 2, 3).reshape(b, SSD_HEADS, SSD_STATE, SSD_HEADDIM)


def ssd_prompt(y, n_seq, seq_len, lw):
    chunk = min(SSD_CHUNK, seq_len)
    params = _ssd_params(lw, chunk)
    tok = lambda s, w: pl.BlockSpec((seq_len, w), lambda b: (b, PK_OFF[s] // w))
    out, s, tail = pl.pallas_call(
        functools.partial(_ssd_prompt_kernel, chunk=chunk),
        grid=(n_seq,),
        in_specs=[tok(7, SSD_INNER), tok(8, SSD_CONV_DIM),
                  pl.BlockSpec((seq_len, 128), lambda b: (b, PK_NARROW // 128))]
                 + [_const_spec(p.shape) for p in params],
        out_specs=[pl.BlockSpec((seq_len, SSD_INNER), lambda b: (b, 0)),
                   pl.BlockSpec((1, SSD_PAIRS, SSD_STATE, 2 * SSD_HEADDIM), lambda b: (b, 0, 0, 0)),
                   pl.BlockSpec((1, TAIL, SSD_CONV_DIM), lambda b: (b, 0, 0))],
        out_shape=[jax.ShapeDtypeStruct((n_seq * seq_len, SSD_INNER), BF16),
                   jax.ShapeDtypeStruct((n_seq, SSD_PAIRS, SSD_STATE, 2 * SSD_HEADDIM), F32),
                   jax.ShapeDtypeStruct((n_seq, TAIL, SSD_CONV_DIM), F32)],
        compiler_params=pltpu.CompilerParams(
            dimension_semantics=("arbitrary",), vmem_limit_bytes=V7X_VMEM_LIMIT),
        name="ssd_prompt",
    )(y, y, y, *params)
    return out, tail[:, TAIL - (CONV_W - 1):], _ssd_state_from_pairs(s)


GDN_QKV = GDN_HEADS * GDN_DK
HIGHEST = lax.Precision.HIGHEST


def _split_bf16(x):
    hi = x.astype(BF16)
    return hi, (x - hi.astype(F32)).astype(BF16)


def _dot3(a, b):
    a_hi, a_lo = _split_bf16(a)
    b_hi, b_lo = _split_bf16(b)
    m = a.shape[0]
    p = jnp.dot(jnp.concatenate([a_hi, a_lo], axis=0), b_hi, preferred_element_type=F32)
    return p[:m] + p[m:] + jnp.dot(a_hi, b_lo, preferred_element_type=F32)


def _two_block_diag(x0, x1):
    z0 = jnp.zeros_like(x0)
    z1 = jnp.zeros_like(x1)
    return jnp.concatenate([jnp.concatenate([x0, z1], axis=1), jnp.concatenate([z0, x1], axis=1)], axis=0)


def _l2_rows(x):
    return x * lax.rsqrt(jnp.sum(x * x, axis=-1, keepdims=True) + EPS)


def _gdn_prompt_kernel(q_ref, k_ref, v_ref, z_ref, nar_ref, cw_ref, dtb_ref, a_ref, nw_ref, tri_ref, sel_ref,
                       o_ref, s_ref, tail_ref, *, chunk):
    s_ref[...] = jnp.zeros_like(s_ref)
    tail_ref[...] = jnp.zeros_like(tail_ref)
    cat = GDN_HEADS * chunk
    lane = lax.broadcasted_iota(jnp.int32, (chunk, cat), 1)
    row = lax.broadcasted_iota(jnp.int32, (chunk, cat), 0)
    col = jnp.bitwise_and(lane, chunk - 1)
    causal = row >= col
    strict = row > col
    eye_cat = jnp.where(row == col, 1.0, 0.0).astype(F32)
    head_mask = [(lane >= h * chunk) & (lane < (h + 1) * chunk) for h in range(GDN_HEADS)]
    nar_lane = lax.broadcasted_iota(jnp.int32, (chunk, 128), 1)
    ones_cc = jnp.ones((chunk, chunk), F32)

    def mm_cat(l_cat, r_cat):
        bd = jnp.concatenate([jnp.where(m, r_cat, 0.0) for m in head_mask], axis=0)
        return _dot3(l_cat, bd)

    def body(c, carry):
        r = pl.ds(pl.multiple_of(c * chunk, chunk), chunk)
        qkv = []
        for i, ref in enumerate((q_ref, k_ref, v_ref)):
            cols = slice(i * GDN_QKV, (i + 1) * GDN_QKV)
            raw = ref[r, :]
            qkv.append(_silu(_causal_conv(raw, tail_ref[0, :, cols], cw_ref.at[:, cols])))
            tail_ref[0, :, cols] = raw[chunk - TAIL:, :]
        q, k, v = qkv
        nar = nar_ref[r, :]
        beta = jax.nn.sigmoid(nar)
        g = jnp.dot(tri_ref[...], a_ref[...] * _softplus(nar + dtb_ref[...]),
                    preferred_element_type=F32, precision=HIGHEST)
        bg = jnp.dot(jnp.where(nar_lane < NARROW_A, beta, g), sel_ref[...],
                     preferred_element_type=F32, precision=HIGHEST)
        b_c, g_c = bg[:, :cat], bg[:, cat:]
        g_r = jnp.dot(ones_cc, g_c * eye_cat, preferred_element_type=F32, precision=HIGHEST)
        decay = jnp.where(causal, jnp.exp(jnp.where(causal, g_c - g_r, 0.0)), 0.0)
        e_in = jnp.exp(g_c)
        e_out = jnp.exp(g_c[chunk - 1:chunk, :] - g_c)
        e_all = jnp.exp(g_c[chunk - 1:chunk, :])

        qn = [_l2_rows(q[:, h * GDN_DK:(h + 1) * GDN_DK]) * (GDN_DK ** -0.5) for h in range(GDN_HEADS)]
        kn = [_l2_rows(k[:, h * GDN_DK:(h + 1) * GDN_DK]) for h in range(GDN_HEADS)]
        vh = [v[:, h * GDN_DV:(h + 1) * GDN_DV] for h in range(GDN_HEADS)]
        kk, qk = [], []
        for p in range(GDN_HEADS // 2):
            h0, h1 = 2 * p, 2 * p + 1
            rhs = _two_block_diag(kn[h0], kn[h1]).astype(BF16)
            kk.append(_dot_nt(jnp.concatenate([kn[h0], kn[h1]], axis=1).astype(BF16), rhs))
            qk.append(_dot_nt(jnp.concatenate([qn[h0], qn[h1]], axis=1).astype(BF16), rhs))
        kk = jnp.concatenate(kk, axis=1)
        scores = jnp.concatenate(qk, axis=1) * decay
        a_cat = jnp.where(strict, b_c * kk * decay, 0.0)

        t_cat = eye_cat - a_cat
        pw = mm_cat(a_cat, a_cat)
        n_sq = chunk.bit_length() - 2
        for i in range(n_sq):
            t_cat = t_cat + mm_cat(t_cat, pw)
            if i + 1 < n_sq:
                pw = mm_cat(pw, pw)

        for p in range(GDN_HEADS // 2):
            heads = (2 * p, 2 * p + 1)
            col1 = lambda x, h: x[:, h * chunk:h * chunk + 1]
            rhs = _two_block_diag(*[jnp.concatenate(
                [vh[h] * col1(b_c, h), kn[h] * (col1(b_c, h) * col1(e_in, h))], axis=1) for h in heads])
            uw = _dot3(t_cat[:, p * 2 * chunk:(p + 1) * 2 * chunk], rhs)
            v_new, q_s = [], []
            for i, h in enumerate(heads):
                u = uw[:, (2 * i) * GDN_DV:(2 * i + 1) * GDN_DV]
                w = uw[:, (2 * i + 1) * GDN_DV:(2 * i + 2) * GDN_DV]
                wq = jnp.concatenate([w, qn[h] * col1(e_in, h)], axis=0).astype(BF16)
                ws = jnp.dot(wq, s_ref[0, h].astype(BF16), preferred_element_type=F32)
                v_new.append(u - ws[:chunk])
                q_s.append(ws[chunk:])
            intra = jnp.dot(scores[:, p * 2 * chunk:(p + 1) * 2 * chunk].astype(BF16),
                            _two_block_diag(*v_new).astype(BF16), preferred_element_type=F32)
            for i, h in enumerate(heads):
                o = q_s[i] + intra[:, i * GDN_DV:(i + 1) * GDN_DV]
                s_ref[0, h] = s_ref[0, h] * col1(e_all, h) + _dot_tn(
                    (kn[h] * col1(e_out, h)).astype(BF16), v_new[i].astype(BF16))
                o = o * lax.rsqrt(jnp.mean(o * o, axis=-1, keepdims=True) + EPS) * nw_ref[...]
                hc = slice(h * GDN_DV, (h + 1) * GDN_DV)
                o_ref[r, hc] = (o * _silu(z_ref[r, hc])).astype(o_ref.dtype)
        return carry

    lax.fori_loop(0, q_ref.shape[0] // chunk, body, 0)


def _gdn_params(lw, chunk):
    cat = GDN_HEADS * chunk
    lanes = jnp.arange(cat) // chunk
    sel = jnp.zeros((128, 2 * cat), F32)
    sel = sel.at[NARROW_B + lanes, jnp.arange(cat)].set(1.0)
    sel = sel.at[NARROW_A + lanes, cat + jnp.arange(cat)].set(1.0)
    return (lw['gdn_conv_w'].astype(F32), _lane_row(lw['gdn_dt_bias'], NARROW_A),
            _lane_row(-jnp.exp(lw['gdn_a_log'].astype(F32)), NARROW_A),
            lw['gdn_norm_w'].astype(F32).reshape(1, GDN_DV), jnp.tril(jnp.ones((chunk, chunk), F32)), sel)


def gdn_prompt(y, n_seq, seq_len, lw):
    chunk = min(GDN_CHUNK, seq_len)
    params = _gdn_params(lw, chunk)
    tok = lambda s: pl.BlockSpec((seq_len, GDN_QKV), lambda b: (b, PK_OFF[s] // GDN_QKV))
    out, s, tail = pl.pallas_call(
        functools.partial(_gdn_prompt_kernel, chunk=chunk),
        grid=(n_seq,),
        in_specs=[tok(1), tok(2), tok(3), tok(4), pl.BlockSpec((seq_len, 128), lambda b: (b, PK_NARROW // 128))]
                 + [_const_spec(p.shape) for p in params],
        out_specs=[pl.BlockSpec((seq_len, GDN_QKV), lambda b: (b, 0)),
                   pl.BlockSpec((1, GDN_HEADS, GDN_DK, GDN_DV), lambda b: (b, 0, 0, 0)),
                   pl.BlockSpec((1, TAIL, GDN_CONV_DIM), lambda b: (b, 0, 0))],
        out_shape=[jax.ShapeDtypeStruct((n_seq * seq_len, GDN_QKV), BF16),
                   jax.ShapeDtypeStruct((n_seq, GDN_HEADS, GDN_DK, GDN_DV), F32),
                   jax.ShapeDtypeStruct((n_seq, TAIL, GDN_CONV_DIM), F32)],
        compiler_params=pltpu.CompilerParams(
            dimension_semantics=("arbitrary",), vmem_limit_bytes=V7X_VMEM_LIMIT),
        name="gdn_prompt",
    )(y, y, y, y, y, *params)
    return out, tail[:, TAIL - (CONV_W - 1):], s


def _conv_steps(buf_ref, raw_ref, w_ref, cols):
    n_t = raw_ref.shape[0]
    xx = [buf_ref[j, :, cols] for j in range(CONV_W - 1)] + [raw_ref[t] for t in range(n_t)]
    w = w_ref[:, cols]
    out = []
    for t in range(n_t):
        y = xx[t] * w[0:1, :]
        for j in range(1, CONV_W):
            y = y + xx[t + j] * w[j:j + 1, :]
        out.append(y)
    return out, xx[n_t:]


def _ssd_sample_kernel(z_ref, xbc_ref, nar_ref, buf_ref, s0_ref, cw_ref, cb_ref, dtb_ref, a_ref, dsk_ref, nw_ref,
                       o_ref, s_ref, nbuf_ref, qd_scr, kd_scr, xs_scr, ga_scr, acc_scr, lq, lk, lx):
    n_t, n_b = z_ref.shape[0], z_ref.shape[1]
    rep = SSD_HEADS // SSD_NGROUPS
    conv, tail = _conv_steps(buf_ref, xbc_ref, cw_ref, slice(0, SSD_CONV_DIM))
    for j in range(CONV_W - 1):
        nbuf_ref[j] = tail[j]
    xbc = [_silu(c + cb_ref[...]) for c in conv]
    dt, g = [], []
    for t in range(n_t):
        dt.append(_softplus(nar_ref[t] + dtb_ref[...]))
        la = dt[t] * a_ref[...]
        g.append(la if t == 0 else g[t - 1] + la)
    for grp in range(SSD_NGROUPS):
        bm = [x[:, SSD_INNER + grp * SSD_STATE:SSD_INNER + (grp + 1) * SSD_STATE] for x in xbc]
        cm = [x[:, SSD_INNER + SSD_BC + grp * SSD_STATE:SSD_INNER + SSD_BC + (grp + 1) * SSD_STATE] for x in xbc]
        sc = [[jnp.sum(cm[i] * bm[j], axis=-1, keepdims=True) for j in range(i + 1)] for i in range(n_t)]
        for h in range(grp * rep, (grp + 1) * rep):
            ln = NARROW_DT + h
            hc = slice(h * SSD_HEADDIM, (h + 1) * SSD_HEADDIM)
            gc = [x[:, ln:ln + 1] for x in g]
            dc = [x[:, ln:ln + 1] for x in dt]
            xs = [x[:, hc] for x in xbc]
            for t in range(n_t):
                qd_scr[h, t] = cm[t] * jnp.exp(gc[t])
                kd_scr[h, t] = bm[t] * (dc[t] * jnp.exp(gc[n_t - 1] - gc[t]))
                xs_scr[h, t] = xs[t]
                acc = None
                for j in range(t + 1):
                    term = (sc[t][j] * dc[j] * jnp.exp(gc[t] - gc[j])) * xs[j]
                    acc = term if acc is None else acc + term
                acc_scr[t, :, hc] = acc
            ga_scr[h] = jnp.broadcast_to(jnp.exp(gc[n_t - 1]), (n_b, SSD_HEADDIM))

    for tile in (lq, lk, lx):
        tile[...] = jnp.zeros_like(tile)

    def per_seq(b, carry):
        row = pl.ds(b, 1)
        for h in range(SSD_HEADS):
            hc = slice(h * SSD_HEADDIM, (h + 1) * SSD_HEADDIM)
            for t in range(n_t):
                lq[h, t:t + 1, :] = qd_scr[h, t, row, :]
                lk[h, t:t + 1, :] = kd_scr[h, t, row, :]
                lx[h, t:t + 1, :] = xs_scr[h, t, row, :]
            s0 = s0_ref[b, h]
            inter = jnp.dot(lq[h].astype(BF16), s0.astype(BF16), preferred_element_type=F32)
            s_ref[b, h] = s0 * ga_scr[h, row, :] + _dot_tn(lk[h].astype(BF16), lx[h].astype(BF16))
            for t in range(n_t):
                xs_scr[h, t, row, :] = inter[t:t + 1, :]
        return carry

    lax.fori_loop(0, n_b, per_seq, 0)

    gw = SSD_INNER // SSD_NGROUPS
    for t in range(n_t):
        for h in range(SSD_HEADS):
            hc = slice(h * SSD_HEADDIM, (h + 1) * SSD_HEADDIM)
            acc_scr[t, :, hc] = acc_scr[t, :, hc] + xs_scr[h, t]
        y = (acc_scr[t] + xbc[t][:, :SSD_INNER] * dsk_ref[...]) * _silu(z_ref[t])
        yn = [y[:, i * gw:(i + 1) * gw] * lax.rsqrt(
            jnp.mean(y[:, i * gw:(i + 1) * gw] * y[:, i * gw:(i + 1) * gw], axis=-1, keepdims=True) + EPS)
            for i in range(SSD_NGROUPS)]
        o_ref[t] = (jnp.concatenate(yn, axis=1) * nw_ref[...]).astype(o_ref.dtype)


def ssd_sample(ys, buf0, s0, lw):
    n_t, n_b, _ = ys.shape
    params = _ssd_params(lw, 1)[:-1]
    tok = lambda s, w: pl.BlockSpec((n_t, SAMPLE_BB, w), lambda i: (0, i, PK_OFF[s] // w))
    bufspec = pl.BlockSpec((CONV_W - 1, SAMPLE_BB, SSD_CONV_DIM), lambda i: (0, i, 0))
    state = pl.BlockSpec((SAMPLE_BB, SSD_HEADS, SSD_STATE, SSD_HEADDIM), lambda i: (i, 0, 0, 0))
    per_tok = lambda w: pltpu.VMEM((SSD_HEADS, n_t, SAMPLE_BB, w), F32)
    tile = lambda w: pltpu.VMEM((SSD_HEADS, LHS_ROWS, w), F32)
    out, s, nbuf = pl.pallas_call(
        _ssd_sample_kernel,
        grid=(n_b // SAMPLE_BB,),
        in_specs=[tok(7, SSD_INNER), tok(8, SSD_CONV_DIM),
                  pl.BlockSpec((n_t, SAMPLE_BB, 128), lambda i: (0, i, PK_NARROW // 128)), bufspec, state]
                 + [_const_spec(p.shape) for p in params],
        out_specs=[pl.BlockSpec((n_t, SAMPLE_BB, SSD_INNER), lambda i: (0, i, 0)), state, bufspec],
        out_shape=[jax.ShapeDtypeStruct((n_t, n_b, SSD_INNER), BF16), jax.ShapeDtypeStruct(s0.shape, F32),
                   jax.ShapeDtypeStruct((CONV_W - 1, n_b, SSD_CONV_DIM), F32)],
        scratch_shapes=[per_tok(SSD_STATE), per_tok(SSD_STATE), per_tok(SSD_HEADDIM),
                        pltpu.VMEM((SSD_HEADS, SAMPLE_BB, SSD_HEADDIM), F32),
                        pltpu.VMEM((n_t, SAMPLE_BB, SSD_INNER), F32),
                        tile(SSD_STATE), tile(SSD_STATE), tile(SSD_HEADDIM)],
        compiler_params=pltpu.CompilerParams(
            dimension_semantics=("arbitrary",), vmem_limit_bytes=V7X_VMEM_LIMIT),
        name="ssd_sample",
    )(ys, ys, ys, jnp.swapaxes(buf0, 0, 1), s0, *params)
    return out, jnp.swapaxes(nbuf, 0, 1), s


def _gdn_sample_kernel(q_ref, k_ref, v_ref, z_ref, nar_ref, buf_ref, s0_ref, cw_ref, dtb_ref, a_ref, nw_ref,
                       o_ref, s_ref, nbuf_ref, w_scr, qd_scr, kd_scr, u_scr, ga_scr, lwq, lk, lu):
    n_t, n_b = q_ref.shape[0], q_ref.shape[1]
    qkv = []
    for i, ref in enumerate((q_ref, k_ref, v_ref)):
        cols = slice(i * GDN_QKV, (i + 1) * GDN_QKV)
        conv, tail = _conv_steps(buf_ref, ref, cw_ref, cols)
        for j in range(CONV_W - 1):
            nbuf_ref[j, :, cols] = tail[j]
        qkv.append([_silu(c) for c in conv])
    beta, g = [], []
    for t in range(n_t):
        nar = nar_ref[t]
        beta.append(jax.nn.sigmoid(nar))
        gl = a_ref[...] * _softplus(nar + dtb_ref[...])
        g.append(gl if t == 0 else g[t - 1] + gl)

    scores = []
    for h in range(GDN_HEADS):
        hc = slice(h * GDN_DK, (h + 1) * GDN_DK)
        qn = [_l2_rows(x[:, hc]) * (GDN_DK ** -0.5) for x in qkv[0]]
        kn = [_l2_rows(x[:, hc]) for x in qkv[1]]
        vh = [x[:, hc] for x in qkv[2]]
        bc = [x[:, NARROW_B + h:NARROW_B + h + 1] for x in beta]
        gc = [x[:, NARROW_A + h:NARROW_A + h + 1] for x in g]
        us, ws = [], []
        for i in range(n_t):
            u = vh[i] * bc[i]
            w = kn[i] * (bc[i] * jnp.exp(gc[i]))
            for j in range(i):
                a_ij = bc[i] * jnp.sum(kn[i] * kn[j], axis=-1, keepdims=True) * jnp.exp(gc[i] - gc[j])
                u = u - a_ij * us[j]
                w = w - a_ij * ws[j]
            us.append(u)
            ws.append(w)
            u_scr[h, i] = u
            w_scr[h, i] = w
            qd_scr[h, i] = qn[i] * jnp.exp(gc[i])
            kd_scr[h, i] = kn[i] * jnp.exp(gc[n_t - 1] - gc[i])
        ga_scr[h] = jnp.broadcast_to(jnp.exp(gc[n_t - 1]), (n_b, GDN_DV))
        scores.append([[jnp.sum(qn[i] * kn[j], axis=-1, keepdims=True) * jnp.exp(gc[i] - gc[j])
                        for j in range(i + 1)] for i in range(n_t)])

    for tile in (lwq, lk, lu):
        tile[...] = jnp.zeros_like(tile)

    def per_seq(b, carry):
        row = pl.ds(b, 1)
        for h in range(GDN_HEADS):
            for t in range(n_t):
                lwq[h, t:t + 1, :] = w_scr[h, t, row, :]
                lwq[h, LHS_ROWS + t:LHS_ROWS + t + 1, :] = qd_scr[h, t, row, :]
                lk[h, t:t + 1, :] = kd_scr[h, t, row, :]
                lu[h, t:t + 1, :] = u_scr[h, t, row, :]
            s0 = s0_ref[b, h]
            wq_s = jnp.dot(lwq[h].astype(BF16), s0.astype(BF16), preferred_element_type=F32)
            v_new = lu[h] - wq_s[:LHS_ROWS]
            s_ref[b, h] = s0 * ga_scr[h, row, :] + _dot_tn(lk[h].astype(BF16), v_new.astype(BF16))
            for t in range(n_t):
                u_scr[h, t, row, :] = v_new[t:t + 1, :]
                qd_scr[h, t, row, :] = wq_s[LHS_ROWS + t:LHS_ROWS + t + 1, :]
        return carry

    lax.fori_loop(0, n_b, per_seq, 0)

    for h in range(GDN_HEADS):
        hc = slice(h * GDN_DV, (h + 1) * GDN_DV)
        for i in range(n_t):
            o = qd_scr[h, i]
            for j in range(i + 1):
                o = o + scores[h][i][j] * u_scr[h, j]
            o = o * lax.rsqrt(jnp.mean(o * o, axis=-1, keepdims=True) + EPS) * nw_ref[...]
            o_ref[i, :, hc] = (o * _silu(z_ref[i, :, hc])).astype(o_ref.dtype)


def gdn_sample(ys, buf0, s0, lw):
    n_t, n_b, _ = ys.shape
    params = _gdn_params(lw, 1)[:4]
    tok = lambda s: pl.BlockSpec((n_t, SAMPLE_BB, GDN_QKV), lambda i: (0, i, PK_OFF[s] // GDN_QKV))
    bufspec = pl.BlockSpec((CONV_W - 1, SAMPLE_BB, GDN_CONV_DIM), lambda i: (0, i, 0))
    state = pl.BlockSpec((SAMPLE_BB, GDN_HEADS, GDN_DK, GDN_DV), lambda i: (i, 0, 0, 0))
    per_tok = pltpu.VMEM((GDN_HEADS, n_t, SAMPLE_BB, GDN_DK), F32)
    tile = lambda rows: pltpu.VMEM((GDN_HEADS, rows, GDN_DK), F32)
    out, s, nbuf = pl.pallas_call(
        _gdn_sample_kernel,
        grid=(n_b // SAMPLE_BB,),
        in_specs=[tok(1), tok(2), tok(3), tok(4),
                  pl.BlockSpec((n_t, SAMPLE_BB, 128), lambda i: (0, i, PK_NARROW // 128)), bufspec, state]
                 + [_const_spec(p.shape) for p in params],
        out_specs=[pl.BlockSpec((n_t, SAMPLE_BB, GDN_QKV), lambda i: (0, i, 0)), state, bufspec],
        out_shape=[jax.ShapeDtypeStruct((n_t, n_b, GDN_QKV), BF16), jax.ShapeDtypeStruct(s0.shape, F32),
                   jax.ShapeDtypeStruct((CONV_W - 1, n_b, GDN_CONV_DIM), F32)],
        scratch_shapes=[per_tok, per_tok, per_tok, per_tok, pltpu.VMEM((GDN_HEADS, SAMPLE_BB, GDN_DV), F32),
                        tile(2 * LHS_ROWS), tile(LHS_ROWS), tile(LHS_ROWS)],
        compiler_params=pltpu.CompilerParams(
            dimension_semantics=("arbitrary",), vmem_limit_bytes=V7X_VMEM_LIMIT),
        name="gdn_sample",
    )(ys, ys, ys, ys, ys, jnp.swapaxes(buf0, 0, 1), s0, *params)
    return out, jnp.swapaxes(nbuf, 0, 1), s


def split_last(t, sizes):
    return jnp.split(t, [int(s) for s in np.cumsum(sizes)[:-1]], axis=-1)


def l2_normalize(x):
    xf = x.astype(jnp.float32)
    return xf * lax.rsqrt(jnp.sum(xf * xf, axis=-1, keepdims=True) + EPS)


def causal_conv(x, buf, w):
    L = x.shape[1]
    xp = jnp.concatenate([buf.astype(x.dtype), x], axis=1)
    y = xp[:, 0:L] * w[0]
    for j in range(1, CONV_W):
        y = y + xp[:, j:j + L] * w[j]
    return y, xp[:, L:]


def rotary(x, positions):
    half = x.shape[-1] // 2
    inv_freq = ROPE_BASE ** (-jnp.arange(half, dtype=jnp.float32) / half)
    ang = positions.astype(jnp.float32)[:, None] * inv_freq[None, :]
    cos = jnp.cos(ang)[None, :, None, :]
    sin = jnp.sin(ang)[None, :, None, :]
    xf = x.astype(jnp.float32)
    x1, x2 = xf[..., :half], xf[..., half:]
    return jnp.concatenate([x1 * cos - x2 * sin, x1 * sin + x2 * cos], axis=-1)


def to_chunks(t, chunk):
    L = t.shape[1]
    pad = (-L) % chunk
    t = jnp.pad(t, [(0, 0), (0, pad)] + [(0, 0)] * (t.ndim - 2))
    n = t.shape[1] // chunk
    t = t.reshape((t.shape[0], n, chunk) + t.shape[2:])
    return jnp.moveaxis(t, 2, 3)


def from_chunks(t, L):
    t = jnp.moveaxis(t, 3, 2)
    t = t.reshape((t.shape[0], t.shape[1] * t.shape[2]) + t.shape[3:])
    return t[:, :L]


def intra_decay(G):
    C = G.shape[-1]
    causal = jnp.tril(jnp.ones((C, C), dtype=bool))
    diff = G[..., :, None] - G[..., None, :]
    return jnp.where(causal, jnp.exp(jnp.where(causal, diff, 0.0)), 0.0)


def decay_linear_attention(q, k, v, log_a, s0, chunk):
    L = q.shape[1]
    C = min(chunk, L)
    qc, kc, vc = (to_chunks(t.astype(jnp.float32), C) for t in (q, k, v))
    G = jnp.cumsum(to_chunks(log_a.astype(jnp.float32), C), axis=-1)
    G_last = G[..., -1]
    scores = jnp.einsum('bnhid,bnhjd->bnhij', qc, kc) * intra_decay(G)
    intra = jnp.einsum('bnhij,bnhjv->bnhiv', scores, vc)
    chunk_states = jnp.einsum('bnhcd,bnhcv->bnhdv', kc * jnp.exp(G_last[..., None] - G)[..., None], vc)

    def step(S, inp):
        cs, gl = inp
        return S * gl[..., None, None] + cs, S

    s_final, s_prev = lax.scan(step, s0.astype(jnp.float32),
                               (jnp.moveaxis(chunk_states, 1, 0), jnp.moveaxis(jnp.exp(G_last), 1, 0)))
    s_prev = jnp.moveaxis(s_prev, 0, 1)
    inter = jnp.einsum('bnhcd,bnhdv->bnhcv', qc * jnp.exp(G)[..., None], s_prev)
    return from_chunks(intra + inter, L), s_final


def gated_delta_rule(q, k, v, beta, g, s0, chunk):
    L = q.shape[1]
    C = min(chunk, L)
    qc, kc, vc = (to_chunks(t.astype(jnp.float32), C) for t in (q, k, v))
    bc = to_chunks(beta.astype(jnp.float32), C)
    G = jnp.cumsum(to_chunks(g.astype(jnp.float32), C), axis=-1)
    decay = intra_decay(G)
    strict = jnp.tril(jnp.ones((C, C), dtype=bool), -1)
    kk = jnp.einsum('bnhid,bnhjd->bnhij', kc, kc)
    a_mat = jnp.where(strict, bc[..., :, None] * kk * decay, 0.0) + jnp.eye(C, dtype=jnp.float32)
    u = lax.linalg.triangular_solve(a_mat, vc * bc[..., None], left_side=True, lower=True, unit_diagonal=True)
    w = lax.linalg.triangular_solve(a_mat, kc * (bc * jnp.exp(G))[..., None], left_side=True, lower=True,
                                    unit_diagonal=True)
    scores = jnp.einsum('bnhid,bnhjd->bnhij', qc, kc) * decay
    q_dec = qc * jnp.exp(G)[..., None]
    k_dec = kc * jnp.exp(G[..., -1:] - G)[..., None]
    g_last = jnp.exp(G[..., -1])

    def step(S, inp):
        u_c, w_c, s_c, qd_c, kd_c, gl_c = inp
        v_new = u_c - jnp.einsum('bhcd,bhdv->bhcv', w_c, S)
        o = jnp.einsum('bhcd,bhdv->bhcv', qd_c, S) + jnp.einsum('bhij,bhjv->bhiv', s_c, v_new)
        S = S * gl_c[..., None, None] + jnp.einsum('bhcd,bhcv->bhdv', kd_c, v_new)
        return S, o

    xs = tuple(jnp.moveaxis(t, 1, 0) for t in (u, w, scores, q_dec, k_dec, g_last))
    s_final, o = lax.scan(step, s0.astype(jnp.float32), xs)
    return from_chunks(jnp.moveaxis(o, 0, 1), L), s_final


def complex_affine_combine(e1, e2):
    a1r, a1i, b1r, b1i = e1
    a2r, a2i, b2r, b2i = e2
    return (a2r * a1r - a2i * a1i, a2r * a1i + a2i * a1r,
            a2r * b1r - a2i * b1i + b2r, a2r * b1i + a2i * b1r + b2i)


def s5_mixer(u, h_re0, h_im0, lw):
    f32 = jnp.float32
    bsz, L, _ = u.shape
    a_re = lw['s5_a_re'].astype(f32)
    a_im = lw['s5_a_im'].astype(f32)
    step = jnp.exp(lw['s5_log_step'].astype(f32))[:, None]
    mag = jnp.exp(a_re * step)
    lam_re = mag * jnp.cos(a_im * step)
    lam_im = mag * jnp.sin(a_im * step)
    den = a_re * a_re + a_im * a_im
    coef_re = ((lam_re - 1.0) * a_re + lam_im * a_im) / den
    coef_im = (lam_im * a_re - (lam_re - 1.0) * a_im) / den
    b_re = lw['s5_b_re'].astype(f32)
    b_im = lw['s5_b_im'].astype(f32)
    bb_re = coef_re[..., None] * b_re - coef_im[..., None] * b_im
    bb_im = coef_re[..., None] * b_im + coef_im[..., None] * b_re
    ug = u.astype(f32).reshape(bsz, L, S5_GROUPS, S5_GROUP_CH)
    drive_re = jnp.einsum('blgc,gnc->blgn', ug, bb_re)
    drive_im = jnp.einsum('blgc,gnc->blgn', ug, bb_im)
    h0_re = h_re0.astype(f32)
    h0_im = h_im0.astype(f32)
    drive_re = drive_re.at[:, 0].add(lam_re * h0_re - lam_im * h0_im)
    drive_im = drive_im.at[:, 0].add(lam_re * h0_im + lam_im * h0_re)
    lam_re_b = jnp.broadcast_to(lam_re, drive_re.shape)
    lam_im_b = jnp.broadcast_to(lam_im, drive_im.shape)
    _, _, hs_re, hs_im = lax.associative_scan(complex_affine_combine, (lam_re_b, lam_im_b, drive_re, drive_im), axis=1)
    c_re = lw['s5_c_re'].astype(f32)
    c_im = lw['s5_c_im'].astype(f32)
    y = jnp.einsum('blgn,gcn->blgc', hs_re, c_re) - jnp.einsum('blgn,gcn->blgc', hs_im, c_im)
    y = y.reshape(bsz, L, S5_CH) + lw['s5_d'].astype(f32) * u.astype(f32)
    y = jax.nn.gelu(y)
    out = y * jax.nn.sigmoid(y @ lw['s5_w_glu'].astype(f32) + lw['s5_b_glu'].astype(f32))
    return out.astype(u.dtype), hs_re[:, -1], hs_im[:, -1]


def gdn_mixer(q, k, v, z, b_logit, a_logit, buf0, s0, lw):
    dt_out = z.dtype
    bsz, L, _ = q.shape
    qkv, buf = causal_conv(jnp.concatenate([q, k, v], axis=-1), buf0, lw['gdn_conv_w'])
    qkv = jax.nn.silu(qkv)
    qh, kh, vh = split_last(qkv, (GDN_HEADS * GDN_DK, GDN_HEADS * GDN_DK, GDN_HEADS * GDN_DV))
    qh = l2_normalize(qh.reshape(bsz, L, GDN_HEADS, GDN_DK)) * (GDN_DK ** -0.5)
    kh = l2_normalize(kh.reshape(bsz, L, GDN_HEADS, GDN_DK))
    vh = vh.reshape(bsz, L, GDN_HEADS, GDN_DV)
    beta = jax.nn.sigmoid(b_logit.astype(jnp.float32))
    g = -jnp.exp(lw['gdn_a_log'].astype(jnp.float32)) * jax.nn.softplus(
        a_logit.astype(jnp.float32) + lw['gdn_dt_bias'].astype(jnp.float32))
    o, s = gated_delta_rule(qh, kh, vh, beta, g, s0, GDN_CHUNK)
    o = o * lax.rsqrt(jnp.mean(o * o, axis=-1, keepdims=True) + EPS) * lw['gdn_norm_w'].astype(jnp.float32)
    o = o * jax.nn.silu(z.astype(jnp.float32).reshape(bsz, L, GDN_HEADS, GDN_DV))
    return o.reshape(bsz, L, GDN_HEADS * GDN_DV).astype(dt_out), buf, s


def ssd_mixer(z, xbc, dt_raw, buf0, s0, lw):
    f32 = jnp.float32
    bsz, L, _ = z.shape
    xbc, buf = causal_conv(xbc, buf0, lw['ssd_conv_w'])
    xbc = jax.nn.silu(xbc + lw['ssd_conv_b'])
    xs, bm, cm = split_last(xbc, (SSD_INNER, SSD_NGROUPS * SSD_STATE, SSD_NGROUPS * SSD_STATE))
    rep = SSD_HEADS // SSD_NGROUPS
    xs = xs.astype(f32).reshape(bsz, L, SSD_HEADS, SSD_HEADDIM)
    bm = jnp.repeat(bm.astype(f32).reshape(bsz, L, SSD_NGROUPS, SSD_STATE), rep, axis=2)
    cm = jnp.repeat(cm.astype(f32).reshape(bsz, L, SSD_NGROUPS, SSD_STATE), rep, axis=2)
    dt = jax.nn.softplus(dt_raw.astype(f32) + lw['ssd_dt_bias'].astype(f32))
    a = -jnp.exp(lw['ssd_a_log'].astype(f32))
    y, s = decay_linear_attention(cm, bm * dt[..., None], xs, dt * a, s0, SSD_CHUNK)
    y = y + xs * lw['ssd_d'].astype(f32)[:, None]
    y = y.reshape(bsz, L, SSD_INNER) * jax.nn.silu(z.astype(f32))
    y = y.reshape(bsz, L, SSD_NGROUPS, SSD_INNER // SSD_NGROUPS)
    y = y * lax.rsqrt(jnp.mean(y * y, axis=-1, keepdims=True) + EPS)
    y = y.reshape(bsz, L, SSD_INNER) * lw['ssd_norm_w'].astype(f32)
    return y.astype(z.dtype), buf, s


def retention_mixer(q, k, v, gate, s0, positions, lw):
    f32 = jnp.float32
    bsz, L, _ = q.shape
    qh = rotary(q.reshape(bsz, L, RET_HEADS, RET_DK), positions)
    kh = rotary(k.reshape(bsz, L, RET_HEADS, RET_DK), positions) * (RET_DK ** -0.5)
    vh = v.reshape(bsz, L, RET_HEADS, RET_DV)
    log_gamma = jnp.log(1.0 - 2.0 ** (-5.0 - jnp.arange(RET_HEADS, dtype=f32)))
    log_a = jnp.broadcast_to(log_gamma, (bsz, L, RET_HEADS))
    o, s = decay_linear_attention(qh, kh, vh, log_a, s0, RET_CHUNK)
    mu = jnp.mean(o, axis=-1, keepdims=True)
    var = jnp.mean((o - mu) ** 2, axis=-1, keepdims=True)
    o = ((o - mu) * lax.rsqrt(var + EPS)).reshape(bsz, L, RET_HEADS * RET_DV)
    o = o * lw['ret_ln_w'].astype(f32) + lw['ret_ln_b'].astype(f32)
    out = jax.nn.silu(gate.astype(f32)) * o
    return out.astype(q.dtype), s


def _mixers(y, st, lw, positions):
    def col(s):
        return y[..., PK_OFF[s]:PK_OFF[s] + IN_SIZES[s]]

    gdn_s0, gdn_buf0, ssd_s0, ssd_buf0, ret_s0 = st
    out_b, gdn_buf, gdn_s = gdn_mixer(col(1), col(2), col(3), col(4), col(5), col(6), gdn_buf0, gdn_s0, lw)
    out_c, ssd_buf, ssd_s = ssd_mixer(col(7), col(8), col(9), ssd_buf0, ssd_s0, lw)
    out_d, ret_s = retention_mixer(col(10), col(11), col(12), col(13), ret_s0, positions, lw)
    mixed = jnp.concatenate([out_b, out_c, out_d], axis=-1).astype(BF16)
    return mixed, (gdn_s, gdn_buf, ssd_s, ssd_buf, ret_s)


TM = 1088


def kernel(x_prompt, x_sample, p_prompt, p_sample, state_s5_re, state_s5_im, state_gdn, state_gdn_conv, state_ssd, state_ssd_conv, state_ret, norm_mix, w_in, s5_a_re, s5_a_im, s5_b_re, s5_b_im, s5_c_re, s5_c_im, s5_d, s5_log_step, s5_w_glu, s5_b_glu, gdn_conv_w, gdn_a_log, gdn_dt_bias, gdn_norm_w, ssd_conv_w, ssd_conv_b, ssd_dt_bias, ssd_a_log, ssd_d, ssd_norm_w, ret_ln_w, ret_ln_b, w_out, norm_ffn, w_ffn_in, w_ffn_out, norm_ple, w_ple_gate, w_ple_proj, norm_final):
    bp, lp, d = x_prompt.shape
    bs, ls, _ = x_sample.shape
    np_tok = bp * lp
    ns_tok = bs * ls

    mixer_w = dict(
        s5_a_re=s5_a_re, s5_a_im=s5_a_im, s5_b_re=s5_b_re, s5_b_im=s5_b_im, s5_c_re=s5_c_re, s5_c_im=s5_c_im,
        s5_d=s5_d, s5_log_step=s5_log_step, s5_w_glu=s5_w_glu, s5_b_glu=s5_b_glu,
        gdn_conv_w=gdn_conv_w, gdn_a_log=gdn_a_log, gdn_dt_bias=gdn_dt_bias, gdn_norm_w=gdn_norm_w,
        ssd_conv_w=ssd_conv_w, ssd_conv_b=ssd_conv_b, ssd_dt_bias=ssd_dt_bias, ssd_a_log=ssd_a_log,
        ssd_d=ssd_d, ssd_norm_w=ssd_norm_w, ret_ln_w=ret_ln_w, ret_ln_b=ret_ln_b)

    assert bp == 4, "the prompt S5 kernel packs two time steps of four sequences per vreg"

    def tm_rows(t):
        return jnp.swapaxes(t, 0, 1).reshape(t.shape[0] * t.shape[1], t.shape[2])

    def bm_seqs(t, b):
        return jnp.swapaxes(t.reshape(t.shape[0] // b, b, t.shape[1]), 0, 1)

    h = jnp.concatenate([x_prompt.reshape(np_tok, d), tm_rows(x_sample)], axis=0)
    new_p, new_s = [], []
    y_final = None
    for i in range(DEPTH):
        lw = {k: v[i] for k, v in mixer_w.items()}
        y = rms_matmul(h, norm_mix[i], _pack_w_in(w_in[i]), tm=TM, tn=1280)
        ys = y[np_tok:].reshape(ls, bs, PK_TOTAL)

        tb = _s5_tables(lw)
        a_p, st5_p = s5_prompt(tm_rows(y[:np_tok, :S5_CH].reshape(bp, lp, S5_CH)), tb, rows=512)
        a_p = bm_seqs(a_p, bp).reshape(np_tok, S5_CH)
        a_s, st5_s = s5_sample(y[np_tok:, :S5_CH], _s5_state_to_lanes(state_s5_re[i], state_s5_im[i]), tb)

        b_p, gdn_buf_p, gdn_s_p = gdn_prompt(y, bp, lp, lw)
        c_p, ssd_buf_p, ssd_s_p = ssd_prompt(y, bp, lp, lw)
        d_p, ret_s_p = ret_prompt(y, bp, lp, lw)
        b_s, gdn_buf_s, gdn_s_s = gdn_sample(ys, state_gdn_conv[i], state_gdn[i], lw)
        c_s, ssd_buf_s, ssd_s_s = ssd_sample(ys, state_ssd_conv[i], state_ssd[i], lw)
        d_s, ret_s_s = ret_sample(ys, state_ret[i], PAST_LEN, lw)
        new_p.append(_s5_lanes_to_state(st5_p[:bp]) + (gdn_s_p, gdn_buf_p, ssd_s_p, ssd_buf_p, ret_s_p))
        new_s.append(_s5_lanes_to_state(st5_s) + (gdn_s_s, gdn_buf_s, ssd_s_s, ssd_buf_s, ret_s_s))
        mixed = jnp.concatenate([
            jnp.concatenate([a_p, b_p, c_p, d_p], axis=1),
            jnp.concatenate([a_s] + [t.reshape(ns_tok, GROUP_WIDTH) for t in (b_s, c_s, d_s)], axis=1)], axis=0)
        h = matmul_residual(mixed, w_out[i].astype(BF16), h, tm=TM)
        h = ffn_residual(h, norm_ffn[i], w_ffn_in[i].astype(BF16), w_ffn_out[i].astype(BF16), tm=TM // 2, th=512)
        p = jnp.concatenate([p_prompt[i].reshape(np_tok, PLE_DIM), tm_rows(p_sample[i])], axis=0)
        final = i == DEPTH - 1
        outs = ple_residual(h, norm_ple[i], w_ple_gate[i].astype(BF16), p, w_ple_proj[i].astype(BF16),
                            norm_final, tm=TM // 2, final=final)
        h = outs[0]
        if final:
            y_final = outs[1]

    y_prompt = y_final[:np_tok].reshape(bp, lp, d)
    y_sample = bm_seqs(y_final[np_tok:], bs)
    stack_p = [jnp.stack([st[j] for st in new_p]) for j in range(7)]
    stack_s = [jnp.stack([st[j] for st in new_s]) for j in range(7)]
    return (y_prompt, y_sample, *stack_p, *stack_s)
```

```python
import functools
import math

import jax
import jax.numpy as jnp
import numpy as np
from jax import lax
from jax.experimental import pallas as pl
from jax.experimental.pallas import tpu as pltpu

F32 = jnp.float32
BF16 = jnp.bfloat16

D_MODEL = 2048
DEPTH = 2
GROUP_WIDTH = D_MODEL // 4
CONV_W = 4
EPS = 1e-6
PLE_DIM = 256
FFN_HIDDEN = ((8 * D_MODEL + 3 * 256 - 1) // (3 * 256)) * 256

S5_CH = GROUP_WIDTH
S5_GROUP_CH = 16
S5_GROUPS = S5_CH // S5_GROUP_CH
S5_STATE = 64

GDN_HEADS = 4
GDN_DK = GROUP_WIDTH // GDN_HEADS
GDN_DV = GROUP_WIDTH // GDN_HEADS
GDN_CHUNK = 64
GDN_CONV_DIM = 2 * GDN_HEADS * GDN_DK + GDN_HEADS * GDN_DV

SSD_INNER = GROUP_WIDTH
SSD_HEADDIM = 64
SSD_HEADS = SSD_INNER // SSD_HEADDIM
SSD_NGROUPS = 2
SSD_STATE = 128
SSD_CHUNK = 128
SSD_CONV_DIM = SSD_INNER + 2 * SSD_NGROUPS * SSD_STATE

RET_HEADS = 4
RET_DK = GROUP_WIDTH // RET_HEADS
RET_DV = GROUP_WIDTH // RET_HEADS
RET_CHUNK = 128
ROPE_BASE = 10000.0
PAST_LEN = 16384

IN_SIZES = (
    S5_CH,
    GDN_HEADS * GDN_DK, GDN_HEADS * GDN_DK, GDN_HEADS * GDN_DV, GDN_HEADS * GDN_DV, GDN_HEADS, GDN_HEADS,
    SSD_INNER, SSD_CONV_DIM, SSD_HEADS,
    RET_HEADS * RET_DK, RET_HEADS * RET_DK, RET_HEADS * RET_DV, RET_HEADS * RET_DV,
)
IN_OFFS = tuple(int(v) for v in np.cumsum((0,) + IN_SIZES))

_REGIONS = ((0, 1, 2, 3, 4), (7, 8), (10, 11, 12, 13))
_NARROW = (5, 6, 9)
PK_OFF = {}
_o = 0
for _reg in _REGIONS:
    for _s in _reg:
        PK_OFF[_s] = _o
        _o += IN_SIZES[_s]
PK_TOTAL = _o
IN_TN = 512
REGION_TILES = tuple(sum(IN_SIZES[s] for s in reg) // IN_TN for reg in _REGIONS)
NARROW_W = 128

V7X_VMEM_LIMIT = 58 * 1024 * 1024


def _split_w_in(w_in):
    wide = [w_in[:, IN_OFFS[reg[0]]:IN_OFFS[reg[-1] + 1]].astype(BF16) for reg in _REGIONS]
    nar = jnp.concatenate([w_in[:, IN_OFFS[s]:IN_OFFS[s + 1]] for s in _NARROW], axis=1)
    nar = jnp.pad(nar, ((0, 0), (0, NARROW_W - nar.shape[1]))).astype(BF16)
    return wide, nar


def _rms_rows(x, nw):
    ms = jnp.mean(x * x, axis=-1, keepdims=True)
    return x * lax.rsqrt(ms + EPS) * nw


def _in_proj_kernel(x_ref, nw_ref, wa_ref, wb_ref, wc_ref, wn_ref, o_ref, nar_ref, xn_ref):
    j = pl.program_id(1)

    @pl.when(j == 0)
    def _():
        xn_ref[...] = _rms_rows(x_ref[...], nw_ref[...]).astype(BF16)
        nar_ref[...] = jnp.dot(xn_ref[...], wn_ref[...], preferred_element_type=F32)

    first = 0
    for w_ref, n_tiles in zip((wa_ref, wb_ref, wc_ref), REGION_TILES):
        @pl.when((j >= first) & (j < first + n_tiles))
        def _(w_ref=w_ref):
            o_ref[...] = jnp.dot(xn_ref[...], w_ref[...], preferred_element_type=F32)
        first += n_tiles


def in_projection(x, nw, w_in, *, tm):
    m, k = x.shape
    wide, nar = _split_w_in(w_in)
    starts = [sum(REGION_TILES[:r]) for r in range(len(REGION_TILES))]

    def region_spec(r):
        return pl.BlockSpec((k, IN_TN), lambda i, j: (0, jnp.clip(j - starts[r], 0, REGION_TILES[r] - 1)))

    return pl.pallas_call(
        _in_proj_kernel,
        grid=(m // tm, PK_TOTAL // IN_TN),
        in_specs=[
            pl.BlockSpec((tm, k), lambda i, j: (i, 0)),
            pl.BlockSpec((1, k), lambda i, j: (0, 0)),
            region_spec(0), region_spec(1), region_spec(2),
            pl.BlockSpec((k, NARROW_W), lambda i, j: (0, 0)),
        ],
        out_specs=[pl.BlockSpec((tm, IN_TN), lambda i, j: (i, j)),
                   pl.BlockSpec((tm, NARROW_W), lambda i, j: (i, 0))],
        out_shape=[jax.ShapeDtypeStruct((m, PK_TOTAL), F32), jax.ShapeDtypeStruct((m, NARROW_W), F32)],
        scratch_shapes=[pltpu.VMEM((tm, k), BF16)],
        compiler_params=pltpu.CompilerParams(
            dimension_semantics=("arbitrary", "arbitrary"), vmem_limit_bytes=V7X_VMEM_LIMIT),
        name="in_projection",
    )(x, nw.reshape(1, k), *wide, nar)


def _mm_res_kernel(a_ref, w_ref, h_ref, o_ref):
    o_ref[...] = h_ref[...] + jnp.dot(a_ref[...], w_ref[...], preferred_element_type=F32)


def matmul_residual(a, w, h, *, tm):
    m, k = a.shape
    n = w.shape[1]
    return pl.pallas_call(
        _mm_res_kernel,
        grid=(m // tm,),
        in_specs=[
            pl.BlockSpec((tm, k), lambda i: (i, 0)),
            pl.BlockSpec((k, n), lambda i: (0, 0)),
            pl.BlockSpec((tm, n), lambda i: (i, 0)),
        ],
        out_specs=pl.BlockSpec((tm, n), lambda i: (i, 0)),
        out_shape=jax.ShapeDtypeStruct((m, n), F32),
        compiler_params=pltpu.CompilerParams(
            dimension_semantics=("arbitrary",), vmem_limit_bytes=V7X_VMEM_LIMIT),
        name="matmul_residual",
    )(a, w, h)


def _silu(x):
    return x * jax.nn.sigmoid(x)


def _ffn_kernel(h_ref, nw_ref, wg_ref, wu_ref, wo_ref, o_ref, xn_ref):
    @pl.when(pl.program_id(1) == 0)
    def _():
        h = h_ref[...]
        xn_ref[...] = _rms_rows(h, nw_ref[...]).astype(BF16)
        o_ref[...] = h

    xn = xn_ref[...]
    gate = jnp.dot(xn, wg_ref[...], preferred_element_type=F32)
    up = jnp.dot(xn, wu_ref[...], preferred_element_type=F32)
    act = (_silu(gate) * up).astype(BF16)
    o_ref[...] += jnp.dot(act, wo_ref[...], preferred_element_type=F32)


def ffn_residual(h, nw, w_in, w_out, *, tm, th):
    m, k = h.shape
    hidden = w_out.shape[0]
    nj = hidden // th
    return pl.pallas_call(
        _ffn_kernel,
        grid=(m // tm, nj),
        in_specs=[
            pl.BlockSpec((tm, k), lambda i, j: (i, 0)),
            pl.BlockSpec((1, k), lambda i, j: (0, 0)),
            pl.BlockSpec((k, th), lambda i, j: (0, j)),
            pl.BlockSpec((k, th), lambda i, j: (0, j + nj)),
            pl.BlockSpec((th, k), lambda i, j: (j, 0)),
        ],
        out_specs=pl.BlockSpec((tm, k), lambda i, j: (i, 0)),
        out_shape=jax.ShapeDtypeStruct((m, k), F32),
        scratch_shapes=[pltpu.VMEM((tm, k), BF16)],
        compiler_params=pltpu.CompilerParams(
            dimension_semantics=("arbitrary", "arbitrary"), vmem_limit_bytes=V7X_VMEM_LIMIT),
        name="ffn_residual",
    )(h, nw.reshape(1, k), w_in, w_in, w_out)


def _ple_kernel(h_ref, nw_ref, wg_ref, p_ref, wp_ref, nf_ref, o_ref, *y_ref):
    h = h_ref[...]
    xn = _rms_rows(h, nw_ref[...]).astype(BF16)
    gate = jax.nn.sigmoid(jnp.dot(xn, wg_ref[...], preferred_element_type=F32))
    proj = jnp.dot(p_ref[...].astype(BF16), wp_ref[...], preferred_element_type=F32)
    out = h + gate * proj
    o_ref[...] = out
    if y_ref:
        y_ref[0][...] = _rms_rows(out, nf_ref[...])


def ple_residual(h, nw, wg, p, wp, nf, *, tm, final):
    m, k = h.shape
    pd = p.shape[1]
    row = pl.BlockSpec((tm, k), lambda i: (i, 0))
    vec = pl.BlockSpec((1, k), lambda i: (0, 0))
    out_shape = [jax.ShapeDtypeStruct((m, k), F32)] * (2 if final else 1)
    return pl.pallas_call(
        _ple_kernel,
        grid=(m // tm,),
        in_specs=[
            row, vec,
            pl.BlockSpec((k, k), lambda i: (0, 0)),
            pl.BlockSpec((tm, pd), lambda i: (i, 0)),
            pl.BlockSpec((pd, k), lambda i: (0, 0)),
            vec,
        ],
        out_specs=[row] * len(out_shape),
        out_shape=out_shape,
        compiler_params=pltpu.CompilerParams(
            dimension_semantics=("arbitrary",), vmem_limit_bytes=V7X_VMEM_LIMIT),
        name="ple_residual",
    )(h, nw.reshape(1, k), wg, p, wp, nf.reshape(1, k))


S5_HALF_CH = S5_CH // 2
S5_HALF_ST = (S5_GROUPS // 2) * S5_STATE
S5_LANES = 4 * S5_HALF_ST
S5_SLAB = 512


def _s5_tables(lw):
    a_re = lw['s5_a_re'].astype(F32)
    a_im = lw['s5_a_im'].astype(F32)
    step = jnp.exp(lw['s5_log_step'].astype(F32))[:, None]
    mag = jnp.exp(a_re * step)
    lam_re = mag * jnp.cos(a_im * step)
    lam_im = mag * jnp.sin(a_im * step)
    den = a_re * a_re + a_im * a_im
    coef_re = ((lam_re - 1.0) * a_re + lam_im * a_im) / den
    coef_im = (lam_im * a_re - (lam_re - 1.0) * a_im) / den
    b_re = lw['s5_b_re'].astype(F32)
    b_im = lw['s5_b_im'].astype(F32)
    bb_re = coef_re[..., None] * b_re - coef_im[..., None] * b_im
    bb_im = coef_re[..., None] * b_im + coef_im[..., None] * b_re
    gh = S5_GROUPS // 2
    eye = jnp.eye(gh, dtype=F32)

    def in_blockdiag(b):
        return jnp.einsum('gnc,gh->gchn', b, eye).reshape(gh * S5_GROUP_CH, gh * S5_STATE)

    def out_blockdiag(c):
        return jnp.einsum('gcn,gh->gnhc', c, eye).reshape(gh * S5_STATE, gh * S5_GROUP_CH)

    c_re = lw['s5_c_re'].astype(F32)
    c_im = lw['s5_c_im'].astype(F32)
    bb = jnp.stack([jnp.concatenate([in_blockdiag(bb_re[h * gh:(h + 1) * gh]),
                                     in_blockdiag(bb_im[h * gh:(h + 1) * gh])], axis=1) for h in range(2)])
    cm = jnp.stack([jnp.concatenate([out_blockdiag(c_re[h * gh:(h + 1) * gh]),
                                     -out_blockdiag(c_im[h * gh:(h + 1) * gh])], axis=0) for h in range(2)])
    lam = jnp.stack([lam_re.reshape(-1), lam_im.reshape(-1)])
    lam2 = jnp.stack([lam[0] * lam[0] - lam[1] * lam[1], 2.0 * lam[0] * lam[1]])
    return dict(bb=bb.astype(BF16), cm=cm.astype(BF16), lam=lam, lam2=lam2,
                d=lw['s5_d'].astype(F32).reshape(1, S5_CH), wglu=lw['s5_w_glu'].astype(BF16),
                bglu=lw['s5_b_glu'].astype(F32).reshape(1, S5_CH))


def _s5_drive(u, bb_ref, sc_ref):
    ub = u.astype(BF16)
    for hf in range(2):
        sc_ref[:, hf * 2 * S5_HALF_ST:(hf + 1) * 2 * S5_HALF_ST] = jnp.dot(
            ub[:, hf * S5_HALF_CH:(hf + 1) * S5_HALF_CH], bb_ref[hf], preferred_element_type=F32)


def _s5_readout(sc_ref, u, cm_ref, d_ref, wglu_ref, bglu_ref):
    ys = [jnp.dot(sc_ref[:, hf * 2 * S5_HALF_ST:(hf + 1) * 2 * S5_HALF_ST].astype(BF16), cm_ref[hf],
                  preferred_element_type=F32) for hf in range(2)]
    y = jnp.concatenate(ys, axis=1) + d_ref[...] * u
    y = jax.nn.gelu(y)
    z = jnp.dot(y.astype(BF16), wglu_ref[...], preferred_element_type=F32) + bglu_ref[...]
    return y * jax.nn.sigmoid(z)


def _s5_slabs():
    for hf in range(2):
        for sl in range(S5_HALF_ST // S5_SLAB):
            re0 = hf * 2 * S5_HALF_ST + sl * S5_SLAB
            yield re0, re0 + S5_HALF_ST, hf * S5_HALF_ST + sl * S5_SLAB


def _s5_prompt_kernel(u_ref, bb_ref, m_ref, cm_ref, d_ref, wglu_ref, bglu_ref, o_ref, st_ref, sc_ref, carry_ref):
    @pl.when(pl.program_id(0) == 0)
    def _():
        carry_ref[...] = jnp.zeros_like(carry_ref)

    u = u_ref[...]
    _s5_drive(u, bb_ref, sc_ref)
    first_step = lax.broadcasted_iota(jnp.int32, (8, S5_SLAB), 0) < 4
    n_pairs = u_ref.shape[0] // 8
    for re0, im0, l0 in _s5_slabs():
        mr = m_ref[0, :, l0:l0 + S5_SLAB]
        mi = m_ref[1, :, l0:l0 + S5_SLAB]
        nr = m_ref[2, :, l0:l0 + S5_SLAB]
        ni = m_ref[3, :, l0:l0 + S5_SLAB]

        def body(k, carry, re0=re0, im0=im0, mr=mr, mi=mi, nr=nr, ni=ni):
            hr, hi = carry
            base = pl.multiple_of(k * 8, 8)
            xr = sc_ref[pl.ds(base, 8), re0:re0 + S5_SLAB]
            xi = sc_ref[pl.ds(base, 8), im0:im0 + S5_SLAB]
            xr_s = pltpu.roll(xr, 4, 0)
            xi_s = pltpu.roll(xi, 4, 0)
            outr = (mr * hr - mi * hi) + xr + (nr * xr_s - ni * xi_s)
            outi = (mr * hi + mi * hr) + xi + (nr * xi_s + ni * xr_s)
            sc_ref[pl.ds(base, 8), re0:re0 + S5_SLAB] = outr
            sc_ref[pl.ds(base, 8), im0:im0 + S5_SLAB] = outi
            return (jnp.where(first_step, pltpu.roll(outr, 4, 0), outr),
                    jnp.where(first_step, pltpu.roll(outi, 4, 0), outi))

        hr, hi = lax.fori_loop(0, n_pairs, body,
                               (carry_ref[:, re0:re0 + S5_SLAB], carry_ref[:, im0:im0 + S5_SLAB]))
        carry_ref[:, re0:re0 + S5_SLAB] = hr
        carry_ref[:, im0:im0 + S5_SLAB] = hi

    o_ref[...] = _s5_readout(sc_ref, u, cm_ref, d_ref, wglu_ref, bglu_ref).astype(o_ref.dtype)
    st_ref[...] = carry_ref[...]


def _const_spec(shape):
    return pl.BlockSpec(shape, lambda c: (0,) * len(shape))


def s5_prompt(u_tm, tb, *, rows):
    n = u_tm.shape[0]
    zero = jnp.zeros_like(tb['lam'])
    m = jnp.stack([jnp.concatenate([jnp.broadcast_to(a[k][None], (4, a.shape[1])),
                                    jnp.broadcast_to(b[k][None], (4, b.shape[1]))], axis=0)
                   for a, b, k in ((tb['lam'], tb['lam2'], 0), (tb['lam'], tb['lam2'], 1),
                                   (zero, tb['lam'], 0), (zero, tb['lam'], 1))])
    return pl.pallas_call(
        _s5_prompt_kernel,
        grid=(n // rows,),
        in_specs=[
            pl.BlockSpec((rows, S5_CH), lambda c: (c, 0)),
            _const_spec(tb['bb'].shape), _const_spec(m.shape), _const_spec(tb['cm'].shape),
            _const_spec((1, S5_CH)), _const_spec((S5_CH, S5_CH)), _const_spec((1, S5_CH)),
        ],
        out_specs=[pl.BlockSpec((rows, S5_CH), lambda c: (c, 0)), _const_spec((8, S5_LANES))],
        out_shape=[jax.ShapeDtypeStruct((n, S5_CH), BF16), jax.ShapeDtypeStruct((8, S5_LANES), F32)],
        scratch_shapes=[pltpu.VMEM((rows, S5_LANES), F32), pltpu.VMEM((8, S5_LANES), F32)],
        compiler_params=pltpu.CompilerParams(
            dimension_semantics=("arbitrary",), vmem_limit_bytes=V7X_VMEM_LIMIT),
        name="s5_prompt",
    )(u_tm, tb['bb'], m, tb['cm'], tb['d'], tb['wglu'], tb['bglu'])


def _s5_sample_kernel(u_ref, h0_ref, bb_ref, lam_ref, cm_ref, d_ref, wglu_ref, bglu_ref, mix_ref,
                      o_ref, st_ref, sc_ref):
    del mix_ref
    u = u_ref[...]
    _s5_drive(u, bb_ref, sc_ref)
    n_seq = h0_ref.shape[0]
    n_steps = u_ref.shape[0] // n_seq
    for re0, im0, l0 in _s5_slabs():
        lr = lam_ref[0:1, l0:l0 + S5_SLAB]
        li = lam_ref[1:2, l0:l0 + S5_SLAB]

        def body(rb, _, re0=re0, im0=im0, lr=lr, li=li):
            r0 = pl.multiple_of(rb * 8, 8)
            hr = h0_ref[pl.ds(r0, 8), re0:re0 + S5_SLAB]
            hi = h0_ref[pl.ds(r0, 8), im0:im0 + S5_SLAB]
            for t in range(n_steps):
                rows = pl.ds(t * n_seq + r0, 8)
                nr = (lr * hr - li * hi) + sc_ref[rows, re0:re0 + S5_SLAB]
                ni = (lr * hi + li * hr) + sc_ref[rows, im0:im0 + S5_SLAB]
                sc_ref[rows, re0:re0 + S5_SLAB] = nr
                sc_ref[rows, im0:im0 + S5_SLAB] = ni
                hr, hi = nr, ni
            st_ref[pl.ds(r0, 8), re0:re0 + S5_SLAB] = hr
            st_ref[pl.ds(r0, 8), im0:im0 + S5_SLAB] = hi
            return 0

        lax.fori_loop(0, n_seq // 8, body, 0)

    o_ref[...] = _s5_readout(sc_ref, u, cm_ref, d_ref, wglu_ref, bglu_ref).astype(o_ref.dtype)


def s5_sample(y, mixed, h0, row0, n_rows, tb):
    nb = h0.shape[0]
    assert row0 % n_rows == 0
    blk = row0 // n_rows
    return pl.pallas_call(
        _s5_sample_kernel,
        grid=(1,),
        in_specs=[pl.BlockSpec((n_rows, S5_CH), lambda c: (blk, PK_OFF[0] // S5_CH)),
                  _const_spec((nb, S5_LANES)), _const_spec(tb['bb'].shape),
                  _const_spec(tb['lam'].shape), _const_spec(tb['cm'].shape), _const_spec((1, S5_CH)),
                  _const_spec((S5_CH, S5_CH)), _const_spec((1, S5_CH)), _ANY_SPEC],
        out_specs=[pl.BlockSpec((n_rows, S5_CH), lambda c: (blk, MIX_S5)), _const_spec((nb, S5_LANES))],
        out_shape=[jax.ShapeDtypeStruct(mixed.shape, mixed.dtype), jax.ShapeDtypeStruct((nb, S5_LANES), F32)],
        input_output_aliases={8: 0},
        scratch_shapes=[pltpu.VMEM((n_rows, S5_LANES), F32)],
        compiler_params=pltpu.CompilerParams(
            dimension_semantics=("arbitrary",), vmem_limit_bytes=V7X_VMEM_LIMIT),
        name="s5_sample",
    )(y, h0, tb['bb'], tb['lam'], tb['cm'], tb['d'], tb['wglu'], tb['bglu'], mixed)


def _s5_state_to_lanes(re, im):
    b = re.shape[0]
    return jnp.stack([re.reshape(b, 2, S5_HALF_ST), im.reshape(b, 2, S5_HALF_ST)], axis=2).reshape(b, S5_LANES)


def _s5_lanes_to_state(st):
    b = st.shape[0]
    st = st.reshape(b, 2, 2, S5_HALF_ST)
    return st[:, :, 0].reshape(b, S5_GROUPS, S5_STATE), st[:, :, 1].reshape(b, S5_GROUPS, S5_STATE)


def _ret_tables(positions, chunk):
    half = RET_DK // 2
    inv_freq = ROPE_BASE ** (-jnp.arange(half, dtype=F32) / half)
    ang = positions.astype(F32)[:, None] * inv_freq[None, :]
    cos = jnp.cos(ang)
    sin = jnp.sin(ang)
    cos2 = jnp.concatenate([cos, cos], axis=1)
    sin2 = jnp.concatenate([-sin, sin], axis=1)
    log_gamma = jnp.log(1.0 - 2.0 ** (-5.0 - jnp.arange(RET_HEADS, dtype=F32)))
    g = (jnp.arange(chunk, dtype=F32) + 1.0)[None, :] * log_gamma[:, None]
    diff = g[:, :, None] - g[:, None, :]
    causal = jnp.tril(jnp.ones((chunk, chunk), dtype=bool))
    dmat = jnp.where(causal, jnp.exp(jnp.where(causal, diff, 0.0)), 0.0)
    lanes = (RET_HEADS, chunk, RET_DK)
    qdec = jnp.broadcast_to(jnp.exp(g)[:, :, None], lanes)
    kdec = jnp.broadcast_to(jnp.exp(g[:, -1:] - g)[:, :, None], lanes)
    gall = jnp.broadcast_to(jnp.exp(g[:, -1])[:, None, None], (RET_HEADS, 1, RET_DV))
    return cos2, sin2, dmat, qdec, kdec, gall


def _ret_rotate(x, cos2, sin2):
    return x * cos2 + pltpu.roll(x, RET_DK // 2, 1) * sin2


def _group_layernorm_gate(o, gate, w, b):
    mu = jnp.mean(o, axis=-1, keepdims=True)
    xc = o - mu
    var = jnp.mean(xc * xc, axis=-1, keepdims=True)
    return _silu(gate) * (xc * lax.rsqrt(var + EPS) * w + b)


def _dot_nt(a, b):
    return lax.dot_general(a, b, (((1,), (1,)), ((), ())), preferred_element_type=F32)


def _dot_tn(a, b):
    return lax.dot_general(a, b, (((0,), (0,)), ((), ())), preferred_element_type=F32)


def _ret_prompt_kernel(q_ref, k_ref, v_ref, g_ref, cos_ref, sin_ref, dm_ref, qd_ref, kd_ref, ga_ref,
                       lnw_ref, lnb_ref, mix_ref, o_ref, s_ref, *, chunk):
    del mix_ref
    dm, qd, kd, ga = dm_ref[0], qd_ref[0], kd_ref[0], ga_ref[0]
    lnw, lnb = lnw_ref[0], lnb_ref[0]

    def body(c, s):
        r = pl.ds(pl.multiple_of(c * chunk, chunk), chunk)
        cos2, sin2 = cos_ref[r, :], sin_ref[r, :]
        q = _ret_rotate(q_ref[r, :], cos2, sin2)
        k = _ret_rotate(k_ref[r, :], cos2, sin2) * (RET_DK ** -0.5)
        vb = v_ref[r, :].astype(BF16)
        scores = _dot_nt(q.astype(BF16), k.astype(BF16)) * dm
        o = jnp.dot(scores.astype(BF16), vb, preferred_element_type=F32)
        o = o + jnp.dot((q * qd).astype(BF16), s.astype(BF16), preferred_element_type=F32)
        o_ref[r, :] = _group_layernorm_gate(o, g_ref[r, :], lnw, lnb).astype(o_ref.dtype)
        return s * ga + _dot_tn((k * kd).astype(BF16), vb)

    s_ref[0, 0] = lax.fori_loop(0, q_ref.shape[0] // chunk, body, jnp.zeros((RET_DK, RET_DV), F32))


MIX_S5, MIX_GDN, MIX_SSD, MIX_RET = range(4)
_ANY_SPEC = pl.BlockSpec(memory_space=pl.ANY)


def _uninit_kernel(o_ref):
    del o_ref


def uninitialized(shape, dtype):
    return pl.pallas_call(_uninit_kernel, out_shape=jax.ShapeDtypeStruct(shape, dtype), out_specs=_ANY_SPEC,
                          name="uninitialized")()


def ret_prompt(y, mixed, n_seq, seq_len, lw):
    chunk = min(RET_CHUNK, seq_len)
    cos2, sin2, dmat, qdec, kdec, gall = _ret_tables(jnp.arange(seq_len, dtype=jnp.int32), chunk)
    col = lambda s: (lambda b, h: (b, PK_OFF[s] // RET_DK + h))
    head = lambda shape: pl.BlockSpec((1,) + shape, lambda b, h: (h, 0, 0))
    tok = lambda s: pl.BlockSpec((seq_len, RET_DK), col(s))
    tab = pl.BlockSpec((seq_len, RET_DK), lambda b, h: (0, 0))
    return pl.pallas_call(
        functools.partial(_ret_prompt_kernel, chunk=chunk),
        grid=(n_seq, RET_HEADS),
        in_specs=[tok(10), tok(11), tok(12), tok(13), tab, tab,
                  head((chunk, chunk)), head((chunk, RET_DK)), head((chunk, RET_DK)), head((1, RET_DV)),
                  head((1, RET_DV)), head((1, RET_DV)), _ANY_SPEC],
        out_specs=[pl.BlockSpec((seq_len, RET_DV), lambda b, h: (b, MIX_RET * RET_HEADS + h)),
                   pl.BlockSpec((1, 1, RET_DK, RET_DV), lambda b, h: (b, h, 0, 0))],
        out_shape=[jax.ShapeDtypeStruct(mixed.shape, mixed.dtype),
                   jax.ShapeDtypeStruct((n_seq, RET_HEADS, RET_DK, RET_DV), F32)],
        input_output_aliases={12: 0},
        compiler_params=pltpu.CompilerParams(
            dimension_semantics=("arbitrary", "arbitrary"), vmem_limit_bytes=V7X_VMEM_LIMIT),
        name="ret_prompt",
    )(y, y, y, y, cos2, sin2, dmat, qdec, kdec, gall,
      lw['ret_ln_w'].astype(F32).reshape(RET_HEADS, 1, RET_DV), lw['ret_ln_b'].astype(F32).reshape(RET_HEADS, 1, RET_DV),
      mixed)


SAMPLE_BB = 16
LHS_ROWS = 16


def _ret_sample_kernel(q_ref, k_ref, v_ref, g_ref, cos_ref, sin_ref, s0_ref, lnw_ref, lnb_ref, mix_ref, o_ref, s_ref,
                       qd_scr, kd_scr, v_scr, acc_scr, lq, lk, lv, *, decay):
    del mix_ref
    n_t, n_b = q_ref.shape[0], q_ref.shape[1]
    for h in range(RET_HEADS):
        hc = slice(h * RET_DK, (h + 1) * RET_DK)
        qs, ks, vs = [], [], []
        for t in range(n_t):
            cos2, sin2 = cos_ref[t:t + 1, :], sin_ref[t:t + 1, :]
            qs.append(_ret_rotate(q_ref[t, :, hc], cos2, sin2))
            ks.append(_ret_rotate(k_ref[t, :, hc], cos2, sin2) * (RET_DK ** -0.5))
            vs.append(v_ref[t, :, hc])
            qd_scr[h, t] = qs[t] * decay['q'][h][t]
            kd_scr[h, t] = ks[t] * decay['k'][h][t]
            v_scr[h, t] = vs[t]
        for i in range(n_t):
            acc = None
            for j in range(i + 1):
                term = (jnp.sum(qs[i] * ks[j], axis=-1, keepdims=True) * decay['m'][h][i][j]) * vs[j]
                acc = term if acc is None else acc + term
            acc_scr[h, i] = acc

    for tile in (lq, lk, lv):
        tile[...] = jnp.zeros_like(tile)

    def per_seq(b, carry):
        row = pl.ds(b, 1)
        for h in range(RET_HEADS):
            for t in range(n_t):
                lq[h, t:t + 1, :] = qd_scr[h, t, row, :]
                lk[h, t:t + 1, :] = kd_scr[h, t, row, :]
                lv[h, t:t + 1, :] = v_scr[h, t, row, :]
            s0 = s0_ref[b, h]
            inter = jnp.dot(lq[h].astype(BF16), s0.astype(BF16), preferred_element_type=F32)
            s_ref[b, h] = s0 * decay['all'][h] + _dot_tn(lk[h].astype(BF16), lv[h].astype(BF16))
            for t in range(n_t):
                acc_scr[h, t, row, :] = acc_scr[h, t, row, :] + inter[t:t + 1, :]
        return carry

    lax.fori_loop(0, n_b, per_seq, 0)

    for h in range(RET_HEADS):
        hc = slice(h * RET_DV, (h + 1) * RET_DV)
        for t in range(n_t):
            o_ref[t, :, hc] = _group_layernorm_gate(acc_scr[h, t], g_ref[t, :, hc], lnw_ref[h], lnb_ref[h]
                                                    ).astype(o_ref.dtype)


def _sample_block(t0, n_t):
    assert t0 % n_t == 0
    return t0 // n_t


def ret_sample(y3, mixed3, s0, t0, n_t, first_pos, lw):
    n_b = y3.shape[1]
    tb = _sample_block(t0, n_t)
    cos2, sin2, _, _, _, _ = _ret_tables(first_pos + jnp.arange(n_t, dtype=jnp.int32), n_t)
    gamma = 1.0 - 2.0 ** (-5.0 - np.arange(RET_HEADS, dtype=np.float64))
    decay = dict(m=[[[float(g ** (i - j)) for j in range(n_t)] for i in range(n_t)] for g in gamma],
                 q=[[float(g ** (i + 1)) for i in range(n_t)] for g in gamma],
                 k=[[float(g ** (n_t - 1 - j)) for j in range(n_t)] for g in gamma],
                 all=[float(g ** n_t) for g in gamma])
    width = RET_HEADS * RET_DK
    tok = lambda s: pl.BlockSpec((n_t, SAMPLE_BB, width), lambda i: (tb, i, PK_OFF[s] // width))
    state = pl.BlockSpec((SAMPLE_BB, RET_HEADS, RET_DK, RET_DV), lambda i: (i, 0, 0, 0))
    per_tok = pltpu.VMEM((RET_HEADS, n_t, SAMPLE_BB, RET_DK), F32)
    tile = pltpu.VMEM((RET_HEADS, LHS_ROWS, RET_DK), F32)
    return pl.pallas_call(
        functools.partial(_ret_sample_kernel, decay=decay),
        grid=(n_b // SAMPLE_BB,),
        in_specs=[tok(10), tok(11), tok(12), tok(13), _const_spec((n_t, RET_DK)), _const_spec((n_t, RET_DK)),
                  state, _const_spec((RET_HEADS, 1, RET_DV)), _const_spec((RET_HEADS, 1, RET_DV)), _ANY_SPEC],
        out_specs=[pl.BlockSpec((n_t, SAMPLE_BB, width), lambda i: (tb, i, MIX_RET)), state],
        out_shape=[jax.ShapeDtypeStruct(mixed3.shape, mixed3.dtype), jax.ShapeDtypeStruct(s0.shape, F32)],
        input_output_aliases={9: 0},
        scratch_shapes=[per_tok, per_tok, per_tok, per_tok, tile, tile, tile],
        compiler_params=pltpu.CompilerParams(
            dimension_semantics=("arbitrary",), vmem_limit_bytes=V7X_VMEM_LIMIT),
        name="ret_sample",
    )(y3, y3, y3, y3, cos2, sin2, s0,
      lw['ret_ln_w'].astype(F32).reshape(RET_HEADS, 1, RET_DV), lw['ret_ln_b'].astype(F32).reshape(RET_HEADS, 1, RET_DV),
      mixed3)


NARROW_B = 0
NARROW_A = GDN_HEADS
NARROW_DT = 2 * GDN_HEADS
TAIL = 8


def _shift_rows(x, tail, s):
    xr = pltpu.roll(x, s, 0)
    tr = pltpu.roll(tail, s, 0)
    row = lax.broadcasted_iota(jnp.int32, tail.shape, 0)
    return jnp.concatenate([jnp.where(row < s, tr, xr[0:TAIL]), xr[TAIL:]], axis=0)


def _causal_conv(x, tail, w_ref):
    y = x * w_ref[CONV_W - 1:CONV_W, :]
    for s in range(1, CONV_W):
        y = y + _shift_rows(x, tail, s) * w_ref[CONV_W - 1 - s:CONV_W - s, :]
    return y


def _softplus(x):
    return jnp.maximum(x, 0.0) + jnp.log1p(jnp.exp(-jnp.abs(x)))


def _lane_row(vals, lane0, width=128):
    return jnp.zeros((1, width), F32).at[0, lane0:lane0 + vals.shape[0]].set(vals.astype(F32))


SSD_PAIRS = SSD_HEADS // 2
SSD_BC = SSD_NGROUPS * SSD_STATE


def _ssd_prompt_kernel(z_ref, xbc_ref, nar_ref, cw_ref, cb_ref, dtb_ref, a_ref, dsk_ref, nw_ref, tri_ref, mix_ref,
                       o_ref, s_ref, tail_ref, *, chunk):
    del mix_ref
    s_ref[...] = jnp.zeros_like(s_ref)
    tail_ref[...] = jnp.zeros_like(tail_ref)
    causal = (lax.broadcasted_iota(jnp.int32, (chunk, chunk), 0)
              >= lax.broadcasted_iota(jnp.int32, (chunk, chunk), 1))
    lane = lax.broadcasted_iota(jnp.int32, (chunk, 2 * SSD_HEADDIM), 1)
    first_head = lane < SSD_HEADDIM
    rep = SSD_HEADS // SSD_NGROUPS

    def body(c, carry):
        r = pl.ds(pl.multiple_of(c * chunk, chunk), chunk)
        raw = xbc_ref[r, :]
        xbc = _silu(_causal_conv(raw, tail_ref[0], cw_ref) + cb_ref[...])
        tail_ref[0] = raw[chunk - TAIL:, :]
        xs = xbc[:, :SSD_INNER]
        dt = _softplus(nar_ref[r, :] + dtb_ref[...])
        g = jnp.dot(tri_ref[...], dt * a_ref[...], preferred_element_type=F32,
                    precision=lax.Precision.HIGHEST)
        e_in = jnp.exp(g)
        e_out = dt * jnp.exp(g[chunk - 1:chunk, :] - g)
        e_all = jnp.exp(g[chunk - 1:chunk, :])
        g_t = g.T
        dt_t = dt.T
        ys = []
        for p in range(SSD_PAIRS):
            grp = (2 * p) // rep
            bm = xbc[:, SSD_INNER + grp * SSD_STATE:SSD_INNER + (grp + 1) * SSD_STATE]
            cm = xbc[:, SSD_INNER + SSD_BC + grp * SSD_STATE:SSD_INNER + SSD_BC + (grp + 1) * SSD_STATE]
            cb = _dot_nt(cm.astype(BF16), bm.astype(BF16))
            xp = xs[:, p * 128:(p + 1) * 128]
            xpb = xp.astype(BF16)
            sp = s_ref[0, p]
            spb = sp.astype(BF16)
            outs, upds, gls = [], [], []
            for hh in range(2):
                ln = NARROW_DT + 2 * p + hh
                diff = g[:, ln:ln + 1] - g_t[ln:ln + 1, :]
                m = jnp.where(causal, cb * jnp.exp(jnp.where(causal, diff, 0.0)) * dt_t[ln:ln + 1, :], 0.0)
                o = jnp.dot(m.astype(BF16), xpb, preferred_element_type=F32)
                o = o + jnp.dot((cm * e_in[:, ln:ln + 1]).astype(BF16), spb, preferred_element_type=F32)
                outs.append(o)
                upds.append(_dot_tn((bm * e_out[:, ln:ln + 1]).astype(BF16), xpb))
                gls.append(e_all[:, ln:ln + 1])
            s_ref[0, p] = sp * jnp.where(first_head, gls[0], gls[1]) + jnp.where(first_head, upds[0], upds[1])
            ys.append(jnp.where(first_head, outs[0], outs[1]) + xp * dsk_ref[:, p * 128:(p + 1) * 128])
        y = jnp.concatenate(ys, axis=1) * _silu(z_ref[r, :])
        gw = SSD_INNER // SSD_NGROUPS
        yn = [y[:, i * gw:(i + 1) * gw] * lax.rsqrt(
            jnp.mean(y[:, i * gw:(i + 1) * gw] * y[:, i * gw:(i + 1) * gw], axis=-1, keepdims=True) + EPS)
            for i in range(SSD_NGROUPS)]
        o_ref[r, :] = (jnp.concatenate(yn, axis=1) * nw_ref[...]).astype(o_ref.dtype)
        return carry

    lax.fori_loop(0, z_ref.shape[0] // chunk, body, 0)


def _ssd_params(lw, chunk):
    return (lw['ssd_conv_w'].astype(F32), lw['ssd_conv_b'].astype(F32).reshape(1, SSD_CONV_DIM),
            _lane_row(lw['ssd_dt_bias'], NARROW_DT), _lane_row(-jnp.exp(lw['ssd_a_log'].astype(F32)), NARROW_DT),
            jnp.repeat(lw['ssd_d'].astype(F32), SSD_HEADDIM).reshape(1, SSD_INNER),
            lw['ssd_norm_w'].astype(F32).reshape(1, SSD_INNER),
            jnp.tril(jnp.ones((chunk, chunk), F32)))


def _ssd_state_from_pairs(s):
    b = s.shape[0]
    s = s.reshape(b, SSD_PAIRS, SSD_STATE, 2, SSD_HEADDIM)
    return jnp.swapaxes(s, 2, 3).reshape(b, SSD_HEADS, SSD_STATE, SSD_HEADDIM)


def ssd_prompt(y, nar, mixed, n_seq, seq_len, lw):
    chunk = min(SSD_CHUNK, seq_len)
    params = _ssd_params(lw, chunk)
    tok = lambda s, w: pl.BlockSpec((seq_len, w), lambda b: (b, PK_OFF[s] // w))
    n_in = 3 + len(params)
    mixed, s, tail = pl.pallas_call(
        functools.partial(_ssd_prompt_kernel, chunk=chunk),
        grid=(n_seq,),
        in_specs=[tok(7, SSD_INNER), tok(8, SSD_CONV_DIM), pl.BlockSpec((seq_len, NARROW_W), lambda b: (b, 0))]
                 + [_const_spec(p.shape) for p in params] + [_ANY_SPEC],
        out_specs=[pl.BlockSpec((seq_len, SSD_INNER), lambda b: (b, MIX_SSD)),
                   pl.BlockSpec((1, SSD_PAIRS, SSD_STATE, 2 * SSD_HEADDIM), lambda b: (b, 0, 0, 0)),
                   pl.BlockSpec((1, TAIL, SSD_CONV_DIM), lambda b: (b, 0, 0))],
        out_shape=[jax.ShapeDtypeStruct(mixed.shape, mixed.dtype),
                   jax.ShapeDtypeStruct((n_seq, SSD_PAIRS, SSD_STATE, 2 * SSD_HEADDIM), F32),
                   jax.ShapeDtypeStruct((n_seq, TAIL, SSD_CONV_DIM), F32)],
        input_output_aliases={n_in: 0},
        compiler_params=pltpu.CompilerParams(
            dimension_semantics=("arbitrary",), vmem_limit_bytes=V7X_VMEM_LIMIT),
        name="ssd_prompt",
    )(y, y, nar, *params, mixed)
    return mixed, tail[:, TAIL - (CONV_W - 1):], _ssd_state_from_pairs(s)


GDN_QKV = GDN_HEADS * GDN_DK
HIGHEST = lax.Precision.HIGHEST


def _split_bf16(x):
    hi = x.astype(BF16)
    return hi, (x - hi.astype(F32)).astype(BF16)


def _dot3(a, b):
    a_hi, a_lo = _split_bf16(a)
    b_hi, b_lo = _split_bf16(b)
    m = a.shape[0]
    p = jnp.dot(jnp.concatenate([a_hi, a_lo], axis=0), b_hi, preferred_element_type=F32)
    return p[:m] + p[m:] + jnp.dot(a_hi, b_lo, preferred_element_type=F32)


def _two_block_diag(x0, x1):
    z0 = jnp.zeros_like(x0)
    z1 = jnp.zeros_like(x1)
    return jnp.concatenate([jnp.concatenate([x0, z1], axis=1), jnp.concatenate([z0, x1], axis=1)], axis=0)


def _l2_rows(x):
    return x * lax.rsqrt(jnp.sum(x * x, axis=-1, keepdims=True) + EPS)


def _gdn_prompt_kernel(q_ref, k_ref, v_ref, z_ref, nar_ref, cw_ref, dtb_ref, a_ref, nw_ref, tri_ref, sel_ref, mix_ref,
                       o_ref, s_ref, tail_ref, *, chunk):
    del mix_ref
    s_ref[...] = jnp.zeros_like(s_ref)
    tail_ref[...] = jnp.zeros_like(tail_ref)
    cat = GDN_HEADS * chunk
    lane = lax.broadcasted_iota(jnp.int32, (chunk, cat), 1)
    row = lax.broadcasted_iota(jnp.int32, (chunk, cat), 0)
    col = jnp.bitwise_and(lane, chunk - 1)
    causal = row >= col
    strict = row > col
    eye_cat = jnp.where(row == col, 1.0, 0.0).astype(F32)
    head_mask = [(lane >= h * chunk) & (lane < (h + 1) * chunk) for h in range(GDN_HEADS)]
    nar_lane = lax.broadcasted_iota(jnp.int32, (chunk, 128), 1)
    ones_cc = jnp.ones((chunk, chunk), F32)

    def mm_cat(l_cat, r_cat):
        bd = jnp.concatenate([jnp.where(m, r_cat, 0.0) for m in head_mask], axis=0)
        return _dot3(l_cat, bd)

    def body(c, carry):
        r = pl.ds(pl.multiple_of(c * chunk, chunk), chunk)
        qkv = []
        for i, ref in enumerate((q_ref, k_ref, v_ref)):
            cols = slice(i * GDN_QKV, (i + 1) * GDN_QKV)
            raw = ref[r, :]
            qkv.append(_silu(_causal_conv(raw, tail_ref[0, :, cols], cw_ref.at[:, cols])))
            tail_ref[0, :, cols] = raw[chunk - TAIL:, :]
        q, k, v = qkv
        nar = nar_ref[r, :]
        beta = jax.nn.sigmoid(nar)
        g = jnp.dot(tri_ref[...], a_ref[...] * _softplus(nar + dtb_ref[...]),
                    preferred_element_type=F32, precision=HIGHEST)
        bg = jnp.dot(jnp.where(nar_lane < NARROW_A, beta, g), sel_ref[...],
                     preferred_element_type=F32, precision=HIGHEST)
        b_c, g_c = bg[:, :cat], bg[:, cat:]
        g_r = jnp.dot(ones_cc, g_c * eye_cat, preferred_element_type=F32, precision=HIGHEST)
        decay = jnp.where(causal, jnp.exp(jnp.where(causal, g_c - g_r, 0.0)), 0.0)
        e_in = jnp.exp(g_c)
        e_out = jnp.exp(g_c[chunk - 1:chunk, :] - g_c)
        e_all = jnp.exp(g_c[chunk - 1:chunk, :])

        qn = [_l2_rows(q[:, h * GDN_DK:(h + 1) * GDN_DK]) * (GDN_DK ** -0.5) for h in range(GDN_HEADS)]
        kn = [_l2_rows(k[:, h * GDN_DK:(h + 1) * GDN_DK]) for h in range(GDN_HEADS)]
        vh = [v[:, h * GDN_DV:(h + 1) * GDN_DV] for h in range(GDN_HEADS)]
        kk, qk = [], []
        for p in range(GDN_HEADS // 2):
            h0, h1 = 2 * p, 2 * p + 1
            rhs = _two_block_diag(kn[h0], kn[h1]).astype(BF16)
            kk.append(_dot_nt(jnp.concatenate([kn[h0], kn[h1]], axis=1).astype(BF16), rhs))
            qk.append(_dot_nt(jnp.concatenate([qn[h0], qn[h1]], axis=1).astype(BF16), rhs))
        kk = jnp.concatenate(kk, axis=1)
        scores = jnp.concatenate(qk, axis=1) * decay
        a_cat = jnp.where(strict, b_c * kk * decay, 0.0)

        t_cat = eye_cat - a_cat
        pw = mm_cat(a_cat, a_cat)
        n_sq = chunk.bit_length() - 2
        for i in range(n_sq):
            t_cat = t_cat + mm_cat(t_cat, pw)
            if i + 1 < n_sq:
                pw = mm_cat(pw, pw)

        for p in range(GDN_HEADS // 2):
            heads = (2 * p, 2 * p + 1)
            col1 = lambda x, h: x[:, h * chunk:h * chunk + 1]
            rhs = _two_block_diag(*[jnp.concatenate(
                [vh[h] * col1(b_c, h), kn[h] * (col1(b_c, h) * col1(e_in, h))], axis=1) for h in heads])
            uw = _dot3(t_cat[:, p * 2 * chunk:(p + 1) * 2 * chunk], rhs)
            v_new, q_s = [], []
            for i, h in enumerate(heads):
                u = uw[:, (2 * i) * GDN_DV:(2 * i + 1) * GDN_DV]
                w = uw[:, (2 * i + 1) * GDN_DV:(2 * i + 2) * GDN_DV]
                wq = jnp.concatenate([w, qn[h] * col1(e_in, h)], axis=0).astype(BF16)
                ws = jnp.dot(wq, s_ref[0, h].astype(BF16), preferred_element_type=F32)
                v_new.append(u - ws[:chunk])
                q_s.append(ws[chunk:])
            intra = jnp.dot(scores[:, p * 2 * chunk:(p + 1) * 2 * chunk].astype(BF16),
                            _two_block_diag(*v_new).astype(BF16), preferred_element_type=F32)
            for i, h in enumerate(heads):
                o = q_s[i] + intra[:, i * GDN_DV:(i + 1) * GDN_DV]
                s_ref[0, h] = s_ref[0, h] * col1(e_all, h) + _dot_tn(
                    (kn[h] * col1(e_out, h)).astype(BF16), v_new[i].astype(BF16))
                o = o * lax.rsqrt(jnp.mean(o * o, axis=-1, keepdims=True) + EPS) * nw_ref[...]
                hc = slice(h * GDN_DV, (h + 1) * GDN_DV)
                o_ref[r, hc] = (o * _silu(z_ref[r, hc])).astype(o_ref.dtype)
        return carry

    lax.fori_loop(0, q_ref.shape[0] // chunk, body, 0)


def _gdn_params(lw, chunk):
    cat = GDN_HEADS * chunk
    lanes = jnp.arange(cat) // chunk
    sel = jnp.zeros((128, 2 * cat), F32)
    sel = sel.at[NARROW_B + lanes, jnp.arange(cat)].set(1.0)
    sel = sel.at[NARROW_A + lanes, cat + jnp.arange(cat)].set(1.0)
    return (lw['gdn_conv_w'].astype(F32), _lane_row(lw['gdn_dt_bias'], NARROW_A),
            _lane_row(-jnp.exp(lw['gdn_a_log'].astype(F32)), NARROW_A),
            lw['gdn_norm_w'].astype(F32).reshape(1, GDN_DV), jnp.tril(jnp.ones((chunk, chunk), F32)), sel)


def gdn_prompt(y, nar, mixed, n_seq, seq_len, lw):
    chunk = min(GDN_CHUNK, seq_len)
    params = _gdn_params(lw, chunk)
    tok = lambda s: pl.BlockSpec((seq_len, GDN_QKV), lambda b: (b, PK_OFF[s] // GDN_QKV))
    n_in = 5 + len(params)
    mixed, s, tail = pl.pallas_call(
        functools.partial(_gdn_prompt_kernel, chunk=chunk),
        grid=(n_seq,),
        in_specs=[tok(1), tok(2), tok(3), tok(4), pl.BlockSpec((seq_len, NARROW_W), lambda b: (b, 0))]
                 + [_const_spec(p.shape) for p in params] + [_ANY_SPEC],
        out_specs=[pl.BlockSpec((seq_len, GDN_QKV), lambda b: (b, MIX_GDN)),
                   pl.BlockSpec((1, GDN_HEADS, GDN_DK, GDN_DV), lambda b: (b, 0, 0, 0)),
                   pl.BlockSpec((1, TAIL, GDN_CONV_DIM), lambda b: (b, 0, 0))],
        out_shape=[jax.ShapeDtypeStruct(mixed.shape, mixed.dtype),
                   jax.ShapeDtypeStruct((n_seq, GDN_HEADS, GDN_DK, GDN_DV), F32),
                   jax.ShapeDtypeStruct((n_seq, TAIL, GDN_CONV_DIM), F32)],
        input_output_aliases={n_in: 0},
        compiler_params=pltpu.CompilerParams(
            dimension_semantics=("arbitrary",), vmem_limit_bytes=V7X_VMEM_LIMIT),
        name="gdn_prompt",
    )(y, y, y, y, nar, *params, mixed)
    return mixed, tail[:, TAIL - (CONV_W - 1):], s


def _conv_steps(buf_ref, raw_ref, w_ref, cols):
    n_t = raw_ref.shape[0]
    xx = [buf_ref[j, :, cols] for j in range(CONV_W - 1)] + [raw_ref[t] for t in range(n_t)]
    w = w_ref[:, cols]
    out = []
    for t in range(n_t):
        y = xx[t] * w[0:1, :]
        for j in range(1, CONV_W):
            y = y + xx[t + j] * w[j:j + 1, :]
        out.append(y)
    return out, xx[n_t:]


def _ssd_sample_kernel(z_ref, xbc_ref, nar_ref, buf_ref, s0_ref, cw_ref, cb_ref, dtb_ref, a_ref, dsk_ref, nw_ref,
                       mix_ref, o_ref, s_ref, nbuf_ref, qd_scr, kd_scr, xs_scr, ga_scr, acc_scr, lq, lk, lx):
    del mix_ref
    n_t, n_b = z_ref.shape[0], z_ref.shape[1]
    rep = SSD_HEADS // SSD_NGROUPS
    conv, tail = _conv_steps(buf_ref, xbc_ref, cw_ref, slice(0, SSD_CONV_DIM))
    for j in range(CONV_W - 1):
        nbuf_ref[j] = tail[j]
    xbc = [_silu(c + cb_ref[...]) for c in conv]
    dt, g = [], []
    for t in range(n_t):
        dt.append(_softplus(nar_ref[t] + dtb_ref[...]))
        la = dt[t] * a_ref[...]
        g.append(la if t == 0 else g[t - 1] + la)
    for grp in range(SSD_NGROUPS):
        bm = [x[:, SSD_INNER + grp * SSD_STATE:SSD_INNER + (grp + 1) * SSD_STATE] for x in xbc]
        cm = [x[:, SSD_INNER + SSD_BC + grp * SSD_STATE:SSD_INNER + SSD_BC + (grp + 1) * SSD_STATE] for x in xbc]
        sc = [[jnp.sum(cm[i] * bm[j], axis=-1, keepdims=True) for j in range(i + 1)] for i in range(n_t)]
        for h in range(grp * rep, (grp + 1) * rep):
            ln = NARROW_DT + h
            hc = slice(h * SSD_HEADDIM, (h + 1) * SSD_HEADDIM)
            gc = [x[:, ln:ln + 1] for x in g]
            dc = [x[:, ln:ln + 1] for x in dt]
            xs = [x[:, hc] for x in xbc]
            for t in range(n_t):
                qd_scr[h, t] = cm[t] * jnp.exp(gc[t])
                kd_scr[h, t] = bm[t] * (dc[t] * jnp.exp(gc[n_t - 1] - gc[t]))
                xs_scr[h, t] = xs[t]
                acc = None
                for j in range(t + 1):
                    term = (sc[t][j] * dc[j] * jnp.exp(gc[t] - gc[j])) * xs[j]
                    acc = term if acc is None else acc + term
                acc_scr[t, :, hc] = acc
            ga_scr[h] = jnp.broadcast_to(jnp.exp(gc[n_t - 1]), (n_b, SSD_HEADDIM))

    for tile in (lq, lk, lx):
        tile[...] = jnp.zeros_like(tile)

    def per_seq(b, carry):
        row = pl.ds(b, 1)
        for h in range(SSD_HEADS):
            hc = slice(h * SSD_HEADDIM, (h + 1) * SSD_HEADDIM)
            for t in range(n_t):
                lq[h, t:t + 1, :] = qd_scr[h, t, row, :]
                lk[h, t:t + 1, :] = kd_scr[h, t, row, :]
                lx[h, t:t + 1, :] = xs_scr[h, t, row, :]
            s0 = s0_ref[b, h]
            inter = jnp.dot(lq[h].astype(BF16), s0.astype(BF16), preferred_element_type=F32)
            s_ref[b, h] = s0 * ga_scr[h, row, :] + _dot_tn(lk[h].astype(BF16), lx[h].astype(BF16))
            for t in range(n_t):
                xs_scr[h, t, row, :] = inter[t:t + 1, :]
        return carry

    lax.fori_loop(0, n_b, per_seq, 0)

    gw = SSD_INNER // SSD_NGROUPS
    for t in range(n_t):
        for h in range(SSD_HEADS):
            hc = slice(h * SSD_HEADDIM, (h + 1) * SSD_HEADDIM)
            acc_scr[t, :, hc] = acc_scr[t, :, hc] + xs_scr[h, t]
        y = (acc_scr[t] + xbc[t][:, :SSD_INNER] * dsk_ref[...]) * _silu(z_ref[t])
        yn = [y[:, i * gw:(i + 1) * gw] * lax.rsqrt(
            jnp.mean(y[:, i * gw:(i + 1) * gw] * y[:, i * gw:(i + 1) * gw], axis=-1, keepdims=True) + EPS)
            for i in range(SSD_NGROUPS)]
        o_ref[t] = (jnp.concatenate(yn, axis=1) * nw_ref[...]).astype(o_ref.dtype)


def ssd_sample(y3, nar3, mixed3, buf0, s0, t0, n_t, lw):
    n_b = y3.shape[1]
    tb = _sample_block(t0, n_t)
    params = _ssd_params(lw, 1)[:-1]
    tok = lambda s, w: pl.BlockSpec((n_t, SAMPLE_BB, w), lambda i: (tb, i, PK_OFF[s] // w))
    bufspec = pl.BlockSpec((CONV_W - 1, SAMPLE_BB, SSD_CONV_DIM), lambda i: (0, i, 0))
    state = pl.BlockSpec((SAMPLE_BB, SSD_HEADS, SSD_STATE, SSD_HEADDIM), lambda i: (i, 0, 0, 0))
    per_tok = lambda w: pltpu.VMEM((SSD_HEADS, n_t, SAMPLE_BB, w), F32)
    tile = lambda w: pltpu.VMEM((SSD_HEADS, LHS_ROWS, w), F32)
    n_in = 5 + len(params)
    mixed3, s, nbuf = pl.pallas_call(
        _ssd_sample_kernel,
        grid=(n_b // SAMPLE_BB,),
        in_specs=[tok(7, SSD_INNER), tok(8, SSD_CONV_DIM),
                  pl.BlockSpec((n_t, SAMPLE_BB, NARROW_W), lambda i: (tb, i, 0)), bufspec, state]
                 + [_const_spec(p.shape) for p in params] + [_ANY_SPEC],
        out_specs=[pl.BlockSpec((n_t, SAMPLE_BB, SSD_INNER), lambda i: (tb, i, MIX_SSD)), state, bufspec],
        out_shape=[jax.ShapeDtypeStruct(mixed3.shape, mixed3.dtype), jax.ShapeDtypeStruct(s0.shape, F32),
                   jax.ShapeDtypeStruct((CONV_W - 1, n_b, SSD_CONV_DIM), F32)],
        input_output_aliases={n_in: 0},
        scratch_shapes=[per_tok(SSD_STATE), per_tok(SSD_STATE), per_tok(SSD_HEADDIM),
                        pltpu.VMEM((SSD_HEADS, SAMPLE_BB, SSD_HEADDIM), F32),
                        pltpu.VMEM((n_t, SAMPLE_BB, SSD_INNER), F32),
                        tile(SSD_STATE), tile(SSD_STATE), tile(SSD_HEADDIM)],
        compiler_params=pltpu.CompilerParams(
            dimension_semantics=("arbitrary",), vmem_limit_bytes=V7X_VMEM_LIMIT),
        name="ssd_sample",
    )(y3, y3, nar3, jnp.swapaxes(buf0, 0, 1), s0, *params, mixed3)
    return mixed3, jnp.swapaxes(nbuf, 0, 1), s


def _gdn_sample_kernel(q_ref, k_ref, v_ref, z_ref, nar_ref, buf_ref, s0_ref, cw_ref, dtb_ref, a_ref, nw_ref,
                       mix_ref, o_ref, s_ref, nbuf_ref, w_scr, qd_scr, kd_scr, u_scr, ga_scr, lwq, lk, lu):
    del mix_ref
    n_t, n_b = q_ref.shape[0], q_ref.shape[1]
    qkv = []
    for i, ref in enumerate((q_ref, k_ref, v_ref)):
        cols = slice(i * GDN_QKV, (i + 1) * GDN_QKV)
        conv, tail = _conv_steps(buf_ref, ref, cw_ref, cols)
        for j in range(CONV_W - 1):
            nbuf_ref[j, :, cols] = tail[j]
        qkv.append([_silu(c) for c in conv])
    beta, g = [], []
    for t in range(n_t):
        nar = nar_ref[t]
        beta.append(jax.nn.sigmoid(nar))
        gl = a_ref[...] * _softplus(nar + dtb_ref[...])
        g.append(gl if t == 0 else g[t - 1] + gl)

    scores = []
    for h in range(GDN_HEADS):
        hc = slice(h * GDN_DK, (h + 1) * GDN_DK)
        qn = [_l2_rows(x[:, hc]) * (GDN_DK ** -0.5) for x in qkv[0]]
        kn = [_l2_rows(x[:, hc]) for x in qkv[1]]
        vh = [x[:, hc] for x in qkv[2]]
        bc = [x[:, NARROW_B + h:NARROW_B + h + 1] for x in beta]
        gc = [x[:, NARROW_A + h:NARROW_A + h + 1] for x in g]
        us, ws = [], []
        for i in range(n_t):
            u = vh[i] * bc[i]
            w = kn[i] * (bc[i] * jnp.exp(gc[i]))
            for j in range(i):
                a_ij = bc[i] * jnp.sum(kn[i] * kn[j], axis=-1, keepdims=True) * jnp.exp(gc[i] - gc[j])
                u = u - a_ij * us[j]
                w = w - a_ij * ws[j]
            us.append(u)
            ws.append(w)
            u_scr[h, i] = u
            w_scr[h, i] = w
            qd_scr[h, i] = qn[i] * jnp.exp(gc[i])
            kd_scr[h, i] = kn[i] * jnp.exp(gc[n_t - 1] - gc[i])
        ga_scr[h] = jnp.broadcast_to(jnp.exp(gc[n_t - 1]), (n_b, GDN_DV))
        scores.append([[jnp.sum(qn[i] * kn[j], axis=-1, keepdims=True) * jnp.exp(gc[i] - gc[j])
                        for j in range(i + 1)] for i in range(n_t)])

    for tile in (lwq, lk, lu):
        tile[...] = jnp.zeros_like(tile)

    def per_seq(b, carry):
        row = pl.ds(b, 1)
        for h in range(GDN_HEADS):
            for t in range(n_t):
                lwq[h, t:t + 1, :] = w_scr[h, t, row, :]
                lwq[h, LHS_ROWS + t:LHS_ROWS + t + 1, :] = qd_scr[h, t, row, :]
                lk[h, t:t + 1, :] = kd_scr[h, t, row, :]
                lu[h, t:t + 1, :] = u_scr[h, t, row, :]
            s0 = s0_ref[b, h]
            wq_s = jnp.dot(lwq[h].astype(BF16), s0.astype(BF16), preferred_element_type=F32)
            v_new = lu[h] - wq_s[:LHS_ROWS]
            s_ref[b, h] = s0 * ga_scr[h, row, :] + _dot_tn(lk[h].astype(BF16), v_new.astype(BF16))
            for t in range(n_t):
                u_scr[h, t, row, :] = v_new[t:t + 1, :]
                qd_scr[h, t, row, :] = wq_s[LHS_ROWS + t:LHS_ROWS + t + 1, :]
        return carry

    lax.fori_loop(0, n_b, per_seq, 0)

    for h in range(GDN_HEADS):
        hc = slice(h * GDN_DV, (h + 1) * GDN_DV)
        for i in range(n_t):
            o = qd_scr[h, i]
            for j in range(i + 1):
                o = o + scores[h][i][j] * u_scr[h, j]
            o = o * lax.rsqrt(jnp.mean(o * o, axis=-1, keepdims=True) + EPS) * nw_ref[...]
            o_ref[i, :, hc] = (o * _silu(z_ref[i, :, hc])).astype(o_ref.dtype)


def gdn_sample(y3, nar3, mixed3, buf0, s0, t0, n_t, lw):
    n_b = y3.shape[1]
    tb = _sample_block(t0, n_t)
    params = _gdn_params(lw, 1)[:4]
    tok = lambda s: pl.BlockSpec((n_t, SAMPLE_BB, GDN_QKV), lambda i: (tb, i, PK_OFF[s] // GDN_QKV))
    bufspec = pl.BlockSpec((CONV_W - 1, SAMPLE_BB, GDN_CONV_DIM), lambda i: (0, i, 0))
    state = pl.BlockSpec((SAMPLE_BB, GDN_HEADS, GDN_DK, GDN_DV), lambda i: (i, 0, 0, 0))
    per_tok = pltpu.VMEM((GDN_HEADS, n_t, SAMPLE_BB, GDN_DK), F32)
    tile = lambda rows: pltpu.VMEM((GDN_HEADS, rows, GDN_DK), F32)
    n_in = 7 + len(params)
    mixed3, s, nbuf = pl.pallas_call(
        _gdn_sample_kernel,
        grid=(n_b // SAMPLE_BB,),
        in_specs=[tok(1), tok(2), tok(3), tok(4),
                  pl.BlockSpec((n_t, SAMPLE_BB, NARROW_W), lambda i: (tb, i, 0)), bufspec, state]
                 + [_const_spec(p.shape) for p in params] + [_ANY_SPEC],
        out_specs=[pl.BlockSpec((n_t, SAMPLE_BB, GDN_QKV), lambda i: (tb, i, MIX_GDN)), state, bufspec],
        out_shape=[jax.ShapeDtypeStruct(mixed3.shape, mixed3.dtype), jax.ShapeDtypeStruct(s0.shape, F32),
                   jax.ShapeDtypeStruct((CONV_W - 1, n_b, GDN_CONV_DIM), F32)],
        input_output_aliases={n_in: 0},
        scratch_shapes=[per_tok, per_tok, per_tok, per_tok, pltpu.VMEM((GDN_HEADS, SAMPLE_BB, GDN_DV), F32),
                        tile(2 * LHS_ROWS), tile(LHS_ROWS), tile(LHS_ROWS)],
        compiler_params=pltpu.CompilerParams(
            dimension_semantics=("arbitrary",), vmem_limit_bytes=V7X_VMEM_LIMIT),
        name="gdn_sample",
    )(y3, y3, y3, y3, nar3, jnp.swapaxes(buf0, 0, 1), s0, *params, mixed3)
    return mixed3, jnp.swapaxes(nbuf, 0, 1), s


def split_last(t, sizes):
    return jnp.split(t, [int(s) for s in np.cumsum(sizes)[:-1]], axis=-1)


def l2_normalize(x):
    xf = x.astype(jnp.float32)
    return xf * lax.rsqrt(jnp.sum(xf * xf, axis=-1, keepdims=True) + EPS)


def causal_conv(x, buf, w):
    L = x.shape[1]
    xp = jnp.concatenate([buf.astype(x.dtype), x], axis=1)
    y = xp[:, 0:L] * w[0]
    for j in range(1, CONV_W):
        y = y + xp[:, j:j + L] * w[j]
    return y, xp[:, L:]


def rotary(x, positions):
    half = x.shape[-1] // 2
    inv_freq = ROPE_BASE ** (-jnp.arange(half, dtype=jnp.float32) / half)
    ang = positions.astype(jnp.float32)[:, None] * inv_freq[None, :]
    cos = jnp.cos(ang)[None, :, None, :]
    sin = jnp.sin(ang)[None, :, None, :]
    xf = x.astype(jnp.float32)
    x1, x2 = xf[..., :half], xf[..., half:]
    return jnp.concatenate([x1 * cos - x2 * sin, x1 * sin + x2 * cos], axis=-1)


def to_chunks(t, chunk):
    L = t.shape[1]
    pad = (-L) % chunk
    t = jnp.pad(t, [(0, 0), (0, pad)] + [(0, 0)] * (t.ndim - 2))
    n = t.shape[1] // chunk
    t = t.reshape((t.shape[0], n, chunk) + t.shape[2:])
    return jnp.moveaxis(t, 2, 3)


def from_chunks(t, L):
    t = jnp.moveaxis(t, 3, 2)
    t = t.reshape((t.shape[0], t.shape[1] * t.shape[2]) + t.shape[3:])
    return t[:, :L]


def intra_decay(G):
    C = G.shape[-1]
    causal = jnp.tril(jnp.ones((C, C), dtype=bool))
    diff = G[..., :, None] - G[..., None, :]
    return jnp.where(causal, jnp.exp(jnp.where(causal, diff, 0.0)), 0.0)


def decay_linear_attention(q, k, v, log_a, s0, chunk):
    L = q.shape[1]
    C = min(chunk, L)
    qc, kc, vc = (to_chunks(t.astype(jnp.float32), C) for t in (q, k, v))
    G = jnp.cumsum(to_chunks(log_a.astype(jnp.float32), C), axis=-1)
    G_last = G[..., -1]
    scores = jnp.einsum('bnhid,bnhjd->bnhij', qc, kc) * intra_decay(G)
    intra = jnp.einsum('bnhij,bnhjv->bnhiv', scores, vc)
    chunk_states = jnp.einsum('bnhcd,bnhcv->bnhdv', kc * jnp.exp(G_last[..., None] - G)[..., None], vc)

    def step(S, inp):
        cs, gl = inp
        return S * gl[..., None, None] + cs, S

    s_final, s_prev = lax.scan(step, s0.astype(jnp.float32),
                               (jnp.moveaxis(chunk_states, 1, 0), jnp.moveaxis(jnp.exp(G_last), 1, 0)))
    s_prev = jnp.moveaxis(s_prev, 0, 1)
    inter = jnp.einsum('bnhcd,bnhdv->bnhcv', qc * jnp.exp(G)[..., None], s_prev)
    return from_chunks(intra + inter, L), s_final


def gated_delta_rule(q, k, v, beta, g, s0, chunk):
    L = q.shape[1]
    C = min(chunk, L)
    qc, kc, vc = (to_chunks(t.astype(jnp.float32), C) for t in (q, k, v))
    bc = to_chunks(beta.astype(jnp.float32), C)
    G = jnp.cumsum(to_chunks(g.astype(jnp.float32), C), axis=-1)
    decay = intra_decay(G)
    strict = jnp.tril(jnp.ones((C, C), dtype=bool), -1)
    kk = jnp.einsum('bnhid,bnhjd->bnhij', kc, kc)
    a_mat = jnp.where(strict, bc[..., :, None] * kk * decay, 0.0) + jnp.eye(C, dtype=jnp.float32)
    u = lax.linalg.triangular_solve(a_mat, vc * bc[..., None], left_side=True, lower=True, unit_diagonal=True)
    w = lax.linalg.triangular_solve(a_mat, kc * (bc * jnp.exp(G))[..., None], left_side=True, lower=True,
                                    unit_diagonal=True)
    scores = jnp.einsum('bnhid,bnhjd->bnhij', qc, kc) * decay
    q_dec = qc * jnp.exp(G)[..., None]
    k_dec = kc * jnp.exp(G[..., -1:] - G)[..., None]
    g_last = jnp.exp(G[..., -1])

    def step(S, inp):
        u_c, w_c, s_c, qd_c, kd_c, gl_c = inp
        v_new = u_c - jnp.einsum('bhcd,bhdv->bhcv', w_c, S)
        o = jnp.einsum('bhcd,bhdv->bhcv', qd_c, S) + jnp.einsum('bhij,bhjv->bhiv', s_c, v_new)
        S = S * gl_c[..., None, None] + jnp.einsum('bhcd,bhcv->bhdv', kd_c, v_new)
        return S, o

    xs = tuple(jnp.moveaxis(t, 1, 0) for t in (u, w, scores, q_dec, k_dec, g_last))
    s_final, o = lax.scan(step, s0.astype(jnp.float32), xs)
    return from_chunks(jnp.moveaxis(o, 0, 1), L), s_final


def complex_affine_combine(e1, e2):
    a1r, a1i, b1r, b1i = e1
    a2r, a2i, b2r, b2i = e2
    return (a2r * a1r - a2i * a1i, a2r * a1i + a2i * a1r,
            a2r * b1r - a2i * b1i + b2r, a2r * b1i + a2i * b1r + b2i)


def s5_mixer(u, h_re0, h_im0, lw):
    f32 = jnp.float32
    bsz, L, _ = u.shape
    a_re = lw['s5_a_re'].astype(f32)
    a_im = lw['s5_a_im'].astype(f32)
    step = jnp.exp(lw['s5_log_step'].astype(f32))[:, None]
    mag = jnp.exp(a_re * step)
    lam_re = mag * jnp.cos(a_im * step)
    lam_im = mag * jnp.sin(a_im * step)
    den = a_re * a_re + a_im * a_im
    coef_re = ((lam_re - 1.0) * a_re + lam_im * a_im) / den
    coef_im = (lam_im * a_re - (lam_re - 1.0) * a_im) / den
    b_re = lw['s5_b_re'].astype(f32)
    b_im = lw['s5_b_im'].astype(f32)
    bb_re = coef_re[..., None] * b_re - coef_im[..., None] * b_im
    bb_im = coef_re[..., None] * b_im + coef_im[..., None] * b_re
    ug = u.astype(f32).reshape(bsz, L, S5_GROUPS, S5_GROUP_CH)
    drive_re = jnp.einsum('blgc,gnc->blgn', ug, bb_re)
    drive_im = jnp.einsum('blgc,gnc->blgn', ug, bb_im)
    h0_re = h_re0.astype(f32)
    h0_im = h_im0.astype(f32)
    drive_re = drive_re.at[:, 0].add(lam_re * h0_re - lam_im * h0_im)
    drive_im = drive_im.at[:, 0].add(lam_re * h0_im + lam_im * h0_re)
    lam_re_b = jnp.broadcast_to(lam_re, drive_re.shape)
    lam_im_b = jnp.broadcast_to(lam_im, drive_im.shape)
    _, _, hs_re, hs_im = lax.associative_scan(complex_affine_combine, (lam_re_b, lam_im_b, drive_re, drive_im), axis=1)
    c_re = lw['s5_c_re'].astype(f32)
    c_im = lw['s5_c_im'].astype(f32)
    y = jnp.einsum('blgn,gcn->blgc', hs_re, c_re) - jnp.einsum('blgn,gcn->blgc', hs_im, c_im)
    y = y.reshape(bsz, L, S5_CH) + lw['s5_d'].astype(f32) * u.astype(f32)
    y = jax.nn.gelu(y)
    out = y * jax.nn.sigmoid(y @ lw['s5_w_glu'].astype(f32) + lw['s5_b_glu'].astype(f32))
    return out.astype(u.dtype), hs_re[:, -1], hs_im[:, -1]


def gdn_mixer(q, k, v, z, b_logit, a_logit, buf0, s0, lw):
    dt_out = z.dtype
    bsz, L, _ = q.shape
    qkv, buf = causal_conv(jnp.concatenate([q, k, v], axis=-1), buf0, lw['gdn_conv_w'])
    qkv = jax.nn.silu(qkv)
    qh, kh, vh = split_last(qkv, (GDN_HEADS * GDN_DK, GDN_HEADS * GDN_DK, GDN_HEADS * GDN_DV))
    qh = l2_normalize(qh.reshape(bsz, L, GDN_HEADS, GDN_DK)) * (GDN_DK ** -0.5)
    kh = l2_normalize(kh.reshape(bsz, L, GDN_HEADS, GDN_DK))
    vh = vh.reshape(bsz, L, GDN_HEADS, GDN_DV)
    beta = jax.nn.sigmoid(b_logit.astype(jnp.float32))
    g = -jnp.exp(lw['gdn_a_log'].astype(jnp.float32)) * jax.nn.softplus(
        a_logit.astype(jnp.float32) + lw['gdn_dt_bias'].astype(jnp.float32))
    o, s = gated_delta_rule(qh, kh, vh, beta, g, s0, GDN_CHUNK)
    o = o * lax.rsqrt(jnp.mean(o * o, axis=-1, keepdims=True) + EPS) * lw['gdn_norm_w'].astype(jnp.float32)
    o = o * jax.nn.silu(z.astype(jnp.float32).reshape(bsz, L, GDN_HEADS, GDN_DV))
    return o.reshape(bsz, L, GDN_HEADS * GDN_DV).astype(dt_out), buf, s


def ssd_mixer(z, xbc, dt_raw, buf0, s0, lw):
    f32 = jnp.float32
    bsz, L, _ = z.shape
    xbc, buf = causal_conv(xbc, buf0, lw['ssd_conv_w'])
    xbc = jax.nn.silu(xbc + lw['ssd_conv_b'])
    xs, bm, cm = split_last(xbc, (SSD_INNER, SSD_NGROUPS * SSD_STATE, SSD_NGROUPS * SSD_STATE))
    rep = SSD_HEADS // SSD_NGROUPS
    xs = xs.astype(f32).reshape(bsz, L, SSD_HEADS, SSD_HEADDIM)
    bm = jnp.repeat(bm.astype(f32).reshape(bsz, L, SSD_NGROUPS, SSD_STATE), rep, axis=2)
    cm = jnp.repeat(cm.astype(f32).reshape(bsz, L, SSD_NGROUPS, SSD_STATE), rep, axis=2)
    dt = jax.nn.softplus(dt_raw.astype(f32) + lw['ssd_dt_bias'].astype(f32))
    a = -jnp.exp(lw['ssd_a_log'].astype(f32))
    y, s = decay_linear_attention(cm, bm * dt[..., None], xs, dt * a, s0, SSD_CHUNK)
    y = y + xs * lw['ssd_d'].astype(f32)[:, None]
    y = y.reshape(bsz, L, SSD_INNER) * jax.nn.silu(z.astype(f32))
    y = y.reshape(bsz, L, SSD_NGROUPS, SSD_INNER // SSD_NGROUPS)
    y = y * lax.rsqrt(jnp.mean(y * y, axis=-1, keepdims=True) + EPS)
    y = y.reshape(bsz, L, SSD_INNER) * lw['ssd_norm_w'].astype(f32)
    return y.astype(z.dtype), buf, s


def retention_mixer(q, k, v, gate, s0, positions, lw):
    f32 = jnp.float32
    bsz, L, _ = q.shape
    qh = rotary(q.reshape(bsz, L, RET_HEADS, RET_DK), positions)
    kh = rotary(k.reshape(bsz, L, RET_HEADS, RET_DK), positions) * (RET_DK ** -0.5)
    vh = v.reshape(bsz, L, RET_HEADS, RET_DV)
    log_gamma = jnp.log(1.0 - 2.0 ** (-5.0 - jnp.arange(RET_HEADS, dtype=f32)))
    log_a = jnp.broadcast_to(log_gamma, (bsz, L, RET_HEADS))
    o, s = decay_linear_attention(qh, kh, vh, log_a, s0, RET_CHUNK)
    mu = jnp.mean(o, axis=-1, keepdims=True)
    var = jnp.mean((o - mu) ** 2, axis=-1, keepdims=True)
    o = ((o - mu) * lax.rsqrt(var + EPS)).reshape(bsz, L, RET_HEADS * RET_DV)
    o = o * lw['ret_ln_w'].astype(f32) + lw['ret_ln_b'].astype(f32)
    out = jax.nn.silu(gate.astype(f32)) * o
    return out.astype(q.dtype), s


def _mixers(y, st, lw, positions):
    def col(s):
        return y[..., PK_OFF[s]:PK_OFF[s] + IN_SIZES[s]]

    gdn_s0, gdn_buf0, ssd_s0, ssd_buf0, ret_s0 = st
    out_b, gdn_buf, gdn_s = gdn_mixer(col(1), col(2), col(3), col(4), col(5), col(6), gdn_buf0, gdn_s0, lw)
    out_c, ssd_buf, ssd_s = ssd_mixer(col(7), col(8), col(9), ssd_buf0, ssd_s0, lw)
    out_d, ret_s = retention_mixer(col(10), col(11), col(12), col(13), ret_s0, positions, lw)
    mixed = jnp.concatenate([out_b, out_c, out_d], axis=-1).astype(BF16)
    return mixed, (gdn_s, gdn_buf, ssd_s, ssd_buf, ret_s)


TM = 1088


def kernel(x_prompt, x_sample, p_prompt, p_sample, state_s5_re, state_s5_im, state_gdn, state_gdn_conv, state_ssd, state_ssd_conv, state_ret, norm_mix, w_in, s5_a_re, s5_a_im, s5_b_re, s5_b_im, s5_c_re, s5_c_im, s5_d, s5_log_step, s5_w_glu, s5_b_glu, gdn_conv_w, gdn_a_log, gdn_dt_bias, gdn_norm_w, ssd_conv_w, ssd_conv_b, ssd_dt_bias, ssd_a_log, ssd_d, ssd_norm_w, ret_ln_w, ret_ln_b, w_out, norm_ffn, w_ffn_in, w_ffn_out, norm_ple, w_ple_gate, w_ple_proj, norm_final):
    bp, lp, d = x_prompt.shape
    bs, ls, _ = x_sample.shape
    np_tok = bp * lp
    ns_tok = bs * ls
    n_tok = np_tok + ns_tok

    mixer_w = dict(
        s5_a_re=s5_a_re, s5_a_im=s5_a_im, s5_b_re=s5_b_re, s5_b_im=s5_b_im, s5_c_re=s5_c_re, s5_c_im=s5_c_im,
        s5_d=s5_d, s5_log_step=s5_log_step, s5_w_glu=s5_w_glu, s5_b_glu=s5_b_glu,
        gdn_conv_w=gdn_conv_w, gdn_a_log=gdn_a_log, gdn_dt_bias=gdn_dt_bias, gdn_norm_w=gdn_norm_w,
        ssd_conv_w=ssd_conv_w, ssd_conv_b=ssd_conv_b, ssd_dt_bias=ssd_dt_bias, ssd_a_log=ssd_a_log,
        ssd_d=ssd_d, ssd_norm_w=ssd_norm_w, ret_ln_w=ret_ln_w, ret_ln_b=ret_ln_b)

    assert bp == 4, "the prompt S5 kernel packs two time steps of four sequences per vreg"

    def tm_rows(t):
        return jnp.swapaxes(t, 0, 1).reshape(t.shape[0] * t.shape[1], t.shape[2])

    def bm_seqs(t, b):
        return jnp.swapaxes(t.reshape(t.shape[0] // b, b, t.shape[1]), 0, 1)

    h = jnp.concatenate([x_prompt.reshape(np_tok, d), tm_rows(x_sample)], axis=0)
    new_p, new_s = [], []
    y_final = None
    for i in range(DEPTH):
        lw = {k: v[i] for k, v in mixer_w.items()}
        y, nar = in_projection(h, norm_mix[i], w_in[i], tm=TM)
        y3 = y.reshape(n_tok // bs, bs, PK_TOTAL)
        nar3 = nar.reshape(n_tok // bs, bs, NARROW_W)
        t0 = np_tok // bs

        tb = _s5_tables(lw)
        a_p, st5_p = s5_prompt(tm_rows(y[:np_tok, :S5_CH].reshape(bp, lp, S5_CH)), tb, rows=512)

        mixed = uninitialized((n_tok, d), BF16)
        mixed, gdn_buf_p, gdn_s_p = gdn_prompt(y, nar, mixed, bp, lp, lw)
        mixed, ssd_buf_p, ssd_s_p = ssd_prompt(y, nar, mixed, bp, lp, lw)
        mixed, ret_s_p = ret_prompt(y, mixed, bp, lp, lw)
        mixed, st5_s = s5_sample(y, mixed, _s5_state_to_lanes(state_s5_re[i], state_s5_im[i]), np_tok, ns_tok, tb)
        mixed3 = mixed.reshape(n_tok // bs, bs, d)
        mixed3, gdn_buf_s, gdn_s_s = gdn_sample(y3, nar3, mixed3, state_gdn_conv[i], state_gdn[i], t0, ls, lw)
        mixed3, ssd_buf_s, ssd_s_s = ssd_sample(y3, nar3, mixed3, state_ssd_conv[i], state_ssd[i], t0, ls, lw)
        mixed3, ret_s_s = ret_sample(y3, mixed3, state_ret[i], t0, ls, PAST_LEN, lw)
        mixed = lax.dynamic_update_slice(mixed3.reshape(n_tok, d), bm_seqs(a_p, bp).reshape(np_tok, S5_CH), (0, 0))
        new_p.append(_s5_lanes_to_state(st5_p[:bp]) + (gdn_s_p, gdn_buf_p, ssd_s_p, ssd_buf_p, ret_s_p))
        new_s.append(_s5_lanes_to_state(st5_s) + (gdn_s_s, gdn_buf_s, ssd_s_s, ssd_buf_s, ret_s_s))
        h = matmul_residual(mixed, w_out[i].astype(BF16), h, tm=TM)
        h = ffn_residual(h, norm_ffn[i], w_ffn_in[i].astype(BF16), w_ffn_out[i].astype(BF16), tm=TM // 2, th=512)
        p = jnp.concatenate([p_prompt[i].reshape(np_tok, PLE_DIM), tm_rows(p_sample[i])], axis=0)
        final = i == DEPTH - 1
        outs = ple_residual(h, norm_ple[i], w_ple_gate[i].astype(BF16), p, w_ple_proj[i].astype(BF16),
                            norm_final, tm=TM // 2, final=final)
        h = outs[0]
        if final:
            y_final = outs[1]

    y_prompt = y_final[:np_tok].reshape(bp, lp, d)
    y_sample = bm_seqs(y_final[np_tok:], bs)
    stack_p = [jnp.stack([st[j] for st in new_p]) for j in range(7)]
    stack_s = [jnp.stack([st[j] for st in new_s]) for j in range(7)]
    return (y_prompt, y_sample, *stack_p, *stack_s)
```

```python
import functools
import math

import jax
import jax.numpy as jnp
import numpy as np
from jax import lax
from jax.experimental import pallas as pl
from jax.experimental.pallas import tpu as pltpu

F32 = jnp.float32
BF16 = jnp.bfloat16

D_MODEL = 2048
DEPTH = 2
GROUP_WIDTH = D_MODEL // 4
CONV_W = 4
EPS = 1e-6
PLE_DIM = 256
FFN_HIDDEN = ((8 * D_MODEL + 3 * 256 - 1) // (3 * 256)) * 256

S5_CH = GROUP_WIDTH
S5_GROUP_CH = 16
S5_GROUPS = S5_CH // S5_GROUP_CH
S5_STATE = 64

GDN_HEADS = 4
GDN_DK = GROUP_WIDTH // GDN_HEADS
GDN_DV = GROUP_WIDTH // GDN_HEADS
GDN_CHUNK = 64
GDN_CONV_DIM = 2 * GDN_HEADS * GDN_DK + GDN_HEADS * GDN_DV

SSD_INNER = GROUP_WIDTH
SSD_HEADDIM = 64
SSD_HEADS = SSD_INNER // SSD_HEADDIM
SSD_NGROUPS = 2
SSD_STATE = 128
SSD_CHUNK = 128
SSD_CONV_DIM = SSD_INNER + 2 * SSD_NGROUPS * SSD_STATE

RET_HEADS = 4
RET_DK = GROUP_WIDTH // RET_HEADS
RET_DV = GROUP_WIDTH // RET_HEADS
RET_CHUNK = 128
ROPE_BASE = 10000.0
PAST_LEN = 16384

IN_SIZES = (
    S5_CH,
    GDN_HEADS * GDN_DK, GDN_HEADS * GDN_DK, GDN_HEADS * GDN_DV, GDN_HEADS * GDN_DV, GDN_HEADS, GDN_HEADS,
    SSD_INNER, SSD_CONV_DIM, SSD_HEADS,
    RET_HEADS * RET_DK, RET_HEADS * RET_DK, RET_HEADS * RET_DV, RET_HEADS * RET_DV,
)
IN_OFFS = tuple(int(v) for v in np.cumsum((0,) + IN_SIZES))

_REGIONS = ((0, 1, 2, 3, 4), (7, 8), (10, 11, 12, 13))
_NARROW = (5, 6, 9)
PK_OFF = {}
_o = 0
for _reg in _REGIONS:
    for _s in _reg:
        PK_OFF[_s] = _o
        _o += IN_SIZES[_s]
PK_TOTAL = _o
IN_TN = 512
REGION_TILES = tuple(sum(IN_SIZES[s] for s in reg) // IN_TN for reg in _REGIONS)
NARROW_W = 128

V7X_VMEM_LIMIT = 58 * 1024 * 1024


def _split_w_in(w_in):
    wide = [w_in[:, IN_OFFS[reg[0]]:IN_OFFS[reg[-1] + 1]].astype(BF16) for reg in _REGIONS]
    nar = jnp.concatenate([w_in[:, IN_OFFS[s]:IN_OFFS[s + 1]] for s in _NARROW], axis=1)
    nar = jnp.pad(nar, ((0, 0), (0, NARROW_W - nar.shape[1]))).astype(BF16)
    return wide, nar


def _rms_rows(x, nw):
    ms = jnp.mean(x * x, axis=-1, keepdims=True)
    return x * lax.rsqrt(ms + EPS) * nw


def _in_proj_kernel(x_ref, nw_ref, wa_ref, wb_ref, wc_ref, wn_ref, o_ref, nar_ref, xn_ref):
    j = pl.program_id(1)

    @pl.when(j == 0)
    def _():
        xn_ref[...] = _rms_rows(x_ref[...], nw_ref[...]).astype(BF16)
        nar_ref[...] = jnp.dot(xn_ref[...], wn_ref[...], preferred_element_type=F32)

    first = 0
    for w_ref, n_tiles in zip((wa_ref, wb_ref, wc_ref), REGION_TILES):
        @pl.when((j >= first) & (j < first + n_tiles))
        def _(w_ref=w_ref):
            o_ref[...] = jnp.dot(xn_ref[...], w_ref[...], preferred_element_type=F32)
        first += n_tiles


def in_projection(x, nw, w_in, *, tm):
    m, k = x.shape
    wide, nar = _split_w_in(w_in)
    starts = [sum(REGION_TILES[:r]) for r in range(len(REGION_TILES))]

    def region_spec(r):
        return pl.BlockSpec((k, IN_TN), lambda i, j: (0, jnp.clip(j - starts[r], 0, REGION_TILES[r] - 1)))

    return pl.pallas_call(
        _in_proj_kernel,
        grid=(m // tm, PK_TOTAL // IN_TN),
        in_specs=[
            pl.BlockSpec((tm, k), lambda i, j: (i, 0)),
            pl.BlockSpec((1, k), lambda i, j: (0, 0)),
            region_spec(0), region_spec(1), region_spec(2),
            pl.BlockSpec((k, NARROW_W), lambda i, j: (0, 0)),
        ],
        out_specs=[pl.BlockSpec((tm, IN_TN), lambda i, j: (i, j)),
                   pl.BlockSpec((tm, NARROW_W), lambda i, j: (i, 0))],
        out_shape=[jax.ShapeDtypeStruct((m, PK_TOTAL), F32), jax.ShapeDtypeStruct((m, NARROW_W), F32)],
        scratch_shapes=[pltpu.VMEM((tm, k), BF16)],
        compiler_params=pltpu.CompilerParams(
            dimension_semantics=("arbitrary", "arbitrary"), vmem_limit_bytes=V7X_VMEM_LIMIT),
        name="in_projection",
    )(x, nw.reshape(1, k), *wide, nar)


def _mm_res_kernel(a_ref, w_ref, h_ref, o_ref):
    o_ref[...] = h_ref[...] + jnp.dot(a_ref[...], w_ref[...], preferred_element_type=F32)


def matmul_residual(a, w, layer, h, *, tm):
    m, k = a.shape
    n = w.shape[2]
    return pl.pallas_call(
        _mm_res_kernel,
        grid=(m // tm,),
        in_specs=[
            pl.BlockSpec((tm, k), lambda i: (i, 0)),
            pl.BlockSpec((None, k, n), lambda i: (layer, 0, 0)),
            pl.BlockSpec((tm, n), lambda i: (i, 0)),
        ],
        out_specs=pl.BlockSpec((tm, n), lambda i: (i, 0)),
        out_shape=jax.ShapeDtypeStruct((m, n), F32),
        compiler_params=pltpu.CompilerParams(
            dimension_semantics=("arbitrary",), vmem_limit_bytes=V7X_VMEM_LIMIT),
        name="matmul_residual",
    )(a, w, h)


def _silu(x):
    return x * jax.nn.sigmoid(x)


def _ffn_kernel(h_ref, nw_ref, wg_ref, wu_ref, wo_ref, o_ref, xn_ref):
    @pl.when(pl.program_id(1) == 0)
    def _():
        h = h_ref[...]
        xn_ref[...] = _rms_rows(h, nw_ref[...]).astype(BF16)
        o_ref[...] = h

    xn = xn_ref[...]
    gate = jnp.dot(xn, wg_ref[...], preferred_element_type=F32)
    up = jnp.dot(xn, wu_ref[...], preferred_element_type=F32)
    act = (_silu(gate) * up).astype(BF16)
    o_ref[...] += jnp.dot(act, wo_ref[...], preferred_element_type=F32)


def ffn_residual(h, nw, w_in, w_out, layer, *, tm, th):
    m, k = h.shape
    hidden = w_out.shape[1]
    nj = hidden // th
    return pl.pallas_call(
        _ffn_kernel,
        grid=(m // tm, nj),
        in_specs=[
            pl.BlockSpec((tm, k), lambda i, j: (i, 0)),
            pl.BlockSpec((1, k), lambda i, j: (0, 0)),
            pl.BlockSpec((None, k, th), lambda i, j: (layer, 0, j)),
            pl.BlockSpec((None, k, th), lambda i, j: (layer, 0, j + nj)),
            pl.BlockSpec((None, th, k), lambda i, j: (layer, j, 0)),
        ],
        out_specs=pl.BlockSpec((tm, k), lambda i, j: (i, 0)),
        out_shape=jax.ShapeDtypeStruct((m, k), F32),
        scratch_shapes=[pltpu.VMEM((tm, k), BF16)],
        compiler_params=pltpu.CompilerParams(
            dimension_semantics=("arbitrary", "arbitrary"), vmem_limit_bytes=V7X_VMEM_LIMIT),
        name="ffn_residual",
    )(h, nw.reshape(1, k), w_in, w_in, w_out)


def _ple_rows(h_ref, nw_ref, wg_ref, pp_ref, ps_ref, wp_ref, n_prompt_tiles):
    h = h_ref[...]
    xn = _rms_rows(h, nw_ref[...]).astype(BF16)
    gate = jax.nn.sigmoid(jnp.dot(xn, wg_ref[...], preferred_element_type=F32))
    p = jnp.where(pl.program_id(0) < n_prompt_tiles, pp_ref[...], ps_ref[...])
    return h + gate * jnp.dot(p.astype(BF16), wp_ref[...], preferred_element_type=F32)


def _ple_kernel(h_ref, nw_ref, wg_ref, pp_ref, ps_ref, wp_ref, o_ref, *, n_prompt_tiles):
    o_ref[...] = _ple_rows(h_ref, nw_ref, wg_ref, pp_ref, ps_ref, wp_ref, n_prompt_tiles)


def _ple_final_kernel(h_ref, nw_ref, wg_ref, pp_ref, ps_ref, wp_ref, nf_ref, yp_ref, ys_ref, *, n_prompt_tiles):
    y = _rms_rows(_ple_rows(h_ref, nw_ref, wg_ref, pp_ref, ps_ref, wp_ref, n_prompt_tiles), nf_ref[...])

    @pl.when(pl.program_id(0) < n_prompt_tiles)
    def _():
        yp_ref[...] = y

    @pl.when(pl.program_id(0) >= n_prompt_tiles)
    def _():
        ys_ref[...] = y


def ple_residual(h, nw, wg, pp, ps, wp, layer, *, tm, nf=None):
    m, k = h.shape
    mp, pd = pp.shape[1], pp.shape[2]
    ms = ps.shape[1]
    assert mp % tm == 0 and ms % tm == 0 and mp + ms == m
    npt, nst = mp // tm, ms // tm
    row = pl.BlockSpec((tm, k), lambda i: (i, 0))
    vec = pl.BlockSpec((1, k), lambda i: (0, 0))
    in_specs = [row, vec,
                pl.BlockSpec((None, k, k), lambda i: (layer, 0, 0)),
                pl.BlockSpec((None, tm, pd), lambda i: (layer, jnp.minimum(i, npt - 1), 0)),
                pl.BlockSpec((None, tm, pd), lambda i: (layer, jnp.clip(i - npt, 0, nst - 1), 0)),
                pl.BlockSpec((None, pd, k), lambda i: (layer, 0, 0))]
    args = [h, nw.reshape(1, k), wg, pp, ps, wp]
    params = pltpu.CompilerParams(dimension_semantics=("arbitrary",), vmem_limit_bytes=V7X_VMEM_LIMIT)
    if nf is None:
        return pl.pallas_call(
            functools.partial(_ple_kernel, n_prompt_tiles=npt), grid=(m // tm,), in_specs=in_specs,
            out_specs=row, out_shape=jax.ShapeDtypeStruct((m, k), F32), compiler_params=params,
            name="ple_residual")(*args)
    return pl.pallas_call(
        functools.partial(_ple_final_kernel, n_prompt_tiles=npt), grid=(m // tm,), in_specs=in_specs + [vec],
        out_specs=[pl.BlockSpec((tm, k), lambda i: (jnp.minimum(i, npt - 1), 0)),
                   pl.BlockSpec((tm, k), lambda i: (jnp.clip(i - npt, 0, nst - 1), 0))],
        out_shape=[jax.ShapeDtypeStruct((mp, k), F32), jax.ShapeDtypeStruct((ms, k), F32)],
        compiler_params=params, name="ple_final")(*args, nf.reshape(1, k))


S5_HALF_CH = S5_CH // 2
S5_HALF_ST = (S5_GROUPS // 2) * S5_STATE
S5_LANES = 4 * S5_HALF_ST
S5_SLAB = 512


def _s5_tables(lw):
    a_re = lw['s5_a_re'].astype(F32)
    a_im = lw['s5_a_im'].astype(F32)
    step = jnp.exp(lw['s5_log_step'].astype(F32))[:, None]
    mag = jnp.exp(a_re * step)
    lam_re = mag * jnp.cos(a_im * step)
    lam_im = mag * jnp.sin(a_im * step)
    den = a_re * a_re + a_im * a_im
    coef_re = ((lam_re - 1.0) * a_re + lam_im * a_im) / den
    coef_im = (lam_im * a_re - (lam_re - 1.0) * a_im) / den
    b_re = lw['s5_b_re'].astype(F32)
    b_im = lw['s5_b_im'].astype(F32)
    bb_re = coef_re[..., None] * b_re - coef_im[..., None] * b_im
    bb_im = coef_re[..., None] * b_im + coef_im[..., None] * b_re
    gh = S5_GROUPS // 2
    eye = jnp.eye(gh, dtype=F32)

    def in_blockdiag(b):
        return jnp.einsum('gnc,gh->gchn', b, eye).reshape(gh * S5_GROUP_CH, gh * S5_STATE)

    def out_blockdiag(c):
        return jnp.einsum('gcn,gh->gnhc', c, eye).reshape(gh * S5_STATE, gh * S5_GROUP_CH)

    c_re = lw['s5_c_re'].astype(F32)
    c_im = lw['s5_c_im'].astype(F32)
    bb = jnp.stack([jnp.concatenate([in_blockdiag(bb_re[h * gh:(h + 1) * gh]),
                                     in_blockdiag(bb_im[h * gh:(h + 1) * gh])], axis=1) for h in range(2)])
    cm = jnp.stack([jnp.concatenate([out_blockdiag(c_re[h * gh:(h + 1) * gh]),
                                     -out_blockdiag(c_im[h * gh:(h + 1) * gh])], axis=0) for h in range(2)])
    lam = jnp.stack([lam_re.reshape(-1), lam_im.reshape(-1)])
    lam2 = jnp.stack([lam[0] * lam[0] - lam[1] * lam[1], 2.0 * lam[0] * lam[1]])
    return dict(bb=bb.astype(BF16), cm=cm.astype(BF16), lam=lam, lam2=lam2,
                d=lw['s5_d'].astype(F32).reshape(1, S5_CH), wglu=lw['s5_w_glu'].astype(BF16),
                bglu=lw['s5_b_glu'].astype(F32).reshape(1, S5_CH))


def _s5_drive(u, bb_ref, sc_ref):
    ub = u.astype(BF16)
    for hf in range(2):
        sc_ref[:, hf * 2 * S5_HALF_ST:(hf + 1) * 2 * S5_HALF_ST] = jnp.dot(
            ub[:, hf * S5_HALF_CH:(hf + 1) * S5_HALF_CH], bb_ref[hf], preferred_element_type=F32)


def _s5_readout(sc_ref, u, cm_ref, d_ref, wglu_ref, bglu_ref):
    ys = [jnp.dot(sc_ref[:, hf * 2 * S5_HALF_ST:(hf + 1) * 2 * S5_HALF_ST].astype(BF16), cm_ref[hf],
                  preferred_element_type=F32) for hf in range(2)]
    y = jnp.concatenate(ys, axis=1) + d_ref[...] * u
    y = jax.nn.gelu(y)
    z = jnp.dot(y.astype(BF16), wglu_ref[...], preferred_element_type=F32) + bglu_ref[...]
    return y * jax.nn.sigmoid(z)


def _s5_slabs():
    for hf in range(2):
        for sl in range(S5_HALF_ST // S5_SLAB):
            re0 = hf * 2 * S5_HALF_ST + sl * S5_SLAB
            yield re0, re0 + S5_HALF_ST, hf * S5_HALF_ST + sl * S5_SLAB


def _s5_prompt_kernel(u_ref, bb_ref, m_ref, cm_ref, d_ref, wglu_ref, bglu_ref, o_ref, st_ref, sc_ref, carry_ref):
    @pl.when(pl.program_id(0) == 0)
    def _():
        carry_ref[...] = jnp.zeros_like(carry_ref)

    u = u_ref[...]
    _s5_drive(u, bb_ref, sc_ref)
    first_step = lax.broadcasted_iota(jnp.int32, (8, S5_SLAB), 0) < 4
    n_pairs = u_ref.shape[0] // 8
    for re0, im0, l0 in _s5_slabs():
        mr = m_ref[0, :, l0:l0 + S5_SLAB]
        mi = m_ref[1, :, l0:l0 + S5_SLAB]
        nr = m_ref[2, :, l0:l0 + S5_SLAB]
        ni = m_ref[3, :, l0:l0 + S5_SLAB]

        def body(k, carry, re0=re0, im0=im0, mr=mr, mi=mi, nr=nr, ni=ni):
            hr, hi = carry
            base = pl.multiple_of(k * 8, 8)
            xr = sc_ref[pl.ds(base, 8), re0:re0 + S5_SLAB]
            xi = sc_ref[pl.ds(base, 8), im0:im0 + S5_SLAB]
            xr_s = pltpu.roll(xr, 4, 0)
            xi_s = pltpu.roll(xi, 4, 0)
            outr = (mr * hr - mi * hi) + xr + (nr * xr_s - ni * xi_s)
            outi = (mr * hi + mi * hr) + xi + (nr * xi_s + ni * xr_s)
            sc_ref[pl.ds(base, 8), re0:re0 + S5_SLAB] = outr
            sc_ref[pl.ds(base, 8), im0:im0 + S5_SLAB] = outi
            return (jnp.where(first_step, pltpu.roll(outr, 4, 0), outr),
                    jnp.where(first_step, pltpu.roll(outi, 4, 0), outi))

        hr, hi = lax.fori_loop(0, n_pairs, body,
                               (carry_ref[:, re0:re0 + S5_SLAB], carry_ref[:, im0:im0 + S5_SLAB]))
        carry_ref[:, re0:re0 + S5_SLAB] = hr
        carry_ref[:, im0:im0 + S5_SLAB] = hi

    o_ref[...] = _s5_readout(sc_ref, u, cm_ref, d_ref, wglu_ref, bglu_ref).astype(o_ref.dtype)
    st_ref[...] = carry_ref[...]


def _const_spec(shape):
    return pl.BlockSpec(shape, lambda c: (0,) * len(shape))


def s5_prompt(u_tm, tb, *, rows):
    n = u_tm.shape[0]
    zero = jnp.zeros_like(tb['lam'])
    m = jnp.stack([jnp.concatenate([jnp.broadcast_to(a[k][None], (4, a.shape[1])),
                                    jnp.broadcast_to(b[k][None], (4, b.shape[1]))], axis=0)
                   for a, b, k in ((tb['lam'], tb['lam2'], 0), (tb['lam'], tb['lam2'], 1),
                                   (zero, tb['lam'], 0), (zero, tb['lam'], 1))])
    return pl.pallas_call(
        _s5_prompt_kernel,
        grid=(n // rows,),
        in_specs=[
            pl.BlockSpec((rows, S5_CH), lambda c: (c, 0)),
            _const_spec(tb['bb'].shape), _const_spec(m.shape), _const_spec(tb['cm'].shape),
            _const_spec((1, S5_CH)), _const_spec((S5_CH, S5_CH)), _const_spec((1, S5_CH)),
        ],
        out_specs=[pl.BlockSpec((rows, S5_CH), lambda c: (c, 0)), _const_spec((8, S5_LANES))],
        out_shape=[jax.ShapeDtypeStruct((n, S5_CH), BF16), jax.ShapeDtypeStruct((8, S5_LANES), F32)],
        scratch_shapes=[pltpu.VMEM((rows, S5_LANES), F32), pltpu.VMEM((8, S5_LANES), F32)],
        compiler_params=pltpu.CompilerParams(
            dimension_semantics=("arbitrary",), vmem_limit_bytes=V7X_VMEM_LIMIT),
        name="s5_prompt",
    )(u_tm, tb['bb'], m, tb['cm'], tb['d'], tb['wglu'], tb['bglu'])


def _s5_sample_kernel(u_ref, h0_ref, bb_ref, lam_ref, cm_ref, d_ref, wglu_ref, bglu_ref, mix_ref,
                      o_ref, st_ref, sc_ref):
    del mix_ref
    u = u_ref[...]
    _s5_drive(u, bb_ref, sc_ref)
    n_seq = h0_ref.shape[0]
    n_steps = u_ref.shape[0] // n_seq
    for re0, im0, l0 in _s5_slabs():
        lr = lam_ref[0:1, l0:l0 + S5_SLAB]
        li = lam_ref[1:2, l0:l0 + S5_SLAB]

        def body(rb, _, re0=re0, im0=im0, lr=lr, li=li):
            r0 = pl.multiple_of(rb * 8, 8)
            hr = h0_ref[pl.ds(r0, 8), re0:re0 + S5_SLAB]
            hi = h0_ref[pl.ds(r0, 8), im0:im0 + S5_SLAB]
            for t in range(n_steps):
                rows = pl.ds(t * n_seq + r0, 8)
                nr = (lr * hr - li * hi) + sc_ref[rows, re0:re0 + S5_SLAB]
                ni = (lr * hi + li * hr) + sc_ref[rows, im0:im0 + S5_SLAB]
                sc_ref[rows, re0:re0 + S5_SLAB] = nr
                sc_ref[rows, im0:im0 + S5_SLAB] = ni
                hr, hi = nr, ni
            st_ref[pl.ds(r0, 8), re0:re0 + S5_SLAB] = hr
            st_ref[pl.ds(r0, 8), im0:im0 + S5_SLAB] = hi
            return 0

        lax.fori_loop(0, n_seq // 8, body, 0)

    o_ref[...] = _s5_readout(sc_ref, u, cm_ref, d_ref, wglu_ref, bglu_ref).astype(o_ref.dtype)


def s5_sample(y, mixed, h0, row0, n_rows, tb):
    nb = h0.shape[0]
    assert row0 % n_rows == 0
    blk = row0 // n_rows
    return pl.pallas_call(
        _s5_sample_kernel,
        grid=(1,),
        in_specs=[pl.BlockSpec((n_rows, S5_CH), lambda c: (blk, PK_OFF[0] // S5_CH)),
                  _const_spec((nb, S5_LANES)), _const_spec(tb['bb'].shape),
                  _const_spec(tb['lam'].shape), _const_spec(tb['cm'].shape), _const_spec((1, S5_CH)),
                  _const_spec((S5_CH, S5_CH)), _const_spec((1, S5_CH)), _ANY_SPEC],
        out_specs=[pl.BlockSpec((n_rows, S5_CH), lambda c: (blk, MIX_S5)), _const_spec((nb, S5_LANES))],
        out_shape=[jax.ShapeDtypeStruct(mixed.shape, mixed.dtype), jax.ShapeDtypeStruct((nb, S5_LANES), F32)],
        input_output_aliases={8: 0},
        scratch_shapes=[pltpu.VMEM((n_rows, S5_LANES), F32)],
        compiler_params=pltpu.CompilerParams(
            dimension_semantics=("arbitrary",), vmem_limit_bytes=V7X_VMEM_LIMIT),
        name="s5_sample",
    )(y, h0, tb['bb'], tb['lam'], tb['cm'], tb['d'], tb['wglu'], tb['bglu'], mixed)


def _s5_state_to_lanes(re, im):
    b = re.shape[0]
    return jnp.stack([re.reshape(b, 2, S5_HALF_ST), im.reshape(b, 2, S5_HALF_ST)], axis=2).reshape(b, S5_LANES)


def _s5_lanes_to_state(st):
    b = st.shape[0]
    st = st.reshape(b, 2, 2, S5_HALF_ST)
    return st[:, :, 0].reshape(b, S5_GROUPS, S5_STATE), st[:, :, 1].reshape(b, S5_GROUPS, S5_STATE)


def _ret_tables(positions, chunk):
    half = RET_DK // 2
    inv_freq = ROPE_BASE ** (-jnp.arange(half, dtype=F32) / half)
    ang = positions.astype(F32)[:, None] * inv_freq[None, :]
    cos = jnp.cos(ang)
    sin = jnp.sin(ang)
    cos2 = jnp.concatenate([cos, cos], axis=1)
    sin2 = jnp.concatenate([-sin, sin], axis=1)
    log_gamma = jnp.log(1.0 - 2.0 ** (-5.0 - jnp.arange(RET_HEADS, dtype=F32)))
    g = (jnp.arange(chunk, dtype=F32) + 1.0)[None, :] * log_gamma[:, None]
    diff = g[:, :, None] - g[:, None, :]
    causal = jnp.tril(jnp.ones((chunk, chunk), dtype=bool))
    dmat = jnp.where(causal, jnp.exp(jnp.where(causal, diff, 0.0)), 0.0)
    lanes = (RET_HEADS, chunk, RET_DK)
    qdec = jnp.broadcast_to(jnp.exp(g)[:, :, None], lanes)
    kdec = jnp.broadcast_to(jnp.exp(g[:, -1:] - g)[:, :, None], lanes)
    gall = jnp.broadcast_to(jnp.exp(g[:, -1])[:, None, None], (RET_HEADS, 1, RET_DV))
    return cos2, sin2, dmat, qdec, kdec, gall


def _ret_rotate(x, cos2, sin2):
    return x * cos2 + pltpu.roll(x, RET_DK // 2, 1) * sin2


def _group_layernorm_gate(o, gate, w, b):
    mu = jnp.mean(o, axis=-1, keepdims=True)
    xc = o - mu
    var = jnp.mean(xc * xc, axis=-1, keepdims=True)
    return _silu(gate) * (xc * lax.rsqrt(var + EPS) * w + b)


def _dot_nt(a, b):
    return lax.dot_general(a, b, (((1,), (1,)), ((), ())), preferred_element_type=F32)


def _dot_tn(a, b):
    return lax.dot_general(a, b, (((0,), (0,)), ((), ())), preferred_element_type=F32)


def _ret_prompt_kernel(q_ref, k_ref, v_ref, g_ref, cos_ref, sin_ref, dm_ref, qd_ref, kd_ref, ga_ref,
                       lnw_ref, lnb_ref, mix_ref, o_ref, s_ref, *, chunk):
    del mix_ref
    dm, qd, kd, ga = dm_ref[0], qd_ref[0], kd_ref[0], ga_ref[0]
    lnw, lnb = lnw_ref[0], lnb_ref[0]

    def body(c, s):
        r = pl.ds(pl.multiple_of(c * chunk, chunk), chunk)
        cos2, sin2 = cos_ref[r, :], sin_ref[r, :]
        q = _ret_rotate(q_ref[r, :], cos2, sin2)
        k = _ret_rotate(k_ref[r, :], cos2, sin2) * (RET_DK ** -0.5)
        vb = v_ref[r, :].astype(BF16)
        scores = _dot_nt(q.astype(BF16), k.astype(BF16)) * dm
        o = jnp.dot(scores.astype(BF16), vb, preferred_element_type=F32)
        o = o + jnp.dot((q * qd).astype(BF16), s.astype(BF16), preferred_element_type=F32)
        o_ref[r, :] = _group_layernorm_gate(o, g_ref[r, :], lnw, lnb).astype(o_ref.dtype)
        return s * ga + _dot_tn((k * kd).astype(BF16), vb)

    s_ref[0, 0] = lax.fori_loop(0, q_ref.shape[0] // chunk, body, jnp.zeros((RET_DK, RET_DV), F32))


MIX_S5, MIX_GDN, MIX_SSD, MIX_RET = range(4)
_ANY_SPEC = pl.BlockSpec(memory_space=pl.ANY)


def _uninit_kernel(o_ref):
    del o_ref


def uninitialized(shape, dtype):
    return pl.pallas_call(_uninit_kernel, out_shape=jax.ShapeDtypeStruct(shape, dtype), out_specs=_ANY_SPEC,
                          name="uninitialized")()


def ret_prompt(y, mixed, n_seq, seq_len, lw):
    chunk = min(RET_CHUNK, seq_len)
    cos2, sin2, dmat, qdec, kdec, gall = _ret_tables(jnp.arange(seq_len, dtype=jnp.int32), chunk)
    col = lambda s: (lambda b, h: (b, PK_OFF[s] // RET_DK + h))
    head = lambda shape: pl.BlockSpec((1,) + shape, lambda b, h: (h, 0, 0))
    tok = lambda s: pl.BlockSpec((seq_len, RET_DK), col(s))
    tab = pl.BlockSpec((seq_len, RET_DK), lambda b, h: (0, 0))
    return pl.pallas_call(
        functools.partial(_ret_prompt_kernel, chunk=chunk),
        grid=(n_seq, RET_HEADS),
        in_specs=[tok(10), tok(11), tok(12), tok(13), tab, tab,
                  head((chunk, chunk)), head((chunk, RET_DK)), head((chunk, RET_DK)), head((1, RET_DV)),
                  head((1, RET_DV)), head((1, RET_DV)), _ANY_SPEC],
        out_specs=[pl.BlockSpec((seq_len, RET_DV), lambda b, h: (b, MIX_RET * RET_HEADS + h)),
                   pl.BlockSpec((1, 1, RET_DK, RET_DV), lambda b, h: (b, h, 0, 0))],
        out_shape=[jax.ShapeDtypeStruct(mixed.shape, mixed.dtype),
                   jax.ShapeDtypeStruct((n_seq, RET_HEADS, RET_DK, RET_DV), F32)],
        input_output_aliases={12: 0},
        compiler_params=pltpu.CompilerParams(
            dimension_semantics=("arbitrary", "arbitrary"), vmem_limit_bytes=V7X_VMEM_LIMIT),
        name="ret_prompt",
    )(y, y, y, y, cos2, sin2, dmat, qdec, kdec, gall,
      lw['ret_ln_w'].astype(F32).reshape(RET_HEADS, 1, RET_DV), lw['ret_ln_b'].astype(F32).reshape(RET_HEADS, 1, RET_DV),
      mixed)


SAMPLE_BB = 16
LHS_ROWS = 16


def _ret_sample_kernel(q_ref, k_ref, v_ref, g_ref, cos_ref, sin_ref, s0_ref, lnw_ref, lnb_ref, mix_ref, sall_ref,
                       o_ref, s_ref, qd_scr, kd_scr, v_scr, acc_scr, lq, lk, lv, *, decay):
    del mix_ref, sall_ref
    n_t, n_b = q_ref.shape[0], q_ref.shape[1]
    for h in range(RET_HEADS):
        hc = slice(h * RET_DK, (h + 1) * RET_DK)
        qs, ks, vs = [], [], []
        for t in range(n_t):
            cos2, sin2 = cos_ref[t:t + 1, :], sin_ref[t:t + 1, :]
            qs.append(_ret_rotate(q_ref[t, :, hc], cos2, sin2))
            ks.append(_ret_rotate(k_ref[t, :, hc], cos2, sin2) * (RET_DK ** -0.5))
            vs.append(v_ref[t, :, hc])
            qd_scr[h, t] = qs[t] * decay['q'][h][t]
            kd_scr[h, t] = ks[t] * decay['k'][h][t]
            v_scr[h, t] = vs[t]
        for i in range(n_t):
            acc = None
            for j in range(i + 1):
                term = (jnp.sum(qs[i] * ks[j], axis=-1, keepdims=True) * decay['m'][h][i][j]) * vs[j]
                acc = term if acc is None else acc + term
            acc_scr[h, i] = acc

    for tile in (lq, lk, lv):
        tile[...] = jnp.zeros_like(tile)

    def per_seq(b, carry):
        row = pl.ds(b, 1)
        for h in range(RET_HEADS):
            for t in range(n_t):
                lq[h, t:t + 1, :] = qd_scr[h, t, row, :]
                lk[h, t:t + 1, :] = kd_scr[h, t, row, :]
                lv[h, t:t + 1, :] = v_scr[h, t, row, :]
            s0 = s0_ref[b, h]
            inter = jnp.dot(lq[h].astype(BF16), s0.astype(BF16), preferred_element_type=F32)
            s_ref[b, h] = s0 * decay['all'][h] + _dot_tn(lk[h].astype(BF16), lv[h].astype(BF16))
            for t in range(n_t):
                acc_scr[h, t, row, :] = acc_scr[h, t, row, :] + inter[t:t + 1, :]
        return carry

    lax.fori_loop(0, n_b, per_seq, 0)

    for h in range(RET_HEADS):
        hc = slice(h * RET_DV, (h + 1) * RET_DV)
        for t in range(n_t):
            o_ref[t, :, hc] = _group_layernorm_gate(acc_scr[h, t], g_ref[t, :, hc], lnw_ref[h], lnb_ref[h]
                                                    ).astype(o_ref.dtype)


def _sample_block(t0, n_t):
    assert t0 % n_t == 0
    return t0 // n_t


def ret_sample(y3, mixed3, states, new_states, layer, t0, n_t, first_pos, lw):
    n_b = y3.shape[1]
    tb = _sample_block(t0, n_t)
    cos2, sin2, _, _, _, _ = _ret_tables(first_pos + jnp.arange(n_t, dtype=jnp.int32), n_t)
    gamma = 1.0 - 2.0 ** (-5.0 - np.arange(RET_HEADS, dtype=np.float64))
    decay = dict(m=[[[float(g ** (i - j)) for j in range(n_t)] for i in range(n_t)] for g in gamma],
                 q=[[float(g ** (i + 1)) for i in range(n_t)] for g in gamma],
                 k=[[float(g ** (n_t - 1 - j)) for j in range(n_t)] for g in gamma],
                 all=[float(g ** n_t) for g in gamma])
    width = RET_HEADS * RET_DK
    tok = lambda s: pl.BlockSpec((n_t, SAMPLE_BB, width), lambda i: (tb, i, PK_OFF[s] // width))
    state = pl.BlockSpec((None, SAMPLE_BB, RET_HEADS, RET_DK, RET_DV), lambda i: (layer, i, 0, 0, 0))
    per_tok = pltpu.VMEM((RET_HEADS, n_t, SAMPLE_BB, RET_DK), F32)
    tile = pltpu.VMEM((RET_HEADS, LHS_ROWS, RET_DK), F32)
    return pl.pallas_call(
        functools.partial(_ret_sample_kernel, decay=decay),
        grid=(n_b // SAMPLE_BB,),
        in_specs=[tok(10), tok(11), tok(12), tok(13), _const_spec((n_t, RET_DK)), _const_spec((n_t, RET_DK)),
                  state, _const_spec((RET_HEADS, 1, RET_DV)), _const_spec((RET_HEADS, 1, RET_DV)), _ANY_SPEC, _ANY_SPEC],
        out_specs=[pl.BlockSpec((n_t, SAMPLE_BB, width), lambda i: (tb, i, MIX_RET)), state],
        out_shape=[jax.ShapeDtypeStruct(mixed3.shape, mixed3.dtype), jax.ShapeDtypeStruct(states.shape, F32)],
        input_output_aliases={9: 0, 10: 1},
        scratch_shapes=[per_tok, per_tok, per_tok, per_tok, tile, tile, tile],
        compiler_params=pltpu.CompilerParams(
            dimension_semantics=("arbitrary",), vmem_limit_bytes=V7X_VMEM_LIMIT),
        name="ret_sample",
    )(y3, y3, y3, y3, cos2, sin2, states,
      lw['ret_ln_w'].astype(F32).reshape(RET_HEADS, 1, RET_DV), lw['ret_ln_b'].astype(F32).reshape(RET_HEADS, 1, RET_DV),
      mixed3, new_states)


NARROW_B = 0
NARROW_A = GDN_HEADS
NARROW_DT = 2 * GDN_HEADS
TAIL = 8


def _shift_rows(x, tail, s):
    xr = pltpu.roll(x, s, 0)
    tr = pltpu.roll(tail, s, 0)
    row = lax.broadcasted_iota(jnp.int32, tail.shape, 0)
    return jnp.concatenate([jnp.where(row < s, tr, xr[0:TAIL]), xr[TAIL:]], axis=0)


def _causal_conv(x, tail, w_ref):
    y = x * w_ref[CONV_W - 1:CONV_W, :]
    for s in range(1, CONV_W):
        y = y + _shift_rows(x, tail, s) * w_ref[CONV_W - 1 - s:CONV_W - s, :]
    return y


def _softplus(x):
    return jnp.maximum(x, 0.0) + jnp.log1p(jnp.exp(-jnp.abs(x)))


def _lane_row(vals, lane0, width=128):
    return jnp.zeros((1, width), F32).at[0, lane0:lane0 + vals.shape[0]].set(vals.astype(F32))


SSD_PAIRS = SSD_HEADS // 2
SSD_BC = SSD_NGROUPS * SSD_STATE


def _ssd_prompt_kernel(z_ref, xbc_ref, nar_ref, cw_ref, cb_ref, dtb_ref, a_ref, dsk_ref, nw_ref, tri_ref, mix_ref,
                       o_ref, s_ref, tail_ref, *, chunk):
    del mix_ref
    s_ref[...] = jnp.zeros_like(s_ref)
    tail_ref[...] = jnp.zeros_like(tail_ref)
    causal = (lax.broadcasted_iota(jnp.int32, (chunk, chunk), 0)
              >= lax.broadcasted_iota(jnp.int32, (chunk, chunk), 1))
    lane = lax.broadcasted_iota(jnp.int32, (chunk, 2 * SSD_HEADDIM), 1)
    first_head = lane < SSD_HEADDIM
    rep = SSD_HEADS // SSD_NGROUPS

    def body(c, carry):
        r = pl.ds(pl.multiple_of(c * chunk, chunk), chunk)
        raw = xbc_ref[r, :]
        xbc = _silu(_causal_conv(raw, tail_ref[0], cw_ref) + cb_ref[...])
        tail_ref[0] = raw[chunk - TAIL:, :]
        xs = xbc[:, :SSD_INNER]
        dt = _softplus(nar_ref[r, :] + dtb_ref[...])
        g = jnp.dot(tri_ref[...], dt * a_ref[...], preferred_element_type=F32,
                    precision=lax.Precision.HIGHEST)
        e_in = jnp.exp(g)
        e_out = dt * jnp.exp(g[chunk - 1:chunk, :] - g)
        e_all = jnp.exp(g[chunk - 1:chunk, :])
        g_t = g.T
        dt_t = dt.T
        ys = []
        for p in range(SSD_PAIRS):
            grp = (2 * p) // rep
            bm = xbc[:, SSD_INNER + grp * SSD_STATE:SSD_INNER + (grp + 1) * SSD_STATE]
            cm = xbc[:, SSD_INNER + SSD_BC + grp * SSD_STATE:SSD_INNER + SSD_BC + (grp + 1) * SSD_STATE]
            cb = _dot_nt(cm.astype(BF16), bm.astype(BF16))
            xp = xs[:, p * 128:(p + 1) * 128]
            xpb = xp.astype(BF16)
            sp = s_ref[0, p]
            spb = sp.astype(BF16)
            outs, upds, gls = [], [], []
            for hh in range(2):
                ln = NARROW_DT + 2 * p + hh
                diff = g[:, ln:ln + 1] - g_t[ln:ln + 1, :]
                m = jnp.where(causal, cb * jnp.exp(jnp.where(causal, diff, 0.0)) * dt_t[ln:ln + 1, :], 0.0)
                o = jnp.dot(m.astype(BF16), xpb, preferred_element_type=F32)
                o = o + jnp.dot((cm * e_in[:, ln:ln + 1]).astype(BF16), spb, preferred_element_type=F32)
                outs.append(o)
                upds.append(_dot_tn((bm * e_out[:, ln:ln + 1]).astype(BF16), xpb))
                gls.append(e_all[:, ln:ln + 1])
            s_ref[0, p] = sp * jnp.where(first_head, gls[0], gls[1]) + jnp.where(first_head, upds[0], upds[1])
            ys.append(jnp.where(first_head, outs[0], outs[1]) + xp * dsk_ref[:, p * 128:(p + 1) * 128])
        y = jnp.concatenate(ys, axis=1) * _silu(z_ref[r, :])
        gw = SSD_INNER // SSD_NGROUPS
        yn = [y[:, i * gw:(i + 1) * gw] * lax.rsqrt(
            jnp.mean(y[:, i * gw:(i + 1) * gw] * y[:, i * gw:(i + 1) * gw], axis=-1, keepdims=True) + EPS)
            for i in range(SSD_NGROUPS)]
        o_ref[r, :] = (jnp.concatenate(yn, axis=1) * nw_ref[...]).astype(o_ref.dtype)
        return carry

    lax.fori_loop(0, z_ref.shape[0] // chunk, body, 0)


def _ssd_params(lw, chunk):
    return (lw['ssd_conv_w'].astype(F32), lw['ssd_conv_b'].astype(F32).reshape(1, SSD_CONV_DIM),
            _lane_row(lw['ssd_dt_bias'], NARROW_DT), _lane_row(-jnp.exp(lw['ssd_a_log'].astype(F32)), NARROW_DT),
            jnp.repeat(lw['ssd_d'].astype(F32), SSD_HEADDIM).reshape(1, SSD_INNER),
            lw['ssd_norm_w'].astype(F32).reshape(1, SSD_INNER),
            jnp.tril(jnp.ones((chunk, chunk), F32)))


def _ssd_state_from_pairs(s):
    b = s.shape[0]
    s = s.reshape(b, SSD_PAIRS, SSD_STATE, 2, SSD_HEADDIM)
    return jnp.swapaxes(s, 2, 3).reshape(b, SSD_HEADS, SSD_STATE, SSD_HEADDIM)


def ssd_prompt(y, nar, mixed, n_seq, seq_len, lw):
    chunk = min(SSD_CHUNK, seq_len)
    params = _ssd_params(lw, chunk)
    tok = lambda s, w: pl.BlockSpec((seq_len, w), lambda b: (b, PK_OFF[s] // w))
    n_in = 3 + len(params)
    mixed, s, tail = pl.pallas_call(
        functools.partial(_ssd_prompt_kernel, chunk=chunk),
        grid=(n_seq,),
        in_specs=[tok(7, SSD_INNER), tok(8, SSD_CONV_DIM), pl.BlockSpec((seq_len, NARROW_W), lambda b: (b, 0))]
                 + [_const_spec(p.shape) for p in params] + [_ANY_SPEC],
        out_specs=[pl.BlockSpec((seq_len, SSD_INNER), lambda b: (b, MIX_SSD)),
                   pl.BlockSpec((1, SSD_PAIRS, SSD_STATE, 2 * SSD_HEADDIM), lambda b: (b, 0, 0, 0)),
                   pl.BlockSpec((1, TAIL, SSD_CONV_DIM), lambda b: (b, 0, 0))],
        out_shape=[jax.ShapeDtypeStruct(mixed.shape, mixed.dtype),
                   jax.ShapeDtypeStruct((n_seq, SSD_PAIRS, SSD_STATE, 2 * SSD_HEADDIM), F32),
                   jax.ShapeDtypeStruct((n_seq, TAIL, SSD_CONV_DIM), F32)],
        input_output_aliases={n_in: 0},
        compiler_params=pltpu.CompilerParams(
            dimension_semantics=("arbitrary",), vmem_limit_bytes=V7X_VMEM_LIMIT),
        name="ssd_prompt",
    )(y, y, nar, *params, mixed)
    return mixed, tail[:, TAIL - (CONV_W - 1):], _ssd_state_from_pairs(s)


GDN_QKV = GDN_HEADS * GDN_DK
HIGHEST = lax.Precision.HIGHEST


def _split_bf16(x):
    hi = x.astype(BF16)
    return hi, (x - hi.astype(F32)).astype(BF16)


def _dot3(a, b):
    a_hi, a_lo = _split_bf16(a)
    b_hi, b_lo = _split_bf16(b)
    m = a.shape[0]
    p = jnp.dot(jnp.concatenate([a_hi, a_lo], axis=0), b_hi, preferred_element_type=F32)
    return p[:m] + p[m:] + jnp.dot(a_hi, b_lo, preferred_element_type=F32)


def _two_block_diag(x0, x1):
    z0 = jnp.zeros_like(x0)
    z1 = jnp.zeros_like(x1)
    return jnp.concatenate([jnp.concatenate([x0, z1], axis=1), jnp.concatenate([z0, x1], axis=1)], axis=0)


def _l2_rows(x):
    return x * lax.rsqrt(jnp.sum(x * x, axis=-1, keepdims=True) + EPS)


def _gdn_prompt_kernel(q_ref, k_ref, v_ref, z_ref, nar_ref, cw_ref, dtb_ref, a_ref, nw_ref, tri_ref, sel_ref, mix_ref,
                       o_ref, s_ref, tail_ref, *, chunk):
    del mix_ref
    s_ref[...] = jnp.zeros_like(s_ref)
    tail_ref[...] = jnp.zeros_like(tail_ref)
    cat = GDN_HEADS * chunk
    lane = lax.broadcasted_iota(jnp.int32, (chunk, cat), 1)
    row = lax.broadcasted_iota(jnp.int32, (chunk, cat), 0)
    col = jnp.bitwise_and(lane, chunk - 1)
    causal = row >= col
    strict = row > col
    eye_cat = jnp.where(row == col, 1.0, 0.0).astype(F32)
    head_mask = [(lane >= h * chunk) & (lane < (h + 1) * chunk) for h in range(GDN_HEADS)]
    nar_lane = lax.broadcasted_iota(jnp.int32, (chunk, 128), 1)
    ones_cc = jnp.ones((chunk, chunk), F32)

    def mm_cat(l_cat, r_cat):
        bd = jnp.concatenate([jnp.where(m, r_cat, 0.0) for m in head_mask], axis=0)
        return _dot3(l_cat, bd)

    def body(c, carry):
        r = pl.ds(pl.multiple_of(c * chunk, chunk), chunk)
        qkv = []
        for i, ref in enumerate((q_ref, k_ref, v_ref)):
            cols = slice(i * GDN_QKV, (i + 1) * GDN_QKV)
            raw = ref[r, :]
            qkv.append(_silu(_causal_conv(raw, tail_ref[0, :, cols], cw_ref.at[:, cols])))
            tail_ref[0, :, cols] = raw[chunk - TAIL:, :]
        q, k, v = qkv
        nar = nar_ref[r, :]
        beta = jax.nn.sigmoid(nar)
        g = jnp.dot(tri_ref[...], a_ref[...] * _softplus(nar + dtb_ref[...]),
                    preferred_element_type=F32, precision=HIGHEST)
        bg = jnp.dot(jnp.where(nar_lane < NARROW_A, beta, g), sel_ref[...],
                     preferred_element_type=F32, precision=HIGHEST)
        b_c, g_c = bg[:, :cat], bg[:, cat:]
        g_r = jnp.dot(ones_cc, g_c * eye_cat, preferred_element_type=F32, precision=HIGHEST)
        decay = jnp.where(causal, jnp.exp(jnp.where(causal, g_c - g_r, 0.0)), 0.0)
        e_in = jnp.exp(g_c)
        e_out = jnp.exp(g_c[chunk - 1:chunk, :] - g_c)
        e_all = jnp.exp(g_c[chunk - 1:chunk, :])

        qn = [_l2_rows(q[:, h * GDN_DK:(h + 1) * GDN_DK]) * (GDN_DK ** -0.5) for h in range(GDN_HEADS)]
        kn = [_l2_rows(k[:, h * GDN_DK:(h + 1) * GDN_DK]) for h in range(GDN_HEADS)]
        vh = [v[:, h * GDN_DV:(h + 1) * GDN_DV] for h in range(GDN_HEADS)]
        kk, qk = [], []
        for p in range(GDN_HEADS // 2):
            h0, h1 = 2 * p, 2 * p + 1
            rhs = _two_block_diag(kn[h0], kn[h1]).astype(BF16)
            kk.append(_dot_nt(jnp.concatenate([kn[h0], kn[h1]], axis=1).astype(BF16), rhs))
            qk.append(_dot_nt(jnp.concatenate([qn[h0], qn[h1]], axis=1).astype(BF16), rhs))
        kk = jnp.concatenate(kk, axis=1)
        scores = jnp.concatenate(qk, axis=1) * decay
        a_cat = jnp.where(strict, b_c * kk * decay, 0.0)

        t_cat = eye_cat - a_cat
        pw = mm_cat(a_cat, a_cat)
        n_sq = chunk.bit_length() - 2
        for i in range(n_sq):
            t_cat = t_cat + mm_cat(t_cat, pw)
            if i + 1 < n_sq:
                pw = mm_cat(pw, pw)

        for p in range(GDN_HEADS // 2):
            heads = (2 * p, 2 * p + 1)
            col1 = lambda x, h: x[:, h * chunk:h * chunk + 1]
            rhs = _two_block_diag(*[jnp.concatenate(
                [vh[h] * col1(b_c, h), kn[h] * (col1(b_c, h) * col1(e_in, h))], axis=1) for h in heads])
            uw = _dot3(t_cat[:, p * 2 * chunk:(p + 1) * 2 * chunk], rhs)
            v_new, q_s = [], []
            for i, h in enumerate(heads):
                u = uw[:, (2 * i) * GDN_DV:(2 * i + 1) * GDN_DV]
                w = uw[:, (2 * i + 1) * GDN_DV:(2 * i + 2) * GDN_DV]
                wq = jnp.concatenate([w, qn[h] * col1(e_in, h)], axis=0).astype(BF16)
                ws = jnp.dot(wq, s_ref[0, h].astype(BF16), preferred_element_type=F32)
                v_new.append(u - ws[:chunk])
                q_s.append(ws[chunk:])
            intra = jnp.dot(scores[:, p * 2 * chunk:(p + 1) * 2 * chunk].astype(BF16),
                            _two_block_diag(*v_new).astype(BF16), preferred_element_type=F32)
            for i, h in enumerate(heads):
                o = q_s[i] + intra[:, i * GDN_DV:(i + 1) * GDN_DV]
                s_ref[0, h] = s_ref[0, h] * col1(e_all, h) + _dot_tn(
                    (kn[h] * col1(e_out, h)).astype(BF16), v_new[i].astype(BF16))
                o = o * lax.rsqrt(jnp.mean(o * o, axis=-1, keepdims=True) + EPS) * nw_ref[...]
                hc = slice(h * GDN_DV, (h + 1) * GDN_DV)
                o_ref[r, hc] = (o * _silu(z_ref[r, hc])).astype(o_ref.dtype)
        return carry

    lax.fori_loop(0, q_ref.shape[0] // chunk, body, 0)


def _gdn_params(lw, chunk):
    cat = GDN_HEADS * chunk
    lanes = jnp.arange(cat) // chunk
    sel = jnp.zeros((128, 2 * cat), F32)
    sel = sel.at[NARROW_B + lanes, jnp.arange(cat)].set(1.0)
    sel = sel.at[NARROW_A + lanes, cat + jnp.arange(cat)].set(1.0)
    return (lw['gdn_conv_w'].astype(F32), _lane_row(lw['gdn_dt_bias'], NARROW_A),
            _lane_row(-jnp.exp(lw['gdn_a_log'].astype(F32)), NARROW_A),
            lw['gdn_norm_w'].astype(F32).reshape(1, GDN_DV), jnp.tril(jnp.ones((chunk, chunk), F32)), sel)


def gdn_prompt(y, nar, mixed, n_seq, seq_len, lw):
    chunk = min(GDN_CHUNK, seq_len)
    params = _gdn_params(lw, chunk)
    tok = lambda s: pl.BlockSpec((seq_len, GDN_QKV), lambda b: (b, PK_OFF[s] // GDN_QKV))
    n_in = 5 + len(params)
    mixed, s, tail = pl.pallas_call(
        functools.partial(_gdn_prompt_kernel, chunk=chunk),
        grid=(n_seq,),
        in_specs=[tok(1), tok(2), tok(3), tok(4), pl.BlockSpec((seq_len, NARROW_W), lambda b: (b, 0))]
                 + [_const_spec(p.shape) for p in params] + [_ANY_SPEC],
        out_specs=[pl.BlockSpec((seq_len, GDN_QKV), lambda b: (b, MIX_GDN)),
                   pl.BlockSpec((1, GDN_HEADS, GDN_DK, GDN_DV), lambda b: (b, 0, 0, 0)),
                   pl.BlockSpec((1, TAIL, GDN_CONV_DIM), lambda b: (b, 0, 0))],
        out_shape=[jax.ShapeDtypeStruct(mixed.shape, mixed.dtype),
                   jax.ShapeDtypeStruct((n_seq, GDN_HEADS, GDN_DK, GDN_DV), F32),
                   jax.ShapeDtypeStruct((n_seq, TAIL, GDN_CONV_DIM), F32)],
        input_output_aliases={n_in: 0},
        compiler_params=pltpu.CompilerParams(
            dimension_semantics=("arbitrary",), vmem_limit_bytes=V7X_VMEM_LIMIT),
        name="gdn_prompt",
    )(y, y, y, y, nar, *params, mixed)
    return mixed, tail[:, TAIL - (CONV_W - 1):], s


def _conv_steps(buf_ref, raw_ref, w_ref, cols):
    n_t = raw_ref.shape[0]
    xx = [buf_ref[j, :, cols] for j in range(CONV_W - 1)] + [raw_ref[t] for t in range(n_t)]
    w = w_ref[:, cols]
    out = []
    for t in range(n_t):
        y = xx[t] * w[0:1, :]
        for j in range(1, CONV_W):
            y = y + xx[t + j] * w[j:j + 1, :]
        out.append(y)
    return out, xx[n_t:]


def _ssd_sample_kernel(z_ref, xbc_ref, nar_ref, buf_ref, s0_ref, cw_ref, cb_ref, dtb_ref, a_ref, dsk_ref, nw_ref,
                       mix_ref, sall_ref, o_ref, s_ref, nbuf_ref, qd_scr, kd_scr, xs_scr, ga_scr, acc_scr, lq, lk, lx):
    del mix_ref, sall_ref
    n_t, n_b = z_ref.shape[0], z_ref.shape[1]
    rep = SSD_HEADS // SSD_NGROUPS
    conv, tail = _conv_steps(buf_ref, xbc_ref, cw_ref, slice(0, SSD_CONV_DIM))
    for j in range(CONV_W - 1):
        nbuf_ref[j] = tail[j]
    xbc = [_silu(c + cb_ref[...]) for c in conv]
    dt, g = [], []
    for t in range(n_t):
        dt.append(_softplus(nar_ref[t] + dtb_ref[...]))
        la = dt[t] * a_ref[...]
        g.append(la if t == 0 else g[t - 1] + la)
    for grp in range(SSD_NGROUPS):
        bm = [x[:, SSD_INNER + grp * SSD_STATE:SSD_INNER + (grp + 1) * SSD_STATE] for x in xbc]
        cm = [x[:, SSD_INNER + SSD_BC + grp * SSD_STATE:SSD_INNER + SSD_BC + (grp + 1) * SSD_STATE] for x in xbc]
        sc = [[jnp.sum(cm[i] * bm[j], axis=-1, keepdims=True) for j in range(i + 1)] for i in range(n_t)]
        for h in range(grp * rep, (grp + 1) * rep):
            ln = NARROW_DT + h
            hc = slice(h * SSD_HEADDIM, (h + 1) * SSD_HEADDIM)
            gc = [x[:, ln:ln + 1] for x in g]
            dc = [x[:, ln:ln + 1] for x in dt]
            xs = [x[:, hc] for x in xbc]
            for t in range(n_t):
                qd_scr[h, t] = cm[t] * jnp.exp(gc[t])
                kd_scr[h, t] = bm[t] * (dc[t] * jnp.exp(gc[n_t - 1] - gc[t]))
                xs_scr[h, t] = xs[t]
                acc = None
                for j in range(t + 1):
                    term = (sc[t][j] * dc[j] * jnp.exp(gc[t] - gc[j])) * xs[j]
                    acc = term if acc is None else acc + term
                acc_scr[t, :, hc] = acc
            ga_scr[h] = jnp.broadcast_to(jnp.exp(gc[n_t - 1]), (n_b, SSD_HEADDIM))

    for tile in (lq, lk, lx):
        tile[...] = jnp.zeros_like(tile)

    def per_seq(b, carry):
        row = pl.ds(b, 1)
        for h in range(SSD_HEADS):
            hc = slice(h * SSD_HEADDIM, (h + 1) * SSD_HEADDIM)
            for t in range(n_t):
                lq[h, t:t + 1, :] = qd_scr[h, t, row, :]
                lk[h, t:t + 1, :] = kd_scr[h, t, row, :]
                lx[h, t:t + 1, :] = xs_scr[h, t, row, :]
            s0 = s0_ref[b, h]
            inter = jnp.dot(lq[h].astype(BF16), s0.astype(BF16), preferred_element_type=F32)
            s_ref[b, h] = s0 * ga_scr[h, row, :] + _dot_tn(lk[h].astype(BF16), lx[h].astype(BF16))
            for t in range(n_t):
                xs_scr[h, t, row, :] = inter[t:t + 1, :]
        return carry

    lax.fori_loop(0, n_b, per_seq, 0)

    gw = SSD_INNER // SSD_NGROUPS
    for t in range(n_t):
        for h in range(SSD_HEADS):
            hc = slice(h * SSD_HEADDIM, (h + 1) * SSD_HEADDIM)
            acc_scr[t, :, hc] = acc_scr[t, :, hc] + xs_scr[h, t]
        y = (acc_scr[t] + xbc[t][:, :SSD_INNER] * dsk_ref[...]) * _silu(z_ref[t])
        yn = [y[:, i * gw:(i + 1) * gw] * lax.rsqrt(
            jnp.mean(y[:, i * gw:(i + 1) * gw] * y[:, i * gw:(i + 1) * gw], axis=-1, keepdims=True) + EPS)
            for i in range(SSD_NGROUPS)]
        o_ref[t] = (jnp.concatenate(yn, axis=1) * nw_ref[...]).astype(o_ref.dtype)


def ssd_sample(y3, nar3, mixed3, buf0, states, new_states, layer, t0, n_t, lw):
    n_b = y3.shape[1]
    tb = _sample_block(t0, n_t)
    params = _ssd_params(lw, 1)[:-1]
    tok = lambda s, w: pl.BlockSpec((n_t, SAMPLE_BB, w), lambda i: (tb, i, PK_OFF[s] // w))
    bufspec = pl.BlockSpec((CONV_W - 1, SAMPLE_BB, SSD_CONV_DIM), lambda i: (0, i, 0))
    state = pl.BlockSpec((None, SAMPLE_BB, SSD_HEADS, SSD_STATE, SSD_HEADDIM), lambda i: (layer, i, 0, 0, 0))
    per_tok = lambda w: pltpu.VMEM((SSD_HEADS, n_t, SAMPLE_BB, w), F32)
    tile = lambda w: pltpu.VMEM((SSD_HEADS, LHS_ROWS, w), F32)
    n_in = 5 + len(params)
    mixed3, s, nbuf = pl.pallas_call(
        _ssd_sample_kernel,
        grid=(n_b // SAMPLE_BB,),
        in_specs=[tok(7, SSD_INNER), tok(8, SSD_CONV_DIM),
                  pl.BlockSpec((n_t, SAMPLE_BB, NARROW_W), lambda i: (tb, i, 0)), bufspec, state]
                 + [_const_spec(p.shape) for p in params] + [_ANY_SPEC, _ANY_SPEC],
        out_specs=[pl.BlockSpec((n_t, SAMPLE_BB, SSD_INNER), lambda i: (tb, i, MIX_SSD)), state, bufspec],
        out_shape=[jax.ShapeDtypeStruct(mixed3.shape, mixed3.dtype), jax.ShapeDtypeStruct(states.shape, F32),
                   jax.ShapeDtypeStruct((CONV_W - 1, n_b, SSD_CONV_DIM), F32)],
        input_output_aliases={n_in: 0, n_in + 1: 1},
        scratch_shapes=[per_tok(SSD_STATE), per_tok(SSD_STATE), per_tok(SSD_HEADDIM),
                        pltpu.VMEM((SSD_HEADS, SAMPLE_BB, SSD_HEADDIM), F32),
                        pltpu.VMEM((n_t, SAMPLE_BB, SSD_INNER), F32),
                        tile(SSD_STATE), tile(SSD_STATE), tile(SSD_HEADDIM)],
        compiler_params=pltpu.CompilerParams(
            dimension_semantics=("arbitrary",), vmem_limit_bytes=V7X_VMEM_LIMIT),
        name="ssd_sample",
    )(y3, y3, nar3, jnp.swapaxes(buf0, 0, 1), states, *params, mixed3, new_states)
    return mixed3, jnp.swapaxes(nbuf, 0, 1), s


def _gdn_sample_kernel(q_ref, k_ref, v_ref, z_ref, nar_ref, buf_ref, s0_ref, cw_ref, dtb_ref, a_ref, nw_ref,
                       mix_ref, sall_ref, o_ref, s_ref, nbuf_ref, w_scr, qd_scr, kd_scr, u_scr, ga_scr, lwq, lk, lu):
    del mix_ref, sall_ref
    n_t, n_b = q_ref.shape[0], q_ref.shape[1]
    qkv = []
    for i, ref in enumerate((q_ref, k_ref, v_ref)):
        cols = slice(i * GDN_QKV, (i + 1) * GDN_QKV)
        conv, tail = _conv_steps(buf_ref, ref, cw_ref, cols)
        for j in range(CONV_W - 1):
            nbuf_ref[j, :, cols] = tail[j]
        qkv.append([_silu(c) for c in conv])
    beta, g = [], []
    for t in range(n_t):
        nar = nar_ref[t]
        beta.append(jax.nn.sigmoid(nar))
        gl = a_ref[...] * _softplus(nar + dtb_ref[...])
        g.append(gl if t == 0 else g[t - 1] + gl)

    scores = []
    for h in range(GDN_HEADS):
        hc = slice(h * GDN_DK, (h + 1) * GDN_DK)
        qn = [_l2_rows(x[:, hc]) * (GDN_DK ** -0.5) for x in qkv[0]]
        kn = [_l2_rows(x[:, hc]) for x in qkv[1]]
        vh = [x[:, hc] for x in qkv[2]]
        bc = [x[:, NARROW_B + h:NARROW_B + h + 1] for x in beta]
        gc = [x[:, NARROW_A + h:NARROW_A + h + 1] for x in g]
        us, ws = [], []
        for i in range(n_t):
            u = vh[i] * bc[i]
            w = kn[i] * (bc[i] * jnp.exp(gc[i]))
            for j in range(i):
                a_ij = bc[i] * jnp.sum(kn[i] * kn[j], axis=-1, keepdims=True) * jnp.exp(gc[i] - gc[j])
                u = u - a_ij * us[j]
                w = w - a_ij * ws[j]
            us.append(u)
            ws.append(w)
            u_scr[h, i] = u
            w_scr[h, i] = w
            qd_scr[h, i] = qn[i] * jnp.exp(gc[i])
            kd_scr[h, i] = kn[i] * jnp.exp(gc[n_t - 1] - gc[i])
        ga_scr[h] = jnp.broadcast_to(jnp.exp(gc[n_t - 1]), (n_b, GDN_DV))
        scores.append([[jnp.sum(qn[i] * kn[j], axis=-1, keepdims=True) * jnp.exp(gc[i] - gc[j])
                        for j in range(i + 1)] for i in range(n_t)])

    for tile in (lwq, lk, lu):
        tile[...] = jnp.zeros_like(tile)

    def per_seq(b, carry):
        row = pl.ds(b, 1)
        for h in range(GDN_HEADS):
            for t in range(n_t):
                lwq[h, t:t + 1, :] = w_scr[h, t, row, :]
                lwq[h, LHS_ROWS + t:LHS_ROWS + t + 1, :] = qd_scr[h, t, row, :]
                lk[h, t:t + 1, :] = kd_scr[h, t, row, :]
                lu[h, t:t + 1, :] = u_scr[h, t, row, :]
            s0 = s0_ref[b, h]
            wq_s = jnp.dot(lwq[h].astype(BF16), s0.astype(BF16), preferred_element_type=F32)
            v_new = lu[h] - wq_s[:LHS_ROWS]
            s_ref[b, h] = s0 * ga_scr[h, row, :] + _dot_tn(lk[h].astype(BF16), v_new.astype(BF16))
            for t in range(n_t):
                u_scr[h, t, row, :] = v_new[t:t + 1, :]
                qd_scr[h, t, row, :] = wq_s[LHS_ROWS + t:LHS_ROWS + t + 1, :]
        return carry

    lax.fori_loop(0, n_b, per_seq, 0)

    for h in range(GDN_HEADS):
        hc = slice(h * GDN_DV, (h + 1) * GDN_DV)
        for i in range(n_t):
            o = qd_scr[h, i]
            for j in range(i + 1):
                o = o + scores[h][i][j] * u_scr[h, j]
            o = o * lax.rsqrt(jnp.mean(o * o, axis=-1, keepdims=True) + EPS) * nw_ref[...]
            o_ref[i, :, hc] = (o * _silu(z_ref[i, :, hc])).astype(o_ref.dtype)


def gdn_sample(y3, nar3, mixed3, buf0, states, new_states, layer, t0, n_t, lw):
    n_b = y3.shape[1]
    tb = _sample_block(t0, n_t)
    params = _gdn_params(lw, 1)[:4]
    tok = lambda s: pl.BlockSpec((n_t, SAMPLE_BB, GDN_QKV), lambda i: (tb, i, PK_OFF[s] // GDN_QKV))
    bufspec = pl.BlockSpec((CONV_W - 1, SAMPLE_BB, GDN_CONV_DIM), lambda i: (0, i, 0))
    state = pl.BlockSpec((None, SAMPLE_BB, GDN_HEADS, GDN_DK, GDN_DV), lambda i: (layer, i, 0, 0, 0))
    per_tok = pltpu.VMEM((GDN_HEADS, n_t, SAMPLE_BB, GDN_DK), F32)
    tile = lambda rows: pltpu.VMEM((GDN_HEADS, rows, GDN_DK), F32)
    n_in = 7 + len(params)
    mixed3, s, nbuf = pl.pallas_call(
        _gdn_sample_kernel,
        grid=(n_b // SAMPLE_BB,),
        in_specs=[tok(1), tok(2), tok(3), tok(4),
                  pl.BlockSpec((n_t, SAMPLE_BB, NARROW_W), lambda i: (tb, i, 0)), bufspec, state]
                 + [_const_spec(p.shape) for p in params] + [_ANY_SPEC, _ANY_SPEC],
        out_specs=[pl.BlockSpec((n_t, SAMPLE_BB, GDN_QKV), lambda i: (tb, i, MIX_GDN)), state, bufspec],
        out_shape=[jax.ShapeDtypeStruct(mixed3.shape, mixed3.dtype), jax.ShapeDtypeStruct(states.shape, F32),
                   jax.ShapeDtypeStruct((CONV_W - 1, n_b, GDN_CONV_DIM), F32)],
        input_output_aliases={n_in: 0, n_in + 1: 1},
        scratch_shapes=[per_tok, per_tok, per_tok, per_tok, pltpu.VMEM((GDN_HEADS, SAMPLE_BB, GDN_DV), F32),
                        tile(2 * LHS_ROWS), tile(LHS_ROWS), tile(LHS_ROWS)],
        compiler_params=pltpu.CompilerParams(
            dimension_semantics=("arbitrary",), vmem_limit_bytes=V7X_VMEM_LIMIT),
        name="gdn_sample",
    )(y3, y3, y3, y3, nar3, jnp.swapaxes(buf0, 0, 1), states, *params, mixed3, new_states)
    return mixed3, jnp.swapaxes(nbuf, 0, 1), s


def split_last(t, sizes):
    return jnp.split(t, [int(s) for s in np.cumsum(sizes)[:-1]], axis=-1)


def l2_normalize(x):
    xf = x.astype(jnp.float32)
    return xf * lax.rsqrt(jnp.sum(xf * xf, axis=-1, keepdims=True) + EPS)


def causal_conv(x, buf, w):
    L = x.shape[1]
    xp = jnp.concatenate([buf.astype(x.dtype), x], axis=1)
    y = xp[:, 0:L] * w[0]
    for j in range(1, CONV_W):
        y = y + xp[:, j:j + L] * w[j]
    return y, xp[:, L:]


def rotary(x, positions):
    half = x.shape[-1] // 2
    inv_freq = ROPE_BASE ** (-jnp.arange(half, dtype=jnp.float32) / half)
    ang = positions.astype(jnp.float32)[:, None] * inv_freq[None, :]
    cos = jnp.cos(ang)[None, :, None, :]
    sin = jnp.sin(ang)[None, :, None, :]
    xf = x.astype(jnp.float32)
    x1, x2 = xf[..., :half], xf[..., half:]
    return jnp.concatenate([x1 * cos - x2 * sin, x1 * sin + x2 * cos], axis=-1)


def to_chunks(t, chunk):
    L = t.shape[1]
    pad = (-L) % chunk
    t = jnp.pad(t, [(0, 0), (0, pad)] + [(0, 0)] * (t.ndim - 2))
    n = t.shape[1] // chunk
    t = t.reshape((t.shape[0], n, chunk) + t.shape[2:])
    return jnp.moveaxis(t, 2, 3)


def from_chunks(t, L):
    t = jnp.moveaxis(t, 3, 2)
    t = t.reshape((t.shape[0], t.shape[1] * t.shape[2]) + t.shape[3:])
    return t[:, :L]


def intra_decay(G):
    C = G.shape[-1]
    causal = jnp.tril(jnp.ones((C, C), dtype=bool))
    diff = G[..., :, None] - G[..., None, :]
    return jnp.where(causal, jnp.exp(jnp.where(causal, diff, 0.0)), 0.0)


def decay_linear_attention(q, k, v, log_a, s0, chunk):
    L = q.shape[1]
    C = min(chunk, L)
    qc, kc, vc = (to_chunks(t.astype(jnp.float32), C) for t in (q, k, v))
    G = jnp.cumsum(to_chunks(log_a.astype(jnp.float32), C), axis=-1)
    G_last = G[..., -1]
    scores = jnp.einsum('bnhid,bnhjd->bnhij', qc, kc) * intra_decay(G)
    intra = jnp.einsum('bnhij,bnhjv->bnhiv', scores, vc)
    chunk_states = jnp.einsum('bnhcd,bnhcv->bnhdv', kc * jnp.exp(G_last[..., None] - G)[..., None], vc)

    def step(S, inp):
        cs, gl = inp
        return S * gl[..., None, None] + cs, S

    s_final, s_prev = lax.scan(step, s0.astype(jnp.float32),
                               (jnp.moveaxis(chunk_states, 1, 0), jnp.moveaxis(jnp.exp(G_last), 1, 0)))
    s_prev = jnp.moveaxis(s_prev, 0, 1)
    inter = jnp.einsum('bnhcd,bnhdv->bnhcv', qc * jnp.exp(G)[..., None], s_prev)
    return from_chunks(intra + inter, L), s_final


def gated_delta_rule(q, k, v, beta, g, s0, chunk):
    L = q.shape[1]
    C = min(chunk, L)
    qc, kc, vc = (to_chunks(t.astype(jnp.float32), C) for t in (q, k, v))
    bc = to_chunks(beta.astype(jnp.float32), C)
    G = jnp.cumsum(to_chunks(g.astype(jnp.float32), C), axis=-1)
    decay = intra_decay(G)
    strict = jnp.tril(jnp.ones((C, C), dtype=bool), -1)
    kk = jnp.einsum('bnhid,bnhjd->bnhij', kc, kc)
    a_mat = jnp.where(strict, bc[..., :, None] * kk * decay, 0.0) + jnp.eye(C, dtype=jnp.float32)
    u = lax.linalg.triangular_solve(a_mat, vc * bc[..., None], left_side=True, lower=True, unit_diagonal=True)
    w = lax.linalg.triangular_solve(a_mat, kc * (bc * jnp.exp(G))[..., None], left_side=True, lower=True,
                                    unit_diagonal=True)
    scores = jnp.einsum('bnhid,bnhjd->bnhij', qc, kc) * decay
    q_dec = qc * jnp.exp(G)[..., None]
    k_dec = kc * jnp.exp(G[..., -1:] - G)[..., None]
    g_last = jnp.exp(G[..., -1])

    def step(S, inp):
        u_c, w_c, s_c, qd_c, kd_c, gl_c = inp
        v_new = u_c - jnp.einsum('bhcd,bhdv->bhcv', w_c, S)
        o = jnp.einsum('bhcd,bhdv->bhcv', qd_c, S) + jnp.einsum('bhij,bhjv->bhiv', s_c, v_new)
        S = S * gl_c[..., None, None] + jnp.einsum('bhcd,bhcv->bhdv', kd_c, v_new)
        return S, o

    xs = tuple(jnp.moveaxis(t, 1, 0) for t in (u, w, scores, q_dec, k_dec, g_last))
    s_final, o = lax.scan(step, s0.astype(jnp.float32), xs)
    return from_chunks(jnp.moveaxis(o, 0, 1), L), s_final


def complex_affine_combine(e1, e2):
    a1r, a1i, b1r, b1i = e1
    a2r, a2i, b2r, b2i = e2
    return (a2r * a1r - a2i * a1i, a2r * a1i + a2i * a1r,
            a2r * b1r - a2i * b1i + b2r, a2r * b1i + a2i * b1r + b2i)


def s5_mixer(u, h_re0, h_im0, lw):
    f32 = jnp.float32
    bsz, L, _ = u.shape
    a_re = lw['s5_a_re'].astype(f32)
    a_im = lw['s5_a_im'].astype(f32)
    step = jnp.exp(lw['s5_log_step'].astype(f32))[:, None]
    mag = jnp.exp(a_re * step)
    lam_re = mag * jnp.cos(a_im * step)
    lam_im = mag * jnp.sin(a_im * step)
    den = a_re * a_re + a_im * a_im
    coef_re = ((lam_re - 1.0) * a_re + lam_im * a_im) / den
    coef_im = (lam_im * a_re - (lam_re - 1.0) * a_im) / den
    b_re = lw['s5_b_re'].astype(f32)
    b_im = lw['s5_b_im'].astype(f32)
    bb_re = coef_re[..., None] * b_re - coef_im[..., None] * b_im
    bb_im = coef_re[..., None] * b_im + coef_im[..., None] * b_re
    ug = u.astype(f32).reshape(bsz, L, S5_GROUPS, S5_GROUP_CH)
    drive_re = jnp.einsum('blgc,gnc->blgn', ug, bb_re)
    drive_im = jnp.einsum('blgc,gnc->blgn', ug, bb_im)
    h0_re = h_re0.astype(f32)
    h0_im = h_im0.astype(f32)
    drive_re = drive_re.at[:, 0].add(lam_re * h0_re - lam_im * h0_im)
    drive_im = drive_im.at[:, 0].add(lam_re * h0_im + lam_im * h0_re)
    lam_re_b = jnp.broadcast_to(lam_re, drive_re.shape)
    lam_im_b = jnp.broadcast_to(lam_im, drive_im.shape)
    _, _, hs_re, hs_im = lax.associative_scan(complex_affine_combine, (lam_re_b, lam_im_b, drive_re, drive_im), axis=1)
    c_re = lw['s5_c_re'].astype(f32)
    c_im = lw['s5_c_im'].astype(f32)
    y = jnp.einsum('blgn,gcn->blgc', hs_re, c_re) - jnp.einsum('blgn,gcn->blgc', hs_im, c_im)
    y = y.reshape(bsz, L, S5_CH) + lw['s5_d'].astype(f32) * u.astype(f32)
    y = jax.nn.gelu(y)
    out = y * jax.nn.sigmoid(y @ lw['s5_w_glu'].astype(f32) + lw['s5_b_glu'].astype(f32))
    return out.astype(u.dtype), hs_re[:, -1], hs_im[:, -1]


def gdn_mixer(q, k, v, z, b_logit, a_logit, buf0, s0, lw):
    dt_out = z.dtype
    bsz, L, _ = q.shape
    qkv, buf = causal_conv(jnp.concatenate([q, k, v], axis=-1), buf0, lw['gdn_conv_w'])
    qkv = jax.nn.silu(qkv)
    qh, kh, vh = split_last(qkv, (GDN_HEADS * GDN_DK, GDN_HEADS * GDN_DK, GDN_HEADS * GDN_DV))
    qh = l2_normalize(qh.reshape(bsz, L, GDN_HEADS, GDN_DK)) * (GDN_DK ** -0.5)
    kh = l2_normalize(kh.reshape(bsz, L, GDN_HEADS, GDN_DK))
    vh = vh.reshape(bsz, L, GDN_HEADS, GDN_DV)
    beta = jax.nn.sigmoid(b_logit.astype(jnp.float32))
    g = -jnp.exp(lw['gdn_a_log'].astype(jnp.float32)) * jax.nn.softplus(
        a_logit.astype(jnp.float32) + lw['gdn_dt_bias'].astype(jnp.float32))
    o, s = gated_delta_rule(qh, kh, vh, beta, g, s0, GDN_CHUNK)
    o = o * lax.rsqrt(jnp.mean(o * o, axis=-1, keepdims=True) + EPS) * lw['gdn_norm_w'].astype(jnp.float32)
    o = o * jax.nn.silu(z.astype(jnp.float32).reshape(bsz, L, GDN_HEADS, GDN_DV))
    return o.reshape(bsz, L, GDN_HEADS * GDN_DV).astype(dt_out), buf, s


def ssd_mixer(z, xbc, dt_raw, buf0, s0, lw):
    f32 = jnp.float32
    bsz, L, _ = z.shape
    xbc, buf = causal_conv(xbc, buf0, lw['ssd_conv_w'])
    xbc = jax.nn.silu(xbc + lw['ssd_conv_b'])
    xs, bm, cm = split_last(xbc, (SSD_INNER, SSD_NGROUPS * SSD_STATE, SSD_NGROUPS * SSD_STATE))
    rep = SSD_HEADS // SSD_NGROUPS
    xs = xs.astype(f32).reshape(bsz, L, SSD_HEADS, SSD_HEADDIM)
    bm = jnp.repeat(bm.astype(f32).reshape(bsz, L, SSD_NGROUPS, SSD_STATE), rep, axis=2)
    cm = jnp.repeat(cm.astype(f32).reshape(bsz, L, SSD_NGROUPS, SSD_STATE), rep, axis=2)
    dt = jax.nn.softplus(dt_raw.astype(f32) + lw['ssd_dt_bias'].astype(f32))
    a = -jnp.exp(lw['ssd_a_log'].astype(f32))
    y, s = decay_linear_attention(cm, bm * dt[..., None], xs, dt * a, s0, SSD_CHUNK)
    y = y + xs * lw['ssd_d'].astype(f32)[:, None]
    y = y.reshape(bsz, L, SSD_INNER) * jax.nn.silu(z.astype(f32))
    y = y.reshape(bsz, L, SSD_NGROUPS, SSD_INNER // SSD_NGROUPS)
    y = y * lax.rsqrt(jnp.mean(y * y, axis=-1, keepdims=True) + EPS)
    y = y.reshape(bsz, L, SSD_INNER) * lw['ssd_norm_w'].astype(f32)
    return y.astype(z.dtype), buf, s


def retention_mixer(q, k, v, gate, s0, positions, lw):
    f32 = jnp.float32
    bsz, L, _ = q.shape
    qh = rotary(q.reshape(bsz, L, RET_HEADS, RET_DK), positions)
    kh = rotary(k.reshape(bsz, L, RET_HEADS, RET_DK), positions) * (RET_DK ** -0.5)
    vh = v.reshape(bsz, L, RET_HEADS, RET_DV)
    log_gamma = jnp.log(1.0 - 2.0 ** (-5.0 - jnp.arange(RET_HEADS, dtype=f32)))
    log_a = jnp.broadcast_to(log_gamma, (bsz, L, RET_HEADS))
    o, s = decay_linear_attention(qh, kh, vh, log_a, s0, RET_CHUNK)
    mu = jnp.mean(o, axis=-1, keepdims=True)
    var = jnp.mean((o - mu) ** 2, axis=-1, keepdims=True)
    o = ((o - mu) * lax.rsqrt(var + EPS)).reshape(bsz, L, RET_HEADS * RET_DV)
    o = o * lw['ret_ln_w'].astype(f32) + lw['ret_ln_b'].astype(f32)
    out = jax.nn.silu(gate.astype(f32)) * o
    return out.astype(q.dtype), s


def _mixers(y, st, lw, positions):
    def col(s):
        return y[..., PK_OFF[s]:PK_OFF[s] + IN_SIZES[s]]

    gdn_s0, gdn_buf0, ssd_s0, ssd_buf0, ret_s0 = st
    out_b, gdn_buf, gdn_s = gdn_mixer(col(1), col(2), col(3), col(4), col(5), col(6), gdn_buf0, gdn_s0, lw)
    out_c, ssd_buf, ssd_s = ssd_mixer(col(7), col(8), col(9), ssd_buf0, ssd_s0, lw)
    out_d, ret_s = retention_mixer(col(10), col(11), col(12), col(13), ret_s0, positions, lw)
    mixed = jnp.concatenate([out_b, out_c, out_d], axis=-1).astype(BF16)
    return mixed, (gdn_s, gdn_buf, ssd_s, ssd_buf, ret_s)


TM = 1088
TM_SPLIT = 512


def kernel(x_prompt, x_sample, p_prompt, p_sample, state_s5_re, state_s5_im, state_gdn, state_gdn_conv, state_ssd, state_ssd_conv, state_ret, norm_mix, w_in, s5_a_re, s5_a_im, s5_b_re, s5_b_im, s5_c_re, s5_c_im, s5_d, s5_log_step, s5_w_glu, s5_b_glu, gdn_conv_w, gdn_a_log, gdn_dt_bias, gdn_norm_w, ssd_conv_w, ssd_conv_b, ssd_dt_bias, ssd_a_log, ssd_d, ssd_norm_w, ret_ln_w, ret_ln_b, w_out, norm_ffn, w_ffn_in, w_ffn_out, norm_ple, w_ple_gate, w_ple_proj, norm_final):
    bp, lp, d = x_prompt.shape
    bs, ls, _ = x_sample.shape
    np_tok = bp * lp
    ns_tok = bs * ls
    n_tok = np_tok + ns_tok

    mixer_w = dict(
        s5_a_re=s5_a_re, s5_a_im=s5_a_im, s5_b_re=s5_b_re, s5_b_im=s5_b_im, s5_c_re=s5_c_re, s5_c_im=s5_c_im,
        s5_d=s5_d, s5_log_step=s5_log_step, s5_w_glu=s5_w_glu, s5_b_glu=s5_b_glu,
        gdn_conv_w=gdn_conv_w, gdn_a_log=gdn_a_log, gdn_dt_bias=gdn_dt_bias, gdn_norm_w=gdn_norm_w,
        ssd_conv_w=ssd_conv_w, ssd_conv_b=ssd_conv_b, ssd_dt_bias=ssd_dt_bias, ssd_a_log=ssd_a_log,
        ssd_d=ssd_d, ssd_norm_w=ssd_norm_w, ret_ln_w=ret_ln_w, ret_ln_b=ret_ln_b)

    assert bp == 4, "the prompt S5 kernel packs two time steps of four sequences per vreg"

    def tm_rows(t):
        t = jnp.swapaxes(t, -3, -2)
        return t.reshape(t.shape[:-3] + (t.shape[-3] * t.shape[-2], t.shape[-1]))

    def bm_seqs(t, b):
        return jnp.swapaxes(t.reshape(t.shape[0] // b, b, t.shape[1]), 0, 1)

    w_out_b, w_ffn_in_b, w_ffn_out_b = (w.astype(BF16) for w in (w_out, w_ffn_in, w_ffn_out))
    w_gate_b, w_proj_b = w_ple_gate.astype(BF16), w_ple_proj.astype(BF16)
    pp = p_prompt.reshape(DEPTH, np_tok, PLE_DIM)
    ps = tm_rows(p_sample)

    h = jnp.concatenate([x_prompt.reshape(np_tok, d), tm_rows(x_sample)], axis=0)
    new_gdn, new_ssd, new_ret = (uninitialized(s.shape, F32) for s in (state_gdn, state_ssd, state_ret))
    new_p, new_s = [], []
    for i in range(DEPTH):
        lw = {k: v[i] for k, v in mixer_w.items()}
        y, nar = in_projection(h, norm_mix[i], w_in[i], tm=TM)
        y3 = y.reshape(n_tok // bs, bs, PK_TOTAL)
        nar3 = nar.reshape(n_tok // bs, bs, NARROW_W)
        t0 = np_tok // bs

        tb = _s5_tables(lw)
        a_p, st5_p = s5_prompt(tm_rows(y[:np_tok, :S5_CH].reshape(bp, lp, S5_CH)), tb, rows=512)

        mixed = uninitialized((n_tok, d), BF16)
        mixed, gdn_buf_p, gdn_s_p = gdn_prompt(y, nar, mixed, bp, lp, lw)
        mixed, ssd_buf_p, ssd_s_p = ssd_prompt(y, nar, mixed, bp, lp, lw)
        mixed, ret_s_p = ret_prompt(y, mixed, bp, lp, lw)
        mixed, st5_s = s5_sample(y, mixed, _s5_state_to_lanes(state_s5_re[i], state_s5_im[i]), np_tok, ns_tok, tb)
        mixed3 = mixed.reshape(n_tok // bs, bs, d)
        mixed3, gdn_buf_s, new_gdn = gdn_sample(y3, nar3, mixed3, state_gdn_conv[i], state_gdn, new_gdn, i, t0, ls, lw)
        mixed3, ssd_buf_s, new_ssd = ssd_sample(y3, nar3, mixed3, state_ssd_conv[i], state_ssd, new_ssd, i, t0, ls, lw)
        mixed3, new_ret = ret_sample(y3, mixed3, state_ret, new_ret, i, t0, ls, PAST_LEN, lw)
        mixed = lax.dynamic_update_slice(mixed3.reshape(n_tok, d), bm_seqs(a_p, bp).reshape(np_tok, S5_CH), (0, 0))
        new_p.append(_s5_lanes_to_state(st5_p[:bp]) + (gdn_s_p, gdn_buf_p, ssd_s_p, ssd_buf_p, ret_s_p))
        new_s.append(_s5_lanes_to_state(st5_s) + (gdn_buf_s, ssd_buf_s))

        h = matmul_residual(mixed, w_out_b, i, h, tm=TM)
        h = ffn_residual(h, norm_ffn[i], w_ffn_in_b, w_ffn_out_b, i, tm=TM // 2, th=512)
        if i + 1 < DEPTH:
            h = ple_residual(h, norm_ple[i], w_gate_b, pp, ps, w_proj_b, i, tm=TM_SPLIT)
        else:
            y_p, y_s = ple_residual(h, norm_ple[i], w_gate_b, pp, ps, w_proj_b, i, tm=TM_SPLIT, nf=norm_final)

    stack_p = [jnp.stack([st[j] for st in new_p]) for j in range(7)]
    s5_re_s, s5_im_s, gdn_buf_s, ssd_buf_s = (jnp.stack([st[j] for st in new_s]) for j in range(4))
    return (y_p.reshape(bp, lp, d), bm_seqs(y_s, bs), *stack_p,
            s5_re_s, s5_im_s, new_gdn, gdn_buf_s, new_ssd, ssd_buf_s, new_ret)
```

```python
import functools
import math

import jax
import jax.numpy as jnp
import numpy as np
from jax import lax
from jax.experimental import pallas as pl
from jax.experimental.pallas import tpu as pltpu

F32 = jnp.float32
BF16 = jnp.bfloat16

D_MODEL = 2048
DEPTH = 2
GROUP_WIDTH = D_MODEL // 4
CONV_W = 4
EPS = 1e-6
PLE_DIM = 256
FFN_HIDDEN = ((8 * D_MODEL + 3 * 256 - 1) // (3 * 256)) * 256

S5_CH = GROUP_WIDTH
S5_GROUP_CH = 16
S5_GROUPS = S5_CH // S5_GROUP_CH
S5_STATE = 64

GDN_HEADS = 4
GDN_DK = GROUP_WIDTH // GDN_HEADS
GDN_DV = GROUP_WIDTH // GDN_HEADS
GDN_CHUNK = 64
GDN_CONV_DIM = 2 * GDN_HEADS * GDN_DK + GDN_HEADS * GDN_DV

SSD_INNER = GROUP_WIDTH
SSD_HEADDIM = 64
SSD_HEADS = SSD_INNER // SSD_HEADDIM
SSD_NGROUPS = 2
SSD_STATE = 128
SSD_CHUNK = 128
SSD_CONV_DIM = SSD_INNER + 2 * SSD_NGROUPS * SSD_STATE

RET_HEADS = 4
RET_DK = GROUP_WIDTH // RET_HEADS
RET_DV = GROUP_WIDTH // RET_HEADS
RET_CHUNK = 128
ROPE_BASE = 10000.0
PAST_LEN = 16384

IN_SIZES = (
    S5_CH,
    GDN_HEADS * GDN_DK, GDN_HEADS * GDN_DK, GDN_HEADS * GDN_DV, GDN_HEADS * GDN_DV, GDN_HEADS, GDN_HEADS,
    SSD_INNER, SSD_CONV_DIM, SSD_HEADS,
    RET_HEADS * RET_DK, RET_HEADS * RET_DK, RET_HEADS * RET_DV, RET_HEADS * RET_DV,
)
IN_OFFS = tuple(int(v) for v in np.cumsum((0,) + IN_SIZES))

_REGIONS = ((0, 1, 2, 3, 4), (7, 8), (10, 11, 12, 13))
_NARROW = (5, 6, 9)
PK_OFF = {}
_o = 0
for _reg in _REGIONS:
    for _s in _reg:
        PK_OFF[_s] = _o
        _o += IN_SIZES[_s]
PK_TOTAL = _o
IN_TN = 512
REGION_TILES = tuple(sum(IN_SIZES[s] for s in reg) // IN_TN for reg in _REGIONS)
NARROW_W = 128

V7X_VMEM_LIMIT = 58 * 1024 * 1024


def _split_w_in(w_in):
    wide = [w_in[:, IN_OFFS[reg[0]]:IN_OFFS[reg[-1] + 1]].astype(BF16) for reg in _REGIONS]
    nar = jnp.concatenate([w_in[:, IN_OFFS[s]:IN_OFFS[s + 1]] for s in _NARROW], axis=1)
    nar = jnp.pad(nar, ((0, 0), (0, NARROW_W - nar.shape[1]))).astype(BF16)
    return wide, nar


def _rms_rows(x, nw):
    ms = jnp.mean(x * x, axis=-1, keepdims=True)
    return x * lax.rsqrt(ms + EPS) * nw


def _in_proj_kernel(x_ref, nw_ref, wa_ref, wb_ref, wc_ref, wn_ref, o_ref, nar_ref, xn_ref):
    j = pl.program_id(1)

    @pl.when(j == 0)
    def _():
        xn_ref[...] = _rms_rows(x_ref[...], nw_ref[...]).astype(BF16)
        nar_ref[...] = jnp.dot(xn_ref[...], wn_ref[...], preferred_element_type=F32)

    first = 0
    for w_ref, n_tiles in zip((wa_ref, wb_ref, wc_ref), REGION_TILES):
        @pl.when((j >= first) & (j < first + n_tiles))
        def _(w_ref=w_ref):
            o_ref[...] = jnp.dot(xn_ref[...], w_ref[...], preferred_element_type=F32)
        first += n_tiles


def in_projection(x, nw, w_in, *, tm):
    m, k = x.shape
    wide, nar = _split_w_in(w_in)
    starts = [sum(REGION_TILES[:r]) for r in range(len(REGION_TILES))]

    def region_spec(r):
        return pl.BlockSpec((k, IN_TN), lambda i, j: (0, jnp.clip(j - starts[r], 0, REGION_TILES[r] - 1)))

    return pl.pallas_call(
        _in_proj_kernel,
        grid=(m // tm, PK_TOTAL // IN_TN),
        in_specs=[
            pl.BlockSpec((tm, k), lambda i, j: (i, 0)),
            pl.BlockSpec((1, k), lambda i, j: (0, 0)),
            region_spec(0), region_spec(1), region_spec(2),
            pl.BlockSpec((k, NARROW_W), lambda i, j: (0, 0)),
        ],
        out_specs=[pl.BlockSpec((tm, IN_TN), lambda i, j: (i, j)),
                   pl.BlockSpec((tm, NARROW_W), lambda i, j: (i, 0))],
        out_shape=[jax.ShapeDtypeStruct((m, PK_TOTAL), F32), jax.ShapeDtypeStruct((m, NARROW_W), F32)],
        scratch_shapes=[pltpu.VMEM((tm, k), BF16)],
        compiler_params=pltpu.CompilerParams(
            dimension_semantics=("arbitrary", "arbitrary"), vmem_limit_bytes=V7X_VMEM_LIMIT),
        name="in_projection",
    )(x, nw.reshape(1, k), *wide, nar)


def _mm_res_kernel(a_ref, w_ref, h_ref, o_ref):
    o_ref[...] = h_ref[...] + jnp.dot(a_ref[...], w_ref[...], preferred_element_type=F32)


def matmul_residual(a, w, layer, h, *, tm):
    m, k = a.shape
    n = w.shape[2]
    return pl.pallas_call(
        _mm_res_kernel,
        grid=(m // tm,),
        in_specs=[
            pl.BlockSpec((tm, k), lambda i: (i, 0)),
            pl.BlockSpec((None, k, n), lambda i: (layer, 0, 0)),
            pl.BlockSpec((tm, n), lambda i: (i, 0)),
        ],
        out_specs=pl.BlockSpec((tm, n), lambda i: (i, 0)),
        out_shape=jax.ShapeDtypeStruct((m, n), F32),
        compiler_params=pltpu.CompilerParams(
            dimension_semantics=("arbitrary",), vmem_limit_bytes=V7X_VMEM_LIMIT),
        name="matmul_residual",
    )(a, w, h)


def _silu(x):
    return x * jax.nn.sigmoid(x)


def _ffn_kernel(h_ref, nw_ref, wg_ref, wu_ref, wo_ref, o_ref, xn_ref):
    @pl.when(pl.program_id(1) == 0)
    def _():
        h = h_ref[...]
        xn_ref[...] = _rms_rows(h, nw_ref[...]).astype(BF16)
        o_ref[...] = h

    xn = xn_ref[...]
    gate = jnp.dot(xn, wg_ref[...], preferred_element_type=F32)
    up = jnp.dot(xn, wu_ref[...], preferred_element_type=F32)
    act = (_silu(gate) * up).astype(BF16)
    o_ref[...] += jnp.dot(act, wo_ref[...], preferred_element_type=F32)


def ffn_residual(h, nw, w_in, w_out, layer, *, tm, th):
    m, k = h.shape
    hidden = w_out.shape[1]
    nj = hidden // th
    return pl.pallas_call(
        _ffn_kernel,
        grid=(m // tm, nj),
        in_specs=[
            pl.BlockSpec((tm, k), lambda i, j: (i, 0)),
            pl.BlockSpec((1, k), lambda i, j: (0, 0)),
            pl.BlockSpec((None, k, th), lambda i, j: (layer, 0, j)),
            pl.BlockSpec((None, k, th), lambda i, j: (layer, 0, j + nj)),
            pl.BlockSpec((None, th, k), lambda i, j: (layer, j, 0)),
        ],
        out_specs=pl.BlockSpec((tm, k), lambda i, j: (i, 0)),
        out_shape=jax.ShapeDtypeStruct((m, k), F32),
        scratch_shapes=[pltpu.VMEM((tm, k), BF16)],
        compiler_params=pltpu.CompilerParams(
            dimension_semantics=("arbitrary", "arbitrary"), vmem_limit_bytes=V7X_VMEM_LIMIT),
        name="ffn_residual",
    )(h, nw.reshape(1, k), w_in, w_in, w_out)


def _ple_rows(h_ref, nw_ref, wg_ref, pp_ref, ps_ref, wp_ref, n_prompt_tiles):
    h = h_ref[...]
    xn = _rms_rows(h, nw_ref[...]).astype(BF16)
    gate = jax.nn.sigmoid(jnp.dot(xn, wg_ref[...], preferred_element_type=F32))
    p = jnp.where(pl.program_id(0) < n_prompt_tiles, pp_ref[...], ps_ref[...])
    return h + gate * jnp.dot(p.astype(BF16), wp_ref[...], preferred_element_type=F32)


def _ple_kernel(h_ref, nw_ref, wg_ref, pp_ref, ps_ref, wp_ref, o_ref, *, n_prompt_tiles):
    o_ref[...] = _ple_rows(h_ref, nw_ref, wg_ref, pp_ref, ps_ref, wp_ref, n_prompt_tiles)


def _ple_final_kernel(h_ref, nw_ref, wg_ref, pp_ref, ps_ref, wp_ref, nf_ref, yp_ref, ys_ref, *, n_prompt_tiles):
    y = _rms_rows(_ple_rows(h_ref, nw_ref, wg_ref, pp_ref, ps_ref, wp_ref, n_prompt_tiles), nf_ref[...])

    @pl.when(pl.program_id(0) < n_prompt_tiles)
    def _():
        yp_ref[...] = y

    @pl.when(pl.program_id(0) >= n_prompt_tiles)
    def _():
        ys_ref[...] = y


def ple_residual(h, nw, wg, pp, ps, wp, layer, *, tm, nf=None):
    m, k = h.shape
    mp, pd = pp.shape[1], pp.shape[2]
    ms = ps.shape[1]
    assert mp % tm == 0 and ms % tm == 0 and mp + ms == m
    npt, nst = mp // tm, ms // tm
    row = pl.BlockSpec((tm, k), lambda i: (i, 0))
    vec = pl.BlockSpec((1, k), lambda i: (0, 0))
    in_specs = [row, vec,
                pl.BlockSpec((None, k, k), lambda i: (layer, 0, 0)),
                pl.BlockSpec((None, tm, pd), lambda i: (layer, jnp.minimum(i, npt - 1), 0)),
                pl.BlockSpec((None, tm, pd), lambda i: (layer, jnp.clip(i - npt, 0, nst - 1), 0)),
                pl.BlockSpec((None, pd, k), lambda i: (layer, 0, 0))]
    args = [h, nw.reshape(1, k), wg, pp, ps, wp]
    params = pltpu.CompilerParams(dimension_semantics=("arbitrary",), vmem_limit_bytes=V7X_VMEM_LIMIT)
    if nf is None:
        return pl.pallas_call(
            functools.partial(_ple_kernel, n_prompt_tiles=npt), grid=(m // tm,), in_specs=in_specs,
            out_specs=row, out_shape=jax.ShapeDtypeStruct((m, k), F32), compiler_params=params,
            name="ple_residual")(*args)
    return pl.pallas_call(
        functools.partial(_ple_final_kernel, n_prompt_tiles=npt), grid=(m // tm,), in_specs=in_specs + [vec],
        out_specs=[pl.BlockSpec((tm, k), lambda i: (jnp.minimum(i, npt - 1), 0)),
                   pl.BlockSpec((tm, k), lambda i: (jnp.clip(i - npt, 0, nst - 1), 0))],
        out_shape=[jax.ShapeDtypeStruct((mp, k), F32), jax.ShapeDtypeStruct((ms, k), F32)],
        compiler_params=params, name="ple_final")(*args, nf.reshape(1, k))


S5_HALF_CH = S5_CH // 2
S5_HALF_ST = (S5_GROUPS // 2) * S5_STATE
S5_LANES = 4 * S5_HALF_ST
S5_SLAB = 512


def _s5_tables(lw):
    a_re = lw['s5_a_re'].astype(F32)
    a_im = lw['s5_a_im'].astype(F32)
    step = jnp.exp(lw['s5_log_step'].astype(F32))[:, None]
    mag = jnp.exp(a_re * step)
    lam_re = mag * jnp.cos(a_im * step)
    lam_im = mag * jnp.sin(a_im * step)
    den = a_re * a_re + a_im * a_im
    coef_re = ((lam_re - 1.0) * a_re + lam_im * a_im) / den
    coef_im = (lam_im * a_re - (lam_re - 1.0) * a_im) / den
    b_re = lw['s5_b_re'].astype(F32)
    b_im = lw['s5_b_im'].astype(F32)
    bb_re = coef_re[..., None] * b_re - coef_im[..., None] * b_im
    bb_im = coef_re[..., None] * b_im + coef_im[..., None] * b_re
    gh = S5_GROUPS // 2
    eye = jnp.eye(gh, dtype=F32)

    def in_blockdiag(b):
        return jnp.einsum('gnc,gh->gchn', b, eye).reshape(gh * S5_GROUP_CH, gh * S5_STATE)

    def out_blockdiag(c):
        return jnp.einsum('gcn,gh->gnhc', c, eye).reshape(gh * S5_STATE, gh * S5_GROUP_CH)

    c_re = lw['s5_c_re'].astype(F32)
    c_im = lw['s5_c_im'].astype(F32)
    bb = jnp.stack([jnp.concatenate([in_blockdiag(bb_re[h * gh:(h + 1) * gh]),
                                     in_blockdiag(bb_im[h * gh:(h + 1) * gh])], axis=1) for h in range(2)])
    cm = jnp.stack([jnp.concatenate([out_blockdiag(c_re[h * gh:(h + 1) * gh]),
                                     -out_blockdiag(c_im[h * gh:(h + 1) * gh])], axis=0) for h in range(2)])
    lam = jnp.stack([lam_re.reshape(-1), lam_im.reshape(-1)])
    lam2 = jnp.stack([lam[0] * lam[0] - lam[1] * lam[1], 2.0 * lam[0] * lam[1]])
    return dict(bb=bb.astype(BF16), cm=cm.astype(BF16), lam=lam, lam2=lam2,
                d=lw['s5_d'].astype(F32).reshape(1, S5_CH), wglu=lw['s5_w_glu'].astype(BF16),
                bglu=lw['s5_b_glu'].astype(F32).reshape(1, S5_CH))


def _s5_drive(u, bb_ref, sc_ref):
    ub = u.astype(BF16)
    for hf in range(2):
        sc_ref[:, hf * 2 * S5_HALF_ST:(hf + 1) * 2 * S5_HALF_ST] = jnp.dot(
            ub[:, hf * S5_HALF_CH:(hf + 1) * S5_HALF_CH], bb_ref[hf], preferred_element_type=F32)


def _s5_readout(sc_ref, u, cm_ref, d_ref, wglu_ref, bglu_ref):
    ys = [jnp.dot(sc_ref[:, hf * 2 * S5_HALF_ST:(hf + 1) * 2 * S5_HALF_ST].astype(BF16), cm_ref[hf],
                  preferred_element_type=F32) for hf in range(2)]
    y = jnp.concatenate(ys, axis=1) + d_ref[...] * u
    y = jax.nn.gelu(y)
    z = jnp.dot(y.astype(BF16), wglu_ref[...], preferred_element_type=F32) + bglu_ref[...]
    return y * jax.nn.sigmoid(z)


def _s5_slabs():
    for hf in range(2):
        for sl in range(S5_HALF_ST // S5_SLAB):
            re0 = hf * 2 * S5_HALF_ST + sl * S5_SLAB
            yield re0, re0 + S5_HALF_ST, hf * S5_HALF_ST + sl * S5_SLAB


def _s5_prompt_kernel(u_ref, bb_ref, m_ref, cm_ref, d_ref, wglu_ref, bglu_ref, o_ref, st_ref, sc_ref, carry_ref):
    @pl.when(pl.program_id(0) == 0)
    def _():
        carry_ref[...] = jnp.zeros_like(carry_ref)

    u = u_ref[...]
    _s5_drive(u, bb_ref, sc_ref)
    first_step = lax.broadcasted_iota(jnp.int32, (8, S5_SLAB), 0) < 4
    n_pairs = u_ref.shape[0] // 8
    for re0, im0, l0 in _s5_slabs():
        mr = m_ref[0, :, l0:l0 + S5_SLAB]
        mi = m_ref[1, :, l0:l0 + S5_SLAB]
        nr = m_ref[2, :, l0:l0 + S5_SLAB]
        ni = m_ref[3, :, l0:l0 + S5_SLAB]

        def body(k, carry, re0=re0, im0=im0, mr=mr, mi=mi, nr=nr, ni=ni):
            hr, hi = carry
            base = pl.multiple_of(k * 8, 8)
            xr = sc_ref[pl.ds(base, 8), re0:re0 + S5_SLAB]
            xi = sc_ref[pl.ds(base, 8), im0:im0 + S5_SLAB]
            xr_s = pltpu.roll(xr, 4, 0)
            xi_s = pltpu.roll(xi, 4, 0)
            outr = (mr * hr - mi * hi) + xr + (nr * xr_s - ni * xi_s)
            outi = (mr * hi + mi * hr) + xi + (nr * xi_s + ni * xr_s)
            sc_ref[pl.ds(base, 8), re0:re0 + S5_SLAB] = outr
            sc_ref[pl.ds(base, 8), im0:im0 + S5_SLAB] = outi
            return (jnp.where(first_step, pltpu.roll(outr, 4, 0), outr),
                    jnp.where(first_step, pltpu.roll(outi, 4, 0), outi))

        hr, hi = lax.fori_loop(0, n_pairs, body,
                               (carry_ref[:, re0:re0 + S5_SLAB], carry_ref[:, im0:im0 + S5_SLAB]))
        carry_ref[:, re0:re0 + S5_SLAB] = hr
        carry_ref[:, im0:im0 + S5_SLAB] = hi

    o_ref[...] = _s5_readout(sc_ref, u, cm_ref, d_ref, wglu_ref, bglu_ref).astype(o_ref.dtype)
    st_ref[...] = carry_ref[...]


def _const_spec(shape):
    return pl.BlockSpec(shape, lambda c: (0,) * len(shape))


def s5_prompt(u_tm, tb, *, rows):
    n = u_tm.shape[0]
    zero = jnp.zeros_like(tb['lam'])
    m = jnp.stack([jnp.concatenate([jnp.broadcast_to(a[k][None], (4, a.shape[1])),
                                    jnp.broadcast_to(b[k][None], (4, b.shape[1]))], axis=0)
                   for a, b, k in ((tb['lam'], tb['lam2'], 0), (tb['lam'], tb['lam2'], 1),
                                   (zero, tb['lam'], 0), (zero, tb['lam'], 1))])
    return pl.pallas_call(
        _s5_prompt_kernel,
        grid=(n // rows,),
        in_specs=[
            pl.BlockSpec((rows, S5_CH), lambda c: (c, 0)),
            _const_spec(tb['bb'].shape), _const_spec(m.shape), _const_spec(tb['cm'].shape),
            _const_spec((1, S5_CH)), _const_spec((S5_CH, S5_CH)), _const_spec((1, S5_CH)),
        ],
        out_specs=[pl.BlockSpec((rows, S5_CH), lambda c: (c, 0)), _const_spec((8, S5_LANES))],
        out_shape=[jax.ShapeDtypeStruct((n, S5_CH), BF16), jax.ShapeDtypeStruct((8, S5_LANES), F32)],
        scratch_shapes=[pltpu.VMEM((rows, S5_LANES), F32), pltpu.VMEM((8, S5_LANES), F32)],
        compiler_params=pltpu.CompilerParams(
            dimension_semantics=("arbitrary",), vmem_limit_bytes=V7X_VMEM_LIMIT),
        name="s5_prompt",
    )(u_tm, tb['bb'], m, tb['cm'], tb['d'], tb['wglu'], tb['bglu'])


def _s5_sample_kernel(u_ref, h0_ref, bb_ref, lam_ref, cm_ref, d_ref, wglu_ref, bglu_ref, mix_ref,
                      o_ref, st_ref, sc_ref):
    del mix_ref
    u = u_ref[...]
    _s5_drive(u, bb_ref, sc_ref)
    n_seq = h0_ref.shape[0]
    n_steps = u_ref.shape[0] // n_seq
    for re0, im0, l0 in _s5_slabs():
        lr = lam_ref[0:1, l0:l0 + S5_SLAB]
        li = lam_ref[1:2, l0:l0 + S5_SLAB]

        def body(rb, _, re0=re0, im0=im0, lr=lr, li=li):
            r0 = pl.multiple_of(rb * 8, 8)
            hr = h0_ref[pl.ds(r0, 8), re0:re0 + S5_SLAB]
            hi = h0_ref[pl.ds(r0, 8), im0:im0 + S5_SLAB]
            for t in range(n_steps):
                rows = pl.ds(t * n_seq + r0, 8)
                nr = (lr * hr - li * hi) + sc_ref[rows, re0:re0 + S5_SLAB]
                ni = (lr * hi + li * hr) + sc_ref[rows, im0:im0 + S5_SLAB]
                sc_ref[rows, re0:re0 + S5_SLAB] = nr
                sc_ref[rows, im0:im0 + S5_SLAB] = ni
                hr, hi = nr, ni
            st_ref[pl.ds(r0, 8), re0:re0 + S5_SLAB] = hr
            st_ref[pl.ds(r0, 8), im0:im0 + S5_SLAB] = hi
            return 0

        lax.fori_loop(0, n_seq // 8, body, 0)

    o_ref[...] = _s5_readout(sc_ref, u, cm_ref, d_ref, wglu_ref, bglu_ref).astype(o_ref.dtype)


def s5_sample(y, mixed, h0, row0, n_rows, tb):
    nb = h0.shape[0]
    assert row0 % n_rows == 0
    blk = row0 // n_rows
    return pl.pallas_call(
        _s5_sample_kernel,
        grid=(1,),
        in_specs=[pl.BlockSpec((n_rows, S5_CH), lambda c: (blk, PK_OFF[0] // S5_CH)),
                  _const_spec((nb, S5_LANES)), _const_spec(tb['bb'].shape),
                  _const_spec(tb['lam'].shape), _const_spec(tb['cm'].shape), _const_spec((1, S5_CH)),
                  _const_spec((S5_CH, S5_CH)), _const_spec((1, S5_CH)), _ANY_SPEC],
        out_specs=[pl.BlockSpec((n_rows, S5_CH), lambda c: (blk, MIX_S5)), _const_spec((nb, S5_LANES))],
        out_shape=[jax.ShapeDtypeStruct(mixed.shape, mixed.dtype), jax.ShapeDtypeStruct((nb, S5_LANES), F32)],
        input_output_aliases={8: 0},
        scratch_shapes=[pltpu.VMEM((n_rows, S5_LANES), F32)],
        compiler_params=pltpu.CompilerParams(
            dimension_semantics=("arbitrary",), vmem_limit_bytes=V7X_VMEM_LIMIT),
        name="s5_sample",
    )(y, h0, tb['bb'], tb['lam'], tb['cm'], tb['d'], tb['wglu'], tb['bglu'], mixed)


def _s5_state_to_lanes(re, im):
    b = re.shape[0]
    return jnp.stack([re.reshape(b, 2, S5_HALF_ST), im.reshape(b, 2, S5_HALF_ST)], axis=2).reshape(b, S5_LANES)


def _s5_lanes_to_state(st):
    b = st.shape[0]
    st = st.reshape(b, 2, 2, S5_HALF_ST)
    return st[:, :, 0].reshape(b, S5_GROUPS, S5_STATE), st[:, :, 1].reshape(b, S5_GROUPS, S5_STATE)


def _ret_tables(positions, chunk):
    half = RET_DK // 2
    inv_freq = ROPE_BASE ** (-jnp.arange(half, dtype=F32) / half)
    ang = positions.astype(F32)[:, None] * inv_freq[None, :]
    cos = jnp.cos(ang)
    sin = jnp.sin(ang)
    cos2 = jnp.concatenate([cos, cos], axis=1)
    sin2 = jnp.concatenate([-sin, sin], axis=1)
    log_gamma = jnp.log(1.0 - 2.0 ** (-5.0 - jnp.arange(RET_HEADS, dtype=F32)))
    g = (jnp.arange(chunk, dtype=F32) + 1.0)[None, :] * log_gamma[:, None]
    diff = g[:, :, None] - g[:, None, :]
    causal = jnp.tril(jnp.ones((chunk, chunk), dtype=bool))
    dmat = jnp.where(causal, jnp.exp(jnp.where(causal, diff, 0.0)), 0.0)
    lanes = (RET_HEADS, chunk, RET_DK)
    qdec = jnp.broadcast_to(jnp.exp(g)[:, :, None], lanes)
    kdec = jnp.broadcast_to(jnp.exp(g[:, -1:] - g)[:, :, None], lanes)
    gall = jnp.broadcast_to(jnp.exp(g[:, -1])[:, None, None], (RET_HEADS, 1, RET_DV))
    return cos2, sin2, dmat, qdec, kdec, gall


def _ret_rotate(x, cos2, sin2):
    return x * cos2 + pltpu.roll(x, RET_DK // 2, 1) * sin2


def _group_layernorm_gate(o, gate, w, b):
    mu = jnp.mean(o, axis=-1, keepdims=True)
    xc = o - mu
    var = jnp.mean(xc * xc, axis=-1, keepdims=True)
    return _silu(gate) * (xc * lax.rsqrt(var + EPS) * w + b)


def _dot_nt(a, b):
    return lax.dot_general(a, b, (((1,), (1,)), ((), ())), preferred_element_type=F32)


def _dot_tn(a, b):
    return lax.dot_general(a, b, (((0,), (0,)), ((), ())), preferred_element_type=F32)


def _ret_prompt_kernel(q_ref, k_ref, v_ref, g_ref, cos_ref, sin_ref, dm_ref, qd_ref, kd_ref, ga_ref,
                       lnw_ref, lnb_ref, mix_ref, o_ref, s_ref, *, chunk):
    del mix_ref
    s_ref[...] = jnp.zeros_like(s_ref)

    def body(c, carry):
        r = pl.ds(pl.multiple_of(c * chunk, chunk), chunk)
        cos2, sin2 = cos_ref[r, :], sin_ref[r, :]
        for h in range(RET_HEADS):
            hc = slice(h * RET_DK, (h + 1) * RET_DK)
            q = _ret_rotate(q_ref[r, hc], cos2, sin2)
            k = _ret_rotate(k_ref[r, hc], cos2, sin2) * (RET_DK ** -0.5)
            vb = v_ref[r, hc].astype(BF16)
            s = s_ref[0, h]
            scores = _dot_nt(q.astype(BF16), k.astype(BF16)) * dm_ref[h]
            o = jnp.dot(scores.astype(BF16), vb, preferred_element_type=F32)
            o = o + jnp.dot((q * qd_ref[h]).astype(BF16), s.astype(BF16), preferred_element_type=F32)
            o_ref[r, hc] = _group_layernorm_gate(o, g_ref[r, hc], lnw_ref[h], lnb_ref[h]).astype(o_ref.dtype)
            s_ref[0, h] = s * ga_ref[h] + _dot_tn((k * kd_ref[h]).astype(BF16), vb)
        return carry

    lax.fori_loop(0, q_ref.shape[0] // chunk, body, 0)


MIX_S5, MIX_GDN, MIX_SSD, MIX_RET = range(4)
_ANY_SPEC = pl.BlockSpec(memory_space=pl.ANY)


def _uninit_kernel(o_ref):
    del o_ref


def uninitialized(shape, dtype):
    return pl.pallas_call(_uninit_kernel, out_shape=jax.ShapeDtypeStruct(shape, dtype), out_specs=_ANY_SPEC,
                          name="uninitialized")()


def ret_prompt(y, mixed, n_seq, seq_len, lw):
    chunk = min(RET_CHUNK, seq_len)
    cos2, sin2, dmat, qdec, kdec, gall = _ret_tables(jnp.arange(seq_len, dtype=jnp.int32), chunk)
    width = RET_HEADS * RET_DK
    heads = lambda shape: _const_spec((RET_HEADS,) + shape)
    tok = lambda s: pl.BlockSpec((seq_len, width), lambda b: (b, PK_OFF[s] // width))
    tab = _const_spec((seq_len, RET_DK))
    return pl.pallas_call(
        functools.partial(_ret_prompt_kernel, chunk=chunk),
        grid=(n_seq,),
        in_specs=[tok(10), tok(11), tok(12), tok(13), tab, tab,
                  heads((chunk, chunk)), heads((chunk, RET_DK)), heads((chunk, RET_DK)), heads((1, RET_DV)),
                  heads((1, RET_DV)), heads((1, RET_DV)), _ANY_SPEC],
        out_specs=[pl.BlockSpec((seq_len, width), lambda b: (b, MIX_RET)),
                   pl.BlockSpec((1, RET_HEADS, RET_DK, RET_DV), lambda b: (b, 0, 0, 0))],
        out_shape=[jax.ShapeDtypeStruct(mixed.shape, mixed.dtype),
                   jax.ShapeDtypeStruct((n_seq, RET_HEADS, RET_DK, RET_DV), F32)],
        input_output_aliases={12: 0},
        compiler_params=pltpu.CompilerParams(
            dimension_semantics=("arbitrary",), vmem_limit_bytes=V7X_VMEM_LIMIT),
        name="ret_prompt",
    )(y, y, y, y, cos2, sin2, dmat, qdec, kdec, gall,
      lw['ret_ln_w'].astype(F32).reshape(RET_HEADS, 1, RET_DV), lw['ret_ln_b'].astype(F32).reshape(RET_HEADS, 1, RET_DV),
      mixed)


SAMPLE_BB = 16
LHS_ROWS = 16


def _ret_sample_kernel(q_ref, k_ref, v_ref, g_ref, cos_ref, sin_ref, s0_ref, lnw_ref, lnb_ref, mix_ref, sall_ref,
                       o_ref, s_ref, qd_scr, kd_scr, v_scr, acc_scr, lq, lk, lv, *, decay):
    del mix_ref, sall_ref
    n_t, n_b = q_ref.shape[0], q_ref.shape[1]
    for h in range(RET_HEADS):
        hc = slice(h * RET_DK, (h + 1) * RET_DK)
        qs, ks, vs = [], [], []
        for t in range(n_t):
            cos2, sin2 = cos_ref[t:t + 1, :], sin_ref[t:t + 1, :]
            qs.append(_ret_rotate(q_ref[t, :, hc], cos2, sin2))
            ks.append(_ret_rotate(k_ref[t, :, hc], cos2, sin2) * (RET_DK ** -0.5))
            vs.append(v_ref[t, :, hc])
            qd_scr[h, t] = qs[t] * decay['q'][h][t]
            kd_scr[h, t] = ks[t] * decay['k'][h][t]
            v_scr[h, t] = vs[t]
        for i in range(n_t):
            acc = None
            for j in range(i + 1):
                term = (jnp.sum(qs[i] * ks[j], axis=-1, keepdims=True) * decay['m'][h][i][j]) * vs[j]
                acc = term if acc is None else acc + term
            acc_scr[h, i] = acc

    for tile in (lq, lk, lv):
        tile[...] = jnp.zeros_like(tile)

    def per_seq(b, carry):
        row = pl.ds(b, 1)
        for h in range(RET_HEADS):
            for t in range(n_t):
                lq[h, t:t + 1, :] = qd_scr[h, t, row, :]
                lk[h, t:t + 1, :] = kd_scr[h, t, row, :]
                lv[h, t:t + 1, :] = v_scr[h, t, row, :]
            s0 = s0_ref[b, h]
            inter = jnp.dot(lq[h].astype(BF16), s0.astype(BF16), preferred_element_type=F32)
            s_ref[b, h] = s0 * decay['all'][h] + _dot_tn(lk[h].astype(BF16), lv[h].astype(BF16))
            for t in range(n_t):
                acc_scr[h, t, row, :] = acc_scr[h, t, row, :] + inter[t:t + 1, :]
        return carry

    lax.fori_loop(0, n_b, per_seq, 0)

    for h in range(RET_HEADS):
        hc = slice(h * RET_DV, (h + 1) * RET_DV)
        for t in range(n_t):
            o_ref[t, :, hc] = _group_layernorm_gate(acc_scr[h, t], g_ref[t, :, hc], lnw_ref[h], lnb_ref[h]
                                                    ).astype(o_ref.dtype)


def _sample_block(t0, n_t):
    assert t0 % n_t == 0
    return t0 // n_t


def ret_sample(y3, mixed3, states, new_states, layer, t0, n_t, first_pos, lw):
    n_b = y3.shape[1]
    tb = _sample_block(t0, n_t)
    cos2, sin2, _, _, _, _ = _ret_tables(first_pos + jnp.arange(n_t, dtype=jnp.int32), n_t)
    gamma = 1.0 - 2.0 ** (-5.0 - np.arange(RET_HEADS, dtype=np.float64))
    decay = dict(m=[[[float(g ** (i - j)) for j in range(n_t)] for i in range(n_t)] for g in gamma],
                 q=[[float(g ** (i + 1)) for i in range(n_t)] for g in gamma],
                 k=[[float(g ** (n_t - 1 - j)) for j in range(n_t)] for g in gamma],
                 all=[float(g ** n_t) for g in gamma])
    width = RET_HEADS * RET_DK
    tok = lambda s: pl.BlockSpec((n_t, SAMPLE_BB, width), lambda i: (tb, i, PK_OFF[s] // width))
    state = pl.BlockSpec((None, SAMPLE_BB, RET_HEADS, RET_DK, RET_DV), lambda i: (layer, i, 0, 0, 0))
    per_tok = pltpu.VMEM((RET_HEADS, n_t, SAMPLE_BB, RET_DK), F32)
    tile = pltpu.VMEM((RET_HEADS, LHS_ROWS, RET_DK), F32)
    return pl.pallas_call(
        functools.partial(_ret_sample_kernel, decay=decay),
        grid=(n_b // SAMPLE_BB,),
        in_specs=[tok(10), tok(11), tok(12), tok(13), _const_spec((n_t, RET_DK)), _const_spec((n_t, RET_DK)),
                  state, _const_spec((RET_HEADS, 1, RET_DV)), _const_spec((RET_HEADS, 1, RET_DV)), _ANY_SPEC, _ANY_SPEC],
        out_specs=[pl.BlockSpec((n_t, SAMPLE_BB, width), lambda i: (tb, i, MIX_RET)), state],
        out_shape=[jax.ShapeDtypeStruct(mixed3.shape, mixed3.dtype), jax.ShapeDtypeStruct(states.shape, F32)],
        input_output_aliases={9: 0, 10: 1},
        scratch_shapes=[per_tok, per_tok, per_tok, per_tok, tile, tile, tile],
        compiler_params=pltpu.CompilerParams(
            dimension_semantics=("arbitrary",), vmem_limit_bytes=V7X_VMEM_LIMIT),
        name="ret_sample",
    )(y3, y3, y3, y3, cos2, sin2, states,
      lw['ret_ln_w'].astype(F32).reshape(RET_HEADS, 1, RET_DV), lw['ret_ln_b'].astype(F32).reshape(RET_HEADS, 1, RET_DV),
      mixed3, new_states)


NARROW_B = 0
NARROW_A = GDN_HEADS
NARROW_DT = 2 * GDN_HEADS
TAIL = 8


def _shift_rows(x, tail, s):
    xr = pltpu.roll(x, s, 0)
    tr = pltpu.roll(tail, s, 0)
    row = lax.broadcasted_iota(jnp.int32, tail.shape, 0)
    return jnp.concatenate([jnp.where(row < s, tr, xr[0:TAIL]), xr[TAIL:]], axis=0)


def _causal_conv(x, tail, w_ref):
    y = x * w_ref[CONV_W - 1:CONV_W, :]
    for s in range(1, CONV_W):
        y = y + _shift_rows(x, tail, s) * w_ref[CONV_W - 1 - s:CONV_W - s, :]
    return y


def _softplus(x):
    return jnp.maximum(x, 0.0) + jnp.log1p(jnp.exp(-jnp.abs(x)))


def _lane_row(vals, lane0, width=128):
    return jnp.zeros((1, width), F32).at[0, lane0:lane0 + vals.shape[0]].set(vals.astype(F32))


SSD_PAIRS = SSD_HEADS // 2
SSD_BC = SSD_NGROUPS * SSD_STATE


def _ssd_prompt_kernel(z_ref, xbc_ref, nar_ref, cw_ref, cb_ref, dtb_ref, a_ref, dsk_ref, nw_ref, tri_ref, mix_ref,
                       o_ref, s_ref, tail_ref, *, chunk):
    del mix_ref
    s_ref[...] = jnp.zeros_like(s_ref)
    tail_ref[...] = jnp.zeros_like(tail_ref)
    causal = (lax.broadcasted_iota(jnp.int32, (chunk, chunk), 0)
              >= lax.broadcasted_iota(jnp.int32, (chunk, chunk), 1))
    lane = lax.broadcasted_iota(jnp.int32, (chunk, 2 * SSD_HEADDIM), 1)
    first_head = lane < SSD_HEADDIM
    rep = SSD_HEADS // SSD_NGROUPS

    def body(c, carry):
        r = pl.ds(pl.multiple_of(c * chunk, chunk), chunk)
        raw = xbc_ref[r, :]
        xbc = _silu(_causal_conv(raw, tail_ref[0], cw_ref) + cb_ref[...])
        tail_ref[0] = raw[chunk - TAIL:, :]
        xs = xbc[:, :SSD_INNER]
        dt = _softplus(nar_ref[r, :] + dtb_ref[...])
        g = _select_rows(tri_ref[...], dt * a_ref[...])
        e_in = jnp.exp(g)
        e_out = dt * jnp.exp(g[chunk - 1:chunk, :] - g)
        e_all = jnp.exp(g[chunk - 1:chunk, :])
        g_t = g.T
        dt_t = dt.T
        ys = []
        for p in range(SSD_PAIRS):
            grp = (2 * p) // rep
            bm = xbc[:, SSD_INNER + grp * SSD_STATE:SSD_INNER + (grp + 1) * SSD_STATE]
            cm = xbc[:, SSD_INNER + SSD_BC + grp * SSD_STATE:SSD_INNER + SSD_BC + (grp + 1) * SSD_STATE]
            cb = _dot_nt(cm.astype(BF16), bm.astype(BF16))
            xp = xs[:, p * 128:(p + 1) * 128]
            xpb = xp.astype(BF16)
            sp = s_ref[0, p]
            spb = sp.astype(BF16)
            outs, upds, gls = [], [], []
            for hh in range(2):
                ln = NARROW_DT + 2 * p + hh
                diff = g[:, ln:ln + 1] - g_t[ln:ln + 1, :]
                m = jnp.where(causal, cb * jnp.exp(jnp.where(causal, diff, 0.0)) * dt_t[ln:ln + 1, :], 0.0)
                o = jnp.dot(m.astype(BF16), xpb, preferred_element_type=F32)
                o = o + jnp.dot((cm * e_in[:, ln:ln + 1]).astype(BF16), spb, preferred_element_type=F32)
                outs.append(o)
                upds.append(_dot_tn((bm * e_out[:, ln:ln + 1]).astype(BF16), xpb))
                gls.append(e_all[:, ln:ln + 1])
            s_ref[0, p] = sp * jnp.where(first_head, gls[0], gls[1]) + jnp.where(first_head, upds[0], upds[1])
            ys.append(jnp.where(first_head, outs[0], outs[1]) + xp * dsk_ref[:, p * 128:(p + 1) * 128])
        y = jnp.concatenate(ys, axis=1) * _silu(z_ref[r, :])
        gw = SSD_INNER // SSD_NGROUPS
        yn = [y[:, i * gw:(i + 1) * gw] * lax.rsqrt(
            jnp.mean(y[:, i * gw:(i + 1) * gw] * y[:, i * gw:(i + 1) * gw], axis=-1, keepdims=True) + EPS)
            for i in range(SSD_NGROUPS)]
        o_ref[r, :] = (jnp.concatenate(yn, axis=1) * nw_ref[...]).astype(o_ref.dtype)
        return carry

    lax.fori_loop(0, z_ref.shape[0] // chunk, body, 0)


def _ssd_params(lw, chunk):
    return (lw['ssd_conv_w'].astype(F32), lw['ssd_conv_b'].astype(F32).reshape(1, SSD_CONV_DIM),
            _lane_row(lw['ssd_dt_bias'], NARROW_DT), _lane_row(-jnp.exp(lw['ssd_a_log'].astype(F32)), NARROW_DT),
            jnp.repeat(lw['ssd_d'].astype(F32), SSD_HEADDIM).reshape(1, SSD_INNER),
            lw['ssd_norm_w'].astype(F32).reshape(1, SSD_INNER),
            jnp.tril(jnp.ones((chunk, chunk), BF16)))


def _ssd_state_from_pairs(s):
    b = s.shape[0]
    s = s.reshape(b, SSD_PAIRS, SSD_STATE, 2, SSD_HEADDIM)
    return jnp.swapaxes(s, 2, 3).reshape(b, SSD_HEADS, SSD_STATE, SSD_HEADDIM)


def ssd_prompt(y, nar, mixed, n_seq, seq_len, lw):
    chunk = min(SSD_CHUNK, seq_len)
    params = _ssd_params(lw, chunk)
    tok = lambda s, w: pl.BlockSpec((seq_len, w), lambda b: (b, PK_OFF[s] // w))
    n_in = 3 + len(params)
    mixed, s, tail = pl.pallas_call(
        functools.partial(_ssd_prompt_kernel, chunk=chunk),
        grid=(n_seq,),
        in_specs=[tok(7, SSD_INNER), tok(8, SSD_CONV_DIM), pl.BlockSpec((seq_len, NARROW_W), lambda b: (b, 0))]
                 + [_const_spec(p.shape) for p in params] + [_ANY_SPEC],
        out_specs=[pl.BlockSpec((seq_len, SSD_INNER), lambda b: (b, MIX_SSD)),
                   pl.BlockSpec((1, SSD_PAIRS, SSD_STATE, 2 * SSD_HEADDIM), lambda b: (b, 0, 0, 0)),
                   pl.BlockSpec((1, TAIL, SSD_CONV_DIM), lambda b: (b, 0, 0))],
        out_shape=[jax.ShapeDtypeStruct(mixed.shape, mixed.dtype),
                   jax.ShapeDtypeStruct((n_seq, SSD_PAIRS, SSD_STATE, 2 * SSD_HEADDIM), F32),
                   jax.ShapeDtypeStruct((n_seq, TAIL, SSD_CONV_DIM), F32)],
        input_output_aliases={n_in: 0},
        compiler_params=pltpu.CompilerParams(
            dimension_semantics=("arbitrary",), vmem_limit_bytes=V7X_VMEM_LIMIT),
        name="ssd_prompt",
    )(y, y, nar, *params, mixed)
    return mixed, tail[:, TAIL - (CONV_W - 1):], _ssd_state_from_pairs(s)


GDN_QKV = GDN_HEADS * GDN_DK
GDN_ROWS = 1024
GDN_UNROLL = 4
HIGHEST = lax.Precision.HIGHEST


def _split_bf16(x):
    hi = x.astype(BF16)
    return hi, (x - hi.astype(F32)).astype(BF16)


def _split3_bf16(x):
    hi = x.astype(BF16)
    r = x - hi.astype(F32)
    mid = r.astype(BF16)
    return hi, mid, (r - mid.astype(F32)).astype(BF16)


def _select_rows(m01, x):
    n = x.shape[1]
    p = jnp.dot(m01, jnp.concatenate(_split3_bf16(x), axis=1), preferred_element_type=F32)
    return p[:, :n] + p[:, n:2 * n] + p[:, 2 * n:]


def _select_cols(x, m01):
    m = x.shape[0]
    p = jnp.dot(jnp.concatenate(_split3_bf16(x), axis=0), m01, preferred_element_type=F32)
    return p[:m] + p[m:2 * m] + p[2 * m:]


def _dot3(a, b):
    a_hi, a_lo = _split_bf16(a)
    b_hi, b_lo = _split_bf16(b)
    m = a.shape[0]
    p = jnp.dot(jnp.concatenate([a_hi, a_lo], axis=0), b_hi, preferred_element_type=F32)
    return p[:m] + p[m:] + jnp.dot(a_hi, b_lo, preferred_element_type=F32)


def _two_block_diag(x0, x1):
    z0 = jnp.zeros_like(x0)
    z1 = jnp.zeros_like(x1)
    return jnp.concatenate([jnp.concatenate([x0, z1], axis=1), jnp.concatenate([z0, x1], axis=1)], axis=0)


def _l2_rows(x):
    return x * lax.rsqrt(jnp.sum(x * x, axis=-1, keepdims=True) + EPS)


def _gdn_prompt_kernel(q_ref, k_ref, v_ref, z_ref, nar_ref, cw_ref, dtb_ref, a_ref, nw_ref, tri_ref, sel_ref, mix_ref,
                       o_ref, s_ref, tail_ref, u_scr, w_scr, qd_scr, kd_scr, sc_scr, ea_scr, *, chunk, unroll):
    del mix_ref

    @pl.when(pl.program_id(1) == 0)
    def _():
        s_ref[...] = jnp.zeros_like(s_ref)
        tail_ref[...] = jnp.zeros_like(tail_ref)

    n_chunks = q_ref.shape[0] // chunk
    cat = GDN_HEADS * chunk
    lane = lax.broadcasted_iota(jnp.int32, (chunk, cat), 1)
    row = lax.broadcasted_iota(jnp.int32, (chunk, cat), 0)
    col = jnp.bitwise_and(lane, chunk - 1)
    causal = row >= col
    strict = row > col
    eye_cat = jnp.where(row == col, 1.0, 0.0).astype(F32)
    head_mask = [(lane >= h * chunk) & (lane < (h + 1) * chunk) for h in range(GDN_HEADS)]
    nar_lane = lax.broadcasted_iota(jnp.int32, (chunk, 128), 1)
    ones_cc = jnp.ones((chunk, chunk), BF16)
    n_sq = chunk.bit_length() - 2

    def mm_cat(l_cat, r_cat):
        bd = jnp.concatenate([jnp.where(m, r_cat, 0.0) for m in head_mask], axis=0)
        return _dot3(l_cat, bd)

    def neumann_step(t_cat, pw, update_t, update_pw):
        return (t_cat + mm_cat(t_cat, pw) if update_t else t_cat), (mm_cat(pw, pw) if update_pw else pw)

    col1 = lambda x, h: x[:, h * chunk:h * chunk + 1]

    def front(c):
        r = pl.ds(pl.multiple_of(c * chunk, chunk), chunk)
        before = pl.ds(pl.multiple_of(jnp.maximum(c * chunk - TAIL, 0), TAIL), TAIL)
        qkv = []
        for i, ref in enumerate((q_ref, k_ref, v_ref)):
            cols = slice(i * GDN_QKV, (i + 1) * GDN_QKV)
            tail = jnp.where(c == 0, tail_ref[0, :, cols], ref[before, :])
            qkv.append(_silu(_causal_conv(ref[r, :], tail, cw_ref.at[:, cols])))
        q, k, v = qkv
        nar = nar_ref[r, :]
        beta = jax.nn.sigmoid(nar)
        g = _select_rows(tri_ref[...], a_ref[...] * _softplus(nar + dtb_ref[...]))
        bg = _select_cols(jnp.where(nar_lane < NARROW_A, beta, g), sel_ref[...])
        b_c, g_c = bg[:, :cat], bg[:, cat:]
        g_r = _select_rows(ones_cc, g_c * eye_cat)
        decay = jnp.where(causal, jnp.exp(jnp.where(causal, g_c - g_r, 0.0)), 0.0)
        e_in = jnp.exp(g_c)
        e_out = jnp.exp(g_c[chunk - 1:chunk, :] - g_c)
        e_all = jnp.exp(g_c[chunk - 1:chunk, :])

        qn = [_l2_rows(q[:, h * GDN_DK:(h + 1) * GDN_DK]) * (GDN_DK ** -0.5) for h in range(GDN_HEADS)]
        kn = [_l2_rows(k[:, h * GDN_DK:(h + 1) * GDN_DK]) for h in range(GDN_HEADS)]
        vh = [v[:, h * GDN_DV:(h + 1) * GDN_DV] for h in range(GDN_HEADS)]
        kk, qk = [], []
        for p in range(GDN_HEADS // 2):
            h0, h1 = 2 * p, 2 * p + 1
            rhs = _two_block_diag(kn[h0], kn[h1]).astype(BF16)
            kk.append(_dot_nt(jnp.concatenate([kn[h0], kn[h1]], axis=1).astype(BF16), rhs))
            qk.append(_dot_nt(jnp.concatenate([qn[h0], qn[h1]], axis=1).astype(BF16), rhs))
        kk = jnp.concatenate(kk, axis=1)
        sc_scr[r, :] = (jnp.concatenate(qk, axis=1) * decay).astype(BF16)
        ea_scr[c] = jnp.broadcast_to(e_all, (TAIL, cat))
        for h in range(GDN_HEADS):
            hc = slice(h * GDN_DV, (h + 1) * GDN_DV)
            qd_scr[r, hc] = (qn[h] * col1(e_in, h)).astype(BF16)
            kd_scr[r, hc] = (kn[h] * col1(e_out, h)).astype(BF16)
        a_cat = jnp.where(strict, b_c * kk * decay, 0.0)
        rhs = [_two_block_diag(*[jnp.concatenate(
            [vh[h] * col1(b_c, h), kn[h] * (col1(b_c, h) * col1(e_in, h))], axis=1) for h in (2 * p, 2 * p + 1)])
            for p in range(GDN_HEADS // 2)]
        return r, a_cat, rhs

    def back(r, t_cat, rhs):
        for p in range(GDN_HEADS // 2):
            uw = _dot3(t_cat[:, p * 2 * chunk:(p + 1) * 2 * chunk], rhs[p])
            for i, h in enumerate((2 * p, 2 * p + 1)):
                hc = slice(h * GDN_DV, (h + 1) * GDN_DV)
                u_scr[r, hc] = uw[:, (2 * i) * GDN_DV:(2 * i + 1) * GDN_DV]
                w_scr[r, hc] = uw[:, (2 * i + 1) * GDN_DV:(2 * i + 2) * GDN_DV].astype(BF16)

    def prepare_group(i, carry):
        rs, ts, rhss = [], [], []
        for j in range(unroll):
            r, a_cat, rhs = front(i * unroll + j)
            rs.append(r)
            ts.append((eye_cat - a_cat, a_cat))
            rhss.append(rhs)
        for step in range(n_sq + 1):
            ts = [neumann_step(t, pw, update_t=step > 0, update_pw=step < n_sq) for t, pw in ts]
        for r, (t_cat, _), rhs in zip(rs, ts, rhss):
            back(r, t_cat, rhs)
        return carry

    lax.fori_loop(0, n_chunks // unroll, prepare_group, 0)

    def recur(c, carry):
        r = pl.ds(pl.multiple_of(c * chunk, chunk), chunk)
        e_all = ea_scr[c][0:1, :]
        for p in range(GDN_HEADS // 2):
            heads = (2 * p, 2 * p + 1)
            v_new, q_s = [], []
            for h in heads:
                hc = slice(h * GDN_DV, (h + 1) * GDN_DV)
                wq = jnp.concatenate([w_scr[r, hc], qd_scr[r, hc]], axis=0)
                ws = jnp.dot(wq, s_ref[0, h].astype(BF16), preferred_element_type=F32)
                v_new.append(u_scr[r, hc] - ws[:chunk])
                q_s.append(ws[chunk:])
            intra = jnp.dot(sc_scr[r, p * 2 * chunk:(p + 1) * 2 * chunk],
                            _two_block_diag(*v_new).astype(BF16), preferred_element_type=F32)
            for i, h in enumerate(heads):
                hc = slice(h * GDN_DV, (h + 1) * GDN_DV)
                o = q_s[i] + intra[:, i * GDN_DV:(i + 1) * GDN_DV]
                s_ref[0, h] = s_ref[0, h] * col1(e_all, h) + _dot_tn(kd_scr[r, hc], v_new[i].astype(BF16))
                o = o * lax.rsqrt(jnp.mean(o * o, axis=-1, keepdims=True) + EPS) * nw_ref[...]
                o_ref[r, hc] = (o * _silu(z_ref[r, hc])).astype(o_ref.dtype)
        return carry

    lax.fori_loop(0, n_chunks, recur, 0)
    last = pl.ds(q_ref.shape[0] - TAIL, TAIL)
    for i, ref in enumerate((q_ref, k_ref, v_ref)):
        tail_ref[0, :, i * GDN_QKV:(i + 1) * GDN_QKV] = ref[last, :]


def _gdn_params(lw, chunk):
    cat = GDN_HEADS * chunk
    lanes = jnp.arange(cat) // chunk
    sel = jnp.zeros((128, 2 * cat), F32)
    sel = sel.at[NARROW_B + lanes, jnp.arange(cat)].set(1.0)
    sel = sel.at[NARROW_A + lanes, cat + jnp.arange(cat)].set(1.0)
    return (lw['gdn_conv_w'].astype(F32), _lane_row(lw['gdn_dt_bias'], NARROW_A),
            _lane_row(-jnp.exp(lw['gdn_a_log'].astype(F32)), NARROW_A),
            lw['gdn_norm_w'].astype(F32).reshape(1, GDN_DV), jnp.tril(jnp.ones((chunk, chunk), BF16)),
            sel.astype(BF16))


def gdn_prompt(y, nar, mixed, n_seq, seq_len, lw):
    chunk = min(GDN_CHUNK, seq_len)
    rows = min(GDN_ROWS, seq_len)
    n_chunks = rows // chunk
    unroll = math.gcd(GDN_UNROLL, n_chunks)
    params = _gdn_params(lw, chunk)
    nblk = seq_len // rows
    tok = lambda s: pl.BlockSpec((rows, GDN_QKV), lambda b, j: (b * nblk + j, PK_OFF[s] // GDN_QKV))
    const = lambda shape: pl.BlockSpec(shape, lambda b, j: (0,) * len(shape))
    n_in = 5 + len(params)
    mixed, s, tail = pl.pallas_call(
        functools.partial(_gdn_prompt_kernel, chunk=chunk, unroll=unroll),
        grid=(n_seq, nblk),
        in_specs=[tok(1), tok(2), tok(3), tok(4), pl.BlockSpec((rows, NARROW_W), lambda b, j: (b * nblk + j, 0))]
                 + [const(p.shape) for p in params] + [_ANY_SPEC],
        out_specs=[pl.BlockSpec((rows, GDN_QKV), lambda b, j: (b * nblk + j, MIX_GDN)),
                   pl.BlockSpec((1, GDN_HEADS, GDN_DK, GDN_DV), lambda b, j: (b, 0, 0, 0)),
                   pl.BlockSpec((1, TAIL, GDN_CONV_DIM), lambda b, j: (b, 0, 0))],
        out_shape=[jax.ShapeDtypeStruct(mixed.shape, mixed.dtype),
                   jax.ShapeDtypeStruct((n_seq, GDN_HEADS, GDN_DK, GDN_DV), F32),
                   jax.ShapeDtypeStruct((n_seq, TAIL, GDN_CONV_DIM), F32)],
        input_output_aliases={n_in: 0},
        scratch_shapes=[pltpu.VMEM((rows, GDN_QKV), F32), pltpu.VMEM((rows, GDN_QKV), BF16),
                        pltpu.VMEM((rows, GDN_QKV), BF16), pltpu.VMEM((rows, GDN_QKV), BF16),
                        pltpu.VMEM((rows, GDN_HEADS * chunk), BF16),
                        pltpu.VMEM((n_chunks, TAIL, GDN_HEADS * chunk), F32)],
        compiler_params=pltpu.CompilerParams(
            dimension_semantics=("arbitrary", "arbitrary"), vmem_limit_bytes=V7X_VMEM_LIMIT),
        name="gdn_prompt",
    )(y, y, y, y, nar, *params, mixed)
    return mixed, tail[:, TAIL - (CONV_W - 1):], s


def _conv_steps(buf_ref, raw_ref, w_ref, cols):
    n_t = raw_ref.shape[0]
    xx = [buf_ref[j, :, cols] for j in range(CONV_W - 1)] + [raw_ref[t] for t in range(n_t)]
    w = w_ref[:, cols]
    out = []
    for t in range(n_t):
        y = xx[t] * w[0:1, :]
        for j in range(1, CONV_W):
            y = y + xx[t + j] * w[j:j + 1, :]
        out.append(y)
    return out, xx[n_t:]


def _ssd_sample_kernel(z_ref, xbc_ref, nar_ref, buf_ref, s0_ref, cw_ref, cb_ref, dtb_ref, a_ref, dsk_ref, nw_ref,
                       mix_ref, sall_ref, o_ref, s_ref, nbuf_ref, qd_scr, kd_scr, xs_scr, ga_scr, acc_scr, lq, lk, lx):
    del mix_ref, sall_ref
    n_t, n_b = z_ref.shape[0], z_ref.shape[1]
    rep = SSD_HEADS // SSD_NGROUPS
    conv, tail = _conv_steps(buf_ref, xbc_ref, cw_ref, slice(0, SSD_CONV_DIM))
    for j in range(CONV_W - 1):
        nbuf_ref[j] = tail[j]
    xbc = [_silu(c + cb_ref[...]) for c in conv]
    dt, g = [], []
    for t in range(n_t):
        dt.append(_softplus(nar_ref[t] + dtb_ref[...]))
        la = dt[t] * a_ref[...]
        g.append(la if t == 0 else g[t - 1] + la)
    for grp in range(SSD_NGROUPS):
        bm = [x[:, SSD_INNER + grp * SSD_STATE:SSD_INNER + (grp + 1) * SSD_STATE] for x in xbc]
        cm = [x[:, SSD_INNER + SSD_BC + grp * SSD_STATE:SSD_INNER + SSD_BC + (grp + 1) * SSD_STATE] for x in xbc]
        sc = [[jnp.sum(cm[i] * bm[j], axis=-1, keepdims=True) for j in range(i + 1)] for i in range(n_t)]
        for h in range(grp * rep, (grp + 1) * rep):
            ln = NARROW_DT + h
            hc = slice(h * SSD_HEADDIM, (h + 1) * SSD_HEADDIM)
            gc = [x[:, ln:ln + 1] for x in g]
            dc = [x[:, ln:ln + 1] for x in dt]
            xs = [x[:, hc] for x in xbc]
            for t in range(n_t):
                qd_scr[h, t] = cm[t] * jnp.exp(gc[t])
                kd_scr[h, t] = bm[t] * (dc[t] * jnp.exp(gc[n_t - 1] - gc[t]))
                xs_scr[h, t] = xs[t]
                acc = None
                for j in range(t + 1):
                    term = (sc[t][j] * dc[j] * jnp.exp(gc[t] - gc[j])) * xs[j]
                    acc = term if acc is None else acc + term
                acc_scr[t, :, hc] = acc
            ga_scr[h] = jnp.broadcast_to(jnp.exp(gc[n_t - 1]), (n_b, SSD_STATE))

    for tile in (lq, lk, lx):
        tile[...] = jnp.zeros_like(tile)

    def per_seq(b, carry):
        row = pl.ds(b, 1)
        for h in range(SSD_HEADS):
            hc = slice(h * SSD_HEADDIM, (h + 1) * SSD_HEADDIM)
            for t in range(n_t):
                lq[h, t:t + 1, :] = qd_scr[h, t, row, :]
                lk[h, t:t + 1, :] = kd_scr[h, t, row, :]
                lx[h, t:t + 1, :] = xs_scr[h, t, row, :]
            s0 = s0_ref[b, h]
            inter = _dot_nt(lq[h].astype(BF16), s0.astype(BF16))
            s_ref[b, h] = s0 * ga_scr[h, row, :] + _dot_tn(lx[h].astype(BF16), lk[h].astype(BF16))
            for t in range(n_t):
                xs_scr[h, t, row, :] = inter[t:t + 1, :]
        return carry

    lax.fori_loop(0, n_b, per_seq, 0)

    gw = SSD_INNER // SSD_NGROUPS
    for t in range(n_t):
        for h in range(SSD_HEADS):
            hc = slice(h * SSD_HEADDIM, (h + 1) * SSD_HEADDIM)
            acc_scr[t, :, hc] = acc_scr[t, :, hc] + xs_scr[h, t]
        y = (acc_scr[t] + xbc[t][:, :SSD_INNER] * dsk_ref[...]) * _silu(z_ref[t])
        yn = [y[:, i * gw:(i + 1) * gw] * lax.rsqrt(
            jnp.mean(y[:, i * gw:(i + 1) * gw] * y[:, i * gw:(i + 1) * gw], axis=-1, keepdims=True) + EPS)
            for i in range(SSD_NGROUPS)]
        o_ref[t] = (jnp.concatenate(yn, axis=1) * nw_ref[...]).astype(o_ref.dtype)


def ssd_sample(y3, nar3, mixed3, buf0, states, new_states, layer, t0, n_t, lw):
    n_b = y3.shape[1]
    tb = _sample_block(t0, n_t)
    params = _ssd_params(lw, 1)[:-1]
    tok = lambda s, w: pl.BlockSpec((n_t, SAMPLE_BB, w), lambda i: (tb, i, PK_OFF[s] // w))
    bufspec = pl.BlockSpec((CONV_W - 1, SAMPLE_BB, SSD_CONV_DIM), lambda i: (0, i, 0))
    state = pl.BlockSpec((None, SAMPLE_BB, SSD_HEADS, SSD_HEADDIM, SSD_STATE), lambda i: (layer, i, 0, 0, 0))
    per_tok = lambda w: pltpu.VMEM((SSD_HEADS, n_t, SAMPLE_BB, w), F32)
    tile = lambda w: pltpu.VMEM((SSD_HEADS, LHS_ROWS, w), F32)
    n_in = 5 + len(params)
    mixed3, s, nbuf = pl.pallas_call(
        _ssd_sample_kernel,
        grid=(n_b // SAMPLE_BB,),
        in_specs=[tok(7, SSD_INNER), tok(8, SSD_CONV_DIM),
                  pl.BlockSpec((n_t, SAMPLE_BB, NARROW_W), lambda i: (tb, i, 0)), bufspec, state]
                 + [_const_spec(p.shape) for p in params] + [_ANY_SPEC, _ANY_SPEC],
        out_specs=[pl.BlockSpec((n_t, SAMPLE_BB, SSD_INNER), lambda i: (tb, i, MIX_SSD)), state, bufspec],
        out_shape=[jax.ShapeDtypeStruct(mixed3.shape, mixed3.dtype), jax.ShapeDtypeStruct(states.shape, F32),
                   jax.ShapeDtypeStruct((CONV_W - 1, n_b, SSD_CONV_DIM), F32)],
        input_output_aliases={n_in: 0, n_in + 1: 1},
        scratch_shapes=[per_tok(SSD_STATE), per_tok(SSD_STATE), per_tok(SSD_HEADDIM),
                        pltpu.VMEM((SSD_HEADS, SAMPLE_BB, SSD_STATE), F32),
                        pltpu.VMEM((n_t, SAMPLE_BB, SSD_INNER), F32),
                        tile(SSD_STATE), tile(SSD_STATE), tile(SSD_HEADDIM)],
        compiler_params=pltpu.CompilerParams(
            dimension_semantics=("arbitrary",), vmem_limit_bytes=V7X_VMEM_LIMIT),
        name="ssd_sample",
    )(y3, y3, nar3, jnp.swapaxes(buf0, 0, 1), states, *params, mixed3, new_states)
    return mixed3, jnp.swapaxes(nbuf, 0, 1), s


def _gdn_sample_kernel(q_ref, k_ref, v_ref, z_ref, nar_ref, buf_ref, s0_ref, cw_ref, dtb_ref, a_ref, nw_ref,
                       mix_ref, sall_ref, o_ref, s_ref, nbuf_ref, w_scr, qd_scr, kd_scr, u_scr, ga_scr, lwq, lk, lu):
    del mix_ref, sall_ref
    n_t, n_b = q_ref.shape[0], q_ref.shape[1]
    qkv = []
    for i, ref in enumerate((q_ref, k_ref, v_ref)):
        cols = slice(i * GDN_QKV, (i + 1) * GDN_QKV)
        conv, tail = _conv_steps(buf_ref, ref, cw_ref, cols)
        for j in range(CONV_W - 1):
            nbuf_ref[j, :, cols] = tail[j]
        qkv.append([_silu(c) for c in conv])
    beta, g = [], []
    for t in range(n_t):
        nar = nar_ref[t]
        beta.append(jax.nn.sigmoid(nar))
        gl = a_ref[...] * _softplus(nar + dtb_ref[...])
        g.append(gl if t == 0 else g[t - 1] + gl)

    scores = []
    for h in range(GDN_HEADS):
        hc = slice(h * GDN_DK, (h + 1) * GDN_DK)
        qn = [_l2_rows(x[:, hc]) * (GDN_DK ** -0.5) for x in qkv[0]]
        kn = [_l2_rows(x[:, hc]) for x in qkv[1]]
        vh = [x[:, hc] for x in qkv[2]]
        bc = [x[:, NARROW_B + h:NARROW_B + h + 1] for x in beta]
        gc = [x[:, NARROW_A + h:NARROW_A + h + 1] for x in g]
        us, ws = [], []
        for i in range(n_t):
            u = vh[i] * bc[i]
            w = kn[i] * (bc[i] * jnp.exp(gc[i]))
            for j in range(i):
                a_ij = bc[i] * jnp.sum(kn[i] * kn[j], axis=-1, keepdims=True) * jnp.exp(gc[i] - gc[j])
                u = u - a_ij * us[j]
                w = w - a_ij * ws[j]
            us.append(u)
            ws.append(w)
            u_scr[h, i] = u
            w_scr[h, i] = w
            qd_scr[h, i] = qn[i] * jnp.exp(gc[i])
            kd_scr[h, i] = kn[i] * jnp.exp(gc[n_t - 1] - gc[i])
        ga_scr[h] = jnp.broadcast_to(jnp.exp(gc[n_t - 1]), (n_b, GDN_DV))
        scores.append([[jnp.sum(qn[i] * kn[j], axis=-1, keepdims=True) * jnp.exp(gc[i] - gc[j])
                        for j in range(i + 1)] for i in range(n_t)])

    for tile in (lwq, lk, lu):
        tile[...] = jnp.zeros_like(tile)

    def per_seq(b, carry):
        row = pl.ds(b, 1)
        for h in range(GDN_HEADS):
            for t in range(n_t):
                lwq[h, t:t + 1, :] = w_scr[h, t, row, :]
                lwq[h, LHS_ROWS + t:LHS_ROWS + t + 1, :] = qd_scr[h, t, row, :]
                lk[h, t:t + 1, :] = kd_scr[h, t, row, :]
                lu[h, t:t + 1, :] = u_scr[h, t, row, :]
            s0 = s0_ref[b, h]
            wq_s = jnp.dot(lwq[h].astype(BF16), s0.astype(BF16), preferred_element_type=F32)
            v_new = lu[h] - wq_s[:LHS_ROWS]
            s_ref[b, h] = s0 * ga_scr[h, row, :] + _dot_tn(lk[h].astype(BF16), v_new.astype(BF16))
            for t in range(n_t):
                u_scr[h, t, row, :] = v_new[t:t + 1, :]
                qd_scr[h, t, row, :] = wq_s[LHS_ROWS + t:LHS_ROWS + t + 1, :]
        return carry

    lax.fori_loop(0, n_b, per_seq, 0)

    for h in range(GDN_HEADS):
        hc = slice(h * GDN_DV, (h + 1) * GDN_DV)
        for i in range(n_t):
            o = qd_scr[h, i]
            for j in range(i + 1):
                o = o + scores[h][i][j] * u_scr[h, j]
            o = o * lax.rsqrt(jnp.mean(o * o, axis=-1, keepdims=True) + EPS) * nw_ref[...]
            o_ref[i, :, hc] = (o * _silu(z_ref[i, :, hc])).astype(o_ref.dtype)


def gdn_sample(y3, nar3, mixed3, buf0, states, new_states, layer, t0, n_t, lw):
    n_b = y3.shape[1]
    tb = _sample_block(t0, n_t)
    params = _gdn_params(lw, 1)[:4]
    tok = lambda s: pl.BlockSpec((n_t, SAMPLE_BB, GDN_QKV), lambda i: (tb, i, PK_OFF[s] // GDN_QKV))
    bufspec = pl.BlockSpec((CONV_W - 1, SAMPLE_BB, GDN_CONV_DIM), lambda i: (0, i, 0))
    state = pl.BlockSpec((None, SAMPLE_BB, GDN_HEADS, GDN_DK, GDN_DV), lambda i: (layer, i, 0, 0, 0))
    per_tok = pltpu.VMEM((GDN_HEADS, n_t, SAMPLE_BB, GDN_DK), F32)
    tile = lambda rows: pltpu.VMEM((GDN_HEADS, rows, GDN_DK), F32)
    n_in = 7 + len(params)
    mixed3, s, nbuf = pl.pallas_call(
        _gdn_sample_kernel,
        grid=(n_b // SAMPLE_BB,),
        in_specs=[tok(1), tok(2), tok(3), tok(4),
                  pl.BlockSpec((n_t, SAMPLE_BB, NARROW_W), lambda i: (tb, i, 0)), bufspec, state]
                 + [_const_spec(p.shape) for p in params] + [_ANY_SPEC, _ANY_SPEC],
        out_specs=[pl.BlockSpec((n_t, SAMPLE_BB, GDN_QKV), lambda i: (tb, i, MIX_GDN)), state, bufspec],
        out_shape=[jax.ShapeDtypeStruct(mixed3.shape, mixed3.dtype), jax.ShapeDtypeStruct(states.shape, F32),
                   jax.ShapeDtypeStruct((CONV_W - 1, n_b, GDN_CONV_DIM), F32)],
        input_output_aliases={n_in: 0, n_in + 1: 1},
        scratch_shapes=[per_tok, per_tok, per_tok, per_tok, pltpu.VMEM((GDN_HEADS, SAMPLE_BB, GDN_DV), F32),
                        tile(2 * LHS_ROWS), tile(LHS_ROWS), tile(LHS_ROWS)],
        compiler_params=pltpu.CompilerParams(
            dimension_semantics=("arbitrary",), vmem_limit_bytes=V7X_VMEM_LIMIT),
        name="gdn_sample",
    )(y3, y3, y3, y3, nar3, jnp.swapaxes(buf0, 0, 1), states, *params, mixed3, new_states)
    return mixed3, jnp.swapaxes(nbuf, 0, 1), s


def split_last(t, sizes):
    return jnp.split(t, [int(s) for s in np.cumsum(sizes)[:-1]], axis=-1)


def l2_normalize(x):
    xf = x.astype(jnp.float32)
    return xf * lax.rsqrt(jnp.sum(xf * xf, axis=-1, keepdims=True) + EPS)


def causal_conv(x, buf, w):
    L = x.shape[1]
    xp = jnp.concatenate([buf.astype(x.dtype), x], axis=1)
    y = xp[:, 0:L] * w[0]
    for j in range(1, CONV_W):
        y = y + xp[:, j:j + L] * w[j]
    return y, xp[:, L:]


def rotary(x, positions):
    half = x.shape[-1] // 2
    inv_freq = ROPE_BASE ** (-jnp.arange(half, dtype=jnp.float32) / half)
    ang = positions.astype(jnp.float32)[:, None] * inv_freq[None, :]
    cos = jnp.cos(ang)[None, :, None, :]
    sin = jnp.sin(ang)[None, :, None, :]
    xf = x.astype(jnp.float32)
    x1, x2 = xf[..., :half], xf[..., half:]
    return jnp.concatenate([x1 * cos - x2 * sin, x1 * sin + x2 * cos], axis=-1)


def to_chunks(t, chunk):
    L = t.shape[1]
    pad = (-L) % chunk
    t = jnp.pad(t, [(0, 0), (0, pad)] + [(0, 0)] * (t.ndim - 2))
    n = t.shape[1] // chunk
    t = t.reshape((t.shape[0], n, chunk) + t.shape[2:])
    return jnp.moveaxis(t, 2, 3)


def from_chunks(t, L):
    t = jnp.moveaxis(t, 3, 2)
    t = t.reshape((t.shape[0], t.shape[1] * t.shape[2]) + t.shape[3:])
    return t[:, :L]


def intra_decay(G):
    C = G.shape[-1]
    causal = jnp.tril(jnp.ones((C, C), dtype=bool))
    diff = G[..., :, None] - G[..., None, :]
    return jnp.where(causal, jnp.exp(jnp.where(causal, diff, 0.0)), 0.0)


def decay_linear_attention(q, k, v, log_a, s0, chunk):
    L = q.shape[1]
    C = min(chunk, L)
    qc, kc, vc = (to_chunks(t.astype(jnp.float32), C) for t in (q, k, v))
    G = jnp.cumsum(to_chunks(log_a.astype(jnp.float32), C), axis=-1)
    G_last = G[..., -1]
    scores = jnp.einsum('bnhid,bnhjd->bnhij', qc, kc) * intra_decay(G)
    intra = jnp.einsum('bnhij,bnhjv->bnhiv', scores, vc)
    chunk_states = jnp.einsum('bnhcd,bnhcv->bnhdv', kc * jnp.exp(G_last[..., None] - G)[..., None], vc)

    def step(S, inp):
        cs, gl = inp
        return S * gl[..., None, None] + cs, S

    s_final, s_prev = lax.scan(step, s0.astype(jnp.float32),
                               (jnp.moveaxis(chunk_states, 1, 0), jnp.moveaxis(jnp.exp(G_last), 1, 0)))
    s_prev = jnp.moveaxis(s_prev, 0, 1)
    inter = jnp.einsum('bnhcd,bnhdv->bnhcv', qc * jnp.exp(G)[..., None], s_prev)
    return from_chunks(intra + inter, L), s_final


def gated_delta_rule(q, k, v, beta, g, s0, chunk):
    L = q.shape[1]
    C = min(chunk, L)
    qc, kc, vc = (to_chunks(t.astype(jnp.float32), C) for t in (q, k, v))
    bc = to_chunks(beta.astype(jnp.float32), C)
    G = jnp.cumsum(to_chunks(g.astype(jnp.float32), C), axis=-1)
    decay = intra_decay(G)
    strict = jnp.tril(jnp.ones((C, C), dtype=bool), -1)
    kk = jnp.einsum('bnhid,bnhjd->bnhij', kc, kc)
    a_mat = jnp.where(strict, bc[..., :, None] * kk * decay, 0.0) + jnp.eye(C, dtype=jnp.float32)
    u = lax.linalg.triangular_solve(a_mat, vc * bc[..., None], left_side=True, lower=True, unit_diagonal=True)
    w = lax.linalg.triangular_solve(a_mat, kc * (bc * jnp.exp(G))[..., None], left_side=True, lower=True,
                                    unit_diagonal=True)
    scores = jnp.einsum('bnhid,bnhjd->bnhij', qc, kc) * decay
    q_dec = qc * jnp.exp(G)[..., None]
    k_dec = kc * jnp.exp(G[..., -1:] - G)[..., None]
    g_last = jnp.exp(G[..., -1])

    def step(S, inp):
        u_c, w_c, s_c, qd_c, kd_c, gl_c = inp
        v_new = u_c - jnp.einsum('bhcd,bhdv->bhcv', w_c, S)
        o = jnp.einsum('bhcd,bhdv->bhcv', qd_c, S) + jnp.einsum('bhij,bhjv->bhiv', s_c, v_new)
        S = S * gl_c[..., None, None] + jnp.einsum('bhcd,bhcv->bhdv', kd_c, v_new)
        return S, o

    xs = tuple(jnp.moveaxis(t, 1, 0) for t in (u, w, scores, q_dec, k_dec, g_last))
    s_final, o = lax.scan(step, s0.astype(jnp.float32), xs)
    return from_chunks(jnp.moveaxis(o, 0, 1), L), s_final


def complex_affine_combine(e1, e2):
    a1r, a1i, b1r, b1i = e1
    a2r, a2i, b2r, b2i = e2
    return (a2r * a1r - a2i * a1i, a2r * a1i + a2i * a1r,
            a2r * b1r - a2i * b1i + b2r, a2r * b1i + a2i * b1r + b2i)


def s5_mixer(u, h_re0, h_im0, lw):
    f32 = jnp.float32
    bsz, L, _ = u.shape
    a_re = lw['s5_a_re'].astype(f32)
    a_im = lw['s5_a_im'].astype(f32)
    step = jnp.exp(lw['s5_log_step'].astype(f32))[:, None]
    mag = jnp.exp(a_re * step)
    lam_re = mag * jnp.cos(a_im * step)
    lam_im = mag * jnp.sin(a_im * step)
    den = a_re * a_re + a_im * a_im
    coef_re = ((lam_re - 1.0) * a_re + lam_im * a_im) / den
    coef_im = (lam_im * a_re - (lam_re - 1.0) * a_im) / den
    b_re = lw['s5_b_re'].astype(f32)
    b_im = lw['s5_b_im'].astype(f32)
    bb_re = coef_re[..., None] * b_re - coef_im[..., None] * b_im
    bb_im = coef_re[..., None] * b_im + coef_im[..., None] * b_re
    ug = u.astype(f32).reshape(bsz, L, S5_GROUPS, S5_GROUP_CH)
    drive_re = jnp.einsum('blgc,gnc->blgn', ug, bb_re)
    drive_im = jnp.einsum('blgc,gnc->blgn', ug, bb_im)
    h0_re = h_re0.astype(f32)
    h0_im = h_im0.astype(f32)
    drive_re = drive_re.at[:, 0].add(lam_re * h0_re - lam_im * h0_im)
    drive_im = drive_im.at[:, 0].add(lam_re * h0_im + lam_im * h0_re)
    lam_re_b = jnp.broadcast_to(lam_re, drive_re.shape)
    lam_im_b = jnp.broadcast_to(lam_im, drive_im.shape)
    _, _, hs_re, hs_im = lax.associative_scan(complex_affine_combine, (lam_re_b, lam_im_b, drive_re, drive_im), axis=1)
    c_re = lw['s5_c_re'].astype(f32)
    c_im = lw['s5_c_im'].astype(f32)
    y = jnp.einsum('blgn,gcn->blgc', hs_re, c_re) - jnp.einsum('blgn,gcn->blgc', hs_im, c_im)
    y = y.reshape(bsz, L, S5_CH) + lw['s5_d'].astype(f32) * u.astype(f32)
    y = jax.nn.gelu(y)
    out = y * jax.nn.sigmoid(y @ lw['s5_w_glu'].astype(f32) + lw['s5_b_glu'].astype(f32))
    return out.astype(u.dtype), hs_re[:, -1], hs_im[:, -1]


def gdn_mixer(q, k, v, z, b_logit, a_logit, buf0, s0, lw):
    dt_out = z.dtype
    bsz, L, _ = q.shape
    qkv, buf = causal_conv(jnp.concatenate([q, k, v], axis=-1), buf0, lw['gdn_conv_w'])
    qkv = jax.nn.silu(qkv)
    qh, kh, vh = split_last(qkv, (GDN_HEADS * GDN_DK, GDN_HEADS * GDN_DK, GDN_HEADS * GDN_DV))
    qh = l2_normalize(qh.reshape(bsz, L, GDN_HEADS, GDN_DK)) * (GDN_DK ** -0.5)
    kh = l2_normalize(kh.reshape(bsz, L, GDN_HEADS, GDN_DK))
    vh = vh.reshape(bsz, L, GDN_HEADS, GDN_DV)
    beta = jax.nn.sigmoid(b_logit.astype(jnp.float32))
    g = -jnp.exp(lw['gdn_a_log'].astype(jnp.float32)) * jax.nn.softplus(
        a_logit.astype(jnp.float32) + lw['gdn_dt_bias'].astype(jnp.float32))
    o, s = gated_delta_rule(qh, kh, vh, beta, g, s0, GDN_CHUNK)
    o = o * lax.rsqrt(jnp.mean(o * o, axis=-1, keepdims=True) + EPS) * lw['gdn_norm_w'].astype(jnp.float32)
    o = o * jax.nn.silu(z.astype(jnp.float32).reshape(bsz, L, GDN_HEADS, GDN_DV))
    return o.reshape(bsz, L, GDN_HEADS * GDN_DV).astype(dt_out), buf, s


def ssd_mixer(z, xbc, dt_raw, buf0, s0, lw):
    f32 = jnp.float32
    bsz, L, _ = z.shape
    xbc, buf = causal_conv(xbc, buf0, lw['ssd_conv_w'])
    xbc = jax.nn.silu(xbc + lw['ssd_conv_b'])
    xs, bm, cm = split_last(xbc, (SSD_INNER, SSD_NGROUPS * SSD_STATE, SSD_NGROUPS * SSD_STATE))
    rep = SSD_HEADS // SSD_NGROUPS
    xs = xs.astype(f32).reshape(bsz, L, SSD_HEADS, SSD_HEADDIM)
    bm = jnp.repeat(bm.astype(f32).reshape(bsz, L, SSD_NGROUPS, SSD_STATE), rep, axis=2)
    cm = jnp.repeat(cm.astype(f32).reshape(bsz, L, SSD_NGROUPS, SSD_STATE), rep, axis=2)
    dt = jax.nn.softplus(dt_raw.astype(f32) + lw['ssd_dt_bias'].astype(f32))
    a = -jnp.exp(lw['ssd_a_log'].astype(f32))
    y, s = decay_linear_attention(cm, bm * dt[..., None], xs, dt * a, s0, SSD_CHUNK)
    y = y + xs * lw['ssd_d'].astype(f32)[:, None]
    y = y.reshape(bsz, L, SSD_INNER) * jax.nn.silu(z.astype(f32))
    y = y.reshape(bsz, L, SSD_NGROUPS, SSD_INNER // SSD_NGROUPS)
    y = y * lax.rsqrt(jnp.mean(y * y, axis=-1, keepdims=True) + EPS)
    y = y.reshape(bsz, L, SSD_INNER) * lw['ssd_norm_w'].astype(f32)
    return y.astype(z.dtype), buf, s


def retention_mixer(q, k, v, gate, s0, positions, lw):
    f32 = jnp.float32
    bsz, L, _ = q.shape
    qh = rotary(q.reshape(bsz, L, RET_HEADS, RET_DK), positions)
    kh = rotary(k.reshape(bsz, L, RET_HEADS, RET_DK), positions) * (RET_DK ** -0.5)
    vh = v.reshape(bsz, L, RET_HEADS, RET_DV)
    log_gamma = jnp.log(1.0 - 2.0 ** (-5.0 - jnp.arange(RET_HEADS, dtype=f32)))
    log_a = jnp.broadcast_to(log_gamma, (bsz, L, RET_HEADS))
    o, s = decay_linear_attention(qh, kh, vh, log_a, s0, RET_CHUNK)
    mu = jnp.mean(o, axis=-1, keepdims=True)
    var = jnp.mean((o - mu) ** 2, axis=-1, keepdims=True)
    o = ((o - mu) * lax.rsqrt(var + EPS)).reshape(bsz, L, RET_HEADS * RET_DV)
    o = o * lw['ret_ln_w'].astype(f32) + lw['ret_ln_b'].astype(f32)
    out = jax.nn.silu(gate.astype(f32)) * o
    return out.astype(q.dtype), s


def _mixers(y, st, lw, positions):
    def col(s):
        return y[..., PK_OFF[s]:PK_OFF[s] + IN_SIZES[s]]

    gdn_s0, gdn_buf0, ssd_s0, ssd_buf0, ret_s0 = st
    out_b, gdn_buf, gdn_s = gdn_mixer(col(1), col(2), col(3), col(4), col(5), col(6), gdn_buf0, gdn_s0, lw)
    out_c, ssd_buf, ssd_s = ssd_mixer(col(7), col(8), col(9), ssd_buf0, ssd_s0, lw)
    out_d, ret_s = retention_mixer(col(10), col(11), col(12), col(13), ret_s0, positions, lw)
    mixed = jnp.concatenate([out_b, out_c, out_d], axis=-1).astype(BF16)
    return mixed, (gdn_s, gdn_buf, ssd_s, ssd_buf, ret_s)


TM = 1088
TM_SPLIT = 512


def kernel(x_prompt, x_sample, p_prompt, p_sample, state_s5_re, state_s5_im, state_gdn, state_gdn_conv, state_ssd, state_ssd_conv, state_ret, norm_mix, w_in, s5_a_re, s5_a_im, s5_b_re, s5_b_im, s5_c_re, s5_c_im, s5_d, s5_log_step, s5_w_glu, s5_b_glu, gdn_conv_w, gdn_a_log, gdn_dt_bias, gdn_norm_w, ssd_conv_w, ssd_conv_b, ssd_dt_bias, ssd_a_log, ssd_d, ssd_norm_w, ret_ln_w, ret_ln_b, w_out, norm_ffn, w_ffn_in, w_ffn_out, norm_ple, w_ple_gate, w_ple_proj, norm_final):
    bp, lp, d = x_prompt.shape
    bs, ls, _ = x_sample.shape
    np_tok = bp * lp
    ns_tok = bs * ls
    n_tok = np_tok + ns_tok

    mixer_w = dict(
        s5_a_re=s5_a_re, s5_a_im=s5_a_im, s5_b_re=s5_b_re, s5_b_im=s5_b_im, s5_c_re=s5_c_re, s5_c_im=s5_c_im,
        s5_d=s5_d, s5_log_step=s5_log_step, s5_w_glu=s5_w_glu, s5_b_glu=s5_b_glu,
        gdn_conv_w=gdn_conv_w, gdn_a_log=gdn_a_log, gdn_dt_bias=gdn_dt_bias, gdn_norm_w=gdn_norm_w,
        ssd_conv_w=ssd_conv_w, ssd_conv_b=ssd_conv_b, ssd_dt_bias=ssd_dt_bias, ssd_a_log=ssd_a_log,
        ssd_d=ssd_d, ssd_norm_w=ssd_norm_w, ret_ln_w=ret_ln_w, ret_ln_b=ret_ln_b)

    assert bp == 4, "the prompt S5 kernel packs two time steps of four sequences per vreg"

    def tm_rows(t):
        t = jnp.swapaxes(t, -3, -2)
        return t.reshape(t.shape[:-3] + (t.shape[-3] * t.shape[-2], t.shape[-1]))

    def bm_seqs(t, b):
        return jnp.swapaxes(t.reshape(t.shape[0] // b, b, t.shape[1]), 0, 1)

    w_out_b, w_ffn_in_b, w_ffn_out_b = (w.astype(BF16) for w in (w_out, w_ffn_in, w_ffn_out))
    w_gate_b, w_proj_b = w_ple_gate.astype(BF16), w_ple_proj.astype(BF16)
    pp = p_prompt.reshape(DEPTH, np_tok, PLE_DIM)
    ps = tm_rows(p_sample)

    h = jnp.concatenate([x_prompt.reshape(np_tok, d), tm_rows(x_sample)], axis=0)
    ssd_t = jnp.swapaxes(state_ssd, -1, -2)
    new_gdn, new_ssd_t, new_ret = (uninitialized(s.shape, F32) for s in (state_gdn, ssd_t, state_ret))
    new_p, new_s = [], []
    for i in range(DEPTH):
        lw = {k: v[i] for k, v in mixer_w.items()}
        y, nar = in_projection(h, norm_mix[i], w_in[i], tm=TM)
        y3 = y.reshape(n_tok // bs, bs, PK_TOTAL)
        nar3 = nar.reshape(n_tok // bs, bs, NARROW_W)
        t0 = np_tok // bs

        tb = _s5_tables(lw)
        a_p, st5_p = s5_prompt(tm_rows(y[:np_tok, :S5_CH].reshape(bp, lp, S5_CH)), tb, rows=512)

        mixed = uninitialized((n_tok, d), BF16)
        mixed, gdn_buf_p, gdn_s_p = gdn_prompt(y, nar, mixed, bp, lp, lw)
        mixed, ssd_buf_p, ssd_s_p = ssd_prompt(y, nar, mixed, bp, lp, lw)
        mixed, ret_s_p = ret_prompt(y, mixed, bp, lp, lw)
        mixed, st5_s = s5_sample(y, mixed, _s5_state_to_lanes(state_s5_re[i], state_s5_im[i]), np_tok, ns_tok, tb)
        mixed3 = mixed.reshape(n_tok // bs, bs, d)
        mixed3, gdn_buf_s, new_gdn = gdn_sample(y3, nar3, mixed3, state_gdn_conv[i], state_gdn, new_gdn, i, t0, ls, lw)
        mixed3, ssd_buf_s, new_ssd_t = ssd_sample(y3, nar3, mixed3, state_ssd_conv[i], ssd_t, new_ssd_t, i, t0, ls, lw)
        mixed3, new_ret = ret_sample(y3, mixed3, state_ret, new_ret, i, t0, ls, PAST_LEN, lw)
        mixed = lax.dynamic_update_slice(mixed3.reshape(n_tok, d), bm_seqs(a_p, bp).reshape(np_tok, S5_CH), (0, 0))
        new_p.append(_s5_lanes_to_state(st5_p[:bp]) + (gdn_s_p, gdn_buf_p, ssd_s_p, ssd_buf_p, ret_s_p))
        new_s.append(_s5_lanes_to_state(st5_s) + (gdn_buf_s, ssd_buf_s))

        h = matmul_residual(mixed, w_out_b, i, h, tm=TM)
        h = ffn_residual(h, norm_ffn[i], w_ffn_in_b, w_ffn_out_b, i, tm=TM // 2, th=512)
        if i + 1 < DEPTH:
            h = ple_residual(h, norm_ple[i], w_gate_b, pp, ps, w_proj_b, i, tm=TM_SPLIT)
        else:
            y_p, y_s = ple_residual(h, norm_ple[i], w_gate_b, pp, ps, w_proj_b, i, tm=TM_SPLIT, nf=norm_final)

    stack_p = [jnp.stack([st[j] for st in new_p]) for j in range(7)]
    s5_re_s, s5_im_s, gdn_buf_s, ssd_buf_s = (jnp.stack([st[j] for st in new_s]) for j in range(4))
    return (y_p.reshape(bp, lp, d), bm_seqs(y_s, bs), *stack_p,
            s5_re_s, s5_im_s, new_gdn, gdn_buf_s, jnp.swapaxes(new_ssd_t, -1, -2), ssd_buf_s, new_ret)
```

```python
import functools
import math

import jax
import jax.numpy as jnp
import numpy as np
from jax import lax
from jax.experimental import pallas as pl
from jax.experimental.pallas import tpu as pltpu

F32 = jnp.float32
BF16 = jnp.bfloat16

D_MODEL = 2048
DEPTH = 2
GROUP_WIDTH = D_MODEL // 4
CONV_W = 4
EPS = 1e-6
PLE_DIM = 256
FFN_HIDDEN = ((8 * D_MODEL + 3 * 256 - 1) // (3 * 256)) * 256

S5_CH = GROUP_WIDTH
S5_GROUP_CH = 16
S5_GROUPS = S5_CH // S5_GROUP_CH
S5_STATE = 64

GDN_HEADS = 4
GDN_DK = GROUP_WIDTH // GDN_HEADS
GDN_DV = GROUP_WIDTH // GDN_HEADS
GDN_CHUNK = 64
GDN_CONV_DIM = 2 * GDN_HEADS * GDN_DK + GDN_HEADS * GDN_DV

SSD_INNER = GROUP_WIDTH
SSD_HEADDIM = 64
SSD_HEADS = SSD_INNER // SSD_HEADDIM
SSD_NGROUPS = 2
SSD_STATE = 128
SSD_CHUNK = 128
SSD_CONV_DIM = SSD_INNER + 2 * SSD_NGROUPS * SSD_STATE

RET_HEADS = 4
RET_DK = GROUP_WIDTH // RET_HEADS
RET_DV = GROUP_WIDTH // RET_HEADS
RET_CHUNK = 128
ROPE_BASE = 10000.0
PAST_LEN = 16384

IN_SIZES = (
    S5_CH,
    GDN_HEADS * GDN_DK, GDN_HEADS * GDN_DK, GDN_HEADS * GDN_DV, GDN_HEADS * GDN_DV, GDN_HEADS, GDN_HEADS,
    SSD_INNER, SSD_CONV_DIM, SSD_HEADS,
    RET_HEADS * RET_DK, RET_HEADS * RET_DK, RET_HEADS * RET_DV, RET_HEADS * RET_DV,
)
IN_OFFS = tuple(int(v) for v in np.cumsum((0,) + IN_SIZES))

_REGIONS = ((0, 1, 2, 3, 4), (7, 8), (10, 11, 12, 13))
_NARROW = (5, 6, 9)
PK_OFF = {}
_o = 0
for _reg in _REGIONS:
    for _s in _reg:
        PK_OFF[_s] = _o
        _o += IN_SIZES[_s]
PK_TOTAL = _o
IN_TN = 512
REGION_TILES = tuple(sum(IN_SIZES[s] for s in reg) // IN_TN for reg in _REGIONS)
NARROW_W = 128

V7X_VMEM_LIMIT = 58 * 1024 * 1024


def _split_w_in(w_in):
    wide = [w_in[:, IN_OFFS[reg[0]]:IN_OFFS[reg[-1] + 1]].astype(BF16) for reg in _REGIONS]
    nar = jnp.concatenate([w_in[:, IN_OFFS[s]:IN_OFFS[s + 1]] for s in _NARROW], axis=1)
    nar = jnp.pad(nar, ((0, 0), (0, NARROW_W - nar.shape[1]))).astype(BF16)
    return wide, nar


def _rms_rows(x, nw):
    ms = jnp.mean(x * x, axis=-1, keepdims=True)
    return x * lax.rsqrt(ms + EPS) * nw


def _in_proj_kernel(x_ref, nw_ref, wa_ref, wb_ref, wc_ref, wn_ref, o_ref, nar_ref, xn_ref):
    j = pl.program_id(1)

    @pl.when(j == 0)
    def _():
        xn_ref[...] = _rms_rows(x_ref[...], nw_ref[...]).astype(BF16)
        nar_ref[...] = jnp.dot(xn_ref[...], wn_ref[...], preferred_element_type=F32)

    first = 0
    for w_ref, n_tiles in zip((wa_ref, wb_ref, wc_ref), REGION_TILES):
        @pl.when((j >= first) & (j < first + n_tiles))
        def _(w_ref=w_ref):
            o_ref[...] = jnp.dot(xn_ref[...], w_ref[...], preferred_element_type=F32)
        first += n_tiles


def in_projection(x, nw, w_in, *, tm):
    m, k = x.shape
    wide, nar = _split_w_in(w_in)
    starts = [sum(REGION_TILES[:r]) for r in range(len(REGION_TILES))]

    def region_spec(r):
        return pl.BlockSpec((k, IN_TN), lambda i, j: (0, jnp.clip(j - starts[r], 0, REGION_TILES[r] - 1)))

    return pl.pallas_call(
        _in_proj_kernel,
        grid=(m // tm, PK_TOTAL // IN_TN),
        in_specs=[
            pl.BlockSpec((tm, k), lambda i, j: (i, 0)),
            pl.BlockSpec((1, k), lambda i, j: (0, 0)),
            region_spec(0), region_spec(1), region_spec(2),
            pl.BlockSpec((k, NARROW_W), lambda i, j: (0, 0)),
        ],
        out_specs=[pl.BlockSpec((tm, IN_TN), lambda i, j: (i, j)),
                   pl.BlockSpec((tm, NARROW_W), lambda i, j: (i, 0))],
        out_shape=[jax.ShapeDtypeStruct((m, PK_TOTAL), F32), jax.ShapeDtypeStruct((m, NARROW_W), F32)],
        scratch_shapes=[pltpu.VMEM((tm, k), BF16)],
        compiler_params=pltpu.CompilerParams(
            dimension_semantics=("arbitrary", "arbitrary"), vmem_limit_bytes=V7X_VMEM_LIMIT),
        name="in_projection",
    )(x, nw.reshape(1, k), *wide, nar)


def _mm_res_kernel(a_ref, w_ref, h_ref, o_ref):
    o_ref[...] = h_ref[...] + jnp.dot(a_ref[...], w_ref[...], preferred_element_type=F32)


def matmul_residual(a, w, layer, h, *, tm):
    m, k = a.shape
    n = w.shape[2]
    return pl.pallas_call(
        _mm_res_kernel,
        grid=(m // tm,),
        in_specs=[
            pl.BlockSpec((tm, k), lambda i: (i, 0)),
            pl.BlockSpec((None, k, n), lambda i: (layer, 0, 0)),
            pl.BlockSpec((tm, n), lambda i: (i, 0)),
        ],
        out_specs=pl.BlockSpec((tm, n), lambda i: (i, 0)),
        out_shape=jax.ShapeDtypeStruct((m, n), F32),
        compiler_params=pltpu.CompilerParams(
            dimension_semantics=("arbitrary",), vmem_limit_bytes=V7X_VMEM_LIMIT),
        name="matmul_residual",
    )(a, w, h)


def _silu(x):
    return x * jax.nn.sigmoid(x)


def _ffn_kernel(h_ref, nw_ref, wg_ref, wu_ref, wo_ref, o_ref, xn_ref):
    @pl.when(pl.program_id(1) == 0)
    def _():
        h = h_ref[...]
        xn_ref[...] = _rms_rows(h, nw_ref[...]).astype(BF16)
        o_ref[...] = h

    xn = xn_ref[...]
    gate = jnp.dot(xn, wg_ref[...], preferred_element_type=F32)
    up = jnp.dot(xn, wu_ref[...], preferred_element_type=F32)
    act = (_silu(gate) * up).astype(BF16)
    o_ref[...] += jnp.dot(act, wo_ref[...], preferred_element_type=F32)


def ffn_residual(h, nw, w_in, w_out, layer, *, tm, th):
    m, k = h.shape
    hidden = w_out.shape[1]
    nj = hidden // th
    return pl.pallas_call(
        _ffn_kernel,
        grid=(m // tm, nj),
        in_specs=[
            pl.BlockSpec((tm, k), lambda i, j: (i, 0)),
            pl.BlockSpec((1, k), lambda i, j: (0, 0)),
            pl.BlockSpec((None, k, th), lambda i, j: (layer, 0, j)),
            pl.BlockSpec((None, k, th), lambda i, j: (layer, 0, j + nj)),
            pl.BlockSpec((None, th, k), lambda i, j: (layer, j, 0)),
        ],
        out_specs=pl.BlockSpec((tm, k), lambda i, j: (i, 0)),
        out_shape=jax.ShapeDtypeStruct((m, k), F32),
        scratch_shapes=[pltpu.VMEM((tm, k), BF16)],
        compiler_params=pltpu.CompilerParams(
            dimension_semantics=("arbitrary", "arbitrary"), vmem_limit_bytes=V7X_VMEM_LIMIT),
        name="ffn_residual",
    )(h, nw.reshape(1, k), w_in, w_in, w_out)


def _ple_rows(h_ref, nw_ref, wg_ref, pp_ref, ps_ref, wp_ref, n_prompt_tiles):
    h = h_ref[...]
    xn = _rms_rows(h, nw_ref[...]).astype(BF16)
    gate = jax.nn.sigmoid(jnp.dot(xn, wg_ref[...], preferred_element_type=F32))
    p = jnp.where(pl.program_id(0) < n_prompt_tiles, pp_ref[...], ps_ref[...])
    return h + gate * jnp.dot(p.astype(BF16), wp_ref[...], preferred_element_type=F32)


def _ple_kernel(h_ref, nw_ref, wg_ref, pp_ref, ps_ref, wp_ref, o_ref, *, n_prompt_tiles):
    o_ref[...] = _ple_rows(h_ref, nw_ref, wg_ref, pp_ref, ps_ref, wp_ref, n_prompt_tiles)


def _ple_final_kernel(h_ref, nw_ref, wg_ref, pp_ref, ps_ref, wp_ref, nf_ref, yp_ref, ys_ref, *, n_prompt_tiles):
    y = _rms_rows(_ple_rows(h_ref, nw_ref, wg_ref, pp_ref, ps_ref, wp_ref, n_prompt_tiles), nf_ref[...])

    @pl.when(pl.program_id(0) < n_prompt_tiles)
    def _():
        yp_ref[...] = y

    @pl.when(pl.program_id(0) >= n_prompt_tiles)
    def _():
        ys_ref[...] = y


def ple_residual(h, nw, wg, pp, ps, wp, layer, *, tm, nf=None):
    m, k = h.shape
    mp, pd = pp.shape[1], pp.shape[2]
    ms = ps.shape[1]
    assert mp % tm == 0 and ms % tm == 0 and mp + ms == m
    npt, nst = mp // tm, ms // tm
    row = pl.BlockSpec((tm, k), lambda i: (i, 0))
    vec = pl.BlockSpec((1, k), lambda i: (0, 0))
    in_specs = [row, vec,
                pl.BlockSpec((None, k, k), lambda i: (layer, 0, 0)),
                pl.BlockSpec((None, tm, pd), lambda i: (layer, jnp.minimum(i, npt - 1), 0)),
                pl.BlockSpec((None, tm, pd), lambda i: (layer, jnp.clip(i - npt, 0, nst - 1), 0)),
                pl.BlockSpec((None, pd, k), lambda i: (layer, 0, 0))]
    args = [h, nw.reshape(1, k), wg, pp, ps, wp]
    params = pltpu.CompilerParams(dimension_semantics=("arbitrary",), vmem_limit_bytes=V7X_VMEM_LIMIT)
    if nf is None:
        return pl.pallas_call(
            functools.partial(_ple_kernel, n_prompt_tiles=npt), grid=(m // tm,), in_specs=in_specs,
            out_specs=row, out_shape=jax.ShapeDtypeStruct((m, k), F32), compiler_params=params,
            name="ple_residual")(*args)
    return pl.pallas_call(
        functools.partial(_ple_final_kernel, n_prompt_tiles=npt), grid=(m // tm,), in_specs=in_specs + [vec],
        out_specs=[pl.BlockSpec((tm, k), lambda i: (jnp.minimum(i, npt - 1), 0)),
                   pl.BlockSpec((tm, k), lambda i: (jnp.clip(i - npt, 0, nst - 1), 0))],
        out_shape=[jax.ShapeDtypeStruct((mp, k), F32), jax.ShapeDtypeStruct((ms, k), F32)],
        compiler_params=params, name="ple_final")(*args, nf.reshape(1, k))


S5_HALF_CH = S5_CH // 2
S5_HALF_ST = (S5_GROUPS // 2) * S5_STATE
S5_LANES = 4 * S5_HALF_ST
S5_SLAB = 512


def _s5_tables(lw):
    a_re = lw['s5_a_re'].astype(F32)
    a_im = lw['s5_a_im'].astype(F32)
    step = jnp.exp(lw['s5_log_step'].astype(F32))[:, None]
    mag = jnp.exp(a_re * step)
    lam_re = mag * jnp.cos(a_im * step)
    lam_im = mag * jnp.sin(a_im * step)
    den = a_re * a_re + a_im * a_im
    coef_re = ((lam_re - 1.0) * a_re + lam_im * a_im) / den
    coef_im = (lam_im * a_re - (lam_re - 1.0) * a_im) / den
    b_re = lw['s5_b_re'].astype(F32)
    b_im = lw['s5_b_im'].astype(F32)
    bb_re = coef_re[..., None] * b_re - coef_im[..., None] * b_im
    bb_im = coef_re[..., None] * b_im + coef_im[..., None] * b_re
    gh = S5_GROUPS // 2
    eye = jnp.eye(gh, dtype=F32)

    def in_blockdiag(b):
        return jnp.einsum('gnc,gh->gchn', b, eye).reshape(gh * S5_GROUP_CH, gh * S5_STATE)

    def out_blockdiag(c):
        return jnp.einsum('gcn,gh->gnhc', c, eye).reshape(gh * S5_STATE, gh * S5_GROUP_CH)

    c_re = lw['s5_c_re'].astype(F32)
    c_im = lw['s5_c_im'].astype(F32)
    bb = jnp.stack([jnp.concatenate([in_blockdiag(bb_re[h * gh:(h + 1) * gh]),
                                     in_blockdiag(bb_im[h * gh:(h + 1) * gh])], axis=1) for h in range(2)])
    cm = jnp.stack([jnp.concatenate([out_blockdiag(c_re[h * gh:(h + 1) * gh]),
                                     -out_blockdiag(c_im[h * gh:(h + 1) * gh])], axis=0) for h in range(2)])
    lam = jnp.stack([lam_re.reshape(-1), lam_im.reshape(-1)])
    lam2 = jnp.stack([lam[0] * lam[0] - lam[1] * lam[1], 2.0 * lam[0] * lam[1]])
    return dict(bb=bb.astype(BF16), cm=cm.astype(BF16), lam=lam, lam2=lam2,
                d=lw['s5_d'].astype(F32).reshape(1, S5_CH), wglu=lw['s5_w_glu'].astype(BF16),
                bglu=lw['s5_b_glu'].astype(F32).reshape(1, S5_CH))


def _s5_drive(u, bb_ref, sc_ref):
    ub = u.astype(BF16)
    for hf in range(2):
        sc_ref[:, hf * 2 * S5_HALF_ST:(hf + 1) * 2 * S5_HALF_ST] = jnp.dot(
            ub[:, hf * S5_HALF_CH:(hf + 1) * S5_HALF_CH], bb_ref[hf], preferred_element_type=F32)


def _s5_readout(sc_ref, u, cm_ref, d_ref, wglu_ref, bglu_ref):
    ys = [jnp.dot(sc_ref[:, hf * 2 * S5_HALF_ST:(hf + 1) * 2 * S5_HALF_ST].astype(BF16), cm_ref[hf],
                  preferred_element_type=F32) for hf in range(2)]
    y = jnp.concatenate(ys, axis=1) + d_ref[...] * u
    y = jax.nn.gelu(y)
    z = jnp.dot(y.astype(BF16), wglu_ref[...], preferred_element_type=F32) + bglu_ref[...]
    return y * jax.nn.sigmoid(z)


def _s5_slabs():
    for hf in range(2):
        for sl in range(S5_HALF_ST // S5_SLAB):
            re0 = hf * 2 * S5_HALF_ST + sl * S5_SLAB
            yield re0, re0 + S5_HALF_ST, hf * S5_HALF_ST + sl * S5_SLAB


def _s5_prompt_kernel(u_ref, bb_ref, m_ref, cm_ref, d_ref, wglu_ref, bglu_ref, o_ref, st_ref, sc_ref, carry_ref):
    @pl.when(pl.program_id(0) == 0)
    def _():
        carry_ref[...] = jnp.zeros_like(carry_ref)

    u = u_ref[...]
    _s5_drive(u, bb_ref, sc_ref)
    first_step = lax.broadcasted_iota(jnp.int32, (8, S5_SLAB), 0) < 4
    n_pairs = u_ref.shape[0] // 8
    for re0, im0, l0 in _s5_slabs():
        mr = m_ref[0, :, l0:l0 + S5_SLAB]
        mi = m_ref[1, :, l0:l0 + S5_SLAB]
        nr = m_ref[2, :, l0:l0 + S5_SLAB]
        ni = m_ref[3, :, l0:l0 + S5_SLAB]

        def body(k, carry, re0=re0, im0=im0, mr=mr, mi=mi, nr=nr, ni=ni):
            hr, hi = carry
            base = pl.multiple_of(k * 8, 8)
            xr = sc_ref[pl.ds(base, 8), re0:re0 + S5_SLAB]
            xi = sc_ref[pl.ds(base, 8), im0:im0 + S5_SLAB]
            xr_s = pltpu.roll(xr, 4, 0)
            xi_s = pltpu.roll(xi, 4, 0)
            outr = (mr * hr - mi * hi) + xr + (nr * xr_s - ni * xi_s)
            outi = (mr * hi + mi * hr) + xi + (nr * xi_s + ni * xr_s)
            sc_ref[pl.ds(base, 8), re0:re0 + S5_SLAB] = outr
            sc_ref[pl.ds(base, 8), im0:im0 + S5_SLAB] = outi
            return (jnp.where(first_step, pltpu.roll(outr, 4, 0), outr),
                    jnp.where(first_step, pltpu.roll(outi, 4, 0), outi))

        hr, hi = lax.fori_loop(0, n_pairs, body,
                               (carry_ref[:, re0:re0 + S5_SLAB], carry_ref[:, im0:im0 + S5_SLAB]))
        carry_ref[:, re0:re0 + S5_SLAB] = hr
        carry_ref[:, im0:im0 + S5_SLAB] = hi

    o_ref[...] = _s5_readout(sc_ref, u, cm_ref, d_ref, wglu_ref, bglu_ref).astype(o_ref.dtype)
    st_ref[...] = carry_ref[...]


def _const_spec(shape):
    return pl.BlockSpec(shape, lambda c: (0,) * len(shape))


def s5_prompt(u_tm, tb, *, rows):
    n = u_tm.shape[0]
    zero = jnp.zeros_like(tb['lam'])
    m = jnp.stack([jnp.concatenate([jnp.broadcast_to(a[k][None], (4, a.shape[1])),
                                    jnp.broadcast_to(b[k][None], (4, b.shape[1]))], axis=0)
                   for a, b, k in ((tb['lam'], tb['lam2'], 0), (tb['lam'], tb['lam2'], 1),
                                   (zero, tb['lam'], 0), (zero, tb['lam'], 1))])
    return pl.pallas_call(
        _s5_prompt_kernel,
        grid=(n // rows,),
        in_specs=[
            pl.BlockSpec((rows, S5_CH), lambda c: (c, 0)),
            _const_spec(tb['bb'].shape), _const_spec(m.shape), _const_spec(tb['cm'].shape),
            _const_spec((1, S5_CH)), _const_spec((S5_CH, S5_CH)), _const_spec((1, S5_CH)),
        ],
        out_specs=[pl.BlockSpec((rows, S5_CH), lambda c: (c, 0)), _const_spec((8, S5_LANES))],
        out_shape=[jax.ShapeDtypeStruct((n, S5_CH), BF16), jax.ShapeDtypeStruct((8, S5_LANES), F32)],
        scratch_shapes=[pltpu.VMEM((rows, S5_LANES), F32), pltpu.VMEM((8, S5_LANES), F32)],
        compiler_params=pltpu.CompilerParams(
            dimension_semantics=("arbitrary",), vmem_limit_bytes=V7X_VMEM_LIMIT),
        name="s5_prompt",
    )(u_tm, tb['bb'], m, tb['cm'], tb['d'], tb['wglu'], tb['bglu'])


def _s5_sample_kernel(u_ref, h0_ref, bb_ref, lam_ref, cm_ref, d_ref, wglu_ref, bglu_ref, mix_ref,
                      o_ref, st_ref, sc_ref):
    del mix_ref
    u = u_ref[...]
    _s5_drive(u, bb_ref, sc_ref)
    n_seq = h0_ref.shape[0]
    n_steps = u_ref.shape[0] // n_seq
    for re0, im0, l0 in _s5_slabs():
        lr = lam_ref[0:1, l0:l0 + S5_SLAB]
        li = lam_ref[1:2, l0:l0 + S5_SLAB]

        def body(rb, _, re0=re0, im0=im0, lr=lr, li=li):
            r0 = pl.multiple_of(rb * 8, 8)
            hr = h0_ref[pl.ds(r0, 8), re0:re0 + S5_SLAB]
            hi = h0_ref[pl.ds(r0, 8), im0:im0 + S5_SLAB]
            for t in range(n_steps):
                rows = pl.ds(t * n_seq + r0, 8)
                nr = (lr * hr - li * hi) + sc_ref[rows, re0:re0 + S5_SLAB]
                ni = (lr * hi + li * hr) + sc_ref[rows, im0:im0 + S5_SLAB]
                sc_ref[rows, re0:re0 + S5_SLAB] = nr
                sc_ref[rows, im0:im0 + S5_SLAB] = ni
                hr, hi = nr, ni
            st_ref[pl.ds(r0, 8), re0:re0 + S5_SLAB] = hr
            st_ref[pl.ds(r0, 8), im0:im0 + S5_SLAB] = hi
            return 0

        lax.fori_loop(0, n_seq // 8, body, 0)

    o_ref[...] = _s5_readout(sc_ref, u, cm_ref, d_ref, wglu_ref, bglu_ref).astype(o_ref.dtype)


def s5_sample(y, mixed, h0, row0, n_rows, tb):
    nb = h0.shape[0]
    assert row0 % n_rows == 0
    blk = row0 // n_rows
    return pl.pallas_call(
        _s5_sample_kernel,
        grid=(1,),
        in_specs=[pl.BlockSpec((n_rows, S5_CH), lambda c: (blk, PK_OFF[0] // S5_CH)),
                  _const_spec((nb, S5_LANES)), _const_spec(tb['bb'].shape),
                  _const_spec(tb['lam'].shape), _const_spec(tb['cm'].shape), _const_spec((1, S5_CH)),
                  _const_spec((S5_CH, S5_CH)), _const_spec((1, S5_CH)), _ANY_SPEC],
        out_specs=[pl.BlockSpec((n_rows, S5_CH), lambda c: (blk, MIX_S5)), _const_spec((nb, S5_LANES))],
        out_shape=[jax.ShapeDtypeStruct(mixed.shape, mixed.dtype), jax.ShapeDtypeStruct((nb, S5_LANES), F32)],
        input_output_aliases={8: 0},
        scratch_shapes=[pltpu.VMEM((n_rows, S5_LANES), F32)],
        compiler_params=pltpu.CompilerParams(
            dimension_semantics=("arbitrary",), vmem_limit_bytes=V7X_VMEM_LIMIT),
        name="s5_sample",
    )(y, h0, tb['bb'], tb['lam'], tb['cm'], tb['d'], tb['wglu'], tb['bglu'], mixed)


def _s5_state_to_lanes(re, im):
    b = re.shape[0]
    return jnp.stack([re.reshape(b, 2, S5_HALF_ST), im.reshape(b, 2, S5_HALF_ST)], axis=2).reshape(b, S5_LANES)


def _s5_lanes_to_state(st):
    b = st.shape[0]
    st = st.reshape(b, 2, 2, S5_HALF_ST)
    return st[:, :, 0].reshape(b, S5_GROUPS, S5_STATE), st[:, :, 1].reshape(b, S5_GROUPS, S5_STATE)


def _ret_tables(positions, chunk):
    half = RET_DK // 2
    inv_freq = ROPE_BASE ** (-jnp.arange(half, dtype=F32) / half)
    ang = positions.astype(F32)[:, None] * inv_freq[None, :]
    cos = jnp.cos(ang)
    sin = jnp.sin(ang)
    cos2 = jnp.concatenate([cos, cos], axis=1)
    sin2 = jnp.concatenate([-sin, sin], axis=1)
    log_gamma = jnp.log(1.0 - 2.0 ** (-5.0 - jnp.arange(RET_HEADS, dtype=F32)))
    g = (jnp.arange(chunk, dtype=F32) + 1.0)[None, :] * log_gamma[:, None]
    diff = g[:, :, None] - g[:, None, :]
    causal = jnp.tril(jnp.ones((chunk, chunk), dtype=bool))
    dmat = jnp.where(causal, jnp.exp(jnp.where(causal, diff, 0.0)), 0.0)
    lanes = (RET_HEADS, chunk, RET_DK)
    qdec = jnp.broadcast_to(jnp.exp(g)[:, :, None], lanes)
    kdec = jnp.broadcast_to(jnp.exp(g[:, -1:] - g)[:, :, None], lanes)
    gall = jnp.broadcast_to(jnp.exp(g[:, -1])[:, None, None], (RET_HEADS, 1, RET_DV))
    return cos2, sin2, dmat, qdec, kdec, gall


def _ret_rotate(x, cos2, sin2):
    return x * cos2 + pltpu.roll(x, RET_DK // 2, 1) * sin2


def _group_layernorm_gate(o, gate, w, b):
    mu = jnp.mean(o, axis=-1, keepdims=True)
    xc = o - mu
    var = jnp.mean(xc * xc, axis=-1, keepdims=True)
    return _silu(gate) * (xc * lax.rsqrt(var + EPS) * w + b)


def _dot_nt(a, b):
    return lax.dot_general(a, b, (((1,), (1,)), ((), ())), preferred_element_type=F32)


def _dot_tn(a, b):
    return lax.dot_general(a, b, (((0,), (0,)), ((), ())), preferred_element_type=F32)


def _ret_prompt_kernel(q_ref, k_ref, v_ref, g_ref, cos_ref, sin_ref, dm_ref, qd_ref, kd_ref, ga_ref,
                       lnw_ref, lnb_ref, mix_ref, o_ref, s_ref, *, chunk):
    del mix_ref
    s_ref[...] = jnp.zeros_like(s_ref)

    def body(c, carry):
        r = pl.ds(pl.multiple_of(c * chunk, chunk), chunk)
        cos2, sin2 = cos_ref[r, :], sin_ref[r, :]
        for h in range(RET_HEADS):
            hc = slice(h * RET_DK, (h + 1) * RET_DK)
            q = _ret_rotate(q_ref[r, hc], cos2, sin2)
            k = _ret_rotate(k_ref[r, hc], cos2, sin2) * (RET_DK ** -0.5)
            vb = v_ref[r, hc].astype(BF16)
            s = s_ref[0, h]
            scores = _dot_nt(q.astype(BF16), k.astype(BF16)) * dm_ref[h]
            o = jnp.dot(scores.astype(BF16), vb, preferred_element_type=F32)
            o = o + jnp.dot((q * qd_ref[h]).astype(BF16), s.astype(BF16), preferred_element_type=F32)
            o_ref[r, hc] = _group_layernorm_gate(o, g_ref[r, hc], lnw_ref[h], lnb_ref[h]).astype(o_ref.dtype)
            s_ref[0, h] = s * ga_ref[h] + _dot_tn((k * kd_ref[h]).astype(BF16), vb)
        return carry

    lax.fori_loop(0, q_ref.shape[0] // chunk, body, 0)


MIX_S5, MIX_GDN, MIX_SSD, MIX_RET = range(4)
_ANY_SPEC = pl.BlockSpec(memory_space=pl.ANY)


def _uninit_kernel(o_ref):
    del o_ref


def uninitialized(shape, dtype):
    return pl.pallas_call(_uninit_kernel, out_shape=jax.ShapeDtypeStruct(shape, dtype), out_specs=_ANY_SPEC,
                          name="uninitialized")()


def ret_prompt(y, mixed, n_seq, seq_len, lw):
    chunk = min(RET_CHUNK, seq_len)
    cos2, sin2, dmat, qdec, kdec, gall = _ret_tables(jnp.arange(seq_len, dtype=jnp.int32), chunk)
    width = RET_HEADS * RET_DK
    heads = lambda shape: _const_spec((RET_HEADS,) + shape)
    tok = lambda s: pl.BlockSpec((seq_len, width), lambda b: (b, PK_OFF[s] // width))
    tab = _const_spec((seq_len, RET_DK))
    return pl.pallas_call(
        functools.partial(_ret_prompt_kernel, chunk=chunk),
        grid=(n_seq,),
        in_specs=[tok(10), tok(11), tok(12), tok(13), tab, tab,
                  heads((chunk, chunk)), heads((chunk, RET_DK)), heads((chunk, RET_DK)), heads((1, RET_DV)),
                  heads((1, RET_DV)), heads((1, RET_DV)), _ANY_SPEC],
        out_specs=[pl.BlockSpec((seq_len, width), lambda b: (b, MIX_RET)),
                   pl.BlockSpec((1, RET_HEADS, RET_DK, RET_DV), lambda b: (b, 0, 0, 0))],
        out_shape=[jax.ShapeDtypeStruct(mixed.shape, mixed.dtype),
                   jax.ShapeDtypeStruct((n_seq, RET_HEADS, RET_DK, RET_DV), F32)],
        input_output_aliases={12: 0},
        compiler_params=pltpu.CompilerParams(
            dimension_semantics=("arbitrary",), vmem_limit_bytes=V7X_VMEM_LIMIT),
        name="ret_prompt",
    )(y, y, y, y, cos2, sin2, dmat, qdec, kdec, gall,
      lw['ret_ln_w'].astype(F32).reshape(RET_HEADS, 1, RET_DV), lw['ret_ln_b'].astype(F32).reshape(RET_HEADS, 1, RET_DV),
      mixed)


SAMPLE_BB = 16
LHS_ROWS = 16


def _ret_sample_kernel(q_ref, k_ref, v_ref, g_ref, cos_ref, sin_ref, s0_ref, lnw_ref, lnb_ref, mix_ref, sall_ref,
                       o_ref, s_ref, qd_scr, kd_scr, v_scr, acc_scr, lq, lk, lv, *, decay):
    del mix_ref, sall_ref
    n_t, n_b = q_ref.shape[0], q_ref.shape[1]
    for h in range(RET_HEADS):
        hc = slice(h * RET_DK, (h + 1) * RET_DK)
        qs, ks, vs = [], [], []
        for t in range(n_t):
            cos2, sin2 = cos_ref[t:t + 1, :], sin_ref[t:t + 1, :]
            qs.append(_ret_rotate(q_ref[t, :, hc], cos2, sin2))
            ks.append(_ret_rotate(k_ref[t, :, hc], cos2, sin2) * (RET_DK ** -0.5))
            vs.append(v_ref[t, :, hc])
            qd_scr[h, t] = qs[t] * decay['q'][h][t]
            kd_scr[h, t] = ks[t] * decay['k'][h][t]
            v_scr[h, t] = vs[t]
        for i in range(n_t):
            acc = None
            for j in range(i + 1):
                term = (jnp.sum(qs[i] * ks[j], axis=-1, keepdims=True) * decay['m'][h][i][j]) * vs[j]
                acc = term if acc is None else acc + term
            acc_scr[h, i] = acc

    for tile in (lq, lk, lv):
        tile[...] = jnp.zeros_like(tile)

    def per_seq(b, carry):
        row = pl.ds(b, 1)
        for h in range(RET_HEADS):
            for t in range(n_t):
                lq[h, t:t + 1, :] = qd_scr[h, t, row, :]
                lk[h, t:t + 1, :] = kd_scr[h, t, row, :]
                lv[h, t:t + 1, :] = v_scr[h, t, row, :]
            s0 = s0_ref[b, h]
            inter = jnp.dot(lq[h].astype(BF16), s0.astype(BF16), preferred_element_type=F32)
            s_ref[b, h] = s0 * decay['all'][h] + _dot_tn(lk[h].astype(BF16), lv[h].astype(BF16))
            for t in range(n_t):
                acc_scr[h, t, row, :] = acc_scr[h, t, row, :] + inter[t:t + 1, :]
        return carry

    lax.fori_loop(0, n_b, per_seq, 0)

    for h in range(RET_HEADS):
        hc = slice(h * RET_DV, (h + 1) * RET_DV)
        for t in range(n_t):
            o_ref[t, :, hc] = _group_layernorm_gate(acc_scr[h, t], g_ref[t, :, hc], lnw_ref[h], lnb_ref[h]
                                                    ).astype(o_ref.dtype)


def _sample_block(t0, n_t):
    assert t0 % n_t == 0
    return t0 // n_t


def ret_sample(y3, mixed3, states, new_states, layer, t0, n_t, first_pos, lw):
    n_b = y3.shape[1]
    tb = _sample_block(t0, n_t)
    cos2, sin2, _, _, _, _ = _ret_tables(first_pos + jnp.arange(n_t, dtype=jnp.int32), n_t)
    gamma = 1.0 - 2.0 ** (-5.0 - np.arange(RET_HEADS, dtype=np.float64))
    decay = dict(m=[[[float(g ** (i - j)) for j in range(n_t)] for i in range(n_t)] for g in gamma],
                 q=[[float(g ** (i + 1)) for i in range(n_t)] for g in gamma],
                 k=[[float(g ** (n_t - 1 - j)) for j in range(n_t)] for g in gamma],
                 all=[float(g ** n_t) for g in gamma])
    width = RET_HEADS * RET_DK
    tok = lambda s: pl.BlockSpec((n_t, SAMPLE_BB, width), lambda i: (tb, i, PK_OFF[s] // width))
    state = pl.BlockSpec((None, SAMPLE_BB, RET_HEADS, RET_DK, RET_DV), lambda i: (layer, i, 0, 0, 0))
    per_tok = pltpu.VMEM((RET_HEADS, n_t, SAMPLE_BB, RET_DK), F32)
    tile = pltpu.VMEM((RET_HEADS, LHS_ROWS, RET_DK), F32)
    return pl.pallas_call(
        functools.partial(_ret_sample_kernel, decay=decay),
        grid=(n_b // SAMPLE_BB,),
        in_specs=[tok(10), tok(11), tok(12), tok(13), _const_spec((n_t, RET_DK)), _const_spec((n_t, RET_DK)),
                  state, _const_spec((RET_HEADS, 1, RET_DV)), _const_spec((RET_HEADS, 1, RET_DV)), _ANY_SPEC, _ANY_SPEC],
        out_specs=[pl.BlockSpec((n_t, SAMPLE_BB, width), lambda i: (tb, i, MIX_RET)), state],
        out_shape=[jax.ShapeDtypeStruct(mixed3.shape, mixed3.dtype), jax.ShapeDtypeStruct(states.shape, F32)],
        input_output_aliases={9: 0, 10: 1},
        scratch_shapes=[per_tok, per_tok, per_tok, per_tok, tile, tile, tile],
        compiler_params=pltpu.CompilerParams(
            dimension_semantics=("arbitrary",), vmem_limit_bytes=V7X_VMEM_LIMIT),
        name="ret_sample",
    )(y3, y3, y3, y3, cos2, sin2, states,
      lw['ret_ln_w'].astype(F32).reshape(RET_HEADS, 1, RET_DV), lw['ret_ln_b'].astype(F32).reshape(RET_HEADS, 1, RET_DV),
      mixed3, new_states)


NARROW_B = 0
NARROW_A = GDN_HEADS
NARROW_DT = 2 * GDN_HEADS
TAIL = 8


def _shift_rows(x, tail, s):
    xr = pltpu.roll(x, s, 0)
    tr = pltpu.roll(tail, s, 0)
    row = lax.broadcasted_iota(jnp.int32, tail.shape, 0)
    return jnp.concatenate([jnp.where(row < s, tr, xr[0:TAIL]), xr[TAIL:]], axis=0)


def _causal_conv(x, tail, w_ref):
    y = x * w_ref[CONV_W - 1:CONV_W, :]
    for s in range(1, CONV_W):
        y = y + _shift_rows(x, tail, s) * w_ref[CONV_W - 1 - s:CONV_W - s, :]
    return y


def _softplus(x):
    return jnp.maximum(x, 0.0) + jnp.log1p(jnp.exp(-jnp.abs(x)))


def _lane_row(vals, lane0, width=128):
    return jnp.pad(vals.astype(F32), (lane0, width - lane0 - vals.shape[0])).reshape(1, width)


SSD_PAIRS = SSD_HEADS // 2
SSD_BC = SSD_NGROUPS * SSD_STATE


def _ssd_prompt_kernel(z_ref, xbc_ref, nar_ref, cw_ref, cb_ref, dtb_ref, a_ref, dsk_ref, nw_ref, tri_ref, mix_ref,
                       o_ref, s_ref, tail_ref, *, chunk):
    del mix_ref
    s_ref[...] = jnp.zeros_like(s_ref)
    tail_ref[...] = jnp.zeros_like(tail_ref)
    causal = (lax.broadcasted_iota(jnp.int32, (chunk, chunk), 0)
              >= lax.broadcasted_iota(jnp.int32, (chunk, chunk), 1))
    lane = lax.broadcasted_iota(jnp.int32, (chunk, 2 * SSD_HEADDIM), 1)
    first_head = lane < SSD_HEADDIM
    rep = SSD_HEADS // SSD_NGROUPS

    def body(c, carry):
        r = pl.ds(pl.multiple_of(c * chunk, chunk), chunk)
        raw = xbc_ref[r, :]
        xbc = _silu(_causal_conv(raw, tail_ref[0], cw_ref) + cb_ref[...])
        tail_ref[0] = raw[chunk - TAIL:, :]
        xs = xbc[:, :SSD_INNER]
        dt = _softplus(nar_ref[r, :] + dtb_ref[...])
        g = _select_rows(tri_ref[...], dt * a_ref[...])
        e_in = jnp.exp(g)
        e_out = dt * jnp.exp(g[chunk - 1:chunk, :] - g)
        e_all = jnp.exp(g[chunk - 1:chunk, :])
        g_t = g.T
        dt_t = dt.T
        ys = []
        for p in range(SSD_PAIRS):
            grp = (2 * p) // rep
            bm = xbc[:, SSD_INNER + grp * SSD_STATE:SSD_INNER + (grp + 1) * SSD_STATE]
            cm = xbc[:, SSD_INNER + SSD_BC + grp * SSD_STATE:SSD_INNER + SSD_BC + (grp + 1) * SSD_STATE]
            cb = _dot_nt(cm.astype(BF16), bm.astype(BF16))
            xp = xs[:, p * 128:(p + 1) * 128]
            xpb = xp.astype(BF16)
            sp = s_ref[0, p]
            spb = sp.astype(BF16)
            outs, upds, gls = [], [], []
            for hh in range(2):
                ln = NARROW_DT + 2 * p + hh
                diff = g[:, ln:ln + 1] - g_t[ln:ln + 1, :]
                m = jnp.where(causal, cb * jnp.exp(jnp.where(causal, diff, 0.0)) * dt_t[ln:ln + 1, :], 0.0)
                o = jnp.dot(m.astype(BF16), xpb, preferred_element_type=F32)
                o = o + jnp.dot((cm * e_in[:, ln:ln + 1]).astype(BF16), spb, preferred_element_type=F32)
                outs.append(o)
                upds.append(_dot_tn((bm * e_out[:, ln:ln + 1]).astype(BF16), xpb))
                gls.append(e_all[:, ln:ln + 1])
            s_ref[0, p] = sp * jnp.where(first_head, gls[0], gls[1]) + jnp.where(first_head, upds[0], upds[1])
            ys.append(jnp.where(first_head, outs[0], outs[1]) + xp * dsk_ref[:, p * 128:(p + 1) * 128])
        y = jnp.concatenate(ys, axis=1) * _silu(z_ref[r, :])
        gw = SSD_INNER // SSD_NGROUPS
        yn = [y[:, i * gw:(i + 1) * gw] * lax.rsqrt(
            jnp.mean(y[:, i * gw:(i + 1) * gw] * y[:, i * gw:(i + 1) * gw], axis=-1, keepdims=True) + EPS)
            for i in range(SSD_NGROUPS)]
        o_ref[r, :] = (jnp.concatenate(yn, axis=1) * nw_ref[...]).astype(o_ref.dtype)
        return carry

    lax.fori_loop(0, z_ref.shape[0] // chunk, body, 0)


def _ssd_params(lw, chunk):
    return (lw['ssd_conv_w'].astype(F32), lw['ssd_conv_b'].astype(F32).reshape(1, SSD_CONV_DIM),
            _lane_row(lw['ssd_dt_bias'], NARROW_DT), _lane_row(-jnp.exp(lw['ssd_a_log'].astype(F32)), NARROW_DT),
            jnp.repeat(lw['ssd_d'].astype(F32), SSD_HEADDIM).reshape(1, SSD_INNER),
            lw['ssd_norm_w'].astype(F32).reshape(1, SSD_INNER),
            jnp.tril(jnp.ones((chunk, chunk), BF16)))


def _ssd_state_from_pairs(s):
    b = s.shape[0]
    s = s.reshape(b, SSD_PAIRS, SSD_STATE, 2, SSD_HEADDIM)
    return jnp.swapaxes(s, 2, 3).reshape(b, SSD_HEADS, SSD_STATE, SSD_HEADDIM)


def ssd_prompt(y, nar, mixed, n_seq, seq_len, lw):
    chunk = min(SSD_CHUNK, seq_len)
    params = _ssd_params(lw, chunk)
    tok = lambda s, w: pl.BlockSpec((seq_len, w), lambda b: (b, PK_OFF[s] // w))
    n_in = 3 + len(params)
    mixed, s, tail = pl.pallas_call(
        functools.partial(_ssd_prompt_kernel, chunk=chunk),
        grid=(n_seq,),
        in_specs=[tok(7, SSD_INNER), tok(8, SSD_CONV_DIM), pl.BlockSpec((seq_len, NARROW_W), lambda b: (b, 0))]
                 + [_const_spec(p.shape) for p in params] + [_ANY_SPEC],
        out_specs=[pl.BlockSpec((seq_len, SSD_INNER), lambda b: (b, MIX_SSD)),
                   pl.BlockSpec((1, SSD_PAIRS, SSD_STATE, 2 * SSD_HEADDIM), lambda b: (b, 0, 0, 0)),
                   pl.BlockSpec((1, TAIL, SSD_CONV_DIM), lambda b: (b, 0, 0))],
        out_shape=[jax.ShapeDtypeStruct(mixed.shape, mixed.dtype),
                   jax.ShapeDtypeStruct((n_seq, SSD_PAIRS, SSD_STATE, 2 * SSD_HEADDIM), F32),
                   jax.ShapeDtypeStruct((n_seq, TAIL, SSD_CONV_DIM), F32)],
        input_output_aliases={n_in: 0},
        compiler_params=pltpu.CompilerParams(
            dimension_semantics=("arbitrary",), vmem_limit_bytes=V7X_VMEM_LIMIT),
        name="ssd_prompt",
    )(y, y, nar, *params, mixed)
    return mixed, tail[:, TAIL - (CONV_W - 1):], _ssd_state_from_pairs(s)


GDN_QKV = GDN_HEADS * GDN_DK
GDN_ROWS = 1024
GDN_UNROLL = 4
HIGHEST = lax.Precision.HIGHEST


def _split_bf16(x):
    hi = x.astype(BF16)
    return hi, (x - hi.astype(F32)).astype(BF16)


def _split3_bf16(x):
    hi = x.astype(BF16)
    r = x - hi.astype(F32)
    mid = r.astype(BF16)
    return hi, mid, (r - mid.astype(F32)).astype(BF16)


def _select_rows(m01, x):
    n = x.shape[1]
    p = jnp.dot(m01, jnp.concatenate(_split3_bf16(x), axis=1), preferred_element_type=F32)
    return p[:, :n] + p[:, n:2 * n] + p[:, 2 * n:]


def _select_cols(x, m01):
    m = x.shape[0]
    p = jnp.dot(jnp.concatenate(_split3_bf16(x), axis=0), m01, preferred_element_type=F32)
    return p[:m] + p[m:2 * m] + p[2 * m:]


def _dot3(a, b):
    a_hi, a_lo = _split_bf16(a)
    b_hi, b_lo = _split_bf16(b)
    m = a.shape[0]
    p = jnp.dot(jnp.concatenate([a_hi, a_lo], axis=0), b_hi, preferred_element_type=F32)
    return p[:m] + p[m:] + jnp.dot(a_hi, b_lo, preferred_element_type=F32)


def _two_block_diag(x0, x1):
    z0 = jnp.zeros_like(x0)
    z1 = jnp.zeros_like(x1)
    return jnp.concatenate([jnp.concatenate([x0, z1], axis=1), jnp.concatenate([z0, x1], axis=1)], axis=0)


def _l2_rows(x):
    return x * lax.rsqrt(jnp.sum(x * x, axis=-1, keepdims=True) + EPS)


def _gdn_prompt_kernel(q_ref, k_ref, v_ref, z_ref, nar_ref, cw_ref, dtb_ref, a_ref, nw_ref, tri_ref, sel_ref, mix_ref,
                       o_ref, s_ref, tail_ref, u_scr, w_scr, qd_scr, kd_scr, sc_scr, ea_scr, *, chunk, unroll):
    del mix_ref

    @pl.when(pl.program_id(1) == 0)
    def _():
        s_ref[...] = jnp.zeros_like(s_ref)
        tail_ref[...] = jnp.zeros_like(tail_ref)

    n_chunks = q_ref.shape[0] // chunk
    cat = GDN_HEADS * chunk
    lane = lax.broadcasted_iota(jnp.int32, (chunk, cat), 1)
    row = lax.broadcasted_iota(jnp.int32, (chunk, cat), 0)
    col = jnp.bitwise_and(lane, chunk - 1)
    causal = row >= col
    strict = row > col
    eye_cat = jnp.where(row == col, 1.0, 0.0).astype(F32)
    head_mask = [(lane >= h * chunk) & (lane < (h + 1) * chunk) for h in range(GDN_HEADS)]
    nar_lane = lax.broadcasted_iota(jnp.int32, (chunk, 128), 1)
    ones_cc = jnp.ones((chunk, chunk), BF16)
    n_sq = chunk.bit_length() - 2

    def mm_cat(l_cat, r_cat):
        bd = jnp.concatenate([jnp.where(m, r_cat, 0.0) for m in head_mask], axis=0)
        return _dot3(l_cat, bd)

    def neumann_step(t_cat, pw, update_t, update_pw):
        return (t_cat + mm_cat(t_cat, pw) if update_t else t_cat), (mm_cat(pw, pw) if update_pw else pw)

    col1 = lambda x, h: x[:, h * chunk:h * chunk + 1]

    def front(c):
        r = pl.ds(pl.multiple_of(c * chunk, chunk), chunk)
        before = pl.ds(pl.multiple_of(jnp.maximum(c * chunk - TAIL, 0), TAIL), TAIL)
        qkv = []
        for i, ref in enumerate((q_ref, k_ref, v_ref)):
            cols = slice(i * GDN_QKV, (i + 1) * GDN_QKV)
            tail = jnp.where(c == 0, tail_ref[0, :, cols], ref[before, :])
            qkv.append(_silu(_causal_conv(ref[r, :], tail, cw_ref.at[:, cols])))
        q, k, v = qkv
        nar = nar_ref[r, :]
        beta = jax.nn.sigmoid(nar)
        g = _select_rows(tri_ref[...], a_ref[...] * _softplus(nar + dtb_ref[...]))
        bg = _select_cols(jnp.where(nar_lane < NARROW_A, beta, g), sel_ref[...])
        b_c, g_c = bg[:, :cat], bg[:, cat:]
        g_r = _select_rows(ones_cc, g_c * eye_cat)
        decay = jnp.where(causal, jnp.exp(jnp.where(causal, g_c - g_r, 0.0)), 0.0)
        e_in = jnp.exp(g_c)
        e_out = jnp.exp(g_c[chunk - 1:chunk, :] - g_c)
        e_all = jnp.exp(g_c[chunk - 1:chunk, :])

        qn = [_l2_rows(q[:, h * GDN_DK:(h + 1) * GDN_DK]) * (GDN_DK ** -0.5) for h in range(GDN_HEADS)]
        kn = [_l2_rows(k[:, h * GDN_DK:(h + 1) * GDN_DK]) for h in range(GDN_HEADS)]
        vh = [v[:, h * GDN_DV:(h + 1) * GDN_DV] for h in range(GDN_HEADS)]
        kk, qk = [], []
        for p in range(GDN_HEADS // 2):
            h0, h1 = 2 * p, 2 * p + 1
            rhs = _two_block_diag(kn[h0], kn[h1]).astype(BF16)
            kk.append(_dot_nt(jnp.concatenate([kn[h0], kn[h1]], axis=1).astype(BF16), rhs))
            qk.append(_dot_nt(jnp.concatenate([qn[h0], qn[h1]], axis=1).astype(BF16), rhs))
        kk = jnp.concatenate(kk, axis=1)
        sc_scr[r, :] = (jnp.concatenate(qk, axis=1) * decay).astype(BF16)
        ea_scr[c] = jnp.broadcast_to(e_all, (TAIL, cat))
        for h in range(GDN_HEADS):
            hc = slice(h * GDN_DV, (h + 1) * GDN_DV)
            qd_scr[r, hc] = (qn[h] * col1(e_in, h)).astype(BF16)
            kd_scr[r, hc] = (kn[h] * col1(e_out, h)).astype(BF16)
        a_cat = jnp.where(strict, b_c * kk * decay, 0.0)
        rhs = [_two_block_diag(*[jnp.concatenate(
            [vh[h] * col1(b_c, h), kn[h] * (col1(b_c, h) * col1(e_in, h))], axis=1) for h in (2 * p, 2 * p + 1)])
            for p in range(GDN_HEADS // 2)]
        return r, a_cat, rhs

    def back(r, t_cat, rhs):
        for p in range(GDN_HEADS // 2):
            uw = _dot3(t_cat[:, p * 2 * chunk:(p + 1) * 2 * chunk], rhs[p])
            for i, h in enumerate((2 * p, 2 * p + 1)):
                hc = slice(h * GDN_DV, (h + 1) * GDN_DV)
                u_scr[r, hc] = uw[:, (2 * i) * GDN_DV:(2 * i + 1) * GDN_DV]
                w_scr[r, hc] = uw[:, (2 * i + 1) * GDN_DV:(2 * i + 2) * GDN_DV].astype(BF16)

    def prepare_group(i, carry):
        rs, ts, rhss = [], [], []
        for j in range(unroll):
            r, a_cat, rhs = front(i * unroll + j)
            rs.append(r)
            ts.append((eye_cat - a_cat, a_cat))
            rhss.append(rhs)
        for step in range(n_sq + 1):
            ts = [neumann_step(t, pw, update_t=step > 0, update_pw=step < n_sq) for t, pw in ts]
        for r, (t_cat, _), rhs in zip(rs, ts, rhss):
            back(r, t_cat, rhs)
        return carry

    lax.fori_loop(0, n_chunks // unroll, prepare_group, 0)

    def recur(c, carry):
        r = pl.ds(pl.multiple_of(c * chunk, chunk), chunk)
        e_all = ea_scr[c][0:1, :]
        for p in range(GDN_HEADS // 2):
            heads = (2 * p, 2 * p + 1)
            v_new, q_s = [], []
            for h in heads:
                hc = slice(h * GDN_DV, (h + 1) * GDN_DV)
                wq = jnp.concatenate([w_scr[r, hc], qd_scr[r, hc]], axis=0)
                ws = jnp.dot(wq, s_ref[0, h].astype(BF16), preferred_element_type=F32)
                v_new.append(u_scr[r, hc] - ws[:chunk])
                q_s.append(ws[chunk:])
            intra = jnp.dot(sc_scr[r, p * 2 * chunk:(p + 1) * 2 * chunk],
                            _two_block_diag(*v_new).astype(BF16), preferred_element_type=F32)
            for i, h in enumerate(heads):
                hc = slice(h * GDN_DV, (h + 1) * GDN_DV)
                o = q_s[i] + intra[:, i * GDN_DV:(i + 1) * GDN_DV]
                s_ref[0, h] = s_ref[0, h] * col1(e_all, h) + _dot_tn(kd_scr[r, hc], v_new[i].astype(BF16))
                o = o * lax.rsqrt(jnp.mean(o * o, axis=-1, keepdims=True) + EPS) * nw_ref[...]
                o_ref[r, hc] = (o * _silu(z_ref[r, hc])).astype(o_ref.dtype)
        return carry

    lax.fori_loop(0, n_chunks, recur, 0)
    last = pl.ds(q_ref.shape[0] - TAIL, TAIL)
    for i, ref in enumerate((q_ref, k_ref, v_ref)):
        tail_ref[0, :, i * GDN_QKV:(i + 1) * GDN_QKV] = ref[last, :]


def _gdn_params(lw, chunk):
    cat = GDN_HEADS * chunk
    lanes = np.arange(cat) // chunk
    sel = np.zeros((128, 2 * cat), np.float32)
    sel[NARROW_B + lanes, np.arange(cat)] = 1.0
    sel[NARROW_A + lanes, cat + np.arange(cat)] = 1.0
    return (lw['gdn_conv_w'].astype(F32), _lane_row(lw['gdn_dt_bias'], NARROW_A),
            _lane_row(-jnp.exp(lw['gdn_a_log'].astype(F32)), NARROW_A),
            lw['gdn_norm_w'].astype(F32).reshape(1, GDN_DV), jnp.tril(jnp.ones((chunk, chunk), BF16)),
            jnp.asarray(sel, BF16))


def gdn_prompt(y, nar, mixed, n_seq, seq_len, lw):
    chunk = min(GDN_CHUNK, seq_len)
    rows = min(GDN_ROWS, seq_len)
    n_chunks = rows // chunk
    unroll = math.gcd(GDN_UNROLL, n_chunks)
    params = _gdn_params(lw, chunk)
    nblk = seq_len // rows
    tok = lambda s: pl.BlockSpec((rows, GDN_QKV), lambda b, j: (b * nblk + j, PK_OFF[s] // GDN_QKV))
    const = lambda shape: pl.BlockSpec(shape, lambda b, j: (0,) * len(shape))
    n_in = 5 + len(params)
    mixed, s, tail = pl.pallas_call(
        functools.partial(_gdn_prompt_kernel, chunk=chunk, unroll=unroll),
        grid=(n_seq, nblk),
        in_specs=[tok(1), tok(2), tok(3), tok(4), pl.BlockSpec((rows, NARROW_W), lambda b, j: (b * nblk + j, 0))]
                 + [const(p.shape) for p in params] + [_ANY_SPEC],
        out_specs=[pl.BlockSpec((rows, GDN_QKV), lambda b, j: (b * nblk + j, MIX_GDN)),
                   pl.BlockSpec((1, GDN_HEADS, GDN_DK, GDN_DV), lambda b, j: (b, 0, 0, 0)),
                   pl.BlockSpec((1, TAIL, GDN_CONV_DIM), lambda b, j: (b, 0, 0))],
        out_shape=[jax.ShapeDtypeStruct(mixed.shape, mixed.dtype),
                   jax.ShapeDtypeStruct((n_seq, GDN_HEADS, GDN_DK, GDN_DV), F32),
                   jax.ShapeDtypeStruct((n_seq, TAIL, GDN_CONV_DIM), F32)],
        input_output_aliases={n_in: 0},
        scratch_shapes=[pltpu.VMEM((rows, GDN_QKV), F32), pltpu.VMEM((rows, GDN_QKV), BF16),
                        pltpu.VMEM((rows, GDN_QKV), BF16), pltpu.VMEM((rows, GDN_QKV), BF16),
                        pltpu.VMEM((rows, GDN_HEADS * chunk), BF16),
                        pltpu.VMEM((n_chunks, TAIL, GDN_HEADS * chunk), F32)],
        compiler_params=pltpu.CompilerParams(
            dimension_semantics=("arbitrary", "arbitrary"), vmem_limit_bytes=V7X_VMEM_LIMIT),
        name="gdn_prompt",
    )(y, y, y, y, nar, *params, mixed)
    return mixed, tail[:, TAIL - (CONV_W - 1):], s


def _conv_steps(buf_ref, raw_ref, w_ref, cols):
    n_t = raw_ref.shape[0]
    xx = [buf_ref[j, :, cols] for j in range(CONV_W - 1)] + [raw_ref[t] for t in range(n_t)]
    w = w_ref[:, cols]
    out = []
    for t in range(n_t):
        y = xx[t] * w[0:1, :]
        for j in range(1, CONV_W):
            y = y + xx[t + j] * w[j:j + 1, :]
        out.append(y)
    return out, xx[n_t:]


def _ssd_sample_kernel(z_ref, xbc_ref, nar_ref, buf_ref, s0_ref, cw_ref, cb_ref, dtb_ref, a_ref, dsk_ref, nw_ref,
                       mix_ref, sall_ref, o_ref, s_ref, nbuf_ref, qd_scr, kd_scr, xs_scr, ga_scr, acc_scr, lq, lk, lx):
    del mix_ref, sall_ref
    n_t, n_b = z_ref.shape[0], z_ref.shape[1]
    rep = SSD_HEADS // SSD_NGROUPS
    conv, tail = _conv_steps(buf_ref, xbc_ref, cw_ref, slice(0, SSD_CONV_DIM))
    for j in range(CONV_W - 1):
        nbuf_ref[j] = tail[j]
    xbc = [_silu(c + cb_ref[...]) for c in conv]
    dt, g = [], []
    for t in range(n_t):
        dt.append(_softplus(nar_ref[t] + dtb_ref[...]))
        la = dt[t] * a_ref[...]
        g.append(la if t == 0 else g[t - 1] + la)
    for grp in range(SSD_NGROUPS):
        bm = [x[:, SSD_INNER + grp * SSD_STATE:SSD_INNER + (grp + 1) * SSD_STATE] for x in xbc]
        cm = [x[:, SSD_INNER + SSD_BC + grp * SSD_STATE:SSD_INNER + SSD_BC + (grp + 1) * SSD_STATE] for x in xbc]
        sc = [[jnp.sum(cm[i] * bm[j], axis=-1, keepdims=True) for j in range(i + 1)] for i in range(n_t)]
        for h in range(grp * rep, (grp + 1) * rep):
            ln = NARROW_DT + h
            hc = slice(h * SSD_HEADDIM, (h + 1) * SSD_HEADDIM)
            gc = [x[:, ln:ln + 1] for x in g]
            dc = [x[:, ln:ln + 1] for x in dt]
            xs = [x[:, hc] for x in xbc]
            for t in range(n_t):
                qd_scr[h, t] = cm[t] * jnp.exp(gc[t])
                kd_scr[h, t] = bm[t] * (dc[t] * jnp.exp(gc[n_t - 1] - gc[t]))
                xs_scr[h, t] = xs[t]
                acc = None
                for j in range(t + 1):
                    term = (sc[t][j] * dc[j] * jnp.exp(gc[t] - gc[j])) * xs[j]
                    acc = term if acc is None else acc + term
                acc_scr[t, :, hc] = acc
            ga_scr[h] = jnp.broadcast_to(jnp.exp(gc[n_t - 1]), (n_b, SSD_STATE))

    for tile in (lq, lk, lx):
        tile[...] = jnp.zeros_like(tile)

    def per_seq(b, carry):
        row = pl.ds(b, 1)
        for h in range(SSD_HEADS):
            hc = slice(h * SSD_HEADDIM, (h + 1) * SSD_HEADDIM)
            for t in range(n_t):
                lq[h, t:t + 1, :] = qd_scr[h, t, row, :]
                lk[h, t:t + 1, :] = kd_scr[h, t, row, :]
                lx[h, t:t + 1, :] = xs_scr[h, t, row, :]
            s0 = s0_ref[b, h]
            inter = _dot_nt(lq[h].astype(BF16), s0.astype(BF16))
            s_ref[b, h] = s0 * ga_scr[h, row, :] + _dot_tn(lx[h].astype(BF16), lk[h].astype(BF16))
            for t in range(n_t):
                xs_scr[h, t, row, :] = inter[t:t + 1, :]
        return carry

    lax.fori_loop(0, n_b, per_seq, 0)

    gw = SSD_INNER // SSD_NGROUPS
    for t in range(n_t):
        for h in range(SSD_HEADS):
            hc = slice(h * SSD_HEADDIM, (h + 1) * SSD_HEADDIM)
            acc_scr[t, :, hc] = acc_scr[t, :, hc] + xs_scr[h, t]
        y = (acc_scr[t] + xbc[t][:, :SSD_INNER] * dsk_ref[...]) * _silu(z_ref[t])
        yn = [y[:, i * gw:(i + 1) * gw] * lax.rsqrt(
            jnp.mean(y[:, i * gw:(i + 1) * gw] * y[:, i * gw:(i + 1) * gw], axis=-1, keepdims=True) + EPS)
            for i in range(SSD_NGROUPS)]
        o_ref[t] = (jnp.concatenate(yn, axis=1) * nw_ref[...]).astype(o_ref.dtype)


def ssd_sample(y3, nar3, mixed3, buf0, states, new_states, layer, t0, n_t, lw):
    n_b = y3.shape[1]
    tb = _sample_block(t0, n_t)
    params = _ssd_params(lw, 1)[:-1]
    tok = lambda s, w: pl.BlockSpec((n_t, SAMPLE_BB, w), lambda i: (tb, i, PK_OFF[s] // w))
    bufspec = pl.BlockSpec((CONV_W - 1, SAMPLE_BB, SSD_CONV_DIM), lambda i: (0, i, 0))
    state = pl.BlockSpec((None, SAMPLE_BB, SSD_HEADS, SSD_HEADDIM, SSD_STATE), lambda i: (layer, i, 0, 0, 0))
    per_tok = lambda w: pltpu.VMEM((SSD_HEADS, n_t, SAMPLE_BB, w), F32)
    tile = lambda w: pltpu.VMEM((SSD_HEADS, LHS_ROWS, w), F32)
    n_in = 5 + len(params)
    mixed3, s, nbuf = pl.pallas_call(
        _ssd_sample_kernel,
        grid=(n_b // SAMPLE_BB,),
        in_specs=[tok(7, SSD_INNER), tok(8, SSD_CONV_DIM),
                  pl.BlockSpec((n_t, SAMPLE_BB, NARROW_W), lambda i: (tb, i, 0)), bufspec, state]
                 + [_const_spec(p.shape) for p in params] + [_ANY_SPEC, _ANY_SPEC],
        out_specs=[pl.BlockSpec((n_t, SAMPLE_BB, SSD_INNER), lambda i: (tb, i, MIX_SSD)), state, bufspec],
        out_shape=[jax.ShapeDtypeStruct(mixed3.shape, mixed3.dtype), jax.ShapeDtypeStruct(states.shape, F32),
                   jax.ShapeDtypeStruct((CONV_W - 1, n_b, SSD_CONV_DIM), F32)],
        input_output_aliases={n_in: 0, n_in + 1: 1},
        scratch_shapes=[per_tok(SSD_STATE), per_tok(SSD_STATE), per_tok(SSD_HEADDIM),
                        pltpu.VMEM((SSD_HEADS, SAMPLE_BB, SSD_STATE), F32),
                        pltpu.VMEM((n_t, SAMPLE_BB, SSD_INNER), F32),
                        tile(SSD_STATE), tile(SSD_STATE), tile(SSD_HEADDIM)],
        compiler_params=pltpu.CompilerParams(
            dimension_semantics=("arbitrary",), vmem_limit_bytes=V7X_VMEM_LIMIT),
        name="ssd_sample",
    )(y3, y3, nar3, jnp.swapaxes(buf0, 0, 1), states, *params, mixed3, new_states)
    return mixed3, jnp.swapaxes(nbuf, 0, 1), s


def _gdn_sample_kernel(q_ref, k_ref, v_ref, z_ref, nar_ref, buf_ref, s0_ref, cw_ref, dtb_ref, a_ref, nw_ref,
                       mix_ref, sall_ref, o_ref, s_ref, nbuf_ref, w_scr, qd_scr, kd_scr, u_scr, ga_scr, lwq, lk, lu):
    del mix_ref, sall_ref
    n_t, n_b = q_ref.shape[0], q_ref.shape[1]
    qkv = []
    for i, ref in enumerate((q_ref, k_ref, v_ref)):
        cols = slice(i * GDN_QKV, (i + 1) * GDN_QKV)
        conv, tail = _conv_steps(buf_ref, ref, cw_ref, cols)
        for j in range(CONV_W - 1):
            nbuf_ref[j, :, cols] = tail[j]
        qkv.append([_silu(c) for c in conv])
    beta, g = [], []
    for t in range(n_t):
        nar = nar_ref[t]
        beta.append(jax.nn.sigmoid(nar))
        gl = a_ref[...] * _softplus(nar + dtb_ref[...])
        g.append(gl if t == 0 else g[t - 1] + gl)

    scores = []
    for h in range(GDN_HEADS):
        hc = slice(h * GDN_DK, (h + 1) * GDN_DK)
        qn = [_l2_rows(x[:, hc]) * (GDN_DK ** -0.5) for x in qkv[0]]
        kn = [_l2_rows(x[:, hc]) for x in qkv[1]]
        vh = [x[:, hc] for x in qkv[2]]
        bc = [x[:, NARROW_B + h:NARROW_B + h + 1] for x in beta]
        gc = [x[:, NARROW_A + h:NARROW_A + h + 1] for x in g]
        us, ws = [], []
        for i in range(n_t):
            u = vh[i] * bc[i]
            w = kn[i] * (bc[i] * jnp.exp(gc[i]))
            for j in range(i):
                a_ij = bc[i] * jnp.sum(kn[i] * kn[j], axis=-1, keepdims=True) * jnp.exp(gc[i] - gc[j])
                u = u - a_ij * us[j]
                w = w - a_ij * ws[j]
            us.append(u)
            ws.append(w)
            u_scr[h, i] = u
            w_scr[h, i] = w
            qd_scr[h, i] = qn[i] * jnp.exp(gc[i])
            kd_scr[h, i] = kn[i] * jnp.exp(gc[n_t - 1] - gc[i])
        ga_scr[h] = jnp.broadcast_to(jnp.exp(gc[n_t - 1]), (n_b, GDN_DV))
        scores.append([[jnp.sum(qn[i] * kn[j], axis=-1, keepdims=True) * jnp.exp(gc[i] - gc[j])
                        for j in range(i + 1)] for i in range(n_t)])

    for tile in (lwq, lk, lu):
        tile[...] = jnp.zeros_like(tile)

    def per_pair(i, carry):
        seqs = [(s, 2 * i + s, pl.ds(2 * i + s, 1)) for s in range(2)]
        for s, b, row in seqs:
            for h in range(GDN_HEADS):
                n = s * GDN_HEADS + h
                for t in range(n_t):
                    lwq[n, t:t + 1, :] = w_scr[h, t, row, :]
                    lwq[n, LHS_ROWS + t:LHS_ROWS + t + 1, :] = qd_scr[h, t, row, :]
                    lk[n, t:t + 1, :] = kd_scr[h, t, row, :]
                    lu[n, t:t + 1, :] = u_scr[h, t, row, :]
        results = []
        for s, b, row in seqs:
            for h in range(GDN_HEADS):
                n = s * GDN_HEADS + h
                s0 = s0_ref[b, h]
                wq_s = jnp.dot(lwq[n].astype(BF16), s0.astype(BF16), preferred_element_type=F32)
                v_new = lu[n] - wq_s[:LHS_ROWS]
                s_ref[b, h] = s0 * ga_scr[h, row, :] + _dot_tn(lk[n].astype(BF16), v_new.astype(BF16))
                results.append((h, row, v_new, wq_s))
        for h, row, v_new, wq_s in results:
            for t in range(n_t):
                u_scr[h, t, row, :] = v_new[t:t + 1, :]
                qd_scr[h, t, row, :] = wq_s[LHS_ROWS + t:LHS_ROWS + t + 1, :]
        return carry

    lax.fori_loop(0, n_b // 2, per_pair, 0)

    for h in range(GDN_HEADS):
        hc = slice(h * GDN_DV, (h + 1) * GDN_DV)
        for i in range(n_t):
            o = qd_scr[h, i]
            for j in range(i + 1):
                o = o + scores[h][i][j] * u_scr[h, j]
            o = o * lax.rsqrt(jnp.mean(o * o, axis=-1, keepdims=True) + EPS) * nw_ref[...]
            o_ref[i, :, hc] = (o * _silu(z_ref[i, :, hc])).astype(o_ref.dtype)


def gdn_sample(y3, nar3, mixed3, buf0, states, new_states, layer, t0, n_t, lw):
    n_b = y3.shape[1]
    tb = _sample_block(t0, n_t)
    params = _gdn_params(lw, 1)[:4]
    tok = lambda s: pl.BlockSpec((n_t, SAMPLE_BB, GDN_QKV), lambda i: (tb, i, PK_OFF[s] // GDN_QKV))
    bufspec = pl.BlockSpec((CONV_W - 1, SAMPLE_BB, GDN_CONV_DIM), lambda i: (0, i, 0))
    state = pl.BlockSpec((None, SAMPLE_BB, GDN_HEADS, GDN_DK, GDN_DV), lambda i: (layer, i, 0, 0, 0))
    per_tok = pltpu.VMEM((GDN_HEADS, n_t, SAMPLE_BB, GDN_DK), F32)
    tile = lambda rows: pltpu.VMEM((2 * GDN_HEADS, rows, GDN_DK), F32)
    n_in = 7 + len(params)
    mixed3, s, nbuf = pl.pallas_call(
        _gdn_sample_kernel,
        grid=(n_b // SAMPLE_BB,),
        in_specs=[tok(1), tok(2), tok(3), tok(4),
                  pl.BlockSpec((n_t, SAMPLE_BB, NARROW_W), lambda i: (tb, i, 0)), bufspec, state]
                 + [_const_spec(p.shape) for p in params] + [_ANY_SPEC, _ANY_SPEC],
        out_specs=[pl.BlockSpec((n_t, SAMPLE_BB, GDN_QKV), lambda i: (tb, i, MIX_GDN)), state, bufspec],
        out_shape=[jax.ShapeDtypeStruct(mixed3.shape, mixed3.dtype), jax.ShapeDtypeStruct(states.shape, F32),
                   jax.ShapeDtypeStruct((CONV_W - 1, n_b, GDN_CONV_DIM), F32)],
        input_output_aliases={n_in: 0, n_in + 1: 1},
        scratch_shapes=[per_tok, per_tok, per_tok, per_tok, pltpu.VMEM((GDN_HEADS, SAMPLE_BB, GDN_DV), F32),
                        tile(2 * LHS_ROWS), tile(LHS_ROWS), tile(LHS_ROWS)],
        compiler_params=pltpu.CompilerParams(
            dimension_semantics=("arbitrary",), vmem_limit_bytes=V7X_VMEM_LIMIT),
        name="gdn_sample",
    )(y3, y3, y3, y3, nar3, jnp.swapaxes(buf0, 0, 1), states, *params, mixed3, new_states)
    return mixed3, jnp.swapaxes(nbuf, 0, 1), s


TM = 1088
TM_SPLIT = 512


def kernel(x_prompt, x_sample, p_prompt, p_sample, state_s5_re, state_s5_im, state_gdn, state_gdn_conv, state_ssd, state_ssd_conv, state_ret, norm_mix, w_in, s5_a_re, s5_a_im, s5_b_re, s5_b_im, s5_c_re, s5_c_im, s5_d, s5_log_step, s5_w_glu, s5_b_glu, gdn_conv_w, gdn_a_log, gdn_dt_bias, gdn_norm_w, ssd_conv_w, ssd_conv_b, ssd_dt_bias, ssd_a_log, ssd_d, ssd_norm_w, ret_ln_w, ret_ln_b, w_out, norm_ffn, w_ffn_in, w_ffn_out, norm_ple, w_ple_gate, w_ple_proj, norm_final):
    bp, lp, d = x_prompt.shape
    bs, ls, _ = x_sample.shape
    np_tok = bp * lp
    ns_tok = bs * ls
    n_tok = np_tok + ns_tok

    mixer_w = dict(
        s5_a_re=s5_a_re, s5_a_im=s5_a_im, s5_b_re=s5_b_re, s5_b_im=s5_b_im, s5_c_re=s5_c_re, s5_c_im=s5_c_im,
        s5_d=s5_d, s5_log_step=s5_log_step, s5_w_glu=s5_w_glu, s5_b_glu=s5_b_glu,
        gdn_conv_w=gdn_conv_w, gdn_a_log=gdn_a_log, gdn_dt_bias=gdn_dt_bias, gdn_norm_w=gdn_norm_w,
        ssd_conv_w=ssd_conv_w, ssd_conv_b=ssd_conv_b, ssd_dt_bias=ssd_dt_bias, ssd_a_log=ssd_a_log,
        ssd_d=ssd_d, ssd_norm_w=ssd_norm_w, ret_ln_w=ret_ln_w, ret_ln_b=ret_ln_b)

    assert bp == 4, "the prompt S5 kernel packs two time steps of four sequences per vreg"

    def tm_rows(t):
        t = jnp.swapaxes(t, -3, -2)
        return t.reshape(t.shape[:-3] + (t.shape[-3] * t.shape[-2], t.shape[-1]))

    def bm_seqs(t, b):
        return jnp.swapaxes(t.reshape(t.shape[0] // b, b, t.shape[1]), 0, 1)

    w_out_b, w_ffn_in_b, w_ffn_out_b = (w.astype(BF16) for w in (w_out, w_ffn_in, w_ffn_out))
    w_gate_b, w_proj_b = w_ple_gate.astype(BF16), w_ple_proj.astype(BF16)
    pp = p_prompt.reshape(DEPTH, np_tok, PLE_DIM)
    ps = tm_rows(p_sample)

    h = jnp.concatenate([x_prompt.reshape(np_tok, d), tm_rows(x_sample)], axis=0)
    ssd_t = jnp.swapaxes(state_ssd, -1, -2)
    new_gdn, new_ssd_t, new_ret = (uninitialized(s.shape, F32) for s in (state_gdn, ssd_t, state_ret))
    new_p, new_s = [], []
    for i in range(DEPTH):
        lw = {k: v[i] for k, v in mixer_w.items()}
        y, nar = in_projection(h, norm_mix[i], w_in[i], tm=TM)
        y3 = y.reshape(n_tok // bs, bs, PK_TOTAL)
        nar3 = nar.reshape(n_tok // bs, bs, NARROW_W)
        t0 = np_tok // bs

        tb = _s5_tables(lw)
        a_p, st5_p = s5_prompt(tm_rows(y[:np_tok, :S5_CH].reshape(bp, lp, S5_CH)), tb, rows=512)

        mixed = uninitialized((n_tok, d), BF16)
        mixed, gdn_buf_p, gdn_s_p = gdn_prompt(y, nar, mixed, bp, lp, lw)
        mixed, ssd_buf_p, ssd_s_p = ssd_prompt(y, nar, mixed, bp, lp, lw)
        mixed, ret_s_p = ret_prompt(y, mixed, bp, lp, lw)
        mixed, st5_s = s5_sample(y, mixed, _s5_state_to_lanes(state_s5_re[i], state_s5_im[i]), np_tok, ns_tok, tb)
        mixed3 = mixed.reshape(n_tok // bs, bs, d)
        mixed3, gdn_buf_s, new_gdn = gdn_sample(y3, nar3, mixed3, state_gdn_conv[i], state_gdn, new_gdn, i, t0, ls, lw)
        mixed3, ssd_buf_s, new_ssd_t = ssd_sample(y3, nar3, mixed3, state_ssd_conv[i], ssd_t, new_ssd_t, i, t0, ls, lw)
        mixed3, new_ret = ret_sample(y3, mixed3, state_ret, new_ret, i, t0, ls, PAST_LEN, lw)
        mixed = lax.dynamic_update_slice(mixed3.reshape(n_tok, d), bm_seqs(a_p, bp).reshape(np_tok, S5_CH), (0, 0))
        new_p.append(_s5_lanes_to_state(st5_p[:bp]) + (gdn_s_p, gdn_buf_p, ssd_s_p, ssd_buf_p, ret_s_p))
        new_s.append(_s5_lanes_to_state(st5_s) + (gdn_buf_s, ssd_buf_s))

        h = matmul_residual(mixed, w_out_b, i, h, tm=TM)
        h = ffn_residual(h, norm_ffn[i], w_ffn_in_b, w_ffn_out_b, i, tm=TM // 2, th=512)
        if i + 1 < DEPTH:
            h = ple_residual(h, norm_ple[i], w_gate_b, pp, ps, w_proj_b, i, tm=TM_SPLIT)
        else:
            y_p, y_s = ple_residual(h, norm_ple[i], w_gate_b, pp, ps, w_proj_b, i, tm=TM_SPLIT, nf=norm_final)

    stack_p = [jnp.stack([st[j] for st in new_p]) for j in range(7)]
    s5_re_s, s5_im_s, gdn_buf_s, ssd_buf_s = (jnp.stack([st[j] for st in new_s]) for j in range(4))
    return (y_p.reshape(bp, lp, d), bm_seqs(y_s, bs), *stack_p,
            s5_re_s, s5_im_s, new_gdn, gdn_buf_s, jnp.swapaxes(new_ssd_t, -1, -2), ssd_buf_s, new_ret)
```

```python
import functools
import math

import jax
import jax.numpy as jnp
import numpy as np
from jax import lax
from jax.experimental import pallas as pl
from jax.experimental.pallas import tpu as pltpu

F32 = jnp.float32
BF16 = jnp.bfloat16

D_MODEL = 2048
DEPTH = 2
GROUP_WIDTH = D_MODEL // 4
CONV_W = 4
EPS = 1e-6
PLE_DIM = 256
FFN_HIDDEN = ((8 * D_MODEL + 3 * 256 - 1) // (3 * 256)) * 256

S5_CH = GROUP_WIDTH
S5_GROUP_CH = 16
S5_GROUPS = S5_CH // S5_GROUP_CH
S5_STATE = 64

GDN_HEADS = 4
GDN_DK = GROUP_WIDTH // GDN_HEADS
GDN_DV = GROUP_WIDTH // GDN_HEADS
GDN_CHUNK = 64
GDN_CONV_DIM = 2 * GDN_HEADS * GDN_DK + GDN_HEADS * GDN_DV

SSD_INNER = GROUP_WIDTH
SSD_HEADDIM = 64
SSD_HEADS = SSD_INNER // SSD_HEADDIM
SSD_NGROUPS = 2
SSD_STATE = 128
SSD_CHUNK = 128
SSD_CONV_DIM = SSD_INNER + 2 * SSD_NGROUPS * SSD_STATE

RET_HEADS = 4
RET_DK = GROUP_WIDTH // RET_HEADS
RET_DV = GROUP_WIDTH // RET_HEADS
RET_CHUNK = 128
ROPE_BASE = 10000.0
PAST_LEN = 16384

IN_SIZES = (
    S5_CH,
    GDN_HEADS * GDN_DK, GDN_HEADS * GDN_DK, GDN_HEADS * GDN_DV, GDN_HEADS * GDN_DV, GDN_HEADS, GDN_HEADS,
    SSD_INNER, SSD_CONV_DIM, SSD_HEADS,
    RET_HEADS * RET_DK, RET_HEADS * RET_DK, RET_HEADS * RET_DV, RET_HEADS * RET_DV,
)
IN_OFFS = tuple(int(v) for v in np.cumsum((0,) + IN_SIZES))

_REGIONS = ((0, 1, 2, 3, 4), (7, 8), (10, 11, 12, 13))
_NARROW = (5, 6, 9)
PK_OFF = {}
_o = 0
for _reg in _REGIONS:
    for _s in _reg:
        PK_OFF[_s] = _o
        _o += IN_SIZES[_s]
PK_TOTAL = _o
IN_TN = 512
REGION_TILES = tuple(sum(IN_SIZES[s] for s in reg) // IN_TN for reg in _REGIONS)
NARROW_W = 128

V7X_VMEM_LIMIT = 58 * 1024 * 1024


def _split_w_in(w_in):
    w_b = w_in.astype(BF16)
    wide = [w_b if IN_OFFS[reg[0]] == 0 else w_b[:, :, IN_OFFS[reg[0]]:IN_OFFS[reg[-1] + 1]] for reg in _REGIONS]
    nar = jnp.concatenate([w_in[:, :, IN_OFFS[s]:IN_OFFS[s + 1]] for s in _NARROW], axis=2)
    nar = jnp.pad(nar, ((0, 0), (0, 0), (0, NARROW_W - nar.shape[2]))).astype(BF16)
    return wide, nar


def _rms_rows(x, nw):
    ms = jnp.mean(x * x, axis=-1, keepdims=True)
    return x * lax.rsqrt(ms + EPS) * nw


def _in_proj_kernel(x_ref, nw_ref, wa_ref, wb_ref, wc_ref, wn_ref, o_ref, nar_ref, xn_ref):
    j = pl.program_id(1)

    @pl.when(j == 0)
    def _():
        xn_ref[...] = _rms_rows(x_ref[...], nw_ref[...]).astype(BF16)
        nar_ref[...] = jnp.dot(xn_ref[...], wn_ref[...], preferred_element_type=F32)

    first = 0
    for w_ref, n_tiles in zip((wa_ref, wb_ref, wc_ref), REGION_TILES):
        @pl.when((j >= first) & (j < first + n_tiles))
        def _(w_ref=w_ref):
            o_ref[...] = jnp.dot(xn_ref[...], w_ref[...], preferred_element_type=F32)
        first += n_tiles


def in_projection(x, nw, wide, nar, layer, *, tm):
    m, k = x.shape
    starts = [sum(REGION_TILES[:r]) for r in range(len(REGION_TILES))]

    def region_spec(r):
        return pl.BlockSpec((None, k, IN_TN),
                            lambda i, j: (layer, 0, jnp.clip(j - starts[r], 0, REGION_TILES[r] - 1)))

    return pl.pallas_call(
        _in_proj_kernel,
        grid=(m // tm, PK_TOTAL // IN_TN),
        in_specs=[
            pl.BlockSpec((tm, k), lambda i, j: (i, 0)),
            pl.BlockSpec((1, k), lambda i, j: (0, 0)),
            region_spec(0), region_spec(1), region_spec(2),
            pl.BlockSpec((None, k, NARROW_W), lambda i, j: (layer, 0, 0)),
        ],
        out_specs=[pl.BlockSpec((tm, IN_TN), lambda i, j: (i, j)),
                   pl.BlockSpec((tm, NARROW_W), lambda i, j: (i, 0))],
        out_shape=[jax.ShapeDtypeStruct((m, PK_TOTAL), F32), jax.ShapeDtypeStruct((m, NARROW_W), F32)],
        scratch_shapes=[pltpu.VMEM((tm, k), BF16)],
        compiler_params=pltpu.CompilerParams(
            dimension_semantics=("arbitrary", "arbitrary"), vmem_limit_bytes=V7X_VMEM_LIMIT),
        name="in_projection",
    )(x, nw.reshape(1, k), *wide, nar)


def _mm_res_kernel(a_ref, w_ref, h_ref, o_ref):
    o_ref[...] = h_ref[...] + jnp.dot(a_ref[...], w_ref[...], preferred_element_type=F32)


def matmul_residual(a, w, layer, h, *, tm):
    m, k = a.shape
    n = w.shape[2]
    return pl.pallas_call(
        _mm_res_kernel,
        grid=(m // tm,),
        in_specs=[
            pl.BlockSpec((tm, k), lambda i: (i, 0)),
            pl.BlockSpec((None, k, n), lambda i: (layer, 0, 0)),
            pl.BlockSpec((tm, n), lambda i: (i, 0)),
        ],
        out_specs=pl.BlockSpec((tm, n), lambda i: (i, 0)),
        out_shape=jax.ShapeDtypeStruct((m, n), F32),
        compiler_params=pltpu.CompilerParams(
            dimension_semantics=("arbitrary",), vmem_limit_bytes=V7X_VMEM_LIMIT),
        name="matmul_residual",
    )(a, w, h)


def _silu(x):
    return x * jax.nn.sigmoid(x)


def _ffn_kernel(h_ref, nw_ref, wg_ref, wu_ref, wo_ref, o_ref, xn_ref):
    @pl.when(pl.program_id(1) == 0)
    def _():
        h = h_ref[...]
        xn_ref[...] = _rms_rows(h, nw_ref[...]).astype(BF16)
        o_ref[...] = h

    xn = xn_ref[...]
    gate = jnp.dot(xn, wg_ref[...], preferred_element_type=F32)
    up = jnp.dot(xn, wu_ref[...], preferred_element_type=F32)
    act = (_silu(gate) * up).astype(BF16)
    o_ref[...] += jnp.dot(act, wo_ref[...], preferred_element_type=F32)


def ffn_residual(h, nw, w_in, w_out, layer, *, tm, th):
    m, k = h.shape
    hidden = w_out.shape[1]
    nj = hidden // th
    return pl.pallas_call(
        _ffn_kernel,
        grid=(m // tm, nj),
        in_specs=[
            pl.BlockSpec((tm, k), lambda i, j: (i, 0)),
            pl.BlockSpec((1, k), lambda i, j: (0, 0)),
            pl.BlockSpec((None, k, th), lambda i, j: (layer, 0, j)),
            pl.BlockSpec((None, k, th), lambda i, j: (layer, 0, j + nj)),
            pl.BlockSpec((None, th, k), lambda i, j: (layer, j, 0)),
        ],
        out_specs=pl.BlockSpec((tm, k), lambda i, j: (i, 0)),
        out_shape=jax.ShapeDtypeStruct((m, k), F32),
        scratch_shapes=[pltpu.VMEM((tm, k), BF16)],
        compiler_params=pltpu.CompilerParams(
            dimension_semantics=("arbitrary", "arbitrary"), vmem_limit_bytes=V7X_VMEM_LIMIT),
        name="ffn_residual",
    )(h, nw.reshape(1, k), w_in, w_in, w_out)


def _ple_rows(h_ref, nw_ref, wg_ref, pp_ref, ps_ref, wp_ref, n_prompt_tiles):
    h = h_ref[...]
    xn = _rms_rows(h, nw_ref[...]).astype(BF16)
    gate = jax.nn.sigmoid(jnp.dot(xn, wg_ref[...], preferred_element_type=F32))
    p = jnp.where(pl.program_id(0) < n_prompt_tiles, pp_ref[...], ps_ref[...])
    return h + gate * jnp.dot(p.astype(BF16), wp_ref[...], preferred_element_type=F32)


def _ple_kernel(h_ref, nw_ref, wg_ref, pp_ref, ps_ref, wp_ref, o_ref, *, n_prompt_tiles):
    o_ref[...] = _ple_rows(h_ref, nw_ref, wg_ref, pp_ref, ps_ref, wp_ref, n_prompt_tiles)


def _ple_final_kernel(h_ref, nw_ref, wg_ref, pp_ref, ps_ref, wp_ref, nf_ref, yp_ref, ys_ref, *, n_prompt_tiles):
    y = _rms_rows(_ple_rows(h_ref, nw_ref, wg_ref, pp_ref, ps_ref, wp_ref, n_prompt_tiles), nf_ref[...])

    @pl.when(pl.program_id(0) < n_prompt_tiles)
    def _():
        yp_ref[...] = y

    @pl.when(pl.program_id(0) >= n_prompt_tiles)
    def _():
        ys_ref[...] = y


def ple_residual(h, nw, wg, pp, ps, wp, layer, *, tm, nf=None):
    m, k = h.shape
    mp, pd = pp.shape[1], pp.shape[2]
    ms = ps.shape[1]
    assert mp % tm == 0 and ms % tm == 0 and mp + ms == m
    npt, nst = mp // tm, ms // tm
    row = pl.BlockSpec((tm, k), lambda i: (i, 0))
    vec = pl.BlockSpec((1, k), lambda i: (0, 0))
    in_specs = [row, vec,
                pl.BlockSpec((None, k, k), lambda i: (layer, 0, 0)),
                pl.BlockSpec((None, tm, pd), lambda i: (layer, jnp.minimum(i, npt - 1), 0)),
                pl.BlockSpec((None, tm, pd), lambda i: (layer, jnp.clip(i - npt, 0, nst - 1), 0)),
                pl.BlockSpec((None, pd, k), lambda i: (layer, 0, 0))]
    args = [h, nw.reshape(1, k), wg, pp, ps, wp]
    params = pltpu.CompilerParams(dimension_semantics=("arbitrary",), vmem_limit_bytes=V7X_VMEM_LIMIT)
    if nf is None:
        return pl.pallas_call(
            functools.partial(_ple_kernel, n_prompt_tiles=npt), grid=(m // tm,), in_specs=in_specs,
            out_specs=row, out_shape=jax.ShapeDtypeStruct((m, k), F32), compiler_params=params,
            name="ple_residual")(*args)
    return pl.pallas_call(
        functools.partial(_ple_final_kernel, n_prompt_tiles=npt), grid=(m // tm,), in_specs=in_specs + [vec],
        out_specs=[pl.BlockSpec((tm, k), lambda i: (jnp.minimum(i, npt - 1), 0)),
                   pl.BlockSpec((tm, k), lambda i: (jnp.clip(i - npt, 0, nst - 1), 0))],
        out_shape=[jax.ShapeDtypeStruct((mp, k), F32), jax.ShapeDtypeStruct((ms, k), F32)],
        compiler_params=params, name="ple_final")(*args, nf.reshape(1, k))


S5_HALF_CH = S5_CH // 2
S5_HALF_ST = (S5_GROUPS // 2) * S5_STATE
S5_LANES = 4 * S5_HALF_ST
S5_SLAB = 512


def _s5_tables(lw):
    a_re = lw['s5_a_re'].astype(F32)
    a_im = lw['s5_a_im'].astype(F32)
    step = jnp.exp(lw['s5_log_step'].astype(F32))[:, None]
    mag = jnp.exp(a_re * step)
    lam_re = mag * jnp.cos(a_im * step)
    lam_im = mag * jnp.sin(a_im * step)
    den = a_re * a_re + a_im * a_im
    coef_re = ((lam_re - 1.0) * a_re + lam_im * a_im) / den
    coef_im = (lam_im * a_re - (lam_re - 1.0) * a_im) / den
    b_re = lw['s5_b_re'].astype(F32)
    b_im = lw['s5_b_im'].astype(F32)
    bb_re = coef_re[..., None] * b_re - coef_im[..., None] * b_im
    bb_im = coef_re[..., None] * b_im + coef_im[..., None] * b_re
    gh = S5_GROUPS // 2
    eye = jnp.eye(gh, dtype=F32)

    def in_blockdiag(b):
        return jnp.einsum('gnc,gh->gchn', b, eye).reshape(gh * S5_GROUP_CH, gh * S5_STATE)

    def out_blockdiag(c):
        return jnp.einsum('gcn,gh->gnhc', c, eye).reshape(gh * S5_STATE, gh * S5_GROUP_CH)

    c_re = lw['s5_c_re'].astype(F32)
    c_im = lw['s5_c_im'].astype(F32)
    bb = jnp.stack([jnp.concatenate([in_blockdiag(bb_re[h * gh:(h + 1) * gh]),
                                     in_blockdiag(bb_im[h * gh:(h + 1) * gh])], axis=1) for h in range(2)])
    cm = jnp.stack([jnp.concatenate([out_blockdiag(c_re[h * gh:(h + 1) * gh]),
                                     -out_blockdiag(c_im[h * gh:(h + 1) * gh])], axis=0) for h in range(2)])
    lam = jnp.stack([lam_re.reshape(-1), lam_im.reshape(-1)])
    lam2 = jnp.stack([lam[0] * lam[0] - lam[1] * lam[1], 2.0 * lam[0] * lam[1]])
    return dict(bb=bb.astype(BF16), cm=cm.astype(BF16), lam=lam, lam2=lam2,
                d=lw['s5_d'].astype(F32).reshape(1, S5_CH), wglu=lw['s5_w_glu'].astype(BF16),
                bglu=lw['s5_b_glu'].astype(F32).reshape(1, S5_CH))


def _s5_drive(u, bb_ref, sc_ref):
    ub = u.astype(BF16)
    for hf in range(2):
        sc_ref[:, hf * 2 * S5_HALF_ST:(hf + 1) * 2 * S5_HALF_ST] = jnp.dot(
            ub[:, hf * S5_HALF_CH:(hf + 1) * S5_HALF_CH], bb_ref[hf], preferred_element_type=F32)


def _s5_readout(sc_ref, u, cm_ref, d_ref, wglu_ref, bglu_ref):
    ys = [jnp.dot(sc_ref[:, hf * 2 * S5_HALF_ST:(hf + 1) * 2 * S5_HALF_ST].astype(BF16), cm_ref[hf],
                  preferred_element_type=F32) for hf in range(2)]
    y = jnp.concatenate(ys, axis=1) + d_ref[...] * u
    y = jax.nn.gelu(y)
    z = jnp.dot(y.astype(BF16), wglu_ref[...], preferred_element_type=F32) + bglu_ref[...]
    return y * jax.nn.sigmoid(z)


def _s5_slabs():
    for hf in range(2):
        for sl in range(S5_HALF_ST // S5_SLAB):
            re0 = hf * 2 * S5_HALF_ST + sl * S5_SLAB
            yield re0, re0 + S5_HALF_ST, hf * S5_HALF_ST + sl * S5_SLAB


def _s5_prompt_kernel(u_ref, bb_ref, m_ref, cm_ref, d_ref, wglu_ref, bglu_ref, o_ref, st_ref, sc_ref, carry_ref):
    @pl.when(pl.program_id(0) == 0)
    def _():
        carry_ref[...] = jnp.zeros_like(carry_ref)

    u = u_ref[...]
    _s5_drive(u, bb_ref, sc_ref)
    first_step = lax.broadcasted_iota(jnp.int32, (8, S5_SLAB), 0) < 4
    n_pairs = u_ref.shape[0] // 8
    for re0, im0, l0 in _s5_slabs():
        mr = m_ref[0, :, l0:l0 + S5_SLAB]
        mi = m_ref[1, :, l0:l0 + S5_SLAB]
        nr = m_ref[2, :, l0:l0 + S5_SLAB]
        ni = m_ref[3, :, l0:l0 + S5_SLAB]

        def body(k, carry, re0=re0, im0=im0, mr=mr, mi=mi, nr=nr, ni=ni):
            hr, hi = carry
            base = pl.multiple_of(k * 8, 8)
            xr = sc_ref[pl.ds(base, 8), re0:re0 + S5_SLAB]
            xi = sc_ref[pl.ds(base, 8), im0:im0 + S5_SLAB]
            xr_s = pltpu.roll(xr, 4, 0)
            xi_s = pltpu.roll(xi, 4, 0)
            outr = (mr * hr - mi * hi) + xr + (nr * xr_s - ni * xi_s)
            outi = (mr * hi + mi * hr) + xi + (nr * xi_s + ni * xr_s)
            sc_ref[pl.ds(base, 8), re0:re0 + S5_SLAB] = outr
            sc_ref[pl.ds(base, 8), im0:im0 + S5_SLAB] = outi
            return (jnp.where(first_step, pltpu.roll(outr, 4, 0), outr),
                    jnp.where(first_step, pltpu.roll(outi, 4, 0), outi))

        hr, hi = lax.fori_loop(0, n_pairs, body,
                               (carry_ref[:, re0:re0 + S5_SLAB], carry_ref[:, im0:im0 + S5_SLAB]))
        carry_ref[:, re0:re0 + S5_SLAB] = hr
        carry_ref[:, im0:im0 + S5_SLAB] = hi

    o_ref[...] = _s5_readout(sc_ref, u, cm_ref, d_ref, wglu_ref, bglu_ref).astype(o_ref.dtype)
    st_ref[...] = carry_ref[...]


def _const_spec(shape):
    return pl.BlockSpec(shape, lambda c: (0,) * len(shape))


def s5_prompt(u_tm, tb, *, rows):
    n = u_tm.shape[0]
    zero = jnp.zeros_like(tb['lam'])
    m = jnp.stack([jnp.concatenate([jnp.broadcast_to(a[k][None], (4, a.shape[1])),
                                    jnp.broadcast_to(b[k][None], (4, b.shape[1]))], axis=0)
                   for a, b, k in ((tb['lam'], tb['lam2'], 0), (tb['lam'], tb['lam2'], 1),
                                   (zero, tb['lam'], 0), (zero, tb['lam'], 1))])
    return pl.pallas_call(
        _s5_prompt_kernel,
        grid=(n // rows,),
        in_specs=[
            pl.BlockSpec((rows, S5_CH), lambda c: (c, 0)),
            _const_spec(tb['bb'].shape), _const_spec(m.shape), _const_spec(tb['cm'].shape),
            _const_spec((1, S5_CH)), _const_spec((S5_CH, S5_CH)), _const_spec((1, S5_CH)),
        ],
        out_specs=[pl.BlockSpec((rows, S5_CH), lambda c: (c, 0)), _const_spec((8, S5_LANES))],
        out_shape=[jax.ShapeDtypeStruct((n, S5_CH), BF16), jax.ShapeDtypeStruct((8, S5_LANES), F32)],
        scratch_shapes=[pltpu.VMEM((rows, S5_LANES), F32), pltpu.VMEM((8, S5_LANES), F32)],
        compiler_params=pltpu.CompilerParams(
            dimension_semantics=("arbitrary",), vmem_limit_bytes=V7X_VMEM_LIMIT),
        name="s5_prompt",
    )(u_tm, tb['bb'], m, tb['cm'], tb['d'], tb['wglu'], tb['bglu'])


def _s5_sample_kernel(u_ref, h0_ref, bb_ref, lam_ref, cm_ref, d_ref, wglu_ref, bglu_ref, mix_ref,
                      o_ref, st_ref, sc_ref):
    del mix_ref
    u = u_ref[...]
    _s5_drive(u, bb_ref, sc_ref)
    n_seq = h0_ref.shape[0]
    n_steps = u_ref.shape[0] // n_seq
    for re0, im0, l0 in _s5_slabs():
        lr = lam_ref[0:1, l0:l0 + S5_SLAB]
        li = lam_ref[1:2, l0:l0 + S5_SLAB]

        def body(rb, _, re0=re0, im0=im0, lr=lr, li=li):
            r0 = pl.multiple_of(rb * 8, 8)
            hr = h0_ref[pl.ds(r0, 8), re0:re0 + S5_SLAB]
            hi = h0_ref[pl.ds(r0, 8), im0:im0 + S5_SLAB]
            for t in range(n_steps):
                rows = pl.ds(t * n_seq + r0, 8)
                nr = (lr * hr - li * hi) + sc_ref[rows, re0:re0 + S5_SLAB]
                ni = (lr * hi + li * hr) + sc_ref[rows, im0:im0 + S5_SLAB]
                sc_ref[rows, re0:re0 + S5_SLAB] = nr
                sc_ref[rows, im0:im0 + S5_SLAB] = ni
                hr, hi = nr, ni
            st_ref[pl.ds(r0, 8), re0:re0 + S5_SLAB] = hr
            st_ref[pl.ds(r0, 8), im0:im0 + S5_SLAB] = hi
            return 0

        lax.fori_loop(0, n_seq // 8, body, 0)

    o_ref[...] = _s5_readout(sc_ref, u, cm_ref, d_ref, wglu_ref, bglu_ref).astype(o_ref.dtype)


def s5_sample(y, mixed, h0, row0, n_rows, tb):
    nb = h0.shape[0]
    assert row0 % n_rows == 0
    blk = row0 // n_rows
    return pl.pallas_call(
        _s5_sample_kernel,
        grid=(1,),
        in_specs=[pl.BlockSpec((n_rows, S5_CH), lambda c: (blk, PK_OFF[0] // S5_CH)),
                  _const_spec((nb, S5_LANES)), _const_spec(tb['bb'].shape),
                  _const_spec(tb['lam'].shape), _const_spec(tb['cm'].shape), _const_spec((1, S5_CH)),
                  _const_spec((S5_CH, S5_CH)), _const_spec((1, S5_CH)), _ANY_SPEC],
        out_specs=[pl.BlockSpec((n_rows, S5_CH), lambda c: (blk, MIX_S5)), _const_spec((nb, S5_LANES))],
        out_shape=[jax.ShapeDtypeStruct(mixed.shape, mixed.dtype), jax.ShapeDtypeStruct((nb, S5_LANES), F32)],
        input_output_aliases={8: 0},
        scratch_shapes=[pltpu.VMEM((n_rows, S5_LANES), F32)],
        compiler_params=pltpu.CompilerParams(
            dimension_semantics=("arbitrary",), vmem_limit_bytes=V7X_VMEM_LIMIT),
        name="s5_sample",
    )(y, h0, tb['bb'], tb['lam'], tb['cm'], tb['d'], tb['wglu'], tb['bglu'], mixed)


def _s5_state_to_lanes(re, im):
    b = re.shape[0]
    return jnp.stack([re.reshape(b, 2, S5_HALF_ST), im.reshape(b, 2, S5_HALF_ST)], axis=2).reshape(b, S5_LANES)


def _s5_lanes_to_state(st):
    b = st.shape[0]
    st = st.reshape(b, 2, 2, S5_HALF_ST)
    return st[:, :, 0].reshape(b, S5_GROUPS, S5_STATE), st[:, :, 1].reshape(b, S5_GROUPS, S5_STATE)


def _ret_tables(positions, chunk):
    half = RET_DK // 2
    inv_freq = ROPE_BASE ** (-jnp.arange(half, dtype=F32) / half)
    ang = positions.astype(F32)[:, None] * inv_freq[None, :]
    cos = jnp.cos(ang)
    sin = jnp.sin(ang)
    cos2 = jnp.concatenate([cos, cos], axis=1)
    sin2 = jnp.concatenate([-sin, sin], axis=1)
    log_gamma = jnp.log(1.0 - 2.0 ** (-5.0 - jnp.arange(RET_HEADS, dtype=F32)))
    g = (jnp.arange(chunk, dtype=F32) + 1.0)[None, :] * log_gamma[:, None]
    diff = g[:, :, None] - g[:, None, :]
    causal = jnp.tril(jnp.ones((chunk, chunk), dtype=bool))
    dmat = jnp.where(causal, jnp.exp(jnp.where(causal, diff, 0.0)), 0.0)
    lanes = (RET_HEADS, chunk, RET_DK)
    qdec = jnp.broadcast_to(jnp.exp(g)[:, :, None], lanes)
    kdec = jnp.broadcast_to(jnp.exp(g[:, -1:] - g)[:, :, None], lanes)
    gall = jnp.broadcast_to(jnp.exp(g[:, -1])[:, None, None], (RET_HEADS, 1, RET_DV))
    return cos2, sin2, dmat, qdec, kdec, gall


def _ret_rotate(x, cos2, sin2):
    return x * cos2 + pltpu.roll(x, RET_DK // 2, 1) * sin2


def _group_layernorm_gate(o, gate, w, b):
    mu = jnp.mean(o, axis=-1, keepdims=True)
    xc = o - mu
    var = jnp.mean(xc * xc, axis=-1, keepdims=True)
    return _silu(gate) * (xc * lax.rsqrt(var + EPS) * w + b)


def _dot_nt(a, b):
    return lax.dot_general(a, b, (((1,), (1,)), ((), ())), preferred_element_type=F32)


def _dot_tn(a, b):
    return lax.dot_general(a, b, (((0,), (0,)), ((), ())), preferred_element_type=F32)


def _ret_prompt_kernel(q_ref, k_ref, v_ref, g_ref, cos_ref, sin_ref, dm_ref, qd_ref, kd_ref, ga_ref,
                       lnw_ref, lnb_ref, mix_ref, o_ref, s_ref, *, chunk):
    del mix_ref
    s_ref[...] = jnp.zeros_like(s_ref)

    def body(c, carry):
        r = pl.ds(pl.multiple_of(c * chunk, chunk), chunk)
        cos2, sin2 = cos_ref[r, :], sin_ref[r, :]
        for h in range(RET_HEADS):
            hc = slice(h * RET_DK, (h + 1) * RET_DK)
            q = _ret_rotate(q_ref[r, hc], cos2, sin2)
            k = _ret_rotate(k_ref[r, hc], cos2, sin2) * (RET_DK ** -0.5)
            vb = v_ref[r, hc].astype(BF16)
            s = s_ref[0, h]
            scores = _dot_nt(q.astype(BF16), k.astype(BF16)) * dm_ref[h]
            o = jnp.dot(scores.astype(BF16), vb, preferred_element_type=F32)
            o = o + jnp.dot((q * qd_ref[h]).astype(BF16), s.astype(BF16), preferred_element_type=F32)
            o_ref[r, hc] = _group_layernorm_gate(o, g_ref[r, hc], lnw_ref[h], lnb_ref[h]).astype(o_ref.dtype)
            s_ref[0, h] = s * ga_ref[h] + _dot_tn((k * kd_ref[h]).astype(BF16), vb)
        return carry

    lax.fori_loop(0, q_ref.shape[0] // chunk, body, 0)


MIX_S5, MIX_GDN, MIX_SSD, MIX_RET = range(4)
_ANY_SPEC = pl.BlockSpec(memory_space=pl.ANY)


def _uninit_kernel(o_ref):
    del o_ref


def uninitialized(shape, dtype):
    return pl.pallas_call(_uninit_kernel, out_shape=jax.ShapeDtypeStruct(shape, dtype), out_specs=_ANY_SPEC,
                          name="uninitialized")()


def ret_prompt(y, mixed, n_seq, seq_len, lw):
    chunk = min(RET_CHUNK, seq_len)
    cos2, sin2, dmat, qdec, kdec, gall = _ret_tables(jnp.arange(seq_len, dtype=jnp.int32), chunk)
    width = RET_HEADS * RET_DK
    heads = lambda shape: _const_spec((RET_HEADS,) + shape)
    tok = lambda s: pl.BlockSpec((seq_len, width), lambda b: (b, PK_OFF[s] // width))
    tab = _const_spec((seq_len, RET_DK))
    return pl.pallas_call(
        functools.partial(_ret_prompt_kernel, chunk=chunk),
        grid=(n_seq,),
        in_specs=[tok(10), tok(11), tok(12), tok(13), tab, tab,
                  heads((chunk, chunk)), heads((chunk, RET_DK)), heads((chunk, RET_DK)), heads((1, RET_DV)),
                  heads((1, RET_DV)), heads((1, RET_DV)), _ANY_SPEC],
        out_specs=[pl.BlockSpec((seq_len, width), lambda b: (b, MIX_RET)),
                   pl.BlockSpec((1, RET_HEADS, RET_DK, RET_DV), lambda b: (b, 0, 0, 0))],
        out_shape=[jax.ShapeDtypeStruct(mixed.shape, mixed.dtype),
                   jax.ShapeDtypeStruct((n_seq, RET_HEADS, RET_DK, RET_DV), F32)],
        input_output_aliases={12: 0},
        compiler_params=pltpu.CompilerParams(
            dimension_semantics=("arbitrary",), vmem_limit_bytes=V7X_VMEM_LIMIT),
        name="ret_prompt",
    )(y, y, y, y, cos2, sin2, dmat, qdec, kdec, gall,
      lw['ret_ln_w'].astype(F32).reshape(RET_HEADS, 1, RET_DV), lw['ret_ln_b'].astype(F32).reshape(RET_HEADS, 1, RET_DV),
      mixed)


SAMPLE_BB = 16
LHS_ROWS = 16


def _ret_sample_kernel(q_ref, k_ref, v_ref, g_ref, cos_ref, sin_ref, s0_ref, lnw_ref, lnb_ref, mix_ref, sall_ref,
                       o_ref, s_ref, qd_scr, kd_scr, v_scr, acc_scr, lq, lk, lv, *, decay):
    del mix_ref, sall_ref
    n_t, n_b = q_ref.shape[0], q_ref.shape[1]
    for h in range(RET_HEADS):
        hc = slice(h * RET_DK, (h + 1) * RET_DK)
        qs, ks, vs = [], [], []
        for t in range(n_t):
            cos2, sin2 = cos_ref[t:t + 1, :], sin_ref[t:t + 1, :]
            qs.append(_ret_rotate(q_ref[t, :, hc], cos2, sin2))
            ks.append(_ret_rotate(k_ref[t, :, hc], cos2, sin2) * (RET_DK ** -0.5))
            vs.append(v_ref[t, :, hc])
            qd_scr[h, t] = qs[t] * decay['q'][h][t]
            kd_scr[h, t] = ks[t] * decay['k'][h][t]
            v_scr[h, t] = vs[t]
        for i in range(n_t):
            acc = None
            for j in range(i + 1):
                term = (jnp.sum(qs[i] * ks[j], axis=-1, keepdims=True) * decay['m'][h][i][j]) * vs[j]
                acc = term if acc is None else acc + term
            acc_scr[h, i] = acc

    for tile in (lq, lk, lv):
        tile[...] = jnp.zeros_like(tile)

    def per_seq(b, carry):
        row = pl.ds(b, 1)
        for h in range(RET_HEADS):
            for t in range(n_t):
                lq[h, t:t + 1, :] = qd_scr[h, t, row, :]
                lk[h, t:t + 1, :] = kd_scr[h, t, row, :]
                lv[h, t:t + 1, :] = v_scr[h, t, row, :]
            s0 = s0_ref[b, h]
            inter = jnp.dot(lq[h].astype(BF16), s0.astype(BF16), preferred_element_type=F32)
            s_ref[b, h] = s0 * decay['all'][h] + _dot_tn(lk[h].astype(BF16), lv[h].astype(BF16))
            for t in range(n_t):
                acc_scr[h, t, row, :] = acc_scr[h, t, row, :] + inter[t:t + 1, :]
        return carry

    lax.fori_loop(0, n_b, per_seq, 0)

    for h in range(RET_HEADS):
        hc = slice(h * RET_DV, (h + 1) * RET_DV)
        for t in range(n_t):
            o_ref[t, :, hc] = _group_layernorm_gate(acc_scr[h, t], g_ref[t, :, hc], lnw_ref[h], lnb_ref[h]
                                                    ).astype(o_ref.dtype)


def _sample_block(t0, n_t):
    assert t0 % n_t == 0
    return t0 // n_t


def ret_sample(y3, mixed3, states, new_states, layer, t0, n_t, first_pos, lw):
    n_b = y3.shape[1]
    tb = _sample_block(t0, n_t)
    cos2, sin2, _, _, _, _ = _ret_tables(first_pos + jnp.arange(n_t, dtype=jnp.int32), n_t)
    gamma = 1.0 - 2.0 ** (-5.0 - np.arange(RET_HEADS, dtype=np.float64))
    decay = dict(m=[[[float(g ** (i - j)) for j in range(n_t)] for i in range(n_t)] for g in gamma],
                 q=[[float(g ** (i + 1)) for i in range(n_t)] for g in gamma],
                 k=[[float(g ** (n_t - 1 - j)) for j in range(n_t)] for g in gamma],
                 all=[float(g ** n_t) for g in gamma])
    width = RET_HEADS * RET_DK
    tok = lambda s: pl.BlockSpec((n_t, SAMPLE_BB, width), lambda i: (tb, i, PK_OFF[s] // width))
    state = pl.BlockSpec((None, SAMPLE_BB, RET_HEADS, RET_DK, RET_DV), lambda i: (layer, i, 0, 0, 0))
    per_tok = pltpu.VMEM((RET_HEADS, n_t, SAMPLE_BB, RET_DK), F32)
    tile = pltpu.VMEM((RET_HEADS, LHS_ROWS, RET_DK), F32)
    return pl.pallas_call(
        functools.partial(_ret_sample_kernel, decay=decay),
        grid=(n_b // SAMPLE_BB,),
        in_specs=[tok(10), tok(11), tok(12), tok(13), _const_spec((n_t, RET_DK)), _const_spec((n_t, RET_DK)),
                  state, _const_spec((RET_HEADS, 1, RET_DV)), _const_spec((RET_HEADS, 1, RET_DV)), _ANY_SPEC, _ANY_SPEC],
        out_specs=[pl.BlockSpec((n_t, SAMPLE_BB, width), lambda i: (tb, i, MIX_RET)), state],
        out_shape=[jax.ShapeDtypeStruct(mixed3.shape, mixed3.dtype), jax.ShapeDtypeStruct(states.shape, F32)],
        input_output_aliases={9: 0, 10: 1},
        scratch_shapes=[per_tok, per_tok, per_tok, per_tok, tile, tile, tile],
        compiler_params=pltpu.CompilerParams(
            dimension_semantics=("arbitrary",), vmem_limit_bytes=V7X_VMEM_LIMIT),
        name="ret_sample",
    )(y3, y3, y3, y3, cos2, sin2, states,
      lw['ret_ln_w'].astype(F32).reshape(RET_HEADS, 1, RET_DV), lw['ret_ln_b'].astype(F32).reshape(RET_HEADS, 1, RET_DV),
      mixed3, new_states)


NARROW_B = 0
NARROW_A = GDN_HEADS
NARROW_DT = 2 * GDN_HEADS
TAIL = 8


def _shift_rows(x, tail, s):
    xr = pltpu.roll(x, s, 0)
    tr = pltpu.roll(tail, s, 0)
    row = lax.broadcasted_iota(jnp.int32, tail.shape, 0)
    return jnp.concatenate([jnp.where(row < s, tr, xr[0:TAIL]), xr[TAIL:]], axis=0)


def _causal_conv(x, tail, w_ref):
    y = x * w_ref[CONV_W - 1:CONV_W, :]
    for s in range(1, CONV_W):
        y = y + _shift_rows(x, tail, s) * w_ref[CONV_W - 1 - s:CONV_W - s, :]
    return y


def _softplus(x):
    return jnp.maximum(x, 0.0) + jnp.log1p(jnp.exp(-jnp.abs(x)))


def _lane_row(vals, lane0, width=NARROW_W):
    return jnp.pad(vals.astype(F32), (lane0, width - lane0 - vals.shape[0])).reshape(1, width)


SSD_PAIRS = SSD_HEADS // 2
SSD_BC = SSD_NGROUPS * SSD_STATE


def _ssd_prompt_kernel(z_ref, xbc_ref, nar_ref, cw_ref, cb_ref, dtb_ref, a_ref, dsk_ref, nw_ref, tri_ref, mix_ref,
                       o_ref, s_ref, tail_ref, *, chunk):
    del mix_ref
    s_ref[...] = jnp.zeros_like(s_ref)
    tail_ref[...] = jnp.zeros_like(tail_ref)
    causal = (lax.broadcasted_iota(jnp.int32, (chunk, chunk), 0)
              >= lax.broadcasted_iota(jnp.int32, (chunk, chunk), 1))
    lane = lax.broadcasted_iota(jnp.int32, (chunk, 2 * SSD_HEADDIM), 1)
    first_head = lane < SSD_HEADDIM
    rep = SSD_HEADS // SSD_NGROUPS

    def body(c, carry):
        r = pl.ds(pl.multiple_of(c * chunk, chunk), chunk)
        raw = xbc_ref[r, :]
        xbc = _silu(_causal_conv(raw, tail_ref[0], cw_ref) + cb_ref[...])
        tail_ref[0] = raw[chunk - TAIL:, :]
        xs = xbc[:, :SSD_INNER]
        dt = _softplus(nar_ref[r, :] + dtb_ref[...])
        g = _select_rows(tri_ref[...], dt * a_ref[...])
        e_in = jnp.exp(g)
        e_out = dt * jnp.exp(g[chunk - 1:chunk, :] - g)
        e_all = jnp.exp(g[chunk - 1:chunk, :])
        g_t = g.T
        dt_t = dt.T
        ys = []
        for p in range(SSD_PAIRS):
            grp = (2 * p) // rep
            bm = xbc[:, SSD_INNER + grp * SSD_STATE:SSD_INNER + (grp + 1) * SSD_STATE]
            cm = xbc[:, SSD_INNER + SSD_BC + grp * SSD_STATE:SSD_INNER + SSD_BC + (grp + 1) * SSD_STATE]
            cb = _dot_nt(cm.astype(BF16), bm.astype(BF16))
            xp = xs[:, p * 128:(p + 1) * 128]
            xpb = xp.astype(BF16)
            sp = s_ref[0, p]
            spb = sp.astype(BF16)
            outs, upds, gls = [], [], []
            for hh in range(2):
                ln = NARROW_DT + 2 * p + hh
                diff = g[:, ln:ln + 1] - g_t[ln:ln + 1, :]
                m = jnp.where(causal, cb * jnp.exp(jnp.where(causal, diff, 0.0)) * dt_t[ln:ln + 1, :], 0.0)
                o = jnp.dot(m.astype(BF16), xpb, preferred_element_type=F32)
                o = o + jnp.dot((cm * e_in[:, ln:ln + 1]).astype(BF16), spb, preferred_element_type=F32)
                outs.append(o)
                upds.append(_dot_tn((bm * e_out[:, ln:ln + 1]).astype(BF16), xpb))
                gls.append(e_all[:, ln:ln + 1])
            s_ref[0, p] = sp * jnp.where(first_head, gls[0], gls[1]) + jnp.where(first_head, upds[0], upds[1])
            ys.append(jnp.where(first_head, outs[0], outs[1]) + xp * dsk_ref[:, p * 128:(p + 1) * 128])
        y = jnp.concatenate(ys, axis=1) * _silu(z_ref[r, :])
        gw = SSD_INNER // SSD_NGROUPS
        yn = [y[:, i * gw:(i + 1) * gw] * lax.rsqrt(
            jnp.mean(y[:, i * gw:(i + 1) * gw] * y[:, i * gw:(i + 1) * gw], axis=-1, keepdims=True) + EPS)
            for i in range(SSD_NGROUPS)]
        o_ref[r, :] = (jnp.concatenate(yn, axis=1) * nw_ref[...]).astype(o_ref.dtype)
        return carry

    lax.fori_loop(0, z_ref.shape[0] // chunk, body, 0)


def _ssd_params(lw, chunk):
    return (lw['ssd_conv_w'].astype(F32), lw['ssd_conv_b'].astype(F32).reshape(1, SSD_CONV_DIM),
            _lane_row(lw['ssd_dt_bias'], NARROW_DT), _lane_row(-jnp.exp(lw['ssd_a_log'].astype(F32)), NARROW_DT),
            jnp.repeat(lw['ssd_d'].astype(F32), SSD_HEADDIM).reshape(1, SSD_INNER),
            lw['ssd_norm_w'].astype(F32).reshape(1, SSD_INNER),
            jnp.tril(jnp.ones((chunk, chunk), BF16)))


def _ssd_state_from_pairs(s):
    b = s.shape[0]
    s = s.reshape(b, SSD_PAIRS, SSD_STATE, 2, SSD_HEADDIM)
    return jnp.swapaxes(s, 2, 3).reshape(b, SSD_HEADS, SSD_STATE, SSD_HEADDIM)


def ssd_prompt(y, nar, mixed, n_seq, seq_len, lw):
    chunk = min(SSD_CHUNK, seq_len)
    params = _ssd_params(lw, chunk)
    tok = lambda s, w: pl.BlockSpec((seq_len, w), lambda b: (b, PK_OFF[s] // w))
    n_in = 3 + len(params)
    mixed, s, tail = pl.pallas_call(
        functools.partial(_ssd_prompt_kernel, chunk=chunk),
        grid=(n_seq,),
        in_specs=[tok(7, SSD_INNER), tok(8, SSD_CONV_DIM), pl.BlockSpec((seq_len, NARROW_W), lambda b: (b, 0))]
                 + [_const_spec(p.shape) for p in params] + [_ANY_SPEC],
        out_specs=[pl.BlockSpec((seq_len, SSD_INNER), lambda b: (b, MIX_SSD)),
                   pl.BlockSpec((1, SSD_PAIRS, SSD_STATE, 2 * SSD_HEADDIM), lambda b: (b, 0, 0, 0)),
                   pl.BlockSpec((1, TAIL, SSD_CONV_DIM), lambda b: (b, 0, 0))],
        out_shape=[jax.ShapeDtypeStruct(mixed.shape, mixed.dtype),
                   jax.ShapeDtypeStruct((n_seq, SSD_PAIRS, SSD_STATE, 2 * SSD_HEADDIM), F32),
                   jax.ShapeDtypeStruct((n_seq, TAIL, SSD_CONV_DIM), F32)],
        input_output_aliases={n_in: 0},
        compiler_params=pltpu.CompilerParams(
            dimension_semantics=("arbitrary",), vmem_limit_bytes=V7X_VMEM_LIMIT),
        name="ssd_prompt",
    )(y, y, nar, *params, mixed)
    return mixed, tail[:, TAIL - (CONV_W - 1):], _ssd_state_from_pairs(s)


GDN_QKV = GDN_HEADS * GDN_DK
GDN_ROWS = 1024
GDN_UNROLL = 4
HIGHEST = lax.Precision.HIGHEST


def _split_bf16(x):
    hi = x.astype(BF16)
    return hi, (x - hi.astype(F32)).astype(BF16)


def _split3_bf16(x):
    hi = x.astype(BF16)
    r = x - hi.astype(F32)
    mid = r.astype(BF16)
    return hi, mid, (r - mid.astype(F32)).astype(BF16)


def _select_rows(m01, x):
    n = x.shape[1]
    p = jnp.dot(m01, jnp.concatenate(_split3_bf16(x), axis=1), preferred_element_type=F32)
    return p[:, :n] + p[:, n:2 * n] + p[:, 2 * n:]


def _select_cols(x, m01):
    m = x.shape[0]
    p = jnp.dot(jnp.concatenate(_split3_bf16(x), axis=0), m01, preferred_element_type=F32)
    return p[:m] + p[m:2 * m] + p[2 * m:]


def _dot3(a, b):
    a_hi, a_lo = _split_bf16(a)
    b_hi, b_lo = _split_bf16(b)
    m = a.shape[0]
    p = jnp.dot(jnp.concatenate([a_hi, a_lo], axis=0), b_hi, preferred_element_type=F32)
    return p[:m] + p[m:] + jnp.dot(a_hi, b_lo, preferred_element_type=F32)


def _two_block_diag(x0, x1):
    z0 = jnp.zeros_like(x0)
    z1 = jnp.zeros_like(x1)
    return jnp.concatenate([jnp.concatenate([x0, z1], axis=1), jnp.concatenate([z0, x1], axis=1)], axis=0)


def _l2_rows(x):
    return x * lax.rsqrt(jnp.sum(x * x, axis=-1, keepdims=True) + EPS)


def _gdn_prompt_kernel(q_ref, k_ref, v_ref, z_ref, nar_ref, cw_ref, dtb_ref, a_ref, nw_ref, tri_ref, sel_ref, mix_ref,
                       o_ref, s_ref, tail_ref, u_scr, w_scr, qd_scr, kd_scr, sc_scr, ea_scr, *, chunk, unroll):
    del mix_ref

    @pl.when(pl.program_id(1) == 0)
    def _():
        s_ref[...] = jnp.zeros_like(s_ref)
        tail_ref[...] = jnp.zeros_like(tail_ref)

    n_chunks = q_ref.shape[0] // chunk
    cat = GDN_HEADS * chunk
    lane = lax.broadcasted_iota(jnp.int32, (chunk, cat), 1)
    row = lax.broadcasted_iota(jnp.int32, (chunk, cat), 0)
    col = jnp.bitwise_and(lane, chunk - 1)
    causal = row >= col
    strict = row > col
    eye_cat = jnp.where(row == col, 1.0, 0.0).astype(F32)
    head_mask = [(lane >= h * chunk) & (lane < (h + 1) * chunk) for h in range(GDN_HEADS)]
    nar_lane = lax.broadcasted_iota(jnp.int32, (chunk, NARROW_W), 1)
    ones_cc = jnp.ones((chunk, chunk), BF16)
    n_sq = chunk.bit_length() - 2

    def mm_cat(l_cat, r_cat):
        bd = jnp.concatenate([jnp.where(m, r_cat, 0.0) for m in head_mask], axis=0)
        return _dot3(l_cat, bd)

    def neumann_step(t_cat, pw, update_t, update_pw):
        return (t_cat + mm_cat(t_cat, pw) if update_t else t_cat), (mm_cat(pw, pw) if update_pw else pw)

    col1 = lambda x, h: x[:, h * chunk:h * chunk + 1]

    def front(c):
        r = pl.ds(pl.multiple_of(c * chunk, chunk), chunk)
        before = pl.ds(pl.multiple_of(jnp.maximum(c * chunk - TAIL, 0), TAIL), TAIL)
        qkv = []
        for i, ref in enumerate((q_ref, k_ref, v_ref)):
            cols = slice(i * GDN_QKV, (i + 1) * GDN_QKV)
            tail = jnp.where(c == 0, tail_ref[0, :, cols], ref[before, :])
            qkv.append(_silu(_causal_conv(ref[r, :], tail, cw_ref.at[:, cols])))
        q, k, v = qkv
        nar = nar_ref[r, :]
        beta = jax.nn.sigmoid(nar)
        g = _select_rows(tri_ref[...], a_ref[...] * _softplus(nar + dtb_ref[...]))
        bg = _select_cols(jnp.where(nar_lane < NARROW_A, beta, g), sel_ref[...])
        b_c, g_c = bg[:, :cat], bg[:, cat:]
        g_r = _select_rows(ones_cc, g_c * eye_cat)
        decay = jnp.where(causal, jnp.exp(jnp.where(causal, g_c - g_r, 0.0)), 0.0)
        e_in = jnp.exp(g_c)
        e_out = jnp.exp(g_c[chunk - 1:chunk, :] - g_c)
        e_all = jnp.exp(g_c[chunk - 1:chunk, :])

        qn = [_l2_rows(q[:, h * GDN_DK:(h + 1) * GDN_DK]) * (GDN_DK ** -0.5) for h in range(GDN_HEADS)]
        kn = [_l2_rows(k[:, h * GDN_DK:(h + 1) * GDN_DK]) for h in range(GDN_HEADS)]
        vh = [v[:, h * GDN_DV:(h + 1) * GDN_DV] for h in range(GDN_HEADS)]
        kk, qk = [], []
        for p in range(GDN_HEADS // 2):
            h0, h1 = 2 * p, 2 * p + 1
            rhs = _two_block_diag(kn[h0], kn[h1]).astype(BF16)
            kk.append(_dot_nt(jnp.concatenate([kn[h0], kn[h1]], axis=1).astype(BF16), rhs))
            qk.append(_dot_nt(jnp.concatenate([qn[h0], qn[h1]], axis=1).astype(BF16), rhs))
        kk = jnp.concatenate(kk, axis=1)
        sc_scr[r, :] = (jnp.concatenate(qk, axis=1) * decay).astype(BF16)
        ea_scr[c] = jnp.broadcast_to(e_all, (TAIL, cat))
        for h in range(GDN_HEADS):
            hc = slice(h * GDN_DV, (h + 1) * GDN_DV)
            qd_scr[r, hc] = (qn[h] * col1(e_in, h)).astype(BF16)
            kd_scr[r, hc] = (kn[h] * col1(e_out, h)).astype(BF16)
        a_cat = jnp.where(strict, b_c * kk * decay, 0.0)
        rhs = [_two_block_diag(*[jnp.concatenate(
            [vh[h] * col1(b_c, h), kn[h] * (col1(b_c, h) * col1(e_in, h))], axis=1) for h in (2 * p, 2 * p + 1)])
            for p in range(GDN_HEADS // 2)]
        return r, a_cat, rhs

    def back(r, t_cat, rhs):
        for p in range(GDN_HEADS // 2):
            uw = _dot3(t_cat[:, p * 2 * chunk:(p + 1) * 2 * chunk], rhs[p])
            for i, h in enumerate((2 * p, 2 * p + 1)):
                hc = slice(h * GDN_DV, (h + 1) * GDN_DV)
                u_scr[r, hc] = uw[:, (2 * i) * GDN_DV:(2 * i + 1) * GDN_DV]
                w_scr[r, hc] = uw[:, (2 * i + 1) * GDN_DV:(2 * i + 2) * GDN_DV].astype(BF16)

    def prepare_group(i, carry):
        rs, ts, rhss = [], [], []
        for j in range(unroll):
            r, a_cat, rhs = front(i * unroll + j)
            rs.append(r)
            ts.append((eye_cat - a_cat, a_cat))
            rhss.append(rhs)
        for step in range(n_sq + 1):
            ts = [neumann_step(t, pw, update_t=step > 0, update_pw=step < n_sq) for t, pw in ts]
        for r, (t_cat, _), rhs in zip(rs, ts, rhss):
            back(r, t_cat, rhs)
        return carry

    lax.fori_loop(0, n_chunks // unroll, prepare_group, 0)

    def recur(c, carry):
        r = pl.ds(pl.multiple_of(c * chunk, chunk), chunk)
        e_all = ea_scr[c][0:1, :]
        for p in range(GDN_HEADS // 2):
            heads = (2 * p, 2 * p + 1)
            v_new, q_s = [], []
            for h in heads:
                hc = slice(h * GDN_DV, (h + 1) * GDN_DV)
                wq = jnp.concatenate([w_scr[r, hc], qd_scr[r, hc]], axis=0)
                ws = jnp.dot(wq, s_ref[0, h].astype(BF16), preferred_element_type=F32)
                v_new.append(u_scr[r, hc] - ws[:chunk])
                q_s.append(ws[chunk:])
            intra = jnp.dot(sc_scr[r, p * 2 * chunk:(p + 1) * 2 * chunk],
                            _two_block_diag(*v_new).astype(BF16), preferred_element_type=F32)
            for i, h in enumerate(heads):
                hc = slice(h * GDN_DV, (h + 1) * GDN_DV)
                o = q_s[i] + intra[:, i * GDN_DV:(i + 1) * GDN_DV]
                s_ref[0, h] = s_ref[0, h] * col1(e_all, h) + _dot_tn(kd_scr[r, hc], v_new[i].astype(BF16))
                o = o * lax.rsqrt(jnp.mean(o * o, axis=-1, keepdims=True) + EPS) * nw_ref[...]
                o_ref[r, hc] = (o * _silu(z_ref[r, hc])).astype(o_ref.dtype)
        return carry

    lax.fori_loop(0, n_chunks, recur, 0)
    last = pl.ds(q_ref.shape[0] - TAIL, TAIL)
    for i, ref in enumerate((q_ref, k_ref, v_ref)):
        tail_ref[0, :, i * GDN_QKV:(i + 1) * GDN_QKV] = ref[last, :]


def _gdn_params(lw, chunk):
    cat = GDN_HEADS * chunk
    lanes = np.arange(cat) // chunk
    sel = np.zeros((128, 2 * cat), np.float32)
    sel[NARROW_B + lanes, np.arange(cat)] = 1.0
    sel[NARROW_A + lanes, cat + np.arange(cat)] = 1.0
    return (lw['gdn_conv_w'].astype(F32), _lane_row(lw['gdn_dt_bias'], NARROW_A),
            _lane_row(-jnp.exp(lw['gdn_a_log'].astype(F32)), NARROW_A),
            lw['gdn_norm_w'].astype(F32).reshape(1, GDN_DV), jnp.tril(jnp.ones((chunk, chunk), BF16)),
            jnp.asarray(sel, BF16))


def gdn_prompt(y, nar, mixed, n_seq, seq_len, lw):
    chunk = min(GDN_CHUNK, seq_len)
    rows = min(GDN_ROWS, seq_len)
    n_chunks = rows // chunk
    unroll = math.gcd(GDN_UNROLL, n_chunks)
    params = _gdn_params(lw, chunk)
    nblk = seq_len // rows
    tok = lambda s: pl.BlockSpec((rows, GDN_QKV), lambda b, j: (b * nblk + j, PK_OFF[s] // GDN_QKV))
    const = lambda shape: pl.BlockSpec(shape, lambda b, j: (0,) * len(shape))
    n_in = 5 + len(params)
    mixed, s, tail = pl.pallas_call(
        functools.partial(_gdn_prompt_kernel, chunk=chunk, unroll=unroll),
        grid=(n_seq, nblk),
        in_specs=[tok(1), tok(2), tok(3), tok(4), pl.BlockSpec((rows, NARROW_W), lambda b, j: (b * nblk + j, 0))]
                 + [const(p.shape) for p in params] + [_ANY_SPEC],
        out_specs=[pl.BlockSpec((rows, GDN_QKV), lambda b, j: (b * nblk + j, MIX_GDN)),
                   pl.BlockSpec((1, GDN_HEADS, GDN_DK, GDN_DV), lambda b, j: (b, 0, 0, 0)),
                   pl.BlockSpec((1, TAIL, GDN_CONV_DIM), lambda b, j: (b, 0, 0))],
        out_shape=[jax.ShapeDtypeStruct(mixed.shape, mixed.dtype),
                   jax.ShapeDtypeStruct((n_seq, GDN_HEADS, GDN_DK, GDN_DV), F32),
                   jax.ShapeDtypeStruct((n_seq, TAIL, GDN_CONV_DIM), F32)],
        input_output_aliases={n_in: 0},
        scratch_shapes=[pltpu.VMEM((rows, GDN_QKV), F32), pltpu.VMEM((rows, GDN_QKV), BF16),
                        pltpu.VMEM((rows, GDN_QKV), BF16), pltpu.VMEM((rows, GDN_QKV), BF16),
                        pltpu.VMEM((rows, GDN_HEADS * chunk), BF16),
                        pltpu.VMEM((n_chunks, TAIL, GDN_HEADS * chunk), F32)],
        compiler_params=pltpu.CompilerParams(
            dimension_semantics=("arbitrary", "arbitrary"), vmem_limit_bytes=V7X_VMEM_LIMIT),
        name="gdn_prompt",
    )(y, y, y, y, nar, *params, mixed)
    return mixed, tail[:, TAIL - (CONV_W - 1):], s


def _conv_steps(buf_ref, raw_ref, w_ref, cols):
    n_t = raw_ref.shape[0]
    xx = [buf_ref[j, :, cols] for j in range(CONV_W - 1)] + [raw_ref[t] for t in range(n_t)]
    w = w_ref[:, cols]
    out = []
    for t in range(n_t):
        y = xx[t] * w[0:1, :]
        for j in range(1, CONV_W):
            y = y + xx[t + j] * w[j:j + 1, :]
        out.append(y)
    return out, xx[n_t:]


def _ssd_sample_kernel(z_ref, xbc_ref, nar_ref, buf_ref, s0_ref, cw_ref, cb_ref, dtb_ref, a_ref, dsk_ref, nw_ref,
                       mix_ref, sall_ref, o_ref, s_ref, nbuf_ref, qd_scr, kd_scr, xs_scr, ga_scr, acc_scr, lq, lk, lx):
    del mix_ref, sall_ref
    n_t, n_b = z_ref.shape[0], z_ref.shape[1]
    rep = SSD_HEADS // SSD_NGROUPS
    conv, tail = _conv_steps(buf_ref, xbc_ref, cw_ref, slice(0, SSD_CONV_DIM))
    for j in range(CONV_W - 1):
        nbuf_ref[j] = tail[j]
    xbc = [_silu(c + cb_ref[...]) for c in conv]
    dt, g = [], []
    for t in range(n_t):
        dt.append(_softplus(nar_ref[t] + dtb_ref[...]))
        la = dt[t] * a_ref[...]
        g.append(la if t == 0 else g[t - 1] + la)
    for grp in range(SSD_NGROUPS):
        bm = [x[:, SSD_INNER + grp * SSD_STATE:SSD_INNER + (grp + 1) * SSD_STATE] for x in xbc]
        cm = [x[:, SSD_INNER + SSD_BC + grp * SSD_STATE:SSD_INNER + SSD_BC + (grp + 1) * SSD_STATE] for x in xbc]
        sc = [[jnp.sum(cm[i] * bm[j], axis=-1, keepdims=True) for j in range(i + 1)] for i in range(n_t)]
        for h in range(grp * rep, (grp + 1) * rep):
            ln = NARROW_DT + h
            hc = slice(h * SSD_HEADDIM, (h + 1) * SSD_HEADDIM)
            gc = [x[:, ln:ln + 1] for x in g]
            dc = [x[:, ln:ln + 1] for x in dt]
            xs = [x[:, hc] for x in xbc]
            for t in range(n_t):
                qd_scr[h, t] = cm[t] * jnp.exp(gc[t])
                kd_scr[h, t] = bm[t] * (dc[t] * jnp.exp(gc[n_t - 1] - gc[t]))
                xs_scr[h, t] = xs[t]
                acc = None
                for j in range(t + 1):
                    term = (sc[t][j] * dc[j] * jnp.exp(gc[t] - gc[j])) * xs[j]
                    acc = term if acc is None else acc + term
                acc_scr[t, :, hc] = acc
            ga_scr[h] = jnp.broadcast_to(jnp.exp(gc[n_t - 1]), (n_b, SSD_STATE))

    for tile in (lq, lk, lx):
        tile[...] = jnp.zeros_like(tile)

    def per_seq(b, carry):
        row = pl.ds(b, 1)
        for h in range(SSD_HEADS):
            hc = slice(h * SSD_HEADDIM, (h + 1) * SSD_HEADDIM)
            for t in range(n_t):
                lq[h, t:t + 1, :] = qd_scr[h, t, row, :]
                lk[h, t:t + 1, :] = kd_scr[h, t, row, :]
                lx[h, t:t + 1, :] = xs_scr[h, t, row, :]
            s0 = s0_ref[b, h]
            inter = _dot_nt(lq[h].astype(BF16), s0.astype(BF16))
            s_ref[b, h] = s0 * ga_scr[h, row, :] + _dot_tn(lx[h].astype(BF16), lk[h].astype(BF16))
            for t in range(n_t):
                xs_scr[h, t, row, :] = inter[t:t + 1, :]
        return carry

    lax.fori_loop(0, n_b, per_seq, 0)

    gw = SSD_INNER // SSD_NGROUPS
    for t in range(n_t):
        for h in range(SSD_HEADS):
            hc = slice(h * SSD_HEADDIM, (h + 1) * SSD_HEADDIM)
            acc_scr[t, :, hc] = acc_scr[t, :, hc] + xs_scr[h, t]
        y = (acc_scr[t] + xbc[t][:, :SSD_INNER] * dsk_ref[...]) * _silu(z_ref[t])
        yn = [y[:, i * gw:(i + 1) * gw] * lax.rsqrt(
            jnp.mean(y[:, i * gw:(i + 1) * gw] * y[:, i * gw:(i + 1) * gw], axis=-1, keepdims=True) + EPS)
            for i in range(SSD_NGROUPS)]
        o_ref[t] = (jnp.concatenate(yn, axis=1) * nw_ref[...]).astype(o_ref.dtype)


def ssd_sample(y3, nar3, mixed3, buf0, states, new_states, layer, t0, n_t, lw):
    n_b = y3.shape[1]
    tb = _sample_block(t0, n_t)
    params = _ssd_params(lw, 1)[:-1]
    tok = lambda s, w: pl.BlockSpec((n_t, SAMPLE_BB, w), lambda i: (tb, i, PK_OFF[s] // w))
    bufspec = pl.BlockSpec((CONV_W - 1, SAMPLE_BB, SSD_CONV_DIM), lambda i: (0, i, 0))
    state = pl.BlockSpec((None, SAMPLE_BB, SSD_HEADS, SSD_HEADDIM, SSD_STATE), lambda i: (layer, i, 0, 0, 0))
    per_tok = lambda w: pltpu.VMEM((SSD_HEADS, n_t, SAMPLE_BB, w), F32)
    tile = lambda w: pltpu.VMEM((SSD_HEADS, LHS_ROWS, w), F32)
    n_in = 5 + len(params)
    mixed3, s, nbuf = pl.pallas_call(
        _ssd_sample_kernel,
        grid=(n_b // SAMPLE_BB,),
        in_specs=[tok(7, SSD_INNER), tok(8, SSD_CONV_DIM),
                  pl.BlockSpec((n_t, SAMPLE_BB, NARROW_W), lambda i: (tb, i, 0)), bufspec, state]
                 + [_const_spec(p.shape) for p in params] + [_ANY_SPEC, _ANY_SPEC],
        out_specs=[pl.BlockSpec((n_t, SAMPLE_BB, SSD_INNER), lambda i: (tb, i, MIX_SSD)), state, bufspec],
        out_shape=[jax.ShapeDtypeStruct(mixed3.shape, mixed3.dtype), jax.ShapeDtypeStruct(states.shape, F32),
                   jax.ShapeDtypeStruct((CONV_W - 1, n_b, SSD_CONV_DIM), F32)],
        input_output_aliases={n_in: 0, n_in + 1: 1},
        scratch_shapes=[per_tok(SSD_STATE), per_tok(SSD_STATE), per_tok(SSD_HEADDIM),
                        pltpu.VMEM((SSD_HEADS, SAMPLE_BB, SSD_STATE), F32),
                        pltpu.VMEM((n_t, SAMPLE_BB, SSD_INNER), F32),
                        tile(SSD_STATE), tile(SSD_STATE), tile(SSD_HEADDIM)],
        compiler_params=pltpu.CompilerParams(
            dimension_semantics=("arbitrary",), vmem_limit_bytes=V7X_VMEM_LIMIT),
        name="ssd_sample",
    )(y3, y3, nar3, jnp.swapaxes(buf0, 0, 1), states, *params, mixed3, new_states)
    return mixed3, jnp.swapaxes(nbuf, 0, 1), s


def _gdn_sample_kernel(q_ref, k_ref, v_ref, z_ref, nar_ref, buf_ref, s0_ref, cw_ref, dtb_ref, a_ref, nw_ref,
                       mix_ref, sall_ref, o_ref, s_ref, nbuf_ref, w_scr, qd_scr, kd_scr, u_scr, ga_scr, lwq, lk, lu):
    del mix_ref, sall_ref
    n_t, n_b = q_ref.shape[0], q_ref.shape[1]
    qkv = []
    for i, ref in enumerate((q_ref, k_ref, v_ref)):
        cols = slice(i * GDN_QKV, (i + 1) * GDN_QKV)
        conv, tail = _conv_steps(buf_ref, ref, cw_ref, cols)
        for j in range(CONV_W - 1):
            nbuf_ref[j, :, cols] = tail[j]
        qkv.append([_silu(c) for c in conv])
    beta, g = [], []
    for t in range(n_t):
        nar = nar_ref[t]
        beta.append(jax.nn.sigmoid(nar))
        gl = a_ref[...] * _softplus(nar + dtb_ref[...])
        g.append(gl if t == 0 else g[t - 1] + gl)

    scores = []
    for h in range(GDN_HEADS):
        hc = slice(h * GDN_DK, (h + 1) * GDN_DK)
        qn = [_l2_rows(x[:, hc]) * (GDN_DK ** -0.5) for x in qkv[0]]
        kn = [_l2_rows(x[:, hc]) for x in qkv[1]]
        vh = [x[:, hc] for x in qkv[2]]
        bc = [x[:, NARROW_B + h:NARROW_B + h + 1] for x in beta]
        gc = [x[:, NARROW_A + h:NARROW_A + h + 1] for x in g]
        us, ws = [], []
        for i in range(n_t):
            u = vh[i] * bc[i]
            w = kn[i] * (bc[i] * jnp.exp(gc[i]))
            for j in range(i):
                a_ij = bc[i] * jnp.sum(kn[i] * kn[j], axis=-1, keepdims=True) * jnp.exp(gc[i] - gc[j])
                u = u - a_ij * us[j]
                w = w - a_ij * ws[j]
            us.append(u)
            ws.append(w)
            u_scr[h, i] = u
            w_scr[h, i] = w
            qd_scr[h, i] = qn[i] * jnp.exp(gc[i])
            kd_scr[h, i] = kn[i] * jnp.exp(gc[n_t - 1] - gc[i])
        ga_scr[h] = jnp.broadcast_to(jnp.exp(gc[n_t - 1]), (n_b, GDN_DV))
        scores.append([[jnp.sum(qn[i] * kn[j], axis=-1, keepdims=True) * jnp.exp(gc[i] - gc[j])
                        for j in range(i + 1)] for i in range(n_t)])

    for tile in (lwq, lk, lu):
        tile[...] = jnp.zeros_like(tile)

    def per_pair(i, carry):
        seqs = [(s, 2 * i + s, pl.ds(2 * i + s, 1)) for s in range(2)]
        for s, b, row in seqs:
            for h in range(GDN_HEADS):
                n = s * GDN_HEADS + h
                for t in range(n_t):
                    lwq[n, t:t + 1, :] = w_scr[h, t, row, :]
                    lwq[n, LHS_ROWS + t:LHS_ROWS + t + 1, :] = qd_scr[h, t, row, :]
                    lk[n, t:t + 1, :] = kd_scr[h, t, row, :]
                    lu[n, t:t + 1, :] = u_scr[h, t, row, :]
        results = []
        for s, b, row in seqs:
            for h in range(GDN_HEADS):
                n = s * GDN_HEADS + h
                s0 = s0_ref[b, h]
                wq_s = jnp.dot(lwq[n].astype(BF16), s0.astype(BF16), preferred_element_type=F32)
                v_new = lu[n] - wq_s[:LHS_ROWS]
                s_ref[b, h] = s0 * ga_scr[h, row, :] + _dot_tn(lk[n].astype(BF16), v_new.astype(BF16))
                results.append((h, row, v_new, wq_s))
        for h, row, v_new, wq_s in results:
            for t in range(n_t):
                u_scr[h, t, row, :] = v_new[t:t + 1, :]
                qd_scr[h, t, row, :] = wq_s[LHS_ROWS + t:LHS_ROWS + t + 1, :]
        return carry

    lax.fori_loop(0, n_b // 2, per_pair, 0)

    for h in range(GDN_HEADS):
        hc = slice(h * GDN_DV, (h + 1) * GDN_DV)
        for i in range(n_t):
            o = qd_scr[h, i]
            for j in range(i + 1):
                o = o + scores[h][i][j] * u_scr[h, j]
            o = o * lax.rsqrt(jnp.mean(o * o, axis=-1, keepdims=True) + EPS) * nw_ref[...]
            o_ref[i, :, hc] = (o * _silu(z_ref[i, :, hc])).astype(o_ref.dtype)


def gdn_sample(y3, nar3, mixed3, buf0, states, new_states, layer, t0, n_t, lw):
    n_b = y3.shape[1]
    tb = _sample_block(t0, n_t)
    params = _gdn_params(lw, 1)[:4]
    tok = lambda s: pl.BlockSpec((n_t, SAMPLE_BB, GDN_QKV), lambda i: (tb, i, PK_OFF[s] // GDN_QKV))
    bufspec = pl.BlockSpec((CONV_W - 1, SAMPLE_BB, GDN_CONV_DIM), lambda i: (0, i, 0))
    state = pl.BlockSpec((None, SAMPLE_BB, GDN_HEADS, GDN_DK, GDN_DV), lambda i: (layer, i, 0, 0, 0))
    per_tok = pltpu.VMEM((GDN_HEADS, n_t, SAMPLE_BB, GDN_DK), F32)
    tile = lambda rows: pltpu.VMEM((2 * GDN_HEADS, rows, GDN_DK), F32)
    n_in = 7 + len(params)
    mixed3, s, nbuf = pl.pallas_call(
        _gdn_sample_kernel,
        grid=(n_b // SAMPLE_BB,),
        in_specs=[tok(1), tok(2), tok(3), tok(4),
                  pl.BlockSpec((n_t, SAMPLE_BB, NARROW_W), lambda i: (tb, i, 0)), bufspec, state]
                 + [_const_spec(p.shape) for p in params] + [_ANY_SPEC, _ANY_SPEC],
        out_specs=[pl.BlockSpec((n_t, SAMPLE_BB, GDN_QKV), lambda i: (tb, i, MIX_GDN)), state, bufspec],
        out_shape=[jax.ShapeDtypeStruct(mixed3.shape, mixed3.dtype), jax.ShapeDtypeStruct(states.shape, F32),
                   jax.ShapeDtypeStruct((CONV_W - 1, n_b, GDN_CONV_DIM), F32)],
        input_output_aliases={n_in: 0, n_in + 1: 1},
        scratch_shapes=[per_tok, per_tok, per_tok, per_tok, pltpu.VMEM((GDN_HEADS, SAMPLE_BB, GDN_DV), F32),
                        tile(2 * LHS_ROWS), tile(LHS_ROWS), tile(LHS_ROWS)],
        compiler_params=pltpu.CompilerParams(
            dimension_semantics=("arbitrary",), vmem_limit_bytes=V7X_VMEM_LIMIT),
        name="gdn_sample",
    )(y3, y3, y3, y3, nar3, jnp.swapaxes(buf0, 0, 1), states, *params, mixed3, new_states)
    return mixed3, jnp.swapaxes(nbuf, 0, 1), s


TM = 1088
TM_SPLIT = 512


def kernel(x_prompt, x_sample, p_prompt, p_sample, state_s5_re, state_s5_im, state_gdn, state_gdn_conv, state_ssd, state_ssd_conv, state_ret, norm_mix, w_in, s5_a_re, s5_a_im, s5_b_re, s5_b_im, s5_c_re, s5_c_im, s5_d, s5_log_step, s5_w_glu, s5_b_glu, gdn_conv_w, gdn_a_log, gdn_dt_bias, gdn_norm_w, ssd_conv_w, ssd_conv_b, ssd_dt_bias, ssd_a_log, ssd_d, ssd_norm_w, ret_ln_w, ret_ln_b, w_out, norm_ffn, w_ffn_in, w_ffn_out, norm_ple, w_ple_gate, w_ple_proj, norm_final):
    bp, lp, d = x_prompt.shape
    bs, ls, _ = x_sample.shape
    np_tok = bp * lp
    ns_tok = bs * ls
    n_tok = np_tok + ns_tok

    mixer_w = dict(
        s5_a_re=s5_a_re, s5_a_im=s5_a_im, s5_b_re=s5_b_re, s5_b_im=s5_b_im, s5_c_re=s5_c_re, s5_c_im=s5_c_im,
        s5_d=s5_d, s5_log_step=s5_log_step, s5_w_glu=s5_w_glu, s5_b_glu=s5_b_glu,
        gdn_conv_w=gdn_conv_w, gdn_a_log=gdn_a_log, gdn_dt_bias=gdn_dt_bias, gdn_norm_w=gdn_norm_w,
        ssd_conv_w=ssd_conv_w, ssd_conv_b=ssd_conv_b, ssd_dt_bias=ssd_dt_bias, ssd_a_log=ssd_a_log,
        ssd_d=ssd_d, ssd_norm_w=ssd_norm_w, ret_ln_w=ret_ln_w, ret_ln_b=ret_ln_b)

    assert bp == 4, "the prompt S5 kernel packs two time steps of four sequences per vreg"

    def tm_rows(t):
        t = jnp.swapaxes(t, -3, -2)
        return t.reshape(t.shape[:-3] + (t.shape[-3] * t.shape[-2], t.shape[-1]))

    def bm_seqs(t, b):
        return jnp.swapaxes(t.reshape(t.shape[0] // b, b, t.shape[1]), 0, 1)

    w_out_b, w_ffn_in_b, w_ffn_out_b = (w.astype(BF16) for w in (w_out, w_ffn_in, w_ffn_out))
    w_gate_b, w_proj_b = w_ple_gate.astype(BF16), w_ple_proj.astype(BF16)
    w_in_wide, w_in_narrow = _split_w_in(w_in)
    pp = p_prompt.reshape(DEPTH, np_tok, PLE_DIM)
    ps = tm_rows(p_sample)

    h = jnp.concatenate([x_prompt.reshape(np_tok, d), tm_rows(x_sample)], axis=0)
    ssd_t = jnp.swapaxes(state_ssd, -1, -2)
    new_gdn, new_ssd_t, new_ret = (uninitialized(s.shape, F32) for s in (state_gdn, ssd_t, state_ret))
    new_p, new_s = [], []
    for i in range(DEPTH):
        lw = {k: v[i] for k, v in mixer_w.items()}
        y, nar = in_projection(h, norm_mix[i], w_in_wide, w_in_narrow, i, tm=TM)
        y3 = y.reshape(n_tok // bs, bs, PK_TOTAL)
        nar3 = nar.reshape(n_tok // bs, bs, NARROW_W)
        t0 = np_tok // bs

        tb = _s5_tables(lw)
        a_p, st5_p = s5_prompt(tm_rows(y[:np_tok, :S5_CH].reshape(bp, lp, S5_CH)), tb, rows=512)

        mixed = uninitialized((n_tok, d), BF16)
        mixed, gdn_buf_p, gdn_s_p = gdn_prompt(y, nar, mixed, bp, lp, lw)
        mixed, ssd_buf_p, ssd_s_p = ssd_prompt(y, nar, mixed, bp, lp, lw)
        mixed, ret_s_p = ret_prompt(y, mixed, bp, lp, lw)
        mixed, st5_s = s5_sample(y, mixed, _s5_state_to_lanes(state_s5_re[i], state_s5_im[i]), np_tok, ns_tok, tb)
        mixed3 = mixed.reshape(n_tok // bs, bs, d)
        mixed3, gdn_buf_s, new_gdn = gdn_sample(y3, nar3, mixed3, state_gdn_conv[i], state_gdn, new_gdn, i, t0, ls, lw)
        mixed3, ssd_buf_s, new_ssd_t = ssd_sample(y3, nar3, mixed3, state_ssd_conv[i], ssd_t, new_ssd_t, i, t0, ls, lw)
        mixed3, new_ret = ret_sample(y3, mixed3, state_ret, new_ret, i, t0, ls, PAST_LEN, lw)
        mixed = lax.dynamic_update_slice(mixed3.reshape(n_tok, d), bm_seqs(a_p, bp).reshape(np_tok, S5_CH), (0, 0))
        new_p.append(_s5_lanes_to_state(st5_p[:bp]) + (gdn_s_p, gdn_buf_p, ssd_s_p, ssd_buf_p, ret_s_p))
        new_s.append(_s5_lanes_to_state(st5_s) + (gdn_buf_s, ssd_buf_s))

        h = matmul_residual(mixed, w_out_b, i, h, tm=TM)
        h = ffn_residual(h, norm_ffn[i], w_ffn_in_b, w_ffn_out_b, i, tm=TM // 2, th=512)
        if i + 1 < DEPTH:
            h = ple_residual(h, norm_ple[i], w_gate_b, pp, ps, w_proj_b, i, tm=TM_SPLIT)
        else:
            y_p, y_s = ple_residual(h, norm_ple[i], w_gate_b, pp, ps, w_proj_b, i, tm=TM_SPLIT, nf=norm_final)

    stack_p = [jnp.stack([st[j] for st in new_p]) for j in range(7)]
    s5_re_s, s5_im_s, gdn_buf_s, ssd_buf_s = (jnp.stack([st[j] for st in new_s]) for j in range(4))
    return (y_p.reshape(bp, lp, d), bm_seqs(y_s, bs), *stack_p,
            s5_re_s, s5_im_s, new_gdn, gdn_buf_s, jnp.swapaxes(new_ssd_t, -1, -2), ssd_buf_s, new_ret)
```

```python
import functools
import math

import jax
import jax.numpy as jnp
import numpy as np
from jax import lax
from jax.experimental import pallas as pl
from jax.experimental.pallas import tpu as pltpu

F32 = jnp.float32
BF16 = jnp.bfloat16

D_MODEL = 2048
DEPTH = 2
GROUP_WIDTH = D_MODEL // 4
CONV_W = 4
EPS = 1e-6
PLE_DIM = 256
FFN_HIDDEN = ((8 * D_MODEL + 3 * 256 - 1) // (3 * 256)) * 256

S5_CH = GROUP_WIDTH
S5_GROUP_CH = 16
S5_GROUPS = S5_CH // S5_GROUP_CH
S5_STATE = 64

GDN_HEADS = 4
GDN_DK = GROUP_WIDTH // GDN_HEADS
GDN_DV = GROUP_WIDTH // GDN_HEADS
GDN_CHUNK = 64
GDN_CONV_DIM = 2 * GDN_HEADS * GDN_DK + GDN_HEADS * GDN_DV

SSD_INNER = GROUP_WIDTH
SSD_HEADDIM = 64
SSD_HEADS = SSD_INNER // SSD_HEADDIM
SSD_NGROUPS = 2
SSD_STATE = 128
SSD_CHUNK = 128
SSD_CONV_DIM = SSD_INNER + 2 * SSD_NGROUPS * SSD_STATE

RET_HEADS = 4
RET_DK = GROUP_WIDTH // RET_HEADS
RET_DV = GROUP_WIDTH // RET_HEADS
RET_CHUNK = 128
ROPE_BASE = 10000.0
PAST_LEN = 16384

IN_SIZES = (
    S5_CH,
    GDN_HEADS * GDN_DK, GDN_HEADS * GDN_DK, GDN_HEADS * GDN_DV, GDN_HEADS * GDN_DV, GDN_HEADS, GDN_HEADS,
    SSD_INNER, SSD_CONV_DIM, SSD_HEADS,
    RET_HEADS * RET_DK, RET_HEADS * RET_DK, RET_HEADS * RET_DV, RET_HEADS * RET_DV,
)
IN_OFFS = tuple(int(v) for v in np.cumsum((0,) + IN_SIZES))

_REGIONS = ((0, 1, 2, 3, 4), (7, 8), (10, 11, 12, 13))
_NARROW = (5, 6, 9)
PK_OFF = {}
_o = 0
for _reg in _REGIONS:
    for _s in _reg:
        PK_OFF[_s] = _o
        _o += IN_SIZES[_s]
PK_TOTAL = _o
IN_TN = 512
REGION_TILES = tuple(sum(IN_SIZES[s] for s in reg) // IN_TN for reg in _REGIONS)
NARROW_W = 128

V7X_VMEM_LIMIT = 58 * 1024 * 1024
RECURRENCE_UNROLL = 4


def _split_w_in(w_in):
    wide = [w_in[:, IN_OFFS[reg[0]]:IN_OFFS[reg[-1] + 1]].astype(BF16) for reg in _REGIONS]
    nar = jnp.concatenate([w_in[:, IN_OFFS[s]:IN_OFFS[s + 1]] for s in _NARROW], axis=1)
    nar = jnp.pad(nar, ((0, 0), (0, NARROW_W - nar.shape[1]))).astype(BF16)
    return wide, nar


def _rms_rows(x, nw):
    ms = jnp.mean(x * x, axis=-1, keepdims=True)
    return x * lax.rsqrt(ms + EPS) * nw


def _in_proj_kernel(x_ref, nw_ref, wa_ref, wb_ref, wc_ref, wn_ref, o_ref, nar_ref, xn_ref):
    j = pl.program_id(1)

    @pl.when(j == 0)
    def _():
        xn_ref[...] = _rms_rows(x_ref[...], nw_ref[...]).astype(BF16)
        nar_ref[...] = jnp.dot(xn_ref[...], wn_ref[...], preferred_element_type=F32)

    first = 0
    for w_ref, n_tiles in zip((wa_ref, wb_ref, wc_ref), REGION_TILES):
        @pl.when((j >= first) & (j < first + n_tiles))
        def _(w_ref=w_ref):
            o_ref[...] = jnp.dot(xn_ref[...], w_ref[...], preferred_element_type=F32)
        first += n_tiles


def in_projection(x, nw, w_in, *, tm):
    m, k = x.shape
    wide, nar = _split_w_in(w_in)
    starts = [sum(REGION_TILES[:r]) for r in range(len(REGION_TILES))]

    def region_spec(r):
        return pl.BlockSpec((k, IN_TN), lambda i, j: (0, jnp.clip(j - starts[r], 0, REGION_TILES[r] - 1)))

    return pl.pallas_call(
        _in_proj_kernel,
        grid=(m // tm, PK_TOTAL // IN_TN),
        in_specs=[
            pl.BlockSpec((tm, k), lambda i, j: (i, 0)),
            pl.BlockSpec((1, k), lambda i, j: (0, 0)),
            region_spec(0), region_spec(1), region_spec(2),
            pl.BlockSpec((k, NARROW_W), lambda i, j: (0, 0)),
        ],
        out_specs=[pl.BlockSpec((tm, IN_TN), lambda i, j: (i, j)),
                   pl.BlockSpec((tm, NARROW_W), lambda i, j: (i, 0))],
        out_shape=[jax.ShapeDtypeStruct((m, PK_TOTAL), F32), jax.ShapeDtypeStruct((m, NARROW_W), F32)],
        scratch_shapes=[pltpu.VMEM((tm, k), BF16)],
        compiler_params=pltpu.CompilerParams(
            dimension_semantics=("arbitrary", "arbitrary"), vmem_limit_bytes=V7X_VMEM_LIMIT),
        name="in_projection",
    )(x, nw.reshape(1, k), *wide, nar)


def _mm_res_kernel(a_ref, w_ref, h_ref, o_ref):
    o_ref[...] = h_ref[...] + jnp.dot(a_ref[...], w_ref[...], preferred_element_type=F32)


def matmul_residual(a, w, layer, h, *, tm):
    m, k = a.shape
    n = w.shape[2]
    return pl.pallas_call(
        _mm_res_kernel,
        grid=(m // tm,),
        in_specs=[
            pl.BlockSpec((tm, k), lambda i: (i, 0)),
            pl.BlockSpec((None, k, n), lambda i: (layer, 0, 0)),
            pl.BlockSpec((tm, n), lambda i: (i, 0)),
        ],
        out_specs=pl.BlockSpec((tm, n), lambda i: (i, 0)),
        out_shape=jax.ShapeDtypeStruct((m, n), F32),
        compiler_params=pltpu.CompilerParams(
            dimension_semantics=("arbitrary",), vmem_limit_bytes=V7X_VMEM_LIMIT),
        name="matmul_residual",
    )(a, w, h)


def _silu(x):
    return x * jax.nn.sigmoid(x)


def _ffn_kernel(h_ref, nw_ref, wg_ref, wu_ref, wo_ref, o_ref, xn_ref):
    @pl.when(pl.program_id(1) == 0)
    def _():
        h = h_ref[...]
        xn_ref[...] = _rms_rows(h, nw_ref[...]).astype(BF16)
        o_ref[...] = h

    xn = xn_ref[...]
    gate = jnp.dot(xn, wg_ref[...], preferred_element_type=F32)
    up = jnp.dot(xn, wu_ref[...], preferred_element_type=F32)
    act = (_silu(gate) * up).astype(BF16)
    o_ref[...] += jnp.dot(act, wo_ref[...], preferred_element_type=F32)


def ffn_residual(h, nw, w_in, w_out, layer, *, tm, th):
    m, k = h.shape
    hidden = w_out.shape[1]
    nj = hidden // th
    return pl.pallas_call(
        _ffn_kernel,
        grid=(m // tm, nj),
        in_specs=[
            pl.BlockSpec((tm, k), lambda i, j: (i, 0)),
            pl.BlockSpec((1, k), lambda i, j: (0, 0)),
            pl.BlockSpec((None, k, th), lambda i, j: (layer, 0, j)),
            pl.BlockSpec((None, k, th), lambda i, j: (layer, 0, j + nj)),
            pl.BlockSpec((None, th, k), lambda i, j: (layer, j, 0)),
        ],
        out_specs=pl.BlockSpec((tm, k), lambda i, j: (i, 0)),
        out_shape=jax.ShapeDtypeStruct((m, k), F32),
        scratch_shapes=[pltpu.VMEM((tm, k), BF16)],
        compiler_params=pltpu.CompilerParams(
            dimension_semantics=("arbitrary", "arbitrary"), vmem_limit_bytes=V7X_VMEM_LIMIT),
        name="ffn_residual",
    )(h, nw.reshape(1, k), w_in, w_in, w_out)


def _ple_rows(h_ref, nw_ref, wg_ref, pp_ref, ps_ref, wp_ref, n_prompt_tiles):
    h = h_ref[...]
    xn = _rms_rows(h, nw_ref[...]).astype(BF16)
    gate = jax.nn.sigmoid(jnp.dot(xn, wg_ref[...], preferred_element_type=F32))
    p = jnp.where(pl.program_id(0) < n_prompt_tiles, pp_ref[...], ps_ref[...])
    return h + gate * jnp.dot(p.astype(BF16), wp_ref[...], preferred_element_type=F32)


def _ple_kernel(h_ref, nw_ref, wg_ref, pp_ref, ps_ref, wp_ref, o_ref, *, n_prompt_tiles):
    o_ref[...] = _ple_rows(h_ref, nw_ref, wg_ref, pp_ref, ps_ref, wp_ref, n_prompt_tiles)


def _ple_final_kernel(h_ref, nw_ref, wg_ref, pp_ref, ps_ref, wp_ref, nf_ref, yp_ref, ys_ref, *, n_prompt_tiles):
    y = _rms_rows(_ple_rows(h_ref, nw_ref, wg_ref, pp_ref, ps_ref, wp_ref, n_prompt_tiles), nf_ref[...])

    @pl.when(pl.program_id(0) < n_prompt_tiles)
    def _():
        yp_ref[...] = y

    @pl.when(pl.program_id(0) >= n_prompt_tiles)
    def _():
        ys_ref[...] = y


def ple_residual(h, nw, wg, pp, ps, wp, layer, *, tm, nf=None):
    m, k = h.shape
    mp, pd = pp.shape[1], pp.shape[2]
    ms = ps.shape[1]
    assert mp % tm == 0 and ms % tm == 0 and mp + ms == m
    npt, nst = mp // tm, ms // tm
    row = pl.BlockSpec((tm, k), lambda i: (i, 0))
    vec = pl.BlockSpec((1, k), lambda i: (0, 0))
    in_specs = [row, vec,
                pl.BlockSpec((None, k, k), lambda i: (layer, 0, 0)),
                pl.BlockSpec((None, tm, pd), lambda i: (layer, jnp.minimum(i, npt - 1), 0)),
                pl.BlockSpec((None, tm, pd), lambda i: (layer, jnp.clip(i - npt, 0, nst - 1), 0)),
                pl.BlockSpec((None, pd, k), lambda i: (layer, 0, 0))]
    args = [h, nw.reshape(1, k), wg, pp, ps, wp]
    params = pltpu.CompilerParams(dimension_semantics=("arbitrary",), vmem_limit_bytes=V7X_VMEM_LIMIT)
    if nf is None:
        return pl.pallas_call(
            functools.partial(_ple_kernel, n_prompt_tiles=npt), grid=(m // tm,), in_specs=in_specs,
            out_specs=row, out_shape=jax.ShapeDtypeStruct((m, k), F32), compiler_params=params,
            name="ple_residual")(*args)
    return pl.pallas_call(
        functools.partial(_ple_final_kernel, n_prompt_tiles=npt), grid=(m // tm,), in_specs=in_specs + [vec],
        out_specs=[pl.BlockSpec((tm, k), lambda i: (jnp.minimum(i, npt - 1), 0)),
                   pl.BlockSpec((tm, k), lambda i: (jnp.clip(i - npt, 0, nst - 1), 0))],
        out_shape=[jax.ShapeDtypeStruct((mp, k), F32), jax.ShapeDtypeStruct((ms, k), F32)],
        compiler_params=params, name="ple_final")(*args, nf.reshape(1, k))


S5_HALF_CH = S5_CH // 2
S5_HALF_ST = (S5_GROUPS // 2) * S5_STATE
S5_LANES = 4 * S5_HALF_ST
S5_SLAB = 512


def _s5_tables(lw):
    a_re = lw['s5_a_re'].astype(F32)
    a_im = lw['s5_a_im'].astype(F32)
    step = jnp.exp(lw['s5_log_step'].astype(F32))[:, None]
    mag = jnp.exp(a_re * step)
    lam_re = mag * jnp.cos(a_im * step)
    lam_im = mag * jnp.sin(a_im * step)
    den = a_re * a_re + a_im * a_im
    coef_re = ((lam_re - 1.0) * a_re + lam_im * a_im) / den
    coef_im = (lam_im * a_re - (lam_re - 1.0) * a_im) / den
    b_re = lw['s5_b_re'].astype(F32)
    b_im = lw['s5_b_im'].astype(F32)
    bb_re = coef_re[..., None] * b_re - coef_im[..., None] * b_im
    bb_im = coef_re[..., None] * b_im + coef_im[..., None] * b_re
    gh = S5_GROUPS // 2
    eye = jnp.eye(gh, dtype=F32)

    def in_blockdiag(b):
        return jnp.einsum('gnc,gh->gchn', b, eye).reshape(gh * S5_GROUP_CH, gh * S5_STATE)

    def out_blockdiag(c):
        return jnp.einsum('gcn,gh->gnhc', c, eye).reshape(gh * S5_STATE, gh * S5_GROUP_CH)

    c_re = lw['s5_c_re'].astype(F32)
    c_im = lw['s5_c_im'].astype(F32)
    bb = jnp.stack([jnp.concatenate([in_blockdiag(bb_re[h * gh:(h + 1) * gh]),
                                     in_blockdiag(bb_im[h * gh:(h + 1) * gh])], axis=1) for h in range(2)])
    cm = jnp.stack([jnp.concatenate([out_blockdiag(c_re[h * gh:(h + 1) * gh]),
                                     -out_blockdiag(c_im[h * gh:(h + 1) * gh])], axis=0) for h in range(2)])
    lam = jnp.stack([lam_re.reshape(-1), lam_im.reshape(-1)])
    lam2 = jnp.stack([lam[0] * lam[0] - lam[1] * lam[1], 2.0 * lam[0] * lam[1]])
    return dict(bb=bb.astype(BF16), cm=cm.astype(BF16), lam=lam, lam2=lam2,
                d=lw['s5_d'].astype(F32).reshape(1, S5_CH), wglu=lw['s5_w_glu'].astype(BF16),
                bglu=lw['s5_b_glu'].astype(F32).reshape(1, S5_CH))


def _s5_drive(u, bb_ref, sc_ref):
    ub = u.astype(BF16)
    for hf in range(2):
        sc_ref[:, hf * 2 * S5_HALF_ST:(hf + 1) * 2 * S5_HALF_ST] = jnp.dot(
            ub[:, hf * S5_HALF_CH:(hf + 1) * S5_HALF_CH], bb_ref[hf], preferred_element_type=F32)


def _s5_readout(sc_ref, u, cm_ref, d_ref, wglu_ref, bglu_ref):
    ys = [jnp.dot(sc_ref[:, hf * 2 * S5_HALF_ST:(hf + 1) * 2 * S5_HALF_ST].astype(BF16), cm_ref[hf],
                  preferred_element_type=F32) for hf in range(2)]
    y = jnp.concatenate(ys, axis=1) + d_ref[...] * u
    y = jax.nn.gelu(y)
    z = jnp.dot(y.astype(BF16), wglu_ref[...], preferred_element_type=F32) + bglu_ref[...]
    return y * jax.nn.sigmoid(z)


def _s5_slabs():
    for hf in range(2):
        for sl in range(S5_HALF_ST // S5_SLAB):
            re0 = hf * 2 * S5_HALF_ST + sl * S5_SLAB
            yield re0, re0 + S5_HALF_ST, hf * S5_HALF_ST + sl * S5_SLAB


def _s5_prompt_kernel(u_ref, bb_ref, m_ref, cm_ref, d_ref, wglu_ref, bglu_ref, o_ref, st_ref, sc_ref, carry_ref):
    @pl.when(pl.program_id(0) == 0)
    def _():
        carry_ref[...] = jnp.zeros_like(carry_ref)

    u = u_ref[...]
    _s5_drive(u, bb_ref, sc_ref)
    first_step = lax.broadcasted_iota(jnp.int32, (8, S5_SLAB), 0) < 4
    n_pairs = u_ref.shape[0] // 8
    for re0, im0, l0 in _s5_slabs():
        mr = m_ref[0, :, l0:l0 + S5_SLAB]
        mi = m_ref[1, :, l0:l0 + S5_SLAB]
        nr = m_ref[2, :, l0:l0 + S5_SLAB]
        ni = m_ref[3, :, l0:l0 + S5_SLAB]

        def body(k, carry, re0=re0, im0=im0, mr=mr, mi=mi, nr=nr, ni=ni):
            hr, hi = carry
            base = pl.multiple_of(k * 8, 8)
            xr = sc_ref[pl.ds(base, 8), re0:re0 + S5_SLAB]
            xi = sc_ref[pl.ds(base, 8), im0:im0 + S5_SLAB]
            xr_s = pltpu.roll(xr, 4, 0)
            xi_s = pltpu.roll(xi, 4, 0)
            outr = (mr * hr - mi * hi) + xr + (nr * xr_s - ni * xi_s)
            outi = (mr * hi + mi * hr) + xi + (nr * xi_s + ni * xr_s)
            sc_ref[pl.ds(base, 8), re0:re0 + S5_SLAB] = outr
            sc_ref[pl.ds(base, 8), im0:im0 + S5_SLAB] = outi
            return (jnp.where(first_step, pltpu.roll(outr, 4, 0), outr),
                    jnp.where(first_step, pltpu.roll(outi, 4, 0), outi))

        hr, hi = lax.fori_loop(0, n_pairs, body,
                               (carry_ref[:, re0:re0 + S5_SLAB], carry_ref[:, im0:im0 + S5_SLAB]))
        carry_ref[:, re0:re0 + S5_SLAB] = hr
        carry_ref[:, im0:im0 + S5_SLAB] = hi

    o_ref[...] = _s5_readout(sc_ref, u, cm_ref, d_ref, wglu_ref, bglu_ref).astype(o_ref.dtype)
    st_ref[...] = carry_ref[...]


def _const_spec(shape):
    return pl.BlockSpec(shape, lambda c: (0,) * len(shape))


def s5_prompt(u_tm, tb, *, rows):
    n = u_tm.shape[0]
    zero = jnp.zeros_like(tb['lam'])
    m = jnp.stack([jnp.concatenate([jnp.broadcast_to(a[k][None], (4, a.shape[1])),
                                    jnp.broadcast_to(b[k][None], (4, b.shape[1]))], axis=0)
                   for a, b, k in ((tb['lam'], tb['lam2'], 0), (tb['lam'], tb['lam2'], 1),
                                   (zero, tb['lam'], 0), (zero, tb['lam'], 1))])
    return pl.pallas_call(
        _s5_prompt_kernel,
        grid=(n // rows,),
        in_specs=[
            pl.BlockSpec((rows, S5_CH), lambda c: (c, 0)),
            _const_spec(tb['bb'].shape), _const_spec(m.shape), _const_spec(tb['cm'].shape),
            _const_spec((1, S5_CH)), _const_spec((S5_CH, S5_CH)), _const_spec((1, S5_CH)),
        ],
        out_specs=[pl.BlockSpec((rows, S5_CH), lambda c: (c, 0)), _const_spec((8, S5_LANES))],
        out_shape=[jax.ShapeDtypeStruct((n, S5_CH), BF16), jax.ShapeDtypeStruct((8, S5_LANES), F32)],
        scratch_shapes=[pltpu.VMEM((rows, S5_LANES), F32), pltpu.VMEM((8, S5_LANES), F32)],
        compiler_params=pltpu.CompilerParams(
            dimension_semantics=("arbitrary",), vmem_limit_bytes=V7X_VMEM_LIMIT),
        name="s5_prompt",
    )(u_tm, tb['bb'], m, tb['cm'], tb['d'], tb['wglu'], tb['bglu'])


def _s5_sample_kernel(u_ref, h0_ref, bb_ref, lam_ref, cm_ref, d_ref, wglu_ref, bglu_ref, mix_ref,
                      o_ref, st_ref, sc_ref):
    del mix_ref
    u = u_ref[...]
    _s5_drive(u, bb_ref, sc_ref)
    n_seq = h0_ref.shape[0]
    n_steps = u_ref.shape[0] // n_seq
    for re0, im0, l0 in _s5_slabs():
        lr = lam_ref[0:1, l0:l0 + S5_SLAB]
        li = lam_ref[1:2, l0:l0 + S5_SLAB]

        def body(rb, _, re0=re0, im0=im0, lr=lr, li=li):
            r0 = pl.multiple_of(rb * 8, 8)
            hr = h0_ref[pl.ds(r0, 8), re0:re0 + S5_SLAB]
            hi = h0_ref[pl.ds(r0, 8), im0:im0 + S5_SLAB]
            for t in range(n_steps):
                rows = pl.ds(t * n_seq + r0, 8)
                nr = (lr * hr - li * hi) + sc_ref[rows, re0:re0 + S5_SLAB]
                ni = (lr * hi + li * hr) + sc_ref[rows, im0:im0 + S5_SLAB]
                sc_ref[rows, re0:re0 + S5_SLAB] = nr
                sc_ref[rows, im0:im0 + S5_SLAB] = ni
                hr, hi = nr, ni
            st_ref[pl.ds(r0, 8), re0:re0 + S5_SLAB] = hr
            st_ref[pl.ds(r0, 8), im0:im0 + S5_SLAB] = hi
            return 0

        lax.fori_loop(0, n_seq // 8, body, 0)

    o_ref[...] = _s5_readout(sc_ref, u, cm_ref, d_ref, wglu_ref, bglu_ref).astype(o_ref.dtype)


def s5_sample(y, mixed, h0, row0, n_rows, tb):
    nb = h0.shape[0]
    assert row0 % n_rows == 0
    blk = row0 // n_rows
    return pl.pallas_call(
        _s5_sample_kernel,
        grid=(1,),
        in_specs=[pl.BlockSpec((n_rows, S5_CH), lambda c: (blk, PK_OFF[0] // S5_CH)),
                  _const_spec((nb, S5_LANES)), _const_spec(tb['bb'].shape),
                  _const_spec(tb['lam'].shape), _const_spec(tb['cm'].shape), _const_spec((1, S5_CH)),
                  _const_spec((S5_CH, S5_CH)), _const_spec((1, S5_CH)), _ANY_SPEC],
        out_specs=[pl.BlockSpec((n_rows, S5_CH), lambda c: (blk, MIX_S5)), _const_spec((nb, S5_LANES))],
        out_shape=[jax.ShapeDtypeStruct(mixed.shape, mixed.dtype), jax.ShapeDtypeStruct((nb, S5_LANES), F32)],
        input_output_aliases={8: 0},
        scratch_shapes=[pltpu.VMEM((n_rows, S5_LANES), F32)],
        compiler_params=pltpu.CompilerParams(
            dimension_semantics=("arbitrary",), vmem_limit_bytes=V7X_VMEM_LIMIT),
        name="s5_sample",
    )(y, h0, tb['bb'], tb['lam'], tb['cm'], tb['d'], tb['wglu'], tb['bglu'], mixed)


def _s5_state_to_lanes(re, im):
    b = re.shape[0]
    return jnp.stack([re.reshape(b, 2, S5_HALF_ST), im.reshape(b, 2, S5_HALF_ST)], axis=2).reshape(b, S5_LANES)


def _s5_lanes_to_state(st):
    b = st.shape[0]
    st = st.reshape(b, 2, 2, S5_HALF_ST)
    return st[:, :, 0].reshape(b, S5_GROUPS, S5_STATE), st[:, :, 1].reshape(b, S5_GROUPS, S5_STATE)


def _ret_tables(positions, chunk):
    half = RET_DK // 2
    inv_freq = ROPE_BASE ** (-jnp.arange(half, dtype=F32) / half)
    ang = positions.astype(F32)[:, None] * inv_freq[None, :]
    cos = jnp.cos(ang)
    sin = jnp.sin(ang)
    cos2 = jnp.concatenate([cos, cos], axis=1)
    sin2 = jnp.concatenate([-sin, sin], axis=1)
    log_gamma = jnp.log(1.0 - 2.0 ** (-5.0 - jnp.arange(RET_HEADS, dtype=F32)))
    g = (jnp.arange(chunk, dtype=F32) + 1.0)[None, :] * log_gamma[:, None]
    diff = g[:, :, None] - g[:, None, :]
    causal = jnp.tril(jnp.ones((chunk, chunk), dtype=bool))
    dmat = jnp.where(causal, jnp.exp(jnp.where(causal, diff, 0.0)), 0.0)
    lanes = (RET_HEADS, chunk, RET_DK)
    qdec = jnp.broadcast_to(jnp.exp(g)[:, :, None], lanes)
    kdec = jnp.broadcast_to(jnp.exp(g[:, -1:] - g)[:, :, None], lanes)
    gall = jnp.broadcast_to(jnp.exp(g[:, -1])[:, None, None], (RET_HEADS, 1, RET_DV))
    return cos2, sin2, dmat, qdec, kdec, gall


def _ret_rotate(x, cos2, sin2):
    return x * cos2 + pltpu.roll(x, RET_DK // 2, 1) * sin2


def _group_layernorm_gate(o, gate, w, b):
    mu = jnp.mean(o, axis=-1, keepdims=True)
    xc = o - mu
    var = jnp.mean(xc * xc, axis=-1, keepdims=True)
    return _silu(gate) * (xc * lax.rsqrt(var + EPS) * w + b)


def _dot_nt(a, b):
    return lax.dot_general(a, b, (((1,), (1,)), ((), ())), preferred_element_type=F32)


def _dot_tn(a, b):
    return lax.dot_general(a, b, (((0,), (0,)), ((), ())), preferred_element_type=F32)


def _ret_prompt_kernel(q_ref, k_ref, v_ref, g_ref, cos_ref, sin_ref, dm_ref, qd_ref, kd_ref, ga_ref,
                       lnw_ref, lnb_ref, mix_ref, o_ref, s_ref, *, chunk):
    del mix_ref
    s_ref[...] = jnp.zeros_like(s_ref)

    def body(c, carry):
        r = pl.ds(pl.multiple_of(c * chunk, chunk), chunk)
        cos2, sin2 = cos_ref[r, :], sin_ref[r, :]
        for h in range(RET_HEADS):
            hc = slice(h * RET_DK, (h + 1) * RET_DK)
            q = _ret_rotate(q_ref[r, hc], cos2, sin2)
            k = _ret_rotate(k_ref[r, hc], cos2, sin2) * (RET_DK ** -0.5)
            vb = v_ref[r, hc].astype(BF16)
            s = s_ref[0, h]
            scores = _dot_nt(q.astype(BF16), k.astype(BF16)) * dm_ref[h]
            o = jnp.dot(scores.astype(BF16), vb, preferred_element_type=F32)
            o = o + jnp.dot((q * qd_ref[h]).astype(BF16), s.astype(BF16), preferred_element_type=F32)
            o_ref[r, hc] = _group_layernorm_gate(o, g_ref[r, hc], lnw_ref[h], lnb_ref[h]).astype(o_ref.dtype)
            s_ref[0, h] = s * ga_ref[h] + _dot_tn((k * kd_ref[h]).astype(BF16), vb)
        return carry

    n_chunks = q_ref.shape[0] // chunk
    lax.fori_loop(0, n_chunks, body, 0, unroll=math.gcd(RECURRENCE_UNROLL, n_chunks))


MIX_S5, MIX_GDN, MIX_SSD, MIX_RET = range(4)
_ANY_SPEC = pl.BlockSpec(memory_space=pl.ANY)


def _uninit_kernel(o_ref):
    del o_ref


def uninitialized(shape, dtype):
    return pl.pallas_call(_uninit_kernel, out_shape=jax.ShapeDtypeStruct(shape, dtype), out_specs=_ANY_SPEC,
                          name="uninitialized")()


def ret_prompt(y, mixed, n_seq, seq_len, lw):
    chunk = min(RET_CHUNK, seq_len)
    cos2, sin2, dmat, qdec, kdec, gall = _ret_tables(jnp.arange(seq_len, dtype=jnp.int32), chunk)
    width = RET_HEADS * RET_DK
    heads = lambda shape: _const_spec((RET_HEADS,) + shape)
    tok = lambda s: pl.BlockSpec((seq_len, width), lambda b: (b, PK_OFF[s] // width))
    tab = _const_spec((seq_len, RET_DK))
    return pl.pallas_call(
        functools.partial(_ret_prompt_kernel, chunk=chunk),
        grid=(n_seq,),
        in_specs=[tok(10), tok(11), tok(12), tok(13), tab, tab,
                  heads((chunk, chunk)), heads((chunk, RET_DK)), heads((chunk, RET_DK)), heads((1, RET_DV)),
                  heads((1, RET_DV)), heads((1, RET_DV)), _ANY_SPEC],
        out_specs=[pl.BlockSpec((seq_len, width), lambda b: (b, MIX_RET)),
                   pl.BlockSpec((1, RET_HEADS, RET_DK, RET_DV), lambda b: (b, 0, 0, 0))],
        out_shape=[jax.ShapeDtypeStruct(mixed.shape, mixed.dtype),
                   jax.ShapeDtypeStruct((n_seq, RET_HEADS, RET_DK, RET_DV), F32)],
        input_output_aliases={12: 0},
        compiler_params=pltpu.CompilerParams(
            dimension_semantics=("arbitrary",), vmem_limit_bytes=V7X_VMEM_LIMIT),
        name="ret_prompt",
    )(y, y, y, y, cos2, sin2, dmat, qdec, kdec, gall,
      lw['ret_ln_w'].astype(F32).reshape(RET_HEADS, 1, RET_DV), lw['ret_ln_b'].astype(F32).reshape(RET_HEADS, 1, RET_DV),
      mixed)


SAMPLE_BB = 16
LHS_ROWS = 16


def _ret_sample_kernel(q_ref, k_ref, v_ref, g_ref, cos_ref, sin_ref, s0_ref, lnw_ref, lnb_ref, mix_ref, sall_ref,
                       o_ref, s_ref, qd_scr, kd_scr, v_scr, acc_scr, lq, lk, lv, *, decay):
    del mix_ref, sall_ref
    n_t, n_b = q_ref.shape[0], q_ref.shape[1]
    for h in range(RET_HEADS):
        hc = slice(h * RET_DK, (h + 1) * RET_DK)
        qs, ks, vs = [], [], []
        for t in range(n_t):
            cos2, sin2 = cos_ref[t:t + 1, :], sin_ref[t:t + 1, :]
            qs.append(_ret_rotate(q_ref[t, :, hc], cos2, sin2))
            ks.append(_ret_rotate(k_ref[t, :, hc], cos2, sin2) * (RET_DK ** -0.5))
            vs.append(v_ref[t, :, hc])
            qd_scr[h, t] = qs[t] * decay['q'][h][t]
            kd_scr[h, t] = ks[t] * decay['k'][h][t]
            v_scr[h, t] = vs[t]
        for i in range(n_t):
            acc = None
            for j in range(i + 1):
                term = (jnp.sum(qs[i] * ks[j], axis=-1, keepdims=True) * decay['m'][h][i][j]) * vs[j]
                acc = term if acc is None else acc + term
            acc_scr[h, i] = acc

    for tile in (lq, lk, lv):
        tile[...] = jnp.zeros_like(tile)

    def per_seq(b, carry):
        row = pl.ds(b, 1)
        for h in range(RET_HEADS):
            for t in range(n_t):
                lq[h, t:t + 1, :] = qd_scr[h, t, row, :]
                lk[h, t:t + 1, :] = kd_scr[h, t, row, :]
                lv[h, t:t + 1, :] = v_scr[h, t, row, :]
            s0 = s0_ref[b, h]
            inter = jnp.dot(lq[h].astype(BF16), s0.astype(BF16), preferred_element_type=F32)
            s_ref[b, h] = s0 * decay['all'][h] + _dot_tn(lk[h].astype(BF16), lv[h].astype(BF16))
            for t in range(n_t):
                acc_scr[h, t, row, :] = acc_scr[h, t, row, :] + inter[t:t + 1, :]
        return carry

    lax.fori_loop(0, n_b, per_seq, 0)

    for h in range(RET_HEADS):
        hc = slice(h * RET_DV, (h + 1) * RET_DV)
        for t in range(n_t):
            o_ref[t, :, hc] = _group_layernorm_gate(acc_scr[h, t], g_ref[t, :, hc], lnw_ref[h], lnb_ref[h]
                                                    ).astype(o_ref.dtype)


def _sample_block(t0, n_t):
    assert t0 % n_t == 0
    return t0 // n_t


def ret_sample(y3, mixed3, states, new_states, layer, t0, n_t, first_pos, lw):
    n_b = y3.shape[1]
    tb = _sample_block(t0, n_t)
    cos2, sin2, _, _, _, _ = _ret_tables(first_pos + jnp.arange(n_t, dtype=jnp.int32), n_t)
    gamma = 1.0 - 2.0 ** (-5.0 - np.arange(RET_HEADS, dtype=np.float64))
    decay = dict(m=[[[float(g ** (i - j)) for j in range(n_t)] for i in range(n_t)] for g in gamma],
                 q=[[float(g ** (i + 1)) for i in range(n_t)] for g in gamma],
                 k=[[float(g ** (n_t - 1 - j)) for j in range(n_t)] for g in gamma],
                 all=[float(g ** n_t) for g in gamma])
    width = RET_HEADS * RET_DK
    tok = lambda s: pl.BlockSpec((n_t, SAMPLE_BB, width), lambda i: (tb, i, PK_OFF[s] // width))
    state = pl.BlockSpec((None, SAMPLE_BB, RET_HEADS, RET_DK, RET_DV), lambda i: (layer, i, 0, 0, 0))
    per_tok = pltpu.VMEM((RET_HEADS, n_t, SAMPLE_BB, RET_DK), F32)
    tile = pltpu.VMEM((RET_HEADS, LHS_ROWS, RET_DK), F32)
    return pl.pallas_call(
        functools.partial(_ret_sample_kernel, decay=decay),
        grid=(n_b // SAMPLE_BB,),
        in_specs=[tok(10), tok(11), tok(12), tok(13), _const_spec((n_t, RET_DK)), _const_spec((n_t, RET_DK)),
                  state, _const_spec((RET_HEADS, 1, RET_DV)), _const_spec((RET_HEADS, 1, RET_DV)), _ANY_SPEC, _ANY_SPEC],
        out_specs=[pl.BlockSpec((n_t, SAMPLE_BB, width), lambda i: (tb, i, MIX_RET)), state],
        out_shape=[jax.ShapeDtypeStruct(mixed3.shape, mixed3.dtype), jax.ShapeDtypeStruct(states.shape, F32)],
        input_output_aliases={9: 0, 10: 1},
        scratch_shapes=[per_tok, per_tok, per_tok, per_tok, tile, tile, tile],
        compiler_params=pltpu.CompilerParams(
            dimension_semantics=("arbitrary",), vmem_limit_bytes=V7X_VMEM_LIMIT),
        name="ret_sample",
    )(y3, y3, y3, y3, cos2, sin2, states,
      lw['ret_ln_w'].astype(F32).reshape(RET_HEADS, 1, RET_DV), lw['ret_ln_b'].astype(F32).reshape(RET_HEADS, 1, RET_DV),
      mixed3, new_states)


NARROW_B = 0
NARROW_A = GDN_HEADS
NARROW_DT = 2 * GDN_HEADS
TAIL = 8


def _shift_rows(x, tail, s):
    xr = pltpu.roll(x, s, 0)
    tr = pltpu.roll(tail, s, 0)
    row = lax.broadcasted_iota(jnp.int32, tail.shape, 0)
    return jnp.concatenate([jnp.where(row < s, tr, xr[0:TAIL]), xr[TAIL:]], axis=0)


def _causal_conv(x, tail, w_ref):
    y = x * w_ref[CONV_W - 1:CONV_W, :]
    for s in range(1, CONV_W):
        y = y + _shift_rows(x, tail, s) * w_ref[CONV_W - 1 - s:CONV_W - s, :]
    return y


def _softplus(x):
    return jnp.maximum(x, 0.0) + jnp.log1p(jnp.exp(-jnp.abs(x)))


def _lane_row(vals, lane0, width=128):
    return jnp.pad(vals.astype(F32), (lane0, width - lane0 - vals.shape[0])).reshape(1, width)


SSD_PAIRS = SSD_HEADS // 2
SSD_BC = SSD_NGROUPS * SSD_STATE


def _ssd_prompt_kernel(z_ref, xbc_ref, nar_ref, cw_ref, cb_ref, dtb_ref, a_ref, dsk_ref, nw_ref, tri_ref, mix_ref,
                       o_ref, s_ref, tail_ref, *, chunk):
    del mix_ref
    s_ref[...] = jnp.zeros_like(s_ref)
    tail_ref[...] = jnp.zeros_like(tail_ref)
    causal = (lax.broadcasted_iota(jnp.int32, (chunk, chunk), 0)
              >= lax.broadcasted_iota(jnp.int32, (chunk, chunk), 1))
    lane = lax.broadcasted_iota(jnp.int32, (chunk, 2 * SSD_HEADDIM), 1)
    first_head = lane < SSD_HEADDIM
    rep = SSD_HEADS // SSD_NGROUPS

    def body(c, carry):
        r = pl.ds(pl.multiple_of(c * chunk, chunk), chunk)
        raw = xbc_ref[r, :]
        xbc = _silu(_causal_conv(raw, tail_ref[0], cw_ref) + cb_ref[...])
        tail_ref[0] = raw[chunk - TAIL:, :]
        xs = xbc[:, :SSD_INNER]
        dt = _softplus(nar_ref[r, :] + dtb_ref[...])
        g = _select_rows(tri_ref[...], dt * a_ref[...])
        e_in = jnp.exp(g)
        e_out = dt * jnp.exp(g[chunk - 1:chunk, :] - g)
        e_all = jnp.exp(g[chunk - 1:chunk, :])
        g_t = g.T
        dt_t = dt.T
        ys = []
        for p in range(SSD_PAIRS):
            grp = (2 * p) // rep
            bm = xbc[:, SSD_INNER + grp * SSD_STATE:SSD_INNER + (grp + 1) * SSD_STATE]
            cm = xbc[:, SSD_INNER + SSD_BC + grp * SSD_STATE:SSD_INNER + SSD_BC + (grp + 1) * SSD_STATE]
            cb = _dot_nt(cm.astype(BF16), bm.astype(BF16))
            xp = xs[:, p * 128:(p + 1) * 128]
            xpb = xp.astype(BF16)
            sp = s_ref[0, p]
            spb = sp.astype(BF16)
            outs, upds, gls = [], [], []
            for hh in range(2):
                ln = NARROW_DT + 2 * p + hh
                diff = g[:, ln:ln + 1] - g_t[ln:ln + 1, :]
                m = jnp.where(causal, cb * jnp.exp(jnp.where(causal, diff, 0.0)) * dt_t[ln:ln + 1, :], 0.0)
                o = jnp.dot(m.astype(BF16), xpb, preferred_element_type=F32)
                o = o + jnp.dot((cm * e_in[:, ln:ln + 1]).astype(BF16), spb, preferred_element_type=F32)
                outs.append(o)
                upds.append(_dot_tn((bm * e_out[:, ln:ln + 1]).astype(BF16), xpb))
                gls.append(e_all[:, ln:ln + 1])
            s_ref[0, p] = sp * jnp.where(first_head, gls[0], gls[1]) + jnp.where(first_head, upds[0], upds[1])
            ys.append(jnp.where(first_head, outs[0], outs[1]) + xp * dsk_ref[:, p * 128:(p + 1) * 128])
        y = jnp.concatenate(ys, axis=1) * _silu(z_ref[r, :])
        gw = SSD_INNER // SSD_NGROUPS
        yn = [y[:, i * gw:(i + 1) * gw] * lax.rsqrt(
            jnp.mean(y[:, i * gw:(i + 1) * gw] * y[:, i * gw:(i + 1) * gw], axis=-1, keepdims=True) + EPS)
            for i in range(SSD_NGROUPS)]
        o_ref[r, :] = (jnp.concatenate(yn, axis=1) * nw_ref[...]).astype(o_ref.dtype)
        return carry

    n_chunks = z_ref.shape[0] // chunk
    lax.fori_loop(0, n_chunks, body, 0, unroll=math.gcd(2, n_chunks))


def _ssd_params(lw, chunk):
    return (lw['ssd_conv_w'].astype(F32), lw['ssd_conv_b'].astype(F32).reshape(1, SSD_CONV_DIM),
            _lane_row(lw['ssd_dt_bias'], NARROW_DT), _lane_row(-jnp.exp(lw['ssd_a_log'].astype(F32)), NARROW_DT),
            jnp.repeat(lw['ssd_d'].astype(F32), SSD_HEADDIM).reshape(1, SSD_INNER),
            lw['ssd_norm_w'].astype(F32).reshape(1, SSD_INNER),
            jnp.tril(jnp.ones((chunk, chunk), BF16)))


def _ssd_state_from_pairs(s):
    b = s.shape[0]
    s = s.reshape(b, SSD_PAIRS, SSD_STATE, 2, SSD_HEADDIM)
    return jnp.swapaxes(s, 2, 3).reshape(b, SSD_HEADS, SSD_STATE, SSD_HEADDIM)


def ssd_prompt(y, nar, mixed, n_seq, seq_len, lw):
    chunk = min(SSD_CHUNK, seq_len)
    params = _ssd_params(lw, chunk)
    tok = lambda s, w: pl.BlockSpec((seq_len, w), lambda b: (b, PK_OFF[s] // w))
    n_in = 3 + len(params)
    mixed, s, tail = pl.pallas_call(
        functools.partial(_ssd_prompt_kernel, chunk=chunk),
        grid=(n_seq,),
        in_specs=[tok(7, SSD_INNER), tok(8, SSD_CONV_DIM), pl.BlockSpec((seq_len, NARROW_W), lambda b: (b, 0))]
                 + [_const_spec(p.shape) for p in params] + [_ANY_SPEC],
        out_specs=[pl.BlockSpec((seq_len, SSD_INNER), lambda b: (b, MIX_SSD)),
                   pl.BlockSpec((1, SSD_PAIRS, SSD_STATE, 2 * SSD_HEADDIM), lambda b: (b, 0, 0, 0)),
                   pl.BlockSpec((1, TAIL, SSD_CONV_DIM), lambda b: (b, 0, 0))],
        out_shape=[jax.ShapeDtypeStruct(mixed.shape, mixed.dtype),
                   jax.ShapeDtypeStruct((n_seq, SSD_PAIRS, SSD_STATE, 2 * SSD_HEADDIM), F32),
                   jax.ShapeDtypeStruct((n_seq, TAIL, SSD_CONV_DIM), F32)],
        input_output_aliases={n_in: 0},
        compiler_params=pltpu.CompilerParams(
            dimension_semantics=("arbitrary",), vmem_limit_bytes=V7X_VMEM_LIMIT),
        name="ssd_prompt",
    )(y, y, nar, *params, mixed)
    return mixed, tail[:, TAIL - (CONV_W - 1):], _ssd_state_from_pairs(s)


GDN_QKV = GDN_HEADS * GDN_DK
GDN_ROWS = 1024
GDN_UNROLL = 4
HIGHEST = lax.Precision.HIGHEST


def _split_bf16(x):
    hi = x.astype(BF16)
    return hi, (x - hi.astype(F32)).astype(BF16)


def _split3_bf16(x):
    hi = x.astype(BF16)
    r = x - hi.astype(F32)
    mid = r.astype(BF16)
    return hi, mid, (r - mid.astype(F32)).astype(BF16)


def _select_rows(m01, x):
    n = x.shape[1]
    p = jnp.dot(m01, jnp.concatenate(_split3_bf16(x), axis=1), preferred_element_type=F32)
    return p[:, :n] + p[:, n:2 * n] + p[:, 2 * n:]


def _select_cols(x, m01):
    m = x.shape[0]
    p = jnp.dot(jnp.concatenate(_split3_bf16(x), axis=0), m01, preferred_element_type=F32)
    return p[:m] + p[m:2 * m] + p[2 * m:]


def _dot3(a, b):
    a_hi, a_lo = _split_bf16(a)
    b_hi, b_lo = _split_bf16(b)
    m = a.shape[0]
    p = jnp.dot(jnp.concatenate([a_hi, a_lo], axis=0), b_hi, preferred_element_type=F32)
    return p[:m] + p[m:] + jnp.dot(a_hi, b_lo, preferred_element_type=F32)


def _two_block_diag(x0, x1):
    z0 = jnp.zeros_like(x0)
    z1 = jnp.zeros_like(x1)
    return jnp.concatenate([jnp.concatenate([x0, z1], axis=1), jnp.concatenate([z0, x1], axis=1)], axis=0)


def _l2_rows(x):
    return x * lax.rsqrt(jnp.sum(x * x, axis=-1, keepdims=True) + EPS)


def _gdn_prompt_kernel(q_ref, k_ref, v_ref, z_ref, nar_ref, cw_ref, dtb_ref, a_ref, nw_ref, tri_ref, sel_ref, mix_ref,
                       o_ref, s_ref, tail_ref, u_scr, w_scr, qd_scr, kd_scr, sc_scr, ea_scr, *, chunk, unroll):
    del mix_ref

    @pl.when(pl.program_id(1) == 0)
    def _():
        s_ref[...] = jnp.zeros_like(s_ref)
        tail_ref[...] = jnp.zeros_like(tail_ref)

    n_chunks = q_ref.shape[0] // chunk
    cat = GDN_HEADS * chunk
    lane = lax.broadcasted_iota(jnp.int32, (chunk, cat), 1)
    row = lax.broadcasted_iota(jnp.int32, (chunk, cat), 0)
    col = jnp.bitwise_and(lane, chunk - 1)
    causal = row >= col
    strict = row > col
    eye_cat = jnp.where(row == col, 1.0, 0.0).astype(F32)
    head_mask = [(lane >= h * chunk) & (lane < (h + 1) * chunk) for h in range(GDN_HEADS)]
    nar_lane = lax.broadcasted_iota(jnp.int32, (chunk, 128), 1)
    ones_cc = jnp.ones((chunk, chunk), BF16)
    n_sq = chunk.bit_length() - 2

    def mm_cat(l_cat, r_cat):
        bd = jnp.concatenate([jnp.where(m, r_cat, 0.0) for m in head_mask], axis=0)
        return _dot3(l_cat, bd)

    def neumann_step(t_cat, pw, update_t, update_pw):
        return (t_cat + mm_cat(t_cat, pw) if update_t else t_cat), (mm_cat(pw, pw) if update_pw else pw)

    col1 = lambda x, h: x[:, h * chunk:h * chunk + 1]

    def front(c):
        r = pl.ds(pl.multiple_of(c * chunk, chunk), chunk)
        before = pl.ds(pl.multiple_of(jnp.maximum(c * chunk - TAIL, 0), TAIL), TAIL)
        qkv = []
        for i, ref in enumerate((q_ref, k_ref, v_ref)):
            cols = slice(i * GDN_QKV, (i + 1) * GDN_QKV)
            tail = jnp.where(c == 0, tail_ref[0, :, cols], ref[before, :])
            qkv.append(_silu(_causal_conv(ref[r, :], tail, cw_ref.at[:, cols])))
        q, k, v = qkv
        nar = nar_ref[r, :]
        beta = jax.nn.sigmoid(nar)
        g = _select_rows(tri_ref[...], a_ref[...] * _softplus(nar + dtb_ref[...]))
        bg = _select_cols(jnp.where(nar_lane < NARROW_A, beta, g), sel_ref[...])
        b_c, g_c = bg[:, :cat], bg[:, cat:]
        g_r = _select_rows(ones_cc, g_c * eye_cat)
        decay = jnp.where(causal, jnp.exp(jnp.where(causal, g_c - g_r, 0.0)), 0.0)
        e_in = jnp.exp(g_c)
        e_out = jnp.exp(g_c[chunk - 1:chunk, :] - g_c)
        e_all = jnp.exp(g_c[chunk - 1:chunk, :])

        qn = [_l2_rows(q[:, h * GDN_DK:(h + 1) * GDN_DK]) * (GDN_DK ** -0.5) for h in range(GDN_HEADS)]
        kn = [_l2_rows(k[:, h * GDN_DK:(h + 1) * GDN_DK]) for h in range(GDN_HEADS)]
        vh = [v[:, h * GDN_DV:(h + 1) * GDN_DV] for h in range(GDN_HEADS)]
        kk, qk = [], []
        for p in range(GDN_HEADS // 2):
            h0, h1 = 2 * p, 2 * p + 1
            rhs = _two_block_diag(kn[h0], kn[h1]).astype(BF16)
            kk.append(_dot_nt(jnp.concatenate([kn[h0], kn[h1]], axis=1).astype(BF16), rhs))
            qk.append(_dot_nt(jnp.concatenate([qn[h0], qn[h1]], axis=1).astype(BF16), rhs))
        kk = jnp.concatenate(kk, axis=1)
        sc_scr[r, :] = (jnp.concatenate(qk, axis=1) * decay).astype(BF16)
        ea_scr[c] = jnp.broadcast_to(e_all, (TAIL, cat))
        for h in range(GDN_HEADS):
            hc = slice(h * GDN_DV, (h + 1) * GDN_DV)
            qd_scr[r, hc] = (qn[h] * col1(e_in, h)).astype(BF16)
            kd_scr[r, hc] = (kn[h] * col1(e_out, h)).astype(BF16)
        a_cat = jnp.where(strict, b_c * kk * decay, 0.0)
        rhs = [_two_block_diag(*[jnp.concatenate(
            [vh[h] * col1(b_c, h), kn[h] * (col1(b_c, h) * col1(e_in, h))], axis=1) for h in (2 * p, 2 * p + 1)])
            for p in range(GDN_HEADS // 2)]
        return r, a_cat, rhs

    def back(r, t_cat, rhs):
        for p in range(GDN_HEADS // 2):
            uw = _dot3(t_cat[:, p * 2 * chunk:(p + 1) * 2 * chunk], rhs[p])
            for i, h in enumerate((2 * p, 2 * p + 1)):
                hc = slice(h * GDN_DV, (h + 1) * GDN_DV)
                u_scr[r, hc] = uw[:, (2 * i) * GDN_DV:(2 * i + 1) * GDN_DV]
                w_scr[r, hc] = uw[:, (2 * i + 1) * GDN_DV:(2 * i + 2) * GDN_DV].astype(BF16)

    def prepare_group(i, carry):
        rs, ts, rhss = [], [], []
        for j in range(unroll):
            r, a_cat, rhs = front(i * unroll + j)
            rs.append(r)
            ts.append((eye_cat - a_cat, a_cat))
            rhss.append(rhs)
        for step in range(n_sq + 1):
            ts = [neumann_step(t, pw, update_t=step > 0, update_pw=step < n_sq) for t, pw in ts]
        for r, (t_cat, _), rhs in zip(rs, ts, rhss):
            back(r, t_cat, rhs)
        return carry

    lax.fori_loop(0, n_chunks // unroll, prepare_group, 0)

    def recur(c, carry):
        r = pl.ds(pl.multiple_of(c * chunk, chunk), chunk)
        e_all = ea_scr[c][0:1, :]
        for p in range(GDN_HEADS // 2):
            heads = (2 * p, 2 * p + 1)
            v_new, q_s = [], []
            for h in heads:
                hc = slice(h * GDN_DV, (h + 1) * GDN_DV)
                wq = jnp.concatenate([w_scr[r, hc], qd_scr[r, hc]], axis=0)
                ws = jnp.dot(wq, s_ref[0, h].astype(BF16), preferred_element_type=F32)
                v_new.append(u_scr[r, hc] - ws[:chunk])
                q_s.append(ws[chunk:])
            intra = jnp.dot(sc_scr[r, p * 2 * chunk:(p + 1) * 2 * chunk],
                            _two_block_diag(*v_new).astype(BF16), preferred_element_type=F32)
            for i, h in enumerate(heads):
                hc = slice(h * GDN_DV, (h + 1) * GDN_DV)
                o = q_s[i] + intra[:, i * GDN_DV:(i + 1) * GDN_DV]
                s_ref[0, h] = s_ref[0, h] * col1(e_all, h) + _dot_tn(kd_scr[r, hc], v_new[i].astype(BF16))
                o = o * lax.rsqrt(jnp.mean(o * o, axis=-1, keepdims=True) + EPS) * nw_ref[...]
                o_ref[r, hc] = (o * _silu(z_ref[r, hc])).astype(o_ref.dtype)
        return carry

    lax.fori_loop(0, n_chunks, recur, 0, unroll=math.gcd(RECURRENCE_UNROLL, n_chunks))
    last = pl.ds(q_ref.shape[0] - TAIL, TAIL)
    for i, ref in enumerate((q_ref, k_ref, v_ref)):
        tail_ref[0, :, i * GDN_QKV:(i + 1) * GDN_QKV] = ref[last, :]


def _gdn_params(lw, chunk):
    cat = GDN_HEADS * chunk
    lanes = np.arange(cat) // chunk
    sel = np.zeros((128, 2 * cat), np.float32)
    sel[NARROW_B + lanes, np.arange(cat)] = 1.0
    sel[NARROW_A + lanes, cat + np.arange(cat)] = 1.0
    return (lw['gdn_conv_w'].astype(F32), _lane_row(lw['gdn_dt_bias'], NARROW_A),
            _lane_row(-jnp.exp(lw['gdn_a_log'].astype(F32)), NARROW_A),
            lw['gdn_norm_w'].astype(F32).reshape(1, GDN_DV), jnp.tril(jnp.ones((chunk, chunk), BF16)),
            jnp.asarray(sel, BF16))


def gdn_prompt(y, nar, mixed, n_seq, seq_len, lw):
    chunk = min(GDN_CHUNK, seq_len)
    rows = min(GDN_ROWS, seq_len)
    n_chunks = rows // chunk
    unroll = math.gcd(GDN_UNROLL, n_chunks)
    params = _gdn_params(lw, chunk)
    nblk = seq_len // rows
    tok = lambda s: pl.BlockSpec((rows, GDN_QKV), lambda b, j: (b * nblk + j, PK_OFF[s] // GDN_QKV))
    const = lambda shape: pl.BlockSpec(shape, lambda b, j: (0,) * len(shape))
    n_in = 5 + len(params)
    mixed, s, tail = pl.pallas_call(
        functools.partial(_gdn_prompt_kernel, chunk=chunk, unroll=unroll),
        grid=(n_seq, nblk),
        in_specs=[tok(1), tok(2), tok(3), tok(4), pl.BlockSpec((rows, NARROW_W), lambda b, j: (b * nblk + j, 0))]
                 + [const(p.shape) for p in params] + [_ANY_SPEC],
        out_specs=[pl.BlockSpec((rows, GDN_QKV), lambda b, j: (b * nblk + j, MIX_GDN)),
                   pl.BlockSpec((1, GDN_HEADS, GDN_DK, GDN_DV), lambda b, j: (b, 0, 0, 0)),
                   pl.BlockSpec((1, TAIL, GDN_CONV_DIM), lambda b, j: (b, 0, 0))],
        out_shape=[jax.ShapeDtypeStruct(mixed.shape, mixed.dtype),
                   jax.ShapeDtypeStruct((n_seq, GDN_HEADS, GDN_DK, GDN_DV), F32),
                   jax.ShapeDtypeStruct((n_seq, TAIL, GDN_CONV_DIM), F32)],
        input_output_aliases={n_in: 0},
        scratch_shapes=[pltpu.VMEM((rows, GDN_QKV), F32), pltpu.VMEM((rows, GDN_QKV), BF16),
                        pltpu.VMEM((rows, GDN_QKV), BF16), pltpu.VMEM((rows, GDN_QKV), BF16),
                        pltpu.VMEM((rows, GDN_HEADS * chunk), BF16),
                        pltpu.VMEM((n_chunks, TAIL, GDN_HEADS * chunk), F32)],
        compiler_params=pltpu.CompilerParams(
            dimension_semantics=("arbitrary", "arbitrary"), vmem_limit_bytes=V7X_VMEM_LIMIT),
        name="gdn_prompt",
    )(y, y, y, y, nar, *params, mixed)
    return mixed, tail[:, TAIL - (CONV_W - 1):], s


def _conv_steps(buf_ref, raw_ref, w_ref, cols):
    n_t = raw_ref.shape[0]
    xx = [buf_ref[j, :, cols] for j in range(CONV_W - 1)] + [raw_ref[t] for t in range(n_t)]
    w = w_ref[:, cols]
    out = []
    for t in range(n_t):
        y = xx[t] * w[0:1, :]
        for j in range(1, CONV_W):
            y = y + xx[t + j] * w[j:j + 1, :]
        out.append(y)
    return out, xx[n_t:]


def _ssd_sample_kernel(z_ref, xbc_ref, nar_ref, buf_ref, s0_ref, cw_ref, cb_ref, dtb_ref, a_ref, dsk_ref, nw_ref,
                       mix_ref, sall_ref, o_ref, s_ref, nbuf_ref, qd_scr, kd_scr, xs_scr, ga_scr, acc_scr, lq, lk, lx):
    del mix_ref, sall_ref
    n_t, n_b = z_ref.shape[0], z_ref.shape[1]
    rep = SSD_HEADS // SSD_NGROUPS
    conv, tail = _conv_steps(buf_ref, xbc_ref, cw_ref, slice(0, SSD_CONV_DIM))
    for j in range(CONV_W - 1):
        nbuf_ref[j] = tail[j]
    xbc = [_silu(c + cb_ref[...]) for c in conv]
    dt, g = [], []
    for t in range(n_t):
        dt.append(_softplus(nar_ref[t] + dtb_ref[...]))
        la = dt[t] * a_ref[...]
        g.append(la if t == 0 else g[t - 1] + la)
    for grp in range(SSD_NGROUPS):
        bm = [x[:, SSD_INNER + grp * SSD_STATE:SSD_INNER + (grp + 1) * SSD_STATE] for x in xbc]
        cm = [x[:, SSD_INNER + SSD_BC + grp * SSD_STATE:SSD_INNER + SSD_BC + (grp + 1) * SSD_STATE] for x in xbc]
        sc = [[jnp.sum(cm[i] * bm[j], axis=-1, keepdims=True) for j in range(i + 1)] for i in range(n_t)]
        for h in range(grp * rep, (grp + 1) * rep):
            ln = NARROW_DT + h
            hc = slice(h * SSD_HEADDIM, (h + 1) * SSD_HEADDIM)
            gc = [x[:, ln:ln + 1] for x in g]
            dc = [x[:, ln:ln + 1] for x in dt]
            xs = [x[:, hc] for x in xbc]
            for t in range(n_t):
                qd_scr[h, t] = cm[t] * jnp.exp(gc[t])
                kd_scr[h, t] = bm[t] * (dc[t] * jnp.exp(gc[n_t - 1] - gc[t]))
                xs_scr[h, t] = xs[t]
                acc = None
                for j in range(t + 1):
                    term = (sc[t][j] * dc[j] * jnp.exp(gc[t] - gc[j])) * xs[j]
                    acc = term if acc is None else acc + term
                acc_scr[t, :, hc] = acc
            ga_scr[h] = jnp.broadcast_to(jnp.exp(gc[n_t - 1]), (n_b, SSD_STATE))

    for tile in (lq, lk, lx):
        tile[...] = jnp.zeros_like(tile)

    def per_seq(b, carry):
        row = pl.ds(b, 1)
        for h in range(SSD_HEADS):
            hc = slice(h * SSD_HEADDIM, (h + 1) * SSD_HEADDIM)
            for t in range(n_t):
                lq[h, t:t + 1, :] = qd_scr[h, t, row, :]
                lk[h, t:t + 1, :] = kd_scr[h, t, row, :]
                lx[h, t:t + 1, :] = xs_scr[h, t, row, :]
            s0 = s0_ref[b, h]
            inter = _dot_nt(lq[h].astype(BF16), s0.astype(BF16))
            s_ref[b, h] = s0 * ga_scr[h, row, :] + _dot_tn(lx[h].astype(BF16), lk[h].astype(BF16))
            for t in range(n_t):
                xs_scr[h, t, row, :] = inter[t:t + 1, :]
        return carry

    lax.fori_loop(0, n_b, per_seq, 0)

    gw = SSD_INNER // SSD_NGROUPS
    for t in range(n_t):
        for h in range(SSD_HEADS):
            hc = slice(h * SSD_HEADDIM, (h + 1) * SSD_HEADDIM)
            acc_scr[t, :, hc] = acc_scr[t, :, hc] + xs_scr[h, t]
        y = (acc_scr[t] + xbc[t][:, :SSD_INNER] * dsk_ref[...]) * _silu(z_ref[t])
        yn = [y[:, i * gw:(i + 1) * gw] * lax.rsqrt(
            jnp.mean(y[:, i * gw:(i + 1) * gw] * y[:, i * gw:(i + 1) * gw], axis=-1, keepdims=True) + EPS)
            for i in range(SSD_NGROUPS)]
        o_ref[t] = (jnp.concatenate(yn, axis=1) * nw_ref[...]).astype(o_ref.dtype)


def ssd_sample(y3, nar3, mixed3, buf0, states, new_states, layer, t0, n_t, lw):
    n_b = y3.shape[1]
    tb = _sample_block(t0, n_t)
    params = _ssd_params(lw, 1)[:-1]
    tok = lambda s, w: pl.BlockSpec((n_t, SAMPLE_BB, w), lambda i: (tb, i, PK_OFF[s] // w))
    bufspec = pl.BlockSpec((CONV_W - 1, SAMPLE_BB, SSD_CONV_DIM), lambda i: (0, i, 0))
    state = pl.BlockSpec((None, SAMPLE_BB, SSD_HEADS, SSD_HEADDIM, SSD_STATE), lambda i: (layer, i, 0, 0, 0))
    per_tok = lambda w: pltpu.VMEM((SSD_HEADS, n_t, SAMPLE_BB, w), F32)
    tile = lambda w: pltpu.VMEM((SSD_HEADS, LHS_ROWS, w), F32)
    n_in = 5 + len(params)
    mixed3, s, nbuf = pl.pallas_call(
        _ssd_sample_kernel,
        grid=(n_b // SAMPLE_BB,),
        in_specs=[tok(7, SSD_INNER), tok(8, SSD_CONV_DIM),
                  pl.BlockSpec((n_t, SAMPLE_BB, NARROW_W), lambda i: (tb, i, 0)), bufspec, state]
                 + [_const_spec(p.shape) for p in params] + [_ANY_SPEC, _ANY_SPEC],
        out_specs=[pl.BlockSpec((n_t, SAMPLE_BB, SSD_INNER), lambda i: (tb, i, MIX_SSD)), state, bufspec],
        out_shape=[jax.ShapeDtypeStruct(mixed3.shape, mixed3.dtype), jax.ShapeDtypeStruct(states.shape, F32),
                   jax.ShapeDtypeStruct((CONV_W - 1, n_b, SSD_CONV_DIM), F32)],
        input_output_aliases={n_in: 0, n_in + 1: 1},
        scratch_shapes=[per_tok(SSD_STATE), per_tok(SSD_STATE), per_tok(SSD_HEADDIM),
                        pltpu.VMEM((SSD_HEADS, SAMPLE_BB, SSD_STATE), F32),
                        pltpu.VMEM((n_t, SAMPLE_BB, SSD_INNER), F32),
                        tile(SSD_STATE), tile(SSD_STATE), tile(SSD_HEADDIM)],
        compiler_params=pltpu.CompilerParams(
            dimension_semantics=("arbitrary",), vmem_limit_bytes=V7X_VMEM_LIMIT),
        name="ssd_sample",
    )(y3, y3, nar3, jnp.swapaxes(buf0, 0, 1), states, *params, mixed3, new_states)
    return mixed3, jnp.swapaxes(nbuf, 0, 1), s


def _gdn_sample_kernel(q_ref, k_ref, v_ref, z_ref, nar_ref, buf_ref, s0_ref, cw_ref, dtb_ref, a_ref, nw_ref,
                       mix_ref, sall_ref, o_ref, s_ref, nbuf_ref, w_scr, qd_scr, kd_scr, u_scr, ga_scr, lwq, lk, lu):
    del mix_ref, sall_ref
    n_t, n_b = q_ref.shape[0], q_ref.shape[1]
    qkv = []
    for i, ref in enumerate((q_ref, k_ref, v_ref)):
        cols = slice(i * GDN_QKV, (i + 1) * GDN_QKV)
        conv, tail = _conv_steps(buf_ref, ref, cw_ref, cols)
        for j in range(CONV_W - 1):
            nbuf_ref[j, :, cols] = tail[j]
        qkv.append([_silu(c) for c in conv])
    beta, g = [], []
    for t in range(n_t):
        nar = nar_ref[t]
        beta.append(jax.nn.sigmoid(nar))
        gl = a_ref[...] * _softplus(nar + dtb_ref[...])
        g.append(gl if t == 0 else g[t - 1] + gl)

    scores = []
    for h in range(GDN_HEADS):
        hc = slice(h * GDN_DK, (h + 1) * GDN_DK)
        qn = [_l2_rows(x[:, hc]) * (GDN_DK ** -0.5) for x in qkv[0]]
        kn = [_l2_rows(x[:, hc]) for x in qkv[1]]
        vh = [x[:, hc] for x in qkv[2]]
        bc = [x[:, NARROW_B + h:NARROW_B + h + 1] for x in beta]
        gc = [x[:, NARROW_A + h:NARROW_A + h + 1] for x in g]
        us, ws = [], []
        for i in range(n_t):
            u = vh[i] * bc[i]
            w = kn[i] * (bc[i] * jnp.exp(gc[i]))
            for j in range(i):
                a_ij = bc[i] * jnp.sum(kn[i] * kn[j], axis=-1, keepdims=True) * jnp.exp(gc[i] - gc[j])
                u = u - a_ij * us[j]
                w = w - a_ij * ws[j]
            us.append(u)
            ws.append(w)
            u_scr[h, i] = u
            w_scr[h, i] = w
            qd_scr[h, i] = qn[i] * jnp.exp(gc[i])
            kd_scr[h, i] = kn[i] * jnp.exp(gc[n_t - 1] - gc[i])
        ga_scr[h] = jnp.broadcast_to(jnp.exp(gc[n_t - 1]), (n_b, GDN_DV))
        scores.append([[jnp.sum(qn[i] * kn[j], axis=-1, keepdims=True) * jnp.exp(gc[i] - gc[j])
                        for j in range(i + 1)] for i in range(n_t)])

    for tile in (lwq, lk, lu):
        tile[...] = jnp.zeros_like(tile)

    def per_pair(i, carry):
        seqs = [(s, 2 * i + s, pl.ds(2 * i + s, 1)) for s in range(2)]
        for s, b, row in seqs:
            for h in range(GDN_HEADS):
                n = s * GDN_HEADS + h
                for t in range(n_t):
                    lwq[n, t:t + 1, :] = w_scr[h, t, row, :]
                    lwq[n, LHS_ROWS + t:LHS_ROWS + t + 1, :] = qd_scr[h, t, row, :]
                    lk[n, t:t + 1, :] = kd_scr[h, t, row, :]
                    lu[n, t:t + 1, :] = u_scr[h, t, row, :]
        results = []
        for s, b, row in seqs:
            for h in range(GDN_HEADS):
                n = s * GDN_HEADS + h
                s0 = s0_ref[b, h]
                wq_s = jnp.dot(lwq[n].astype(BF16), s0.astype(BF16), preferred_element_type=F32)
                v_new = lu[n] - wq_s[:LHS_ROWS]
                s_ref[b, h] = s0 * ga_scr[h, row, :] + _dot_tn(lk[n].astype(BF16), v_new.astype(BF16))
                results.append((h, row, v_new, wq_s))
        for h, row, v_new, wq_s in results:
            for t in range(n_t):
                u_scr[h, t, row, :] = v_new[t:t + 1, :]
                qd_scr[h, t, row, :] = wq_s[LHS_ROWS + t:LHS_ROWS + t + 1, :]
        return carry

    lax.fori_loop(0, n_b // 2, per_pair, 0)

    for h in range(GDN_HEADS):
        hc = slice(h * GDN_DV, (h + 1) * GDN_DV)
        for i in range(n_t):
            o = qd_scr[h, i]
            for j in range(i + 1):
                o = o + scores[h][i][j] * u_scr[h, j]
            o = o * lax.rsqrt(jnp.mean(o * o, axis=-1, keepdims=True) + EPS) * nw_ref[...]
            o_ref[i, :, hc] = (o * _silu(z_ref[i, :, hc])).astype(o_ref.dtype)


def gdn_sample(y3, nar3, mixed3, buf0, states, new_states, layer, t0, n_t, lw):
    n_b = y3.shape[1]
    tb = _sample_block(t0, n_t)
    params = _gdn_params(lw, 1)[:4]
    tok = lambda s: pl.BlockSpec((n_t, SAMPLE_BB, GDN_QKV), lambda i: (tb, i, PK_OFF[s] // GDN_QKV))
    bufspec = pl.BlockSpec((CONV_W - 1, SAMPLE_BB, GDN_CONV_DIM), lambda i: (0, i, 0))
    state = pl.BlockSpec((None, SAMPLE_BB, GDN_HEADS, GDN_DK, GDN_DV), lambda i: (layer, i, 0, 0, 0))
    per_tok = pltpu.VMEM((GDN_HEADS, n_t, SAMPLE_BB, GDN_DK), F32)
    tile = lambda rows: pltpu.VMEM((2 * GDN_HEADS, rows, GDN_DK), F32)
    n_in = 7 + len(params)
    mixed3, s, nbuf = pl.pallas_call(
        _gdn_sample_kernel,
        grid=(n_b // SAMPLE_BB,),
        in_specs=[tok(1), tok(2), tok(3), tok(4),
                  pl.BlockSpec((n_t, SAMPLE_BB, NARROW_W), lambda i: (tb, i, 0)), bufspec, state]
                 + [_const_spec(p.shape) for p in params] + [_ANY_SPEC, _ANY_SPEC],
        out_specs=[pl.BlockSpec((n_t, SAMPLE_BB, GDN_QKV), lambda i: (tb, i, MIX_GDN)), state, bufspec],
        out_shape=[jax.ShapeDtypeStruct(mixed3.shape, mixed3.dtype), jax.ShapeDtypeStruct(states.shape, F32),
                   jax.ShapeDtypeStruct((CONV_W - 1, n_b, GDN_CONV_DIM), F32)],
        input_output_aliases={n_in: 0, n_in + 1: 1},
        scratch_shapes=[per_tok, per_tok, per_tok, per_tok, pltpu.VMEM((GDN_HEADS, SAMPLE_BB, GDN_DV), F32),
                        tile(2 * LHS_ROWS), tile(LHS_ROWS), tile(LHS_ROWS)],
        compiler_params=pltpu.CompilerParams(
            dimension_semantics=("arbitrary",), vmem_limit_bytes=V7X_VMEM_LIMIT),
        name="gdn_sample",
    )(y3, y3, y3, y3, nar3, jnp.swapaxes(buf0, 0, 1), states, *params, mixed3, new_states)
    return mixed3, jnp.swapaxes(nbuf, 0, 1), s


TM = 1088
TM_SPLIT = 512


def kernel(x_prompt, x_sample, p_prompt, p_sample, state_s5_re, state_s5_im, state_gdn, state_gdn_conv, state_ssd, state_ssd_conv, state_ret, norm_mix, w_in, s5_a_re, s5_a_im, s5_b_re, s5_b_im, s5_c_re, s5_c_im, s5_d, s5_log_step, s5_w_glu, s5_b_glu, gdn_conv_w, gdn_a_log, gdn_dt_bias, gdn_norm_w, ssd_conv_w, ssd_conv_b, ssd_dt_bias, ssd_a_log, ssd_d, ssd_norm_w, ret_ln_w, ret_ln_b, w_out, norm_ffn, w_ffn_in, w_ffn_out, norm_ple, w_ple_gate, w_ple_proj, norm_final):
    bp, lp, d = x_prompt.shape
    bs, ls, _ = x_sample.shape
    np_tok = bp * lp
    ns_tok = bs * ls
    n_tok = np_tok + ns_tok

    mixer_w = dict(
        s5_a_re=s5_a_re, s5_a_im=s5_a_im, s5_b_re=s5_b_re, s5_b_im=s5_b_im, s5_c_re=s5_c_re, s5_c_im=s5_c_im,
        s5_d=s5_d, s5_log_step=s5_log_step, s5_w_glu=s5_w_glu, s5_b_glu=s5_b_glu,
        gdn_conv_w=gdn_conv_w, gdn_a_log=gdn_a_log, gdn_dt_bias=gdn_dt_bias, gdn_norm_w=gdn_norm_w,
        ssd_conv_w=ssd_conv_w, ssd_conv_b=ssd_conv_b, ssd_dt_bias=ssd_dt_bias, ssd_a_log=ssd_a_log,
        ssd_d=ssd_d, ssd_norm_w=ssd_norm_w, ret_ln_w=ret_ln_w, ret_ln_b=ret_ln_b)

    assert bp == 4, "the prompt S5 kernel packs two time steps of four sequences per vreg"

    def tm_rows(t):
        t = jnp.swapaxes(t, -3, -2)
        return t.reshape(t.shape[:-3] + (t.shape[-3] * t.shape[-2], t.shape[-1]))

    def bm_seqs(t, b):
        return jnp.swapaxes(t.reshape(t.shape[0] // b, b, t.shape[1]), 0, 1)

    w_out_b, w_ffn_in_b, w_ffn_out_b = (w.astype(BF16) for w in (w_out, w_ffn_in, w_ffn_out))
    w_gate_b, w_proj_b = w_ple_gate.astype(BF16), w_ple_proj.astype(BF16)
    pp = p_prompt.reshape(DEPTH, np_tok, PLE_DIM)
    ps = tm_rows(p_sample)

    h = jnp.concatenate([x_prompt.reshape(np_tok, d), tm_rows(x_sample)], axis=0)
    ssd_t = jnp.swapaxes(state_ssd, -1, -2)
    new_gdn, new_ssd_t, new_ret = (uninitialized(s.shape, F32) for s in (state_gdn, ssd_t, state_ret))
    new_p, new_s = [], []
    for i in range(DEPTH):
        lw = {k: v[i] for k, v in mixer_w.items()}
        y, nar = in_projection(h, norm_mix[i], w_in[i], tm=TM)
        y3 = y.reshape(n_tok // bs, bs, PK_TOTAL)
        nar3 = nar.reshape(n_tok // bs, bs, NARROW_W)
        t0 = np_tok // bs

        tb = _s5_tables(lw)
        a_p, st5_p = s5_prompt(tm_rows(y[:np_tok, :S5_CH].reshape(bp, lp, S5_CH)), tb, rows=512)

        mixed = uninitialized((n_tok, d), BF16)
        mixed, gdn_buf_p, gdn_s_p = gdn_prompt(y, nar, mixed, bp, lp, lw)
        mixed, ssd_buf_p, ssd_s_p = ssd_prompt(y, nar, mixed, bp, lp, lw)
        mixed, ret_s_p = ret_prompt(y, mixed, bp, lp, lw)
        mixed, st5_s = s5_sample(y, mixed, _s5_state_to_lanes(state_s5_re[i], state_s5_im[i]), np_tok, ns_tok, tb)
        mixed3 = mixed.reshape(n_tok // bs, bs, d)
        mixed3, gdn_buf_s, new_gdn = gdn_sample(y3, nar3, mixed3, state_gdn_conv[i], state_gdn, new_gdn, i, t0, ls, lw)
        mixed3, ssd_buf_s, new_ssd_t = ssd_sample(y3, nar3, mixed3, state_ssd_conv[i], ssd_t, new_ssd_t, i, t0, ls, lw)
        mixed3, new_ret = ret_sample(y3, mixed3, state_ret, new_ret, i, t0, ls, PAST_LEN, lw)
        mixed = lax.dynamic_update_slice(mixed3.reshape(n_tok, d), bm_seqs(a_p, bp).reshape(np_tok, S5_CH), (0, 0))
        new_p.append(_s5_lanes_to_state(st5_p[:bp]) + (gdn_s_p, gdn_buf_p, ssd_s_p, ssd_buf_p, ret_s_p))
        new_s.append(_s5_lanes_to_state(st5_s) + (gdn_buf_s, ssd_buf_s))

        h = matmul_residual(mixed, w_out_b, i, h, tm=TM)
        h = ffn_residual(h, norm_ffn[i], w_ffn_in_b, w_ffn_out_b, i, tm=TM // 2, th=512)
        if i + 1 < DEPTH:
            h = ple_residual(h, norm_ple[i], w_gate_b, pp, ps, w_proj_b, i, tm=TM_SPLIT)
        else:
            y_p, y_s = ple_residual(h, norm_ple[i], w_gate_b, pp, ps, w_proj_b, i, tm=TM_SPLIT, nf=norm_final)

    stack_p = [jnp.stack([st[j] for st in new_p]) for j in range(7)]
    s5_re_s, s5_im_s, gdn_buf_s, ssd_buf_s = (jnp.stack([st[j] for st in new_s]) for j in range(4))
    return (y_p.reshape(bp, lp, d), bm_seqs(y_s, bs), *stack_p,
            s5_re_s, s5_im_s, new_gdn, gdn_buf_s, jnp.swapaxes(new_ssd_t, -1, -2), ssd_buf_s, new_ret)
```

```python
import functools
import math

import jax
import jax.numpy as jnp
import numpy as np
from jax import lax
from jax.experimental import pallas as pl
from jax.experimental.pallas import tpu as pltpu

F32 = jnp.float32
BF16 = jnp.bfloat16

D_MODEL = 2048
DEPTH = 2
GROUP_WIDTH = D_MODEL // 4
CONV_W = 4
EPS = 1e-6
PLE_DIM = 256
FFN_HIDDEN = ((8 * D_MODEL + 3 * 256 - 1) // (3 * 256)) * 256

S5_CH = GROUP_WIDTH
S5_GROUP_CH = 16
S5_GROUPS = S5_CH // S5_GROUP_CH
S5_STATE = 64

GDN_HEADS = 4
GDN_DK = GROUP_WIDTH // GDN_HEADS
GDN_DV = GROUP_WIDTH // GDN_HEADS
GDN_CHUNK = 64
GDN_CONV_DIM = 2 * GDN_HEADS * GDN_DK + GDN_HEADS * GDN_DV

SSD_INNER = GROUP_WIDTH
SSD_HEADDIM = 64
SSD_HEADS = SSD_INNER // SSD_HEADDIM
SSD_NGROUPS = 2
SSD_STATE = 128
SSD_CHUNK = 128
SSD_CONV_DIM = SSD_INNER + 2 * SSD_NGROUPS * SSD_STATE

RET_HEADS = 4
RET_DK = GROUP_WIDTH // RET_HEADS
RET_DV = GROUP_WIDTH // RET_HEADS
RET_CHUNK = 128
ROPE_BASE = 10000.0
PAST_LEN = 16384

IN_SIZES = (
    S5_CH,
    GDN_HEADS * GDN_DK, GDN_HEADS * GDN_DK, GDN_HEADS * GDN_DV, GDN_HEADS * GDN_DV, GDN_HEADS, GDN_HEADS,
    SSD_INNER, SSD_CONV_DIM, SSD_HEADS,
    RET_HEADS * RET_DK, RET_HEADS * RET_DK, RET_HEADS * RET_DV, RET_HEADS * RET_DV,
)
IN_OFFS = tuple(int(v) for v in np.cumsum((0,) + IN_SIZES))

_REGIONS = ((0, 1, 2, 3, 4), (7, 8), (10, 11, 12, 13))
_NARROW = (5, 6, 9)
PK_OFF = {}
_o = 0
for _reg in _REGIONS:
    for _s in _reg:
        PK_OFF[_s] = _o
        _o += IN_SIZES[_s]
PK_TOTAL = _o
IN_TN = 512
REGION_TILES = tuple(sum(IN_SIZES[s] for s in reg) // IN_TN for reg in _REGIONS)
NARROW_W = 128

V7X_VMEM_LIMIT = 58 * 1024 * 1024
RECURRENCE_UNROLL = 4


def _split_w_in(w_in):
    wide = [w_in[:, IN_OFFS[reg[0]]:IN_OFFS[reg[-1] + 1]].astype(BF16) for reg in _REGIONS]
    nar = jnp.concatenate([w_in[:, IN_OFFS[s]:IN_OFFS[s + 1]] for s in _NARROW], axis=1)
    nar = jnp.pad(nar, ((0, 0), (0, NARROW_W - nar.shape[1]))).astype(BF16)
    return wide, nar


def _rms_rows(x, nw):
    ms = jnp.mean(x * x, axis=-1, keepdims=True)
    return x * lax.rsqrt(ms + EPS) * nw


def _in_proj_kernel(x_ref, nw_ref, wa_ref, wb_ref, wc_ref, wn_ref, o_ref, nar_ref, xn_ref):
    j = pl.program_id(1)

    @pl.when(j == 0)
    def _():
        xn_ref[...] = _rms_rows(x_ref[...], nw_ref[...]).astype(BF16)
        nar_ref[...] = jnp.dot(xn_ref[...], wn_ref[...], preferred_element_type=F32)

    first = 0
    for w_ref, n_tiles in zip((wa_ref, wb_ref, wc_ref), REGION_TILES):
        @pl.when((j >= first) & (j < first + n_tiles))
        def _(w_ref=w_ref):
            o_ref[...] = jnp.dot(xn_ref[...], w_ref[...], preferred_element_type=F32)
        first += n_tiles


def in_projection(x, nw, w_in, *, tm):
    m, k = x.shape
    wide, nar = _split_w_in(w_in)
    starts = [sum(REGION_TILES[:r]) for r in range(len(REGION_TILES))]

    def region_spec(r):
        return pl.BlockSpec((k, IN_TN), lambda i, j: (0, jnp.clip(j - starts[r], 0, REGION_TILES[r] - 1)))

    return pl.pallas_call(
        _in_proj_kernel,
        grid=(m // tm, PK_TOTAL // IN_TN),
        in_specs=[
            pl.BlockSpec((tm, k), lambda i, j: (i, 0)),
            pl.BlockSpec((1, k), lambda i, j: (0, 0)),
            region_spec(0), region_spec(1), region_spec(2),
            pl.BlockSpec((k, NARROW_W), lambda i, j: (0, 0)),
        ],
        out_specs=[pl.BlockSpec((tm, IN_TN), lambda i, j: (i, j)),
                   pl.BlockSpec((tm, NARROW_W), lambda i, j: (i, 0))],
        out_shape=[jax.ShapeDtypeStruct((m, PK_TOTAL), F32), jax.ShapeDtypeStruct((m, NARROW_W), F32)],
        scratch_shapes=[pltpu.VMEM((tm, k), BF16)],
        compiler_params=pltpu.CompilerParams(
            dimension_semantics=("arbitrary", "arbitrary"), vmem_limit_bytes=V7X_VMEM_LIMIT),
        name="in_projection",
    )(x, nw.reshape(1, k), *wide, nar)


def _silu(x):
    return x * jax.nn.sigmoid(x)


def _ffn_kernel(a_ref, wmix_ref, h_ref, nw_ref, wg_ref, wu_ref, wo_ref, o_ref, xn_ref):
    @pl.when(pl.program_id(1) == 0)
    def _():
        h = h_ref[...] + jnp.dot(a_ref[...], wmix_ref[...], preferred_element_type=F32)
        xn_ref[...] = _rms_rows(h, nw_ref[...]).astype(BF16)
        o_ref[...] = h

    xn = xn_ref[...]
    gate = jnp.dot(xn, wg_ref[...], preferred_element_type=F32)
    up = jnp.dot(xn, wu_ref[...], preferred_element_type=F32)
    act = (_silu(gate) * up).astype(BF16)
    o_ref[...] += jnp.dot(act, wo_ref[...], preferred_element_type=F32)


def ffn_residual(a, w_mix, h, nw, w_in, w_out, layer, *, tm, th):
    m, k = h.shape
    hidden = w_out.shape[1]
    nj = hidden // th
    return pl.pallas_call(
        _ffn_kernel,
        grid=(m // tm, nj),
        in_specs=[
            pl.BlockSpec((tm, k), lambda i, j: (i, 0)),
            pl.BlockSpec((None, k, k), lambda i, j: (layer, 0, 0), pipeline_mode=pl.Buffered(1)),
            pl.BlockSpec((tm, k), lambda i, j: (i, 0)),
            pl.BlockSpec((1, k), lambda i, j: (0, 0)),
            pl.BlockSpec((None, k, th), lambda i, j: (layer, 0, j)),
            pl.BlockSpec((None, k, th), lambda i, j: (layer, 0, j + nj)),
            pl.BlockSpec((None, th, k), lambda i, j: (layer, j, 0)),
        ],
        out_specs=pl.BlockSpec((tm, k), lambda i, j: (i, 0)),
        out_shape=jax.ShapeDtypeStruct((m, k), F32),
        scratch_shapes=[pltpu.VMEM((tm, k), BF16)],
        compiler_params=pltpu.CompilerParams(
            dimension_semantics=("arbitrary", "arbitrary"), vmem_limit_bytes=V7X_VMEM_LIMIT),
        name="ffn_residual",
    )(a, w_mix, h, nw.reshape(1, k), w_in, w_in, w_out)


def _ple_rows(h_ref, nw_ref, wg_ref, pp_ref, ps_ref, wp_ref, n_prompt_tiles):
    h = h_ref[...]
    xn = _rms_rows(h, nw_ref[...]).astype(BF16)
    gate = jax.nn.sigmoid(jnp.dot(xn, wg_ref[...], preferred_element_type=F32))
    p = jnp.where(pl.program_id(0) < n_prompt_tiles, pp_ref[...], ps_ref[...])
    return h + gate * jnp.dot(p.astype(BF16), wp_ref[...], preferred_element_type=F32)


def _ple_kernel(h_ref, nw_ref, wg_ref, pp_ref, ps_ref, wp_ref, o_ref, *, n_prompt_tiles):
    o_ref[...] = _ple_rows(h_ref, nw_ref, wg_ref, pp_ref, ps_ref, wp_ref, n_prompt_tiles)


def _ple_final_kernel(h_ref, nw_ref, wg_ref, pp_ref, ps_ref, wp_ref, nf_ref, yp_ref, ys_ref, *, n_prompt_tiles):
    y = _rms_rows(_ple_rows(h_ref, nw_ref, wg_ref, pp_ref, ps_ref, wp_ref, n_prompt_tiles), nf_ref[...])

    @pl.when(pl.program_id(0) < n_prompt_tiles)
    def _():
        yp_ref[...] = y

    @pl.when(pl.program_id(0) >= n_prompt_tiles)
    def _():
        ys_ref[...] = y


def ple_residual(h, nw, wg, pp, ps, wp, layer, *, tm, nf=None):
    m, k = h.shape
    mp, pd = pp.shape[1], pp.shape[2]
    ms = ps.shape[1]
    assert mp % tm == 0 and ms % tm == 0 and mp + ms == m
    npt, nst = mp // tm, ms // tm
    row = pl.BlockSpec((tm, k), lambda i: (i, 0))
    vec = pl.BlockSpec((1, k), lambda i: (0, 0))
    in_specs = [row, vec,
                pl.BlockSpec((None, k, k), lambda i: (layer, 0, 0)),
                pl.BlockSpec((None, tm, pd), lambda i: (layer, jnp.minimum(i, npt - 1), 0)),
                pl.BlockSpec((None, tm, pd), lambda i: (layer, jnp.clip(i - npt, 0, nst - 1), 0)),
                pl.BlockSpec((None, pd, k), lambda i: (layer, 0, 0))]
    args = [h, nw.reshape(1, k), wg, pp, ps, wp]
    params = pltpu.CompilerParams(dimension_semantics=("arbitrary",), vmem_limit_bytes=V7X_VMEM_LIMIT)
    if nf is None:
        return pl.pallas_call(
            functools.partial(_ple_kernel, n_prompt_tiles=npt), grid=(m // tm,), in_specs=in_specs,
            out_specs=row, out_shape=jax.ShapeDtypeStruct((m, k), F32), compiler_params=params,
            name="ple_residual")(*args)
    return pl.pallas_call(
        functools.partial(_ple_final_kernel, n_prompt_tiles=npt), grid=(m // tm,), in_specs=in_specs + [vec],
        out_specs=[pl.BlockSpec((tm, k), lambda i: (jnp.minimum(i, npt - 1), 0)),
                   pl.BlockSpec((tm, k), lambda i: (jnp.clip(i - npt, 0, nst - 1), 0))],
        out_shape=[jax.ShapeDtypeStruct((mp, k), F32), jax.ShapeDtypeStruct((ms, k), F32)],
        compiler_params=params, name="ple_final")(*args, nf.reshape(1, k))


S5_HALF_CH = S5_CH // 2
S5_HALF_ST = (S5_GROUPS // 2) * S5_STATE
S5_LANES = 4 * S5_HALF_ST
S5_SLAB = 512


def _s5_tables(lw):
    a_re = lw['s5_a_re'].astype(F32)
    a_im = lw['s5_a_im'].astype(F32)
    step = jnp.exp(lw['s5_log_step'].astype(F32))[:, None]
    mag = jnp.exp(a_re * step)
    lam_re = mag * jnp.cos(a_im * step)
    lam_im = mag * jnp.sin(a_im * step)
    den = a_re * a_re + a_im * a_im
    coef_re = ((lam_re - 1.0) * a_re + lam_im * a_im) / den
    coef_im = (lam_im * a_re - (lam_re - 1.0) * a_im) / den
    b_re = lw['s5_b_re'].astype(F32)
    b_im = lw['s5_b_im'].astype(F32)
    bb_re = coef_re[..., None] * b_re - coef_im[..., None] * b_im
    bb_im = coef_re[..., None] * b_im + coef_im[..., None] * b_re
    gh = S5_GROUPS // 2
    eye = jnp.eye(gh, dtype=F32)

    def in_blockdiag(b):
        return jnp.einsum('gnc,gh->gchn', b, eye).reshape(gh * S5_GROUP_CH, gh * S5_STATE)

    def out_blockdiag(c):
        return jnp.einsum('gcn,gh->gnhc', c, eye).reshape(gh * S5_STATE, gh * S5_GROUP_CH)

    c_re = lw['s5_c_re'].astype(F32)
    c_im = lw['s5_c_im'].astype(F32)
    bb = jnp.stack([jnp.concatenate([in_blockdiag(bb_re[h * gh:(h + 1) * gh]),
                                     in_blockdiag(bb_im[h * gh:(h + 1) * gh])], axis=1) for h in range(2)])
    cm = jnp.stack([jnp.concatenate([out_blockdiag(c_re[h * gh:(h + 1) * gh]),
                                     -out_blockdiag(c_im[h * gh:(h + 1) * gh])], axis=0) for h in range(2)])
    lam = jnp.stack([lam_re.reshape(-1), lam_im.reshape(-1)])
    lam2 = jnp.stack([lam[0] * lam[0] - lam[1] * lam[1], 2.0 * lam[0] * lam[1]])
    return dict(bb=bb.astype(BF16), cm=cm.astype(BF16), lam=lam, lam2=lam2,
                d=lw['s5_d'].astype(F32).reshape(1, S5_CH), wglu=lw['s5_w_glu'].astype(BF16),
                bglu=lw['s5_b_glu'].astype(F32).reshape(1, S5_CH))


def _s5_drive(u, bb_ref, sc_ref):
    ub = u.astype(BF16)
    for hf in range(2):
        sc_ref[:, hf * 2 * S5_HALF_ST:(hf + 1) * 2 * S5_HALF_ST] = jnp.dot(
            ub[:, hf * S5_HALF_CH:(hf + 1) * S5_HALF_CH], bb_ref[hf], preferred_element_type=F32)


def _s5_readout(sc_ref, u, cm_ref, d_ref, wglu_ref, bglu_ref):
    ys = [jnp.dot(sc_ref[:, hf * 2 * S5_HALF_ST:(hf + 1) * 2 * S5_HALF_ST].astype(BF16), cm_ref[hf],
                  preferred_element_type=F32) for hf in range(2)]
    y = jnp.concatenate(ys, axis=1) + d_ref[...] * u
    y = jax.nn.gelu(y)
    z = jnp.dot(y.astype(BF16), wglu_ref[...], preferred_element_type=F32) + bglu_ref[...]
    return y * jax.nn.sigmoid(z)


def _s5_slabs():
    for hf in range(2):
        for sl in range(S5_HALF_ST // S5_SLAB):
            re0 = hf * 2 * S5_HALF_ST + sl * S5_SLAB
            yield re0, re0 + S5_HALF_ST, hf * S5_HALF_ST + sl * S5_SLAB


def _s5_prompt_kernel(u_ref, bb_ref, m_ref, cm_ref, d_ref, wglu_ref, bglu_ref, o_ref, st_ref, sc_ref, carry_ref):
    @pl.when(pl.program_id(0) == 0)
    def _():
        carry_ref[...] = jnp.zeros_like(carry_ref)

    u = u_ref[...]
    _s5_drive(u, bb_ref, sc_ref)
    first_step = lax.broadcasted_iota(jnp.int32, (8, S5_SLAB), 0) < 4
    n_pairs = u_ref.shape[0] // 8
    for re0, im0, l0 in _s5_slabs():
        mr = m_ref[0, :, l0:l0 + S5_SLAB]
        mi = m_ref[1, :, l0:l0 + S5_SLAB]
        nr = m_ref[2, :, l0:l0 + S5_SLAB]
        ni = m_ref[3, :, l0:l0 + S5_SLAB]

        def body(k, carry, re0=re0, im0=im0, mr=mr, mi=mi, nr=nr, ni=ni):
            hr, hi = carry
            base = pl.multiple_of(k * 8, 8)
            xr = sc_ref[pl.ds(base, 8), re0:re0 + S5_SLAB]
            xi = sc_ref[pl.ds(base, 8), im0:im0 + S5_SLAB]
            xr_s = pltpu.roll(xr, 4, 0)
            xi_s = pltpu.roll(xi, 4, 0)
            outr = (mr * hr - mi * hi) + xr + (nr * xr_s - ni * xi_s)
            outi = (mr * hi + mi * hr) + xi + (nr * xi_s + ni * xr_s)
            sc_ref[pl.ds(base, 8), re0:re0 + S5_SLAB] = outr
            sc_ref[pl.ds(base, 8), im0:im0 + S5_SLAB] = outi
            return (jnp.where(first_step, pltpu.roll(outr, 4, 0), outr),
                    jnp.where(first_step, pltpu.roll(outi, 4, 0), outi))

        hr, hi = lax.fori_loop(0, n_pairs, body,
                               (carry_ref[:, re0:re0 + S5_SLAB], carry_ref[:, im0:im0 + S5_SLAB]))
        carry_ref[:, re0:re0 + S5_SLAB] = hr
        carry_ref[:, im0:im0 + S5_SLAB] = hi

    o_ref[...] = _s5_readout(sc_ref, u, cm_ref, d_ref, wglu_ref, bglu_ref).astype(o_ref.dtype)
    st_ref[...] = carry_ref[...]


def _const_spec(shape):
    return pl.BlockSpec(shape, lambda c: (0,) * len(shape))


def s5_prompt(u_tm, tb, *, rows):
    n = u_tm.shape[0]
    zero = jnp.zeros_like(tb['lam'])
    m = jnp.stack([jnp.concatenate([jnp.broadcast_to(a[k][None], (4, a.shape[1])),
                                    jnp.broadcast_to(b[k][None], (4, b.shape[1]))], axis=0)
                   for a, b, k in ((tb['lam'], tb['lam2'], 0), (tb['lam'], tb['lam2'], 1),
                                   (zero, tb['lam'], 0), (zero, tb['lam'], 1))])
    return pl.pallas_call(
        _s5_prompt_kernel,
        grid=(n // rows,),
        in_specs=[
            pl.BlockSpec((rows, S5_CH), lambda c: (c, 0)),
            _const_spec(tb['bb'].shape), _const_spec(m.shape), _const_spec(tb['cm'].shape),
            _const_spec((1, S5_CH)), _const_spec((S5_CH, S5_CH)), _const_spec((1, S5_CH)),
        ],
        out_specs=[pl.BlockSpec((rows, S5_CH), lambda c: (c, 0)), _const_spec((8, S5_LANES))],
        out_shape=[jax.ShapeDtypeStruct((n, S5_CH), BF16), jax.ShapeDtypeStruct((8, S5_LANES), F32)],
        scratch_shapes=[pltpu.VMEM((rows, S5_LANES), F32), pltpu.VMEM((8, S5_LANES), F32)],
        compiler_params=pltpu.CompilerParams(
            dimension_semantics=("arbitrary",), vmem_limit_bytes=V7X_VMEM_LIMIT),
        name="s5_prompt",
    )(u_tm, tb['bb'], m, tb['cm'], tb['d'], tb['wglu'], tb['bglu'])


def _s5_sample_kernel(u_ref, h0_ref, bb_ref, lam_ref, cm_ref, d_ref, wglu_ref, bglu_ref, mix_ref,
                      o_ref, st_ref, sc_ref):
    del mix_ref
    u = u_ref[...]
    _s5_drive(u, bb_ref, sc_ref)
    n_seq = h0_ref.shape[0]
    n_steps = u_ref.shape[0] // n_seq
    for re0, im0, l0 in _s5_slabs():
        lr = lam_ref[0:1, l0:l0 + S5_SLAB]
        li = lam_ref[1:2, l0:l0 + S5_SLAB]

        def body(rb, _, re0=re0, im0=im0, lr=lr, li=li):
            r0 = pl.multiple_of(rb * 8, 8)
            hr = h0_ref[pl.ds(r0, 8), re0:re0 + S5_SLAB]
            hi = h0_ref[pl.ds(r0, 8), im0:im0 + S5_SLAB]
            for t in range(n_steps):
                rows = pl.ds(t * n_seq + r0, 8)
                nr = (lr * hr - li * hi) + sc_ref[rows, re0:re0 + S5_SLAB]
                ni = (lr * hi + li * hr) + sc_ref[rows, im0:im0 + S5_SLAB]
                sc_ref[rows, re0:re0 + S5_SLAB] = nr
                sc_ref[rows, im0:im0 + S5_SLAB] = ni
                hr, hi = nr, ni
            st_ref[pl.ds(r0, 8), re0:re0 + S5_SLAB] = hr
            st_ref[pl.ds(r0, 8), im0:im0 + S5_SLAB] = hi
            return 0

        lax.fori_loop(0, n_seq // 8, body, 0)

    o_ref[...] = _s5_readout(sc_ref, u, cm_ref, d_ref, wglu_ref, bglu_ref).astype(o_ref.dtype)


def s5_sample(y, mixed, h0, row0, n_rows, tb):
    nb = h0.shape[0]
    assert row0 % n_rows == 0
    blk = row0 // n_rows
    return pl.pallas_call(
        _s5_sample_kernel,
        grid=(1,),
        in_specs=[pl.BlockSpec((n_rows, S5_CH), lambda c: (blk, PK_OFF[0] // S5_CH)),
                  _const_spec((nb, S5_LANES)), _const_spec(tb['bb'].shape),
                  _const_spec(tb['lam'].shape), _const_spec(tb['cm'].shape), _const_spec((1, S5_CH)),
                  _const_spec((S5_CH, S5_CH)), _const_spec((1, S5_CH)), _ANY_SPEC],
        out_specs=[pl.BlockSpec((n_rows, S5_CH), lambda c: (blk, MIX_S5)), _const_spec((nb, S5_LANES))],
        out_shape=[jax.ShapeDtypeStruct(mixed.shape, mixed.dtype), jax.ShapeDtypeStruct((nb, S5_LANES), F32)],
        input_output_aliases={8: 0},
        scratch_shapes=[pltpu.VMEM((n_rows, S5_LANES), F32)],
        compiler_params=pltpu.CompilerParams(
            dimension_semantics=("arbitrary",), vmem_limit_bytes=V7X_VMEM_LIMIT),
        name="s5_sample",
    )(y, h0, tb['bb'], tb['lam'], tb['cm'], tb['d'], tb['wglu'], tb['bglu'], mixed)


def _s5_state_to_lanes(re, im):
    b = re.shape[0]
    return jnp.stack([re.reshape(b, 2, S5_HALF_ST), im.reshape(b, 2, S5_HALF_ST)], axis=2).reshape(b, S5_LANES)


def _s5_lanes_to_state(st):
    b = st.shape[0]
    st = st.reshape(b, 2, 2, S5_HALF_ST)
    return st[:, :, 0].reshape(b, S5_GROUPS, S5_STATE), st[:, :, 1].reshape(b, S5_GROUPS, S5_STATE)


def _ret_tables(positions, chunk):
    half = RET_DK // 2
    inv_freq = ROPE_BASE ** (-jnp.arange(half, dtype=F32) / half)
    ang = positions.astype(F32)[:, None] * inv_freq[None, :]
    cos = jnp.cos(ang)
    sin = jnp.sin(ang)
    cos2 = jnp.concatenate([cos, cos], axis=1)
    sin2 = jnp.concatenate([-sin, sin], axis=1)
    log_gamma = jnp.log(1.0 - 2.0 ** (-5.0 - jnp.arange(RET_HEADS, dtype=F32)))
    g = (jnp.arange(chunk, dtype=F32) + 1.0)[None, :] * log_gamma[:, None]
    diff = g[:, :, None] - g[:, None, :]
    causal = jnp.tril(jnp.ones((chunk, chunk), dtype=bool))
    dmat = jnp.where(causal, jnp.exp(jnp.where(causal, diff, 0.0)), 0.0)
    lanes = (RET_HEADS, chunk, RET_DK)
    qdec = jnp.broadcast_to(jnp.exp(g)[:, :, None], lanes)
    kdec = jnp.broadcast_to(jnp.exp(g[:, -1:] - g)[:, :, None], lanes)
    gall = jnp.broadcast_to(jnp.exp(g[:, -1])[:, None, None], (RET_HEADS, 1, RET_DV))
    return cos2, sin2, dmat, qdec, kdec, gall


def _ret_rotate(x, cos2, sin2):
    return x * cos2 + pltpu.roll(x, RET_DK // 2, 1) * sin2


def _group_layernorm_gate(o, gate, w, b):
    mu = jnp.mean(o, axis=-1, keepdims=True)
    xc = o - mu
    var = jnp.mean(xc * xc, axis=-1, keepdims=True)
    return _silu(gate) * (xc * lax.rsqrt(var + EPS) * w + b)


def _dot_nt(a, b):
    return lax.dot_general(a, b, (((1,), (1,)), ((), ())), preferred_element_type=F32)


def _dot_tn(a, b):
    return lax.dot_general(a, b, (((0,), (0,)), ((), ())), preferred_element_type=F32)


def _ret_prompt_kernel(q_ref, k_ref, v_ref, g_ref, cos_ref, sin_ref, dm_ref, qd_ref, kd_ref, ga_ref,
                       lnw_ref, lnb_ref, mix_ref, o_ref, s_ref, *, chunk):
    del mix_ref
    s_ref[...] = jnp.zeros_like(s_ref)

    def body(c, carry):
        r = pl.ds(pl.multiple_of(c * chunk, chunk), chunk)
        cos2, sin2 = cos_ref[r, :], sin_ref[r, :]
        for h in range(RET_HEADS):
            hc = slice(h * RET_DK, (h + 1) * RET_DK)
            q = _ret_rotate(q_ref[r, hc], cos2, sin2)
            k = _ret_rotate(k_ref[r, hc], cos2, sin2) * (RET_DK ** -0.5)
            vb = v_ref[r, hc].astype(BF16)
            s = s_ref[0, h]
            scores = _dot_nt(q.astype(BF16), k.astype(BF16)) * dm_ref[h]
            o = jnp.dot(scores.astype(BF16), vb, preferred_element_type=F32)
            o = o + jnp.dot((q * qd_ref[h]).astype(BF16), s.astype(BF16), preferred_element_type=F32)
            o_ref[r, hc] = _group_layernorm_gate(o, g_ref[r, hc], lnw_ref[h], lnb_ref[h]).astype(o_ref.dtype)
            s_ref[0, h] = s * ga_ref[h] + _dot_tn((k * kd_ref[h]).astype(BF16), vb)
        return carry

    n_chunks = q_ref.shape[0] // chunk
    lax.fori_loop(0, n_chunks, body, 0, unroll=math.gcd(RECURRENCE_UNROLL, n_chunks))


MIX_S5, MIX_GDN, MIX_SSD, MIX_RET = range(4)
_ANY_SPEC = pl.BlockSpec(memory_space=pl.ANY)


def _uninit_kernel(o_ref):
    del o_ref


def uninitialized(shape, dtype):
    return pl.pallas_call(_uninit_kernel, out_shape=jax.ShapeDtypeStruct(shape, dtype), out_specs=_ANY_SPEC,
                          name="uninitialized")()


def ret_prompt(y, mixed, n_seq, seq_len, lw):
    chunk = min(RET_CHUNK, seq_len)
    cos2, sin2, dmat, qdec, kdec, gall = _ret_tables(jnp.arange(seq_len, dtype=jnp.int32), chunk)
    width = RET_HEADS * RET_DK
    heads = lambda shape: _const_spec((RET_HEADS,) + shape)
    tok = lambda s: pl.BlockSpec((seq_len, width), lambda b: (b, PK_OFF[s] // width))
    tab = _const_spec((seq_len, RET_DK))
    return pl.pallas_call(
        functools.partial(_ret_prompt_kernel, chunk=chunk),
        grid=(n_seq,),
        in_specs=[tok(10), tok(11), tok(12), tok(13), tab, tab,
                  heads((chunk, chunk)), heads((chunk, RET_DK)), heads((chunk, RET_DK)), heads((1, RET_DV)),
                  heads((1, RET_DV)), heads((1, RET_DV)), _ANY_SPEC],
        out_specs=[pl.BlockSpec((seq_len, width), lambda b: (b, MIX_RET)),
                   pl.BlockSpec((1, RET_HEADS, RET_DK, RET_DV), lambda b: (b, 0, 0, 0))],
        out_shape=[jax.ShapeDtypeStruct(mixed.shape, mixed.dtype),
                   jax.ShapeDtypeStruct((n_seq, RET_HEADS, RET_DK, RET_DV), F32)],
        input_output_aliases={12: 0},
        compiler_params=pltpu.CompilerParams(
            dimension_semantics=("arbitrary",), vmem_limit_bytes=V7X_VMEM_LIMIT),
        name="ret_prompt",
    )(y, y, y, y, cos2, sin2, dmat, qdec, kdec, gall,
      lw['ret_ln_w'].astype(F32).reshape(RET_HEADS, 1, RET_DV), lw['ret_ln_b'].astype(F32).reshape(RET_HEADS, 1, RET_DV),
      mixed)


SAMPLE_BB = 16
LHS_ROWS = 16


def _ret_sample_kernel(q_ref, k_ref, v_ref, g_ref, cos_ref, sin_ref, s0_ref, lnw_ref, lnb_ref, mix_ref, sall_ref,
                       o_ref, s_ref, qd_scr, kd_scr, v_scr, acc_scr, lq, lk, lv, *, decay):
    del mix_ref, sall_ref
    n_t, n_b = q_ref.shape[0], q_ref.shape[1]
    for h in range(RET_HEADS):
        hc = slice(h * RET_DK, (h + 1) * RET_DK)
        qs, ks, vs = [], [], []
        for t in range(n_t):
            cos2, sin2 = cos_ref[t:t + 1, :], sin_ref[t:t + 1, :]
            qs.append(_ret_rotate(q_ref[t, :, hc], cos2, sin2))
            ks.append(_ret_rotate(k_ref[t, :, hc], cos2, sin2) * (RET_DK ** -0.5))
            vs.append(v_ref[t, :, hc])
            qd_scr[h, t] = qs[t] * decay['q'][h][t]
            kd_scr[h, t] = ks[t] * decay['k'][h][t]
            v_scr[h, t] = vs[t]
        for i in range(n_t):
            acc = None
            for j in range(i + 1):
                term = (jnp.sum(qs[i] * ks[j], axis=-1, keepdims=True) * decay['m'][h][i][j]) * vs[j]
                acc = term if acc is None else acc + term
            acc_scr[h, i] = acc

    for tile in (lq, lk, lv):
        tile[...] = jnp.zeros_like(tile)

    def per_seq(b, carry):
        row = pl.ds(b, 1)
        for h in range(RET_HEADS):
            for t in range(n_t):
                lq[h, t:t + 1, :] = qd_scr[h, t, row, :]
                lk[h, t:t + 1, :] = kd_scr[h, t, row, :]
                lv[h, t:t + 1, :] = v_scr[h, t, row, :]
            s0 = s0_ref[b, h]
            inter = jnp.dot(lq[h].astype(BF16), s0.astype(BF16), preferred_element_type=F32)
            s_ref[b, h] = s0 * decay['all'][h] + _dot_tn(lk[h].astype(BF16), lv[h].astype(BF16))
            for t in range(n_t):
                acc_scr[h, t, row, :] = acc_scr[h, t, row, :] + inter[t:t + 1, :]
        return carry

    lax.fori_loop(0, n_b, per_seq, 0)

    for h in range(RET_HEADS):
        hc = slice(h * RET_DV, (h + 1) * RET_DV)
        for t in range(n_t):
            o_ref[t, :, hc] = _group_layernorm_gate(acc_scr[h, t], g_ref[t, :, hc], lnw_ref[h], lnb_ref[h]
                                                    ).astype(o_ref.dtype)


def _sample_block(t0, n_t):
    assert t0 % n_t == 0
    return t0 // n_t


def ret_sample(y3, mixed3, states, new_states, layer, t0, n_t, first_pos, lw):
    n_b = y3.shape[1]
    tb = _sample_block(t0, n_t)
    cos2, sin2, _, _, _, _ = _ret_tables(first_pos + jnp.arange(n_t, dtype=jnp.int32), n_t)
    gamma = 1.0 - 2.0 ** (-5.0 - np.arange(RET_HEADS, dtype=np.float64))
    decay = dict(m=[[[float(g ** (i - j)) for j in range(n_t)] for i in range(n_t)] for g in gamma],
                 q=[[float(g ** (i + 1)) for i in range(n_t)] for g in gamma],
                 k=[[float(g ** (n_t - 1 - j)) for j in range(n_t)] for g in gamma],
                 all=[float(g ** n_t) for g in gamma])
    width = RET_HEADS * RET_DK
    tok = lambda s: pl.BlockSpec((n_t, SAMPLE_BB, width), lambda i: (tb, i, PK_OFF[s] // width))
    state = pl.BlockSpec((None, SAMPLE_BB, RET_HEADS, RET_DK, RET_DV), lambda i: (layer, i, 0, 0, 0))
    per_tok = pltpu.VMEM((RET_HEADS, n_t, SAMPLE_BB, RET_DK), F32)
    tile = pltpu.VMEM((RET_HEADS, LHS_ROWS, RET_DK), F32)
    return pl.pallas_call(
        functools.partial(_ret_sample_kernel, decay=decay),
        grid=(n_b // SAMPLE_BB,),
        in_specs=[tok(10), tok(11), tok(12), tok(13), _const_spec((n_t, RET_DK)), _const_spec((n_t, RET_DK)),
                  state, _const_spec((RET_HEADS, 1, RET_DV)), _const_spec((RET_HEADS, 1, RET_DV)), _ANY_SPEC, _ANY_SPEC],
        out_specs=[pl.BlockSpec((n_t, SAMPLE_BB, width), lambda i: (tb, i, MIX_RET)), state],
        out_shape=[jax.ShapeDtypeStruct(mixed3.shape, mixed3.dtype), jax.ShapeDtypeStruct(states.shape, F32)],
        input_output_aliases={9: 0, 10: 1},
        scratch_shapes=[per_tok, per_tok, per_tok, per_tok, tile, tile, tile],
        compiler_params=pltpu.CompilerParams(
            dimension_semantics=("arbitrary",), vmem_limit_bytes=V7X_VMEM_LIMIT),
        name="ret_sample",
    )(y3, y3, y3, y3, cos2, sin2, states,
      lw['ret_ln_w'].astype(F32).reshape(RET_HEADS, 1, RET_DV), lw['ret_ln_b'].astype(F32).reshape(RET_HEADS, 1, RET_DV),
      mixed3, new_states)


NARROW_B = 0
NARROW_A = GDN_HEADS
NARROW_DT = 2 * GDN_HEADS
TAIL = 8


def _shift_rows(x, tail, s):
    xr = pltpu.roll(x, s, 0)
    tr = pltpu.roll(tail, s, 0)
    row = lax.broadcasted_iota(jnp.int32, tail.shape, 0)
    return jnp.concatenate([jnp.where(row < s, tr, xr[0:TAIL]), xr[TAIL:]], axis=0)


def _causal_conv(x, tail, w_ref):
    y = x * w_ref[CONV_W - 1:CONV_W, :]
    for s in range(1, CONV_W):
        y = y + _shift_rows(x, tail, s) * w_ref[CONV_W - 1 - s:CONV_W - s, :]
    return y


def _softplus(x):
    return jnp.maximum(x, 0.0) + jnp.log1p(jnp.exp(-jnp.abs(x)))


def _lane_row(vals, lane0, width=128):
    return jnp.pad(vals.astype(F32), (lane0, width - lane0 - vals.shape[0])).reshape(1, width)


SSD_PAIRS = SSD_HEADS // 2
SSD_BC = SSD_NGROUPS * SSD_STATE


def _ssd_prompt_kernel(z_ref, xbc_ref, nar_ref, cw_ref, cb_ref, dtb_ref, a_ref, dsk_ref, nw_ref, tri_ref, mix_ref,
                       o_ref, s_ref, tail_ref, *, chunk):
    del mix_ref
    s_ref[...] = jnp.zeros_like(s_ref)
    tail_ref[...] = jnp.zeros_like(tail_ref)
    causal = (lax.broadcasted_iota(jnp.int32, (chunk, chunk), 0)
              >= lax.broadcasted_iota(jnp.int32, (chunk, chunk), 1))
    lane = lax.broadcasted_iota(jnp.int32, (chunk, 2 * SSD_HEADDIM), 1)
    first_head = lane < SSD_HEADDIM
    rep = SSD_HEADS // SSD_NGROUPS

    def body(c, carry):
        r = pl.ds(pl.multiple_of(c * chunk, chunk), chunk)
        raw = xbc_ref[r, :]
        xbc = _silu(_causal_conv(raw, tail_ref[0], cw_ref) + cb_ref[...])
        tail_ref[0] = raw[chunk - TAIL:, :]
        xs = xbc[:, :SSD_INNER]
        dt = _softplus(nar_ref[r, :] + dtb_ref[...])
        g = _select_rows(tri_ref[...], dt * a_ref[...])
        e_in = jnp.exp(g)
        e_out = dt * jnp.exp(g[chunk - 1:chunk, :] - g)
        e_all = jnp.exp(g[chunk - 1:chunk, :])
        g_t = g.T
        dt_t = dt.T
        ys = []
        for p in range(SSD_PAIRS):
            grp = (2 * p) // rep
            bm = xbc[:, SSD_INNER + grp * SSD_STATE:SSD_INNER + (grp + 1) * SSD_STATE]
            cm = xbc[:, SSD_INNER + SSD_BC + grp * SSD_STATE:SSD_INNER + SSD_BC + (grp + 1) * SSD_STATE]
            cb = _dot_nt(cm.astype(BF16), bm.astype(BF16))
            xp = xs[:, p * 128:(p + 1) * 128]
            xpb = xp.astype(BF16)
            sp = s_ref[0, p]
            spb = sp.astype(BF16)
            outs, upds, gls = [], [], []
            for hh in range(2):
                ln = NARROW_DT + 2 * p + hh
                diff = g[:, ln:ln + 1] - g_t[ln:ln + 1, :]
                m = jnp.where(causal, cb * jnp.exp(jnp.where(causal, diff, 0.0)) * dt_t[ln:ln + 1, :], 0.0)
                o = jnp.dot(m.astype(BF16), xpb, preferred_element_type=F32)
                o = o + jnp.dot((cm * e_in[:, ln:ln + 1]).astype(BF16), spb, preferred_element_type=F32)
                outs.append(o)
                upds.append(_dot_tn((bm * e_out[:, ln:ln + 1]).astype(BF16), xpb))
                gls.append(e_all[:, ln:ln + 1])
            s_ref[0, p] = sp * jnp.where(first_head, gls[0], gls[1]) + jnp.where(first_head, upds[0], upds[1])
            ys.append(jnp.where(first_head, outs[0], outs[1]) + xp * dsk_ref[:, p * 128:(p + 1) * 128])
        y = jnp.concatenate(ys, axis=1) * _silu(z_ref[r, :])
        gw = SSD_INNER // SSD_NGROUPS
        yn = [y[:, i * gw:(i + 1) * gw] * lax.rsqrt(
            jnp.mean(y[:, i * gw:(i + 1) * gw] * y[:, i * gw:(i + 1) * gw], axis=-1, keepdims=True) + EPS)
            for i in range(SSD_NGROUPS)]
        o_ref[r, :] = (jnp.concatenate(yn, axis=1) * nw_ref[...]).astype(o_ref.dtype)
        return carry

    n_chunks = z_ref.shape[0] // chunk
    lax.fori_loop(0, n_chunks, body, 0, unroll=math.gcd(2, n_chunks))


def _ssd_params(lw, chunk):
    return (lw['ssd_conv_w'].astype(F32), lw['ssd_conv_b'].astype(F32).reshape(1, SSD_CONV_DIM),
            _lane_row(lw['ssd_dt_bias'], NARROW_DT), _lane_row(-jnp.exp(lw['ssd_a_log'].astype(F32)), NARROW_DT),
            jnp.repeat(lw['ssd_d'].astype(F32), SSD_HEADDIM).reshape(1, SSD_INNER),
            lw['ssd_norm_w'].astype(F32).reshape(1, SSD_INNER),
            jnp.tril(jnp.ones((chunk, chunk), BF16)))


def _ssd_state_from_pairs(s):
    b = s.shape[0]
    s = s.reshape(b, SSD_PAIRS, SSD_STATE, 2, SSD_HEADDIM)
    return jnp.swapaxes(s, 2, 3).reshape(b, SSD_HEADS, SSD_STATE, SSD_HEADDIM)


def ssd_prompt(y, nar, mixed, n_seq, seq_len, lw):
    chunk = min(SSD_CHUNK, seq_len)
    params = _ssd_params(lw, chunk)
    tok = lambda s, w: pl.BlockSpec((seq_len, w), lambda b: (b, PK_OFF[s] // w))
    n_in = 3 + len(params)
    mixed, s, tail = pl.pallas_call(
        functools.partial(_ssd_prompt_kernel, chunk=chunk),
        grid=(n_seq,),
        in_specs=[tok(7, SSD_INNER), tok(8, SSD_CONV_DIM), pl.BlockSpec((seq_len, NARROW_W), lambda b: (b, 0))]
                 + [_const_spec(p.shape) for p in params] + [_ANY_SPEC],
        out_specs=[pl.BlockSpec((seq_len, SSD_INNER), lambda b: (b, MIX_SSD)),
                   pl.BlockSpec((1, SSD_PAIRS, SSD_STATE, 2 * SSD_HEADDIM), lambda b: (b, 0, 0, 0)),
                   pl.BlockSpec((1, TAIL, SSD_CONV_DIM), lambda b: (b, 0, 0))],
        out_shape=[jax.ShapeDtypeStruct(mixed.shape, mixed.dtype),
                   jax.ShapeDtypeStruct((n_seq, SSD_PAIRS, SSD_STATE, 2 * SSD_HEADDIM), F32),
                   jax.ShapeDtypeStruct((n_seq, TAIL, SSD_CONV_DIM), F32)],
        input_output_aliases={n_in: 0},
        compiler_params=pltpu.CompilerParams(
            dimension_semantics=("arbitrary",), vmem_limit_bytes=V7X_VMEM_LIMIT),
        name="ssd_prompt",
    )(y, y, nar, *params, mixed)
    return mixed, tail[:, TAIL - (CONV_W - 1):], _ssd_state_from_pairs(s)


GDN_QKV = GDN_HEADS * GDN_DK
GDN_ROWS = 1024
GDN_UNROLL = 4
HIGHEST = lax.Precision.HIGHEST


def _split_bf16(x):
    hi = x.astype(BF16)
    return hi, (x - hi.astype(F32)).astype(BF16)


def _split3_bf16(x):
    hi = x.astype(BF16)
    r = x - hi.astype(F32)
    mid = r.astype(BF16)
    return hi, mid, (r - mid.astype(F32)).astype(BF16)


def _select_rows(m01, x):
    n = x.shape[1]
    p = jnp.dot(m01, jnp.concatenate(_split3_bf16(x), axis=1), preferred_element_type=F32)
    return p[:, :n] + p[:, n:2 * n] + p[:, 2 * n:]


def _select_cols(x, m01):
    m = x.shape[0]
    p = jnp.dot(jnp.concatenate(_split3_bf16(x), axis=0), m01, preferred_element_type=F32)
    return p[:m] + p[m:2 * m] + p[2 * m:]


def _dot3(a, b):
    a_hi, a_lo = _split_bf16(a)
    b_hi, b_lo = _split_bf16(b)
    m = a.shape[0]
    p = jnp.dot(jnp.concatenate([a_hi, a_lo], axis=0), b_hi, preferred_element_type=F32)
    return p[:m] + p[m:] + jnp.dot(a_hi, b_lo, preferred_element_type=F32)


def _two_block_diag(x0, x1):
    z0 = jnp.zeros_like(x0)
    z1 = jnp.zeros_like(x1)
    return jnp.concatenate([jnp.concatenate([x0, z1], axis=1), jnp.concatenate([z0, x1], axis=1)], axis=0)


def _l2_rows(x):
    return x * lax.rsqrt(jnp.sum(x * x, axis=-1, keepdims=True) + EPS)


def _gdn_prompt_kernel(q_ref, k_ref, v_ref, z_ref, nar_ref, cw_ref, dtb_ref, a_ref, nw_ref, tri_ref, sel_ref, mix_ref,
                       o_ref, s_ref, tail_ref, u_scr, w_scr, qd_scr, kd_scr, sc_scr, ea_scr, *, chunk, unroll):
    del mix_ref

    @pl.when(pl.program_id(1) == 0)
    def _():
        s_ref[...] = jnp.zeros_like(s_ref)
        tail_ref[...] = jnp.zeros_like(tail_ref)

    n_chunks = q_ref.shape[0] // chunk
    cat = GDN_HEADS * chunk
    lane = lax.broadcasted_iota(jnp.int32, (chunk, cat), 1)
    row = lax.broadcasted_iota(jnp.int32, (chunk, cat), 0)
    col = jnp.bitwise_and(lane, chunk - 1)
    causal = row >= col
    strict = row > col
    eye_cat = jnp.where(row == col, 1.0, 0.0).astype(F32)
    head_mask = [(lane >= h * chunk) & (lane < (h + 1) * chunk) for h in range(GDN_HEADS)]
    nar_lane = lax.broadcasted_iota(jnp.int32, (chunk, 128), 1)
    ones_cc = jnp.ones((chunk, chunk), BF16)
    n_sq = chunk.bit_length() - 2

    def mm_cat(l_cat, r_cat):
        bd = jnp.concatenate([jnp.where(m, r_cat, 0.0) for m in head_mask], axis=0)
        return _dot3(l_cat, bd)

    def neumann_step(t_cat, pw, update_t, update_pw):
        return (t_cat + mm_cat(t_cat, pw) if update_t else t_cat), (mm_cat(pw, pw) if update_pw else pw)

    col1 = lambda x, h: x[:, h * chunk:h * chunk + 1]

    def front(c):
        r = pl.ds(pl.multiple_of(c * chunk, chunk), chunk)
        before = pl.ds(pl.multiple_of(jnp.maximum(c * chunk - TAIL, 0), TAIL), TAIL)
        qkv = []
        for i, ref in enumerate((q_ref, k_ref, v_ref)):
            cols = slice(i * GDN_QKV, (i + 1) * GDN_QKV)
            tail = jnp.where(c == 0, tail_ref[0, :, cols], ref[before, :])
            qkv.append(_silu(_causal_conv(ref[r, :], tail, cw_ref.at[:, cols])))
        q, k, v = qkv
        nar = nar_ref[r, :]
        beta = jax.nn.sigmoid(nar)
        g = _select_rows(tri_ref[...], a_ref[...] * _softplus(nar + dtb_ref[...]))
        bg = _select_cols(jnp.where(nar_lane < NARROW_A, beta, g), sel_ref[...])
        b_c, g_c = bg[:, :cat], bg[:, cat:]
        g_r = _select_rows(ones_cc, g_c * eye_cat)
        decay = jnp.where(causal, jnp.exp(jnp.where(causal, g_c - g_r, 0.0)), 0.0)
        e_in = jnp.exp(g_c)
        e_out = jnp.exp(g_c[chunk - 1:chunk, :] - g_c)
        e_all = jnp.exp(g_c[chunk - 1:chunk, :])

        qn = [_l2_rows(q[:, h * GDN_DK:(h + 1) * GDN_DK]) * (GDN_DK ** -0.5) for h in range(GDN_HEADS)]
        kn = [_l2_rows(k[:, h * GDN_DK:(h + 1) * GDN_DK]) for h in range(GDN_HEADS)]
        vh = [v[:, h * GDN_DV:(h + 1) * GDN_DV] for h in range(GDN_HEADS)]
        kk, qk = [], []
        for p in range(GDN_HEADS // 2):
            h0, h1 = 2 * p, 2 * p + 1
            rhs = _two_block_diag(kn[h0], kn[h1]).astype(BF16)
            kk.append(_dot_nt(jnp.concatenate([kn[h0], kn[h1]], axis=1).astype(BF16), rhs))
            qk.append(_dot_nt(jnp.concatenate([qn[h0], qn[h1]], axis=1).astype(BF16), rhs))
        kk = jnp.concatenate(kk, axis=1)
        sc_scr[r, :] = (jnp.concatenate(qk, axis=1) * decay).astype(BF16)
        ea_scr[c] = jnp.broadcast_to(e_all, (TAIL, cat))
        for h in range(GDN_HEADS):
            hc = slice(h * GDN_DV, (h + 1) * GDN_DV)
            qd_scr[r, hc] = (qn[h] * col1(e_in, h)).astype(BF16)
            kd_scr[r, hc] = (kn[h] * col1(e_out, h)).astype(BF16)
        a_cat = jnp.where(strict, b_c * kk * decay, 0.0)
        rhs = [_two_block_diag(*[jnp.concatenate(
            [vh[h] * col1(b_c, h), kn[h] * (col1(b_c, h) * col1(e_in, h))], axis=1) for h in (2 * p, 2 * p + 1)])
            for p in range(GDN_HEADS // 2)]
        return r, a_cat, rhs

    def back(r, t_cat, rhs):
        for p in range(GDN_HEADS // 2):
            uw = _dot3(t_cat[:, p * 2 * chunk:(p + 1) * 2 * chunk], rhs[p])
            for i, h in enumerate((2 * p, 2 * p + 1)):
                hc = slice(h * GDN_DV, (h + 1) * GDN_DV)
                u_scr[r, hc] = uw[:, (2 * i) * GDN_DV:(2 * i + 1) * GDN_DV]
                w_scr[r, hc] = uw[:, (2 * i + 1) * GDN_DV:(2 * i + 2) * GDN_DV].astype(BF16)

    def prepare_group(i, carry):
        rs, ts, rhss = [], [], []
        for j in range(unroll):
            r, a_cat, rhs = front(i * unroll + j)
            rs.append(r)
            ts.append((eye_cat - a_cat, a_cat))
            rhss.append(rhs)
        for step in range(n_sq + 1):
            ts = [neumann_step(t, pw, update_t=step > 0, update_pw=step < n_sq) for t, pw in ts]
        for r, (t_cat, _), rhs in zip(rs, ts, rhss):
            back(r, t_cat, rhs)
        return carry

    lax.fori_loop(0, n_chunks // unroll, prepare_group, 0)

    def recur(c, carry):
        r = pl.ds(pl.multiple_of(c * chunk, chunk), chunk)
        e_all = ea_scr[c][0:1, :]
        for p in range(GDN_HEADS // 2):
            heads = (2 * p, 2 * p + 1)
            v_new, q_s = [], []
            for h in heads:
                hc = slice(h * GDN_DV, (h + 1) * GDN_DV)
                wq = jnp.concatenate([w_scr[r, hc], qd_scr[r, hc]], axis=0)
                ws = jnp.dot(wq, s_ref[0, h].astype(BF16), preferred_element_type=F32)
                v_new.append(u_scr[r, hc] - ws[:chunk])
                q_s.append(ws[chunk:])
            intra = jnp.dot(sc_scr[r, p * 2 * chunk:(p + 1) * 2 * chunk],
                            _two_block_diag(*v_new).astype(BF16), preferred_element_type=F32)
            for i, h in enumerate(heads):
                hc = slice(h * GDN_DV, (h + 1) * GDN_DV)
                o = q_s[i] + intra[:, i * GDN_DV:(i + 1) * GDN_DV]
                s_ref[0, h] = s_ref[0, h] * col1(e_all, h) + _dot_tn(kd_scr[r, hc], v_new[i].astype(BF16))
                o = o * lax.rsqrt(jnp.mean(o * o, axis=-1, keepdims=True) + EPS) * nw_ref[...]
                o_ref[r, hc] = (o * _silu(z_ref[r, hc])).astype(o_ref.dtype)
        return carry

    lax.fori_loop(0, n_chunks, recur, 0, unroll=math.gcd(RECURRENCE_UNROLL, n_chunks))
    last = pl.ds(q_ref.shape[0] - TAIL, TAIL)
    for i, ref in enumerate((q_ref, k_ref, v_ref)):
        tail_ref[0, :, i * GDN_QKV:(i + 1) * GDN_QKV] = ref[last, :]


def _gdn_params(lw, chunk):
    cat = GDN_HEADS * chunk
    lanes = np.arange(cat) // chunk
    sel = np.zeros((128, 2 * cat), np.float32)
    sel[NARROW_B + lanes, np.arange(cat)] = 1.0
    sel[NARROW_A + lanes, cat + np.arange(cat)] = 1.0
    return (lw['gdn_conv_w'].astype(F32), _lane_row(lw['gdn_dt_bias'], NARROW_A),
            _lane_row(-jnp.exp(lw['gdn_a_log'].astype(F32)), NARROW_A),
            lw['gdn_norm_w'].astype(F32).reshape(1, GDN_DV), jnp.tril(jnp.ones((chunk, chunk), BF16)),
            jnp.asarray(sel, BF16))


def gdn_prompt(y, nar, mixed, n_seq, seq_len, lw):
    chunk = min(GDN_CHUNK, seq_len)
    rows = min(GDN_ROWS, seq_len)
    n_chunks = rows // chunk
    unroll = math.gcd(GDN_UNROLL, n_chunks)
    params = _gdn_params(lw, chunk)
    nblk = seq_len // rows
    tok = lambda s: pl.BlockSpec((rows, GDN_QKV), lambda b, j: (b * nblk + j, PK_OFF[s] // GDN_QKV))
    const = lambda shape: pl.BlockSpec(shape, lambda b, j: (0,) * len(shape))
    n_in = 5 + len(params)
    mixed, s, tail = pl.pallas_call(
        functools.partial(_gdn_prompt_kernel, chunk=chunk, unroll=unroll),
        grid=(n_seq, nblk),
        in_specs=[tok(1), tok(2), tok(3), tok(4), pl.BlockSpec((rows, NARROW_W), lambda b, j: (b * nblk + j, 0))]
                 + [const(p.shape) for p in params] + [_ANY_SPEC],
        out_specs=[pl.BlockSpec((rows, GDN_QKV), lambda b, j: (b * nblk + j, MIX_GDN)),
                   pl.BlockSpec((1, GDN_HEADS, GDN_DK, GDN_DV), lambda b, j: (b, 0, 0, 0)),
                   pl.BlockSpec((1, TAIL, GDN_CONV_DIM), lambda b, j: (b, 0, 0))],
        out_shape=[jax.ShapeDtypeStruct(mixed.shape, mixed.dtype),
                   jax.ShapeDtypeStruct((n_seq, GDN_HEADS, GDN_DK, GDN_DV), F32),
                   jax.ShapeDtypeStruct((n_seq, TAIL, GDN_CONV_DIM), F32)],
        input_output_aliases={n_in: 0},
        scratch_shapes=[pltpu.VMEM((rows, GDN_QKV), F32), pltpu.VMEM((rows, GDN_QKV), BF16),
                        pltpu.VMEM((rows, GDN_QKV), BF16), pltpu.VMEM((rows, GDN_QKV), BF16),
                        pltpu.VMEM((rows, GDN_HEADS * chunk), BF16),
                        pltpu.VMEM((n_chunks, TAIL, GDN_HEADS * chunk), F32)],
        compiler_params=pltpu.CompilerParams(
            dimension_semantics=("arbitrary", "arbitrary"), vmem_limit_bytes=V7X_VMEM_LIMIT),
        name="gdn_prompt",
    )(y, y, y, y, nar, *params, mixed)
    return mixed, tail[:, TAIL - (CONV_W - 1):], s


def _conv_steps(buf_ref, raw_ref, w_ref, cols):
    n_t = raw_ref.shape[0]
    xx = [buf_ref[j, :, cols] for j in range(CONV_W - 1)] + [raw_ref[t] for t in range(n_t)]
    w = w_ref[:, cols]
    out = []
    for t in range(n_t):
        y = xx[t] * w[0:1, :]
        for j in range(1, CONV_W):
            y = y + xx[t + j] * w[j:j + 1, :]
        out.append(y)
    return out, xx[n_t:]


def _ssd_sample_kernel(z_ref, xbc_ref, nar_ref, buf_ref, s0_ref, cw_ref, cb_ref, dtb_ref, a_ref, dsk_ref, nw_ref,
                       mix_ref, sall_ref, o_ref, s_ref, nbuf_ref, qd_scr, kd_scr, xs_scr, ga_scr, acc_scr, lq, lk, lx):
    del mix_ref, sall_ref
    n_t, n_b = z_ref.shape[0], z_ref.shape[1]
    rep = SSD_HEADS // SSD_NGROUPS
    conv, tail = _conv_steps(buf_ref, xbc_ref, cw_ref, slice(0, SSD_CONV_DIM))
    for j in range(CONV_W - 1):
        nbuf_ref[j] = tail[j]
    xbc = [_silu(c + cb_ref[...]) for c in conv]
    dt, g = [], []
    for t in range(n_t):
        dt.append(_softplus(nar_ref[t] + dtb_ref[...]))
        la = dt[t] * a_ref[...]
        g.append(la if t == 0 else g[t - 1] + la)
    for grp in range(SSD_NGROUPS):
        bm = [x[:, SSD_INNER + grp * SSD_STATE:SSD_INNER + (grp + 1) * SSD_STATE] for x in xbc]
        cm = [x[:, SSD_INNER + SSD_BC + grp * SSD_STATE:SSD_INNER + SSD_BC + (grp + 1) * SSD_STATE] for x in xbc]
        sc = [[jnp.sum(cm[i] * bm[j], axis=-1, keepdims=True) for j in range(i + 1)] for i in range(n_t)]
        for h in range(grp * rep, (grp + 1) * rep):
            ln = NARROW_DT + h
            hc = slice(h * SSD_HEADDIM, (h + 1) * SSD_HEADDIM)
            gc = [x[:, ln:ln + 1] for x in g]
            dc = [x[:, ln:ln + 1] for x in dt]
            xs = [x[:, hc] for x in xbc]
            for t in range(n_t):
                qd_scr[h, t] = cm[t] * jnp.exp(gc[t])
                kd_scr[h, t] = bm[t] * (dc[t] * jnp.exp(gc[n_t - 1] - gc[t]))
                xs_scr[h, t] = xs[t]
                acc = None
                for j in range(t + 1):
                    term = (sc[t][j] * dc[j] * jnp.exp(gc[t] - gc[j])) * xs[j]
                    acc = term if acc is None else acc + term
                acc_scr[t, :, hc] = acc
            ga_scr[h] = jnp.broadcast_to(jnp.exp(gc[n_t - 1]), (n_b, SSD_STATE))

    for tile in (lq, lk, lx):
        tile[...] = jnp.zeros_like(tile)

    def per_seq(b, carry):
        row = pl.ds(b, 1)
        for h in range(SSD_HEADS):
            hc = slice(h * SSD_HEADDIM, (h + 1) * SSD_HEADDIM)
            for t in range(n_t):
                lq[h, t:t + 1, :] = qd_scr[h, t, row, :]
                lk[h, t:t + 1, :] = kd_scr[h, t, row, :]
                lx[h, t:t + 1, :] = xs_scr[h, t, row, :]
            s0 = s0_ref[b, h]
            inter = _dot_nt(lq[h].astype(BF16), s0.astype(BF16))
            s_ref[b, h] = s0 * ga_scr[h, row, :] + _dot_tn(lx[h].astype(BF16), lk[h].astype(BF16))
            for t in range(n_t):
                xs_scr[h, t, row, :] = inter[t:t + 1, :]
        return carry

    lax.fori_loop(0, n_b, per_seq, 0)

    gw = SSD_INNER // SSD_NGROUPS
    for t in range(n_t):
        for h in range(SSD_HEADS):
            hc = slice(h * SSD_HEADDIM, (h + 1) * SSD_HEADDIM)
            acc_scr[t, :, hc] = acc_scr[t, :, hc] + xs_scr[h, t]
        y = (acc_scr[t] + xbc[t][:, :SSD_INNER] * dsk_ref[...]) * _silu(z_ref[t])
        yn = [y[:, i * gw:(i + 1) * gw] * lax.rsqrt(
            jnp.mean(y[:, i * gw:(i + 1) * gw] * y[:, i * gw:(i + 1) * gw], axis=-1, keepdims=True) + EPS)
            for i in range(SSD_NGROUPS)]
        o_ref[t] = (jnp.concatenate(yn, axis=1) * nw_ref[...]).astype(o_ref.dtype)


def ssd_sample(y3, nar3, mixed3, buf0, states, new_states, layer, t0, n_t, lw):
    n_b = y3.shape[1]
    tb = _sample_block(t0, n_t)
    params = _ssd_params(lw, 1)[:-1]
    tok = lambda s, w: pl.BlockSpec((n_t, SAMPLE_BB, w), lambda i: (tb, i, PK_OFF[s] // w))
    bufspec = pl.BlockSpec((CONV_W - 1, SAMPLE_BB, SSD_CONV_DIM), lambda i: (0, i, 0))
    state = pl.BlockSpec((None, SAMPLE_BB, SSD_HEADS, SSD_HEADDIM, SSD_STATE), lambda i: (layer, i, 0, 0, 0))
    per_tok = lambda w: pltpu.VMEM((SSD_HEADS, n_t, SAMPLE_BB, w), F32)
    tile = lambda w: pltpu.VMEM((SSD_HEADS, LHS_ROWS, w), F32)
    n_in = 5 + len(params)
    mixed3, s, nbuf = pl.pallas_call(
        _ssd_sample_kernel,
        grid=(n_b // SAMPLE_BB,),
        in_specs=[tok(7, SSD_INNER), tok(8, SSD_CONV_DIM),
                  pl.BlockSpec((n_t, SAMPLE_BB, NARROW_W), lambda i: (tb, i, 0)), bufspec, state]
                 + [_const_spec(p.shape) for p in params] + [_ANY_SPEC, _ANY_SPEC],
        out_specs=[pl.BlockSpec((n_t, SAMPLE_BB, SSD_INNER), lambda i: (tb, i, MIX_SSD)), state, bufspec],
        out_shape=[jax.ShapeDtypeStruct(mixed3.shape, mixed3.dtype), jax.ShapeDtypeStruct(states.shape, F32),
                   jax.ShapeDtypeStruct((CONV_W - 1, n_b, SSD_CONV_DIM), F32)],
        input_output_aliases={n_in: 0, n_in + 1: 1},
        scratch_shapes=[per_tok(SSD_STATE), per_tok(SSD_STATE), per_tok(SSD_HEADDIM),
                        pltpu.VMEM((SSD_HEADS, SAMPLE_BB, SSD_STATE), F32),
                        pltpu.VMEM((n_t, SAMPLE_BB, SSD_INNER), F32),
                        tile(SSD_STATE), tile(SSD_STATE), tile(SSD_HEADDIM)],
        compiler_params=pltpu.CompilerParams(
            dimension_semantics=("arbitrary",), vmem_limit_bytes=V7X_VMEM_LIMIT),
        name="ssd_sample",
    )(y3, y3, nar3, jnp.swapaxes(buf0, 0, 1), states, *params, mixed3, new_states)
    return mixed3, jnp.swapaxes(nbuf, 0, 1), s


def _gdn_sample_kernel(q_ref, k_ref, v_ref, z_ref, nar_ref, buf_ref, s0_ref, cw_ref, dtb_ref, a_ref, nw_ref,
                       mix_ref, sall_ref, o_ref, s_ref, nbuf_ref, w_scr, qd_scr, kd_scr, u_scr, ga_scr, lwq, lk, lu):
    del mix_ref, sall_ref
    n_t, n_b = q_ref.shape[0], q_ref.shape[1]
    qkv = []
    for i, ref in enumerate((q_ref, k_ref, v_ref)):
        cols = slice(i * GDN_QKV, (i + 1) * GDN_QKV)
        conv, tail = _conv_steps(buf_ref, ref, cw_ref, cols)
        for j in range(CONV_W - 1):
            nbuf_ref[j, :, cols] = tail[j]
        qkv.append([_silu(c) for c in conv])
    beta, g = [], []
    for t in range(n_t):
        nar = nar_ref[t]
        beta.append(jax.nn.sigmoid(nar))
        gl = a_ref[...] * _softplus(nar + dtb_ref[...])
        g.append(gl if t == 0 else g[t - 1] + gl)

    scores = []
    for h in range(GDN_HEADS):
        hc = slice(h * GDN_DK, (h + 1) * GDN_DK)
        qn = [_l2_rows(x[:, hc]) * (GDN_DK ** -0.5) for x in qkv[0]]
        kn = [_l2_rows(x[:, hc]) for x in qkv[1]]
        vh = [x[:, hc] for x in qkv[2]]
        bc = [x[:, NARROW_B + h:NARROW_B + h + 1] for x in beta]
        gc = [x[:, NARROW_A + h:NARROW_A + h + 1] for x in g]
        us, ws = [], []
        for i in range(n_t):
            u = vh[i] * bc[i]
            w = kn[i] * (bc[i] * jnp.exp(gc[i]))
            for j in range(i):
                a_ij = bc[i] * jnp.sum(kn[i] * kn[j], axis=-1, keepdims=True) * jnp.exp(gc[i] - gc[j])
                u = u - a_ij * us[j]
                w = w - a_ij * ws[j]
            us.append(u)
            ws.append(w)
            u_scr[h, i] = u
            w_scr[h, i] = w
            qd_scr[h, i] = qn[i] * jnp.exp(gc[i])
            kd_scr[h, i] = kn[i] * jnp.exp(gc[n_t - 1] - gc[i])
        ga_scr[h] = jnp.broadcast_to(jnp.exp(gc[n_t - 1]), (n_b, GDN_DV))
        scores.append([[jnp.sum(qn[i] * kn[j], axis=-1, keepdims=True) * jnp.exp(gc[i] - gc[j])
                        for j in range(i + 1)] for i in range(n_t)])

    for tile in (lwq, lk, lu):
        tile[...] = jnp.zeros_like(tile)

    def per_pair(i, carry):
        seqs = [(s, 2 * i + s, pl.ds(2 * i + s, 1)) for s in range(2)]
        for s, b, row in seqs:
            for h in range(GDN_HEADS):
                n = s * GDN_HEADS + h
                for t in range(n_t):
                    lwq[n, t:t + 1, :] = w_scr[h, t, row, :]
                    lwq[n, LHS_ROWS + t:LHS_ROWS + t + 1, :] = qd_scr[h, t, row, :]
                    lk[n, t:t + 1, :] = kd_scr[h, t, row, :]
                    lu[n, t:t + 1, :] = u_scr[h, t, row, :]
        results = []
        for s, b, row in seqs:
            for h in range(GDN_HEADS):
                n = s * GDN_HEADS + h
                s0 = s0_ref[b, h]
                wq_s = jnp.dot(lwq[n].astype(BF16), s0.astype(BF16), preferred_element_type=F32)
                v_new = lu[n] - wq_s[:LHS_ROWS]
                s_ref[b, h] = s0 * ga_scr[h, row, :] + _dot_tn(lk[n].astype(BF16), v_new.astype(BF16))
                results.append((h, row, v_new, wq_s))
        for h, row, v_new, wq_s in results:
            for t in range(n_t):
                u_scr[h, t, row, :] = v_new[t:t + 1, :]
                qd_scr[h, t, row, :] = wq_s[LHS_ROWS + t:LHS_ROWS + t + 1, :]
        return carry

    lax.fori_loop(0, n_b // 2, per_pair, 0)

    for h in range(GDN_HEADS):
        hc = slice(h * GDN_DV, (h + 1) * GDN_DV)
        for i in range(n_t):
            o = qd_scr[h, i]
            for j in range(i + 1):
                o = o + scores[h][i][j] * u_scr[h, j]
            o = o * lax.rsqrt(jnp.mean(o * o, axis=-1, keepdims=True) + EPS) * nw_ref[...]
            o_ref[i, :, hc] = (o * _silu(z_ref[i, :, hc])).astype(o_ref.dtype)


def gdn_sample(y3, nar3, mixed3, buf0, states, new_states, layer, t0, n_t, lw):
    n_b = y3.shape[1]
    tb = _sample_block(t0, n_t)
    params = _gdn_params(lw, 1)[:4]
    tok = lambda s: pl.BlockSpec((n_t, SAMPLE_BB, GDN_QKV), lambda i: (tb, i, PK_OFF[s] // GDN_QKV))
    bufspec = pl.BlockSpec((CONV_W - 1, SAMPLE_BB, GDN_CONV_DIM), lambda i: (0, i, 0))
    state = pl.BlockSpec((None, SAMPLE_BB, GDN_HEADS, GDN_DK, GDN_DV), lambda i: (layer, i, 0, 0, 0))
    per_tok = pltpu.VMEM((GDN_HEADS, n_t, SAMPLE_BB, GDN_DK), F32)
    tile = lambda rows: pltpu.VMEM((2 * GDN_HEADS, rows, GDN_DK), F32)
    n_in = 7 + len(params)
    mixed3, s, nbuf = pl.pallas_call(
        _gdn_sample_kernel,
        grid=(n_b // SAMPLE_BB,),
        in_specs=[tok(1), tok(2), tok(3), tok(4),
                  pl.BlockSpec((n_t, SAMPLE_BB, NARROW_W), lambda i: (tb, i, 0)), bufspec, state]
                 + [_const_spec(p.shape) for p in params] + [_ANY_SPEC, _ANY_SPEC],
        out_specs=[pl.BlockSpec((n_t, SAMPLE_BB, GDN_QKV), lambda i: (tb, i, MIX_GDN)), state, bufspec],
        out_shape=[jax.ShapeDtypeStruct(mixed3.shape, mixed3.dtype), jax.ShapeDtypeStruct(states.shape, F32),
                   jax.ShapeDtypeStruct((CONV_W - 1, n_b, GDN_CONV_DIM), F32)],
        input_output_aliases={n_in: 0, n_in + 1: 1},
        scratch_shapes=[per_tok, per_tok, per_tok, per_tok, pltpu.VMEM((GDN_HEADS, SAMPLE_BB, GDN_DV), F32),
                        tile(2 * LHS_ROWS), tile(LHS_ROWS), tile(LHS_ROWS)],
        compiler_params=pltpu.CompilerParams(
            dimension_semantics=("arbitrary",), vmem_limit_bytes=V7X_VMEM_LIMIT),
        name="gdn_sample",
    )(y3, y3, y3, y3, nar3, jnp.swapaxes(buf0, 0, 1), states, *params, mixed3, new_states)
    return mixed3, jnp.swapaxes(nbuf, 0, 1), s


TM = 1088
TM_SPLIT = 512


def kernel(x_prompt, x_sample, p_prompt, p_sample, state_s5_re, state_s5_im, state_gdn, state_gdn_conv, state_ssd, state_ssd_conv, state_ret, norm_mix, w_in, s5_a_re, s5_a_im, s5_b_re, s5_b_im, s5_c_re, s5_c_im, s5_d, s5_log_step, s5_w_glu, s5_b_glu, gdn_conv_w, gdn_a_log, gdn_dt_bias, gdn_norm_w, ssd_conv_w, ssd_conv_b, ssd_dt_bias, ssd_a_log, ssd_d, ssd_norm_w, ret_ln_w, ret_ln_b, w_out, norm_ffn, w_ffn_in, w_ffn_out, norm_ple, w_ple_gate, w_ple_proj, norm_final):
    bp, lp, d = x_prompt.shape
    bs, ls, _ = x_sample.shape
    np_tok = bp * lp
    ns_tok = bs * ls
    n_tok = np_tok + ns_tok

    mixer_w = dict(
        s5_a_re=s5_a_re, s5_a_im=s5_a_im, s5_b_re=s5_b_re, s5_b_im=s5_b_im, s5_c_re=s5_c_re, s5_c_im=s5_c_im,
        s5_d=s5_d, s5_log_step=s5_log_step, s5_w_glu=s5_w_glu, s5_b_glu=s5_b_glu,
        gdn_conv_w=gdn_conv_w, gdn_a_log=gdn_a_log, gdn_dt_bias=gdn_dt_bias, gdn_norm_w=gdn_norm_w,
        ssd_conv_w=ssd_conv_w, ssd_conv_b=ssd_conv_b, ssd_dt_bias=ssd_dt_bias, ssd_a_log=ssd_a_log,
        ssd_d=ssd_d, ssd_norm_w=ssd_norm_w, ret_ln_w=ret_ln_w, ret_ln_b=ret_ln_b)

    assert bp == 4, "the prompt S5 kernel packs two time steps of four sequences per vreg"

    def tm_rows(t):
        t = jnp.swapaxes(t, -3, -2)
        return t.reshape(t.shape[:-3] + (t.shape[-3] * t.shape[-2], t.shape[-1]))

    def bm_seqs(t, b):
        return jnp.swapaxes(t.reshape(t.shape[0] // b, b, t.shape[1]), 0, 1)

    w_out_b, w_ffn_in_b, w_ffn_out_b = (w.astype(BF16) for w in (w_out, w_ffn_in, w_ffn_out))
    w_gate_b, w_proj_b = w_ple_gate.astype(BF16), w_ple_proj.astype(BF16)
    pp = p_prompt.reshape(DEPTH, np_tok, PLE_DIM)
    ps = tm_rows(p_sample)

    h = jnp.concatenate([x_prompt.reshape(np_tok, d), tm_rows(x_sample)], axis=0)
    ssd_t = jnp.swapaxes(state_ssd, -1, -2)
    new_gdn, new_ssd_t, new_ret = (uninitialized(s.shape, F32) for s in (state_gdn, ssd_t, state_ret))
    new_p, new_s = [], []
    for i in range(DEPTH):
        lw = {k: v[i] for k, v in mixer_w.items()}
        y, nar = in_projection(h, norm_mix[i], w_in[i], tm=TM)
        y3 = y.reshape(n_tok // bs, bs, PK_TOTAL)
        nar3 = nar.reshape(n_tok // bs, bs, NARROW_W)
        t0 = np_tok // bs

        tb = _s5_tables(lw)
        a_p, st5_p = s5_prompt(tm_rows(y[:np_tok, :S5_CH].reshape(bp, lp, S5_CH)), tb, rows=512)

        mixed = uninitialized((n_tok, d), BF16)
        mixed, gdn_buf_p, gdn_s_p = gdn_prompt(y, nar, mixed, bp, lp, lw)
        mixed, ssd_buf_p, ssd_s_p = ssd_prompt(y, nar, mixed, bp, lp, lw)
        mixed, ret_s_p = ret_prompt(y, mixed, bp, lp, lw)
        mixed, st5_s = s5_sample(y, mixed, _s5_state_to_lanes(state_s5_re[i], state_s5_im[i]), np_tok, ns_tok, tb)
        mixed3 = mixed.reshape(n_tok // bs, bs, d)
        mixed3, gdn_buf_s, new_gdn = gdn_sample(y3, nar3, mixed3, state_gdn_conv[i], state_gdn, new_gdn, i, t0, ls, lw)
        mixed3, ssd_buf_s, new_ssd_t = ssd_sample(y3, nar3, mixed3, state_ssd_conv[i], ssd_t, new_ssd_t, i, t0, ls, lw)
        mixed3, new_ret = ret_sample(y3, mixed3, state_ret, new_ret, i, t0, ls, PAST_LEN, lw)
        mixed = lax.dynamic_update_slice(mixed3.reshape(n_tok, d), bm_seqs(a_p, bp).reshape(np_tok, S5_CH), (0, 0))
        new_p.append(_s5_lanes_to_state(st5_p[:bp]) + (gdn_s_p, gdn_buf_p, ssd_s_p, ssd_buf_p, ret_s_p))
        new_s.append(_s5_lanes_to_state(st5_s) + (gdn_buf_s, ssd_buf_s))

        h = ffn_residual(mixed, w_out_b, h, norm_ffn[i], w_ffn_in_b, w_ffn_out_b, i, tm=TM // 2, th=512)
        if i + 1 < DEPTH:
            h = ple_residual(h, norm_ple[i], w_gate_b, pp, ps, w_proj_b, i, tm=TM_SPLIT)
        else:
            y_p, y_s = ple_residual(h, norm_ple[i], w_gate_b, pp, ps, w_proj_b, i, tm=TM_SPLIT, nf=norm_final)

    stack_p = [jnp.stack([st[j] for st in new_p]) for j in range(7)]
    s5_re_s, s5_im_s, gdn_buf_s, ssd_buf_s = (jnp.stack([st[j] for st in new_s]) for j in range(4))
    return (y_p.reshape(bp, lp, d), bm_seqs(y_s, bs), *stack_p,
            s5_re_s, s5_im_s, new_gdn, gdn_buf_s, jnp.swapaxes(new_ssd_t, -1, -2), ssd_buf_s, new_ret)
```
